```python
import jax
import jax.numpy as jnp
from jax import lax
import numpy as np

D_MODEL = 1024
BATCH = 4
SEQ = 4096
DEPTH = 4

N_HEADS = 8
HEAD_DIM = 64
N_KV = 2
HPG = N_HEADS // N_KV
ROT_DIM = HEAD_DIM // 4
ROPE_THETA = 500000.0
CMP_LEN = 32
CMP_STRIDE = 16
SLC_LEN = 64
SLC_TOPN = 16
WINDOW = 512
Q_BLOCK = 128
SLC_Q_CHUNK = 64
SCALE = HEAD_DIM ** -0.5
NEG = -1e30
FORCE_INIT = 1e6
FORCE_LOCAL = 2e6
POOL_GROUPS = 4
POOL_WINDOWS = (2, 4, 8, 16)
POOL_WIDTH = 512
POOL_GW = POOL_WIDTH // POOL_GROUPS
ATT_WIDTH = N_HEADS * HEAD_DIM
KV_WIDTH = 3 * 2 * N_KV * HEAD_DIM
GATE_NSA = 3 * N_HEADS
MERGE_GATES = 2 * D_MODEL
IN_WIDTH = ATT_WIDTH + KV_WIDTH + GATE_NSA + POOL_WIDTH + MERGE_GATES
SPLITS = (ATT_WIDTH, ATT_WIDTH + KV_WIDTH, ATT_WIDTH + KV_WIDTH + GATE_NSA,
          ATT_WIDTH + KV_WIDTH + GATE_NSA + POOL_WIDTH)
PEER_HEADS = 8
N_KEYS = 128
N_EXPERTS = N_KEYS * N_KEYS
PEER_TOPK = 16
PEER_DK = 128
PEER_CHUNK = 128
DN_ALPHA = (2 * DEPTH) ** 0.25
DN_BETA = (8 * DEPTH) ** -0.25
LN_EPS = 1e-5

kernel_name = 'nsa_pool_peer_hybrid'


def layer_norm(x, g, b):
    xf = x.astype(jnp.float32)
    mu = jnp.mean(xf, -1, keepdims=True)
    var = jnp.mean(jnp.square(xf - mu), -1, keepdims=True)
    return ((xf - mu) * lax.rsqrt(var + LN_EPS) * g + b).astype(x.dtype)


def rope_tables(pos):
    inv = ROPE_THETA ** (-jnp.arange(0, ROT_DIM, 2, dtype=jnp.float32) / ROT_DIM)
    ang = pos.astype(jnp.float32)[:, None] * inv[None, :]
    return jnp.cos(ang), jnp.sin(ang)


def apply_rope(z, cos, sin):
    half = ROT_DIM // 2
    z1, z2, zp = z[..., :half], z[..., half:ROT_DIM], z[..., ROT_DIM:]
    co = cos[:, None, :].astype(z.dtype)
    si = sin[:, None, :].astype(z.dtype)
    return jnp.concatenate([z1 * co - z2 * si, z1 * si + z2 * co, zp], axis=-1)


def compress(z, pe, w1, w2):
    B, S, G, dh = z.shape
    n_cmp = (S - CMP_LEN) // CMP_STRIDE + 1
    idx = jnp.arange(n_cmp)[:, None] * CMP_STRIDE + jnp.arange(CMP_LEN)[None, :]
    blocks = z[:, idx] + pe[:, None, :]
    flat = jnp.moveaxis(blocks, 3, 2).reshape(B, n_cmp, G, CMP_LEN * dh)
    return jax.nn.gelu(flat @ w1) @ w2


def cmp_to_slc(n_cmp, n_slc):
    st = jnp.arange(n_cmp) * CMP_STRIDE
    js = jnp.arange(n_slc) * SLC_LEN
    ov = (jnp.minimum(st[:, None] + CMP_LEN, js[None, :] + SLC_LEN)
          - jnp.maximum(st[:, None], js[None, :]))
    return jnp.maximum(ov, 0).astype(jnp.float32) / CMP_STRIDE


def sel_attend(qg, idx, k, v, t):
    B, S, G, _, dh = qg.shape
    n_slc = S // SLC_LEN
    n_sel = idx.shape[-1]
    kt = k.reshape(B, n_slc, SLC_LEN, G, dh).transpose(0, 3, 1, 2, 4)
    vt = v.reshape(B, n_slc, SLC_LEN, G, dh).transpose(0, 3, 1, 2, 4)
    n_chunk = S // SLC_Q_CHUNK
    bi = jnp.arange(B)[:, None, None, None]
    gi = jnp.arange(G)[None, None, :, None]

    def chunk(args):
        qc, ic, tc = args
        kb = kt[bi, gi, ic]
        vb = vt[bi, gi, ic]
        s = jnp.einsum('bcghd,bcgnld->bcghnl', qc, kb).astype(jnp.float32) * SCALE
        kpos = ic[..., None] * SLC_LEN + jnp.arange(SLC_LEN)
        ok = kpos <= tc[None, :, None, None, None]
        s = jnp.where(ok[:, :, :, None], s, NEG)
        p = jax.nn.softmax(s.reshape(s.shape[:4] + (-1,)), axis=-1).reshape(s.shape)
        return jnp.einsum('bcghnl,bcgnld->bcghd', p.astype(vb.dtype), vb)

    qs = qg.reshape(B, n_chunk, SLC_Q_CHUNK, G, HPG, dh).swapaxes(0, 1)
    isel = idx.reshape(B, n_chunk, SLC_Q_CHUNK, G, n_sel).swapaxes(0, 1)
    ts = t.reshape(n_chunk, SLC_Q_CHUNK)
    o = lax.map(chunk, (qs, isel, ts))
    return o.swapaxes(0, 1).reshape(B, S, G, HPG, dh)


def win_attend(qg, k, v, t):
    B, S, G, _, dh = qg.shape
    nb = S // Q_BLOCK
    nw = WINDOW // Q_BLOCK

    def bands(z):
        zp = jnp.pad(z, ((0, 0), (WINDOW, 0), (0, 0), (0, 0))).reshape(B, nb + nw, Q_BLOCK, G, dh)
        return jnp.concatenate([zp[:, i:i + nb] for i in range(nw + 1)], axis=2)

    kb, vb = bands(k), bands(v)
    qb = qg.reshape(B, nb, Q_BLOCK, G, HPG, dh)
    s = jnp.einsum('bnqghd,bnkgd->bnghqk', qb, kb).astype(jnp.float32) * SCALE
    tq = t.reshape(nb, Q_BLOCK)
    kpos = (jnp.arange(nb) * Q_BLOCK - WINDOW)[:, None] + jnp.arange((nw + 1) * Q_BLOCK)[None, :]
    diff = tq[:, :, None] - kpos[:, None, :]
    ok = (diff >= 0) & (diff < WINDOW) & (kpos[:, None, :] >= 0)
    s = jnp.where(ok[None, :, None, None], s, NEG)
    p = jax.nn.softmax(s, axis=-1)
    o = jnp.einsum('bnghqk,bnkgd->bnqghd', p.astype(vb.dtype), vb)
    return o.reshape(B, S, G, HPG, dh)


def multiscale_pool(p, w_pool, pool_scale):
    B, S, _ = p.shape
    pf = p.reshape(B, S, POOL_GROUPS, POOL_GW).astype(jnp.float32)
    cs = jnp.pad(jnp.cumsum(pf, axis=1), ((0, 0), (1, 0), (0, 0), (0, 0)))
    t = jnp.arange(S)
    win = jnp.array(POOL_WINDOWS)
    lo_idx = jnp.maximum(t[:, None] + 1 - win[None, :], 0)
    lo = cs[:, lo_idx, jnp.arange(POOL_GROUPS)[None, :]]
    cnt = jnp.minimum(t[:, None] + 1, win[None, :]).astype(jnp.float32)
    pooled = (cs[:, 1:] - lo) / cnt[None, :, :, None] - pf
    y = jnp.einsum('bsgc,gcd->bsgd', pooled.astype(p.dtype), w_pool).reshape(B, S, POOL_WIDTH)
    return y * pool_scale


def token_mixer(h, w_in, cmp_pe, cmp_w1, cmp_w2, w_pool, pool_scale, w_lift, w_o, cos, sin, cos_c, sin_c):
    B, S, _ = h.shape
    t = jnp.arange(S)
    q, kv, g_nsa, p_in, g_mrg = jnp.split(h @ w_in, SPLITS, axis=-1)
    q = apply_rope(q.reshape(B, S, N_HEADS, HEAD_DIM), cos, sin).reshape(B, S, N_KV, HPG, HEAD_DIM)
    kv = kv.reshape(B, S, 3, 2, N_KV, HEAD_DIM)

    k_c = apply_rope(compress(kv[:, :, 0, 0], cmp_pe[0], cmp_w1[0], cmp_w2[0]), cos_c, sin_c)
    v_c = compress(kv[:, :, 0, 1], cmp_pe[1], cmp_w1[1], cmp_w2[1])
    n_cmp = k_c.shape[1]
    cmp_end = jnp.arange(n_cmp) * CMP_STRIDE + CMP_LEN - 1
    vis = cmp_end[None, :] <= t[:, None]
    s = jnp.einsum('bsghd,bcgd->bsghc', q, k_c).astype(jnp.float32) * SCALE
    s = jnp.where(vis[None, :, None, None], s, NEG)
    any_vis = jnp.any(vis, axis=-1).astype(jnp.float32)[None, :, None, None, None]
    p_cmp = jax.nn.softmax(s, axis=-1) * any_vis
    o_cmp = jnp.einsum('bsghc,bcgd->bsghd', p_cmp.astype(v_c.dtype), v_c)

    n_slc = S // SLC_LEN
    imp = jnp.einsum('bsghc,cj->bsgj', p_cmp, cmp_to_slc(n_cmp, n_slc))
    blk = jnp.arange(n_slc)
    cur = (t // SLC_LEN)[:, None]
    score = jnp.where((blk[None, :] <= cur)[None, :, None, :], imp, NEG)
    score = jnp.where((blk == 0)[None, None, None, :], FORCE_INIT, score)
    score = jnp.where((blk[None, :] == cur)[None, :, None, :], FORCE_LOCAL, score)
    _, sel_idx = lax.top_k(score, min(SLC_TOPN, n_slc))
    o_slc = sel_attend(q, sel_idx, apply_rope(kv[:, :, 1, 0], cos, sin), kv[:, :, 1, 1], t)

    o_win = win_attend(q, apply_rope(kv[:, :, 2, 0], cos, sin), kv[:, :, 2, 1], t)

    gate = jax.nn.sigmoid(g_nsa.reshape(B, S, N_KV, HPG, 3))
    o_att = (gate[..., 0:1] * o_cmp + gate[..., 1:2] * o_slc
             + gate[..., 2:3] * o_win).reshape(B, S, ATT_WIDTH)

    o_pool = multiscale_pool(p_in, w_pool, pool_scale)

    g_a, g_b = jnp.split(jax.nn.sigmoid(g_mrg), 2, axis=-1)
    merged = g_a * (o_att @ w_lift[0]) + g_b * (o_pool @ w_lift[1])
    return merged @ w_o


def peer_ffn(h, w_pq, sub_keys, u_tab, v_tab):
    B, S, D = h.shape
    q = (h @ w_pq).reshape(B, S, PEER_HEADS, 2, PEER_DK // 2)
    s = jnp.einsum('bshpd,hpkd->bshpk', q, sub_keys).astype(jnp.float32)
    s1, i1 = lax.top_k(s[..., 0, :], PEER_TOPK)
    s2, i2 = lax.top_k(s[..., 1, :], PEER_TOPK)
    cand = (s1[..., :, None] + s2[..., None, :]).reshape(B, S, PEER_HEADS, -1)
    cand_idx = (i1[..., :, None] * N_KEYS + i2[..., None, :]).reshape(B, S, PEER_HEADS, -1)
    sv, si = lax.top_k(cand, PEER_TOPK)
    eidx = jnp.take_along_axis(cand_idx, si, axis=-1)
    g = jax.nn.softmax(sv, axis=-1)
    n_chunk = (B * S) // PEER_CHUNK
    hf = h.reshape(n_chunk, PEER_CHUNK, D)
    ef = eidx.reshape(n_chunk, PEER_CHUNK, PEER_HEADS * PEER_TOPK)
    gf = g.reshape(n_chunk, PEER_CHUNK, PEER_HEADS * PEER_TOPK).astype(h.dtype)

    def body(args):
        hc, ec, gc = args
        a = jax.nn.gelu(jnp.einsum('cd,ced->ce', hc, u_tab[ec]))
        return jnp.einsum('ce,ced->cd', gc * a, v_tab[ec])

    return lax.map(body, (hf, ef, gf)).reshape(B, S, D)


def setup_inputs(seed: int = 0) -> dict:
    key = jax.random.key(seed)
    ks = jax.random.split(key, 20)
    f32 = jnp.float32
    nrm = lambda k, shp, sc: jax.random.normal(k, shp, f32) * sc
    return {
        'x': nrm(ks[0], (BATCH, SEQ, D_MODEL), 1.0),
        'c': nrm(ks[1], (BATCH, D_MODEL), 1.0),
        'w_ada': nrm(ks[2], (DEPTH, D_MODEL, 6 * D_MODEL), D_MODEL ** -0.5),
        'b_ada': nrm(ks[3], (DEPTH, 6 * D_MODEL), 0.02),
        'w_in': nrm(ks[4], (DEPTH, D_MODEL, IN_WIDTH), D_MODEL ** -0.5),
        'cmp_pe': nrm(ks[5], (DEPTH, 2, CMP_LEN, HEAD_DIM), 0.02),
        'cmp_w1': nrm(ks[6], (DEPTH, 2, CMP_LEN * HEAD_DIM, HEAD_DIM), (CMP_LEN * HEAD_DIM) ** -0.5),
        'cmp_w2': nrm(ks[7], (DEPTH, 2, HEAD_DIM, HEAD_DIM), HEAD_DIM ** -0.5),
        'w_pool': nrm(ks[8], (DEPTH, POOL_GROUPS, POOL_GW, POOL_GW), POOL_GW ** -0.5),
        'pool_scale': 1.0 + nrm(ks[9], (DEPTH, POOL_WIDTH), 0.02),
        'w_lift': nrm(ks[10], (DEPTH, 2, ATT_WIDTH, D_MODEL), ATT_WIDTH ** -0.5),
        'w_o': nrm(ks[11], (DEPTH, D_MODEL, D_MODEL), D_MODEL ** -0.5 * DN_BETA),
        'ln_g': 1.0 + nrm(ks[12], (DEPTH, 2, D_MODEL), 0.02),
        'ln_b': nrm(ks[13], (DEPTH, 2, D_MODEL), 0.02),
        'peer_wq': nrm(ks[14], (DEPTH, D_MODEL, PEER_HEADS * PEER_DK), D_MODEL ** -0.5),
        'peer_keys': nrm(ks[15], (DEPTH, PEER_HEADS, 2, N_KEYS, PEER_DK // 2), (PEER_DK // 2) ** -0.5),
        'peer_u': nrm(ks[16], (DEPTH, N_EXPERTS, D_MODEL), D_MODEL ** -0.5),
        'peer_v': nrm(ks[17], (DEPTH, N_EXPERTS, D_MODEL), (PEER_HEADS * PEER_TOPK) ** -0.5 * DN_BETA),
    }


def reference(x, c, w_ada, b_ada, w_in, cmp_pe, cmp_w1, cmp_w2, w_pool, pool_scale, w_lift, w_o,
              ln_g, ln_b, peer_wq, peer_keys, peer_u, peer_v):
    S = x.shape[1]
    cos, sin = rope_tables(jnp.arange(S))
    n_cmp = (S - CMP_LEN) // CMP_STRIDE + 1
    cos_c, sin_c = rope_tables(jnp.arange(n_cmp) * CMP_STRIDE + CMP_LEN - 1)
    c_act = jax.nn.silu(c)
    for l in range(DEPTH):
        mods = c_act @ w_ada[l] + b_ada[l]
        sh1, sc1, g1, sh2, sc2, g2 = [m[:, None, :] for m in jnp.split(mods, 6, axis=-1)]
        h = x * (1.0 + sc1) + sh1
        y = token_mixer(h, w_in[l], cmp_pe[l], cmp_w1[l], cmp_w2[l], w_pool[l], pool_scale[l],
                        w_lift[l], w_o[l], cos, sin, cos_c, sin_c)
        x = layer_norm(DN_ALPHA * x + g1 * y, ln_g[l, 0], ln_b[l, 0])
        h = x * (1.0 + sc2) + sh2
        y = peer_ffn(h, peer_wq[l], peer_keys[l], peer_u[l], peer_v[l])
        x = layer_norm(DN_ALPHA * x + g2 * y, ln_g[l, 1], ln_b[l, 1])
    return x
```

```python
import functools

import jax
import jax.numpy as jnp
import numpy as np
from jax import lax
from jax.experimental import pallas as pl
from jax.experimental.pallas import tpu as pltpu

F32 = jnp.float32
BF16 = jnp.bfloat16
I32 = jnp.int32
HI = lax.Precision.HIGHEST

D_MODEL = 1024
N_HEADS = 8
HEAD_DIM = 64
N_KV = 2
HPG = N_HEADS // N_KV
ROT_DIM = HEAD_DIM // 4
ROT_HALF = ROT_DIM // 2
ROPE_THETA = 500000.0
CMP_LEN = 32
CMP_STRIDE = 16
SLC_LEN = 64
SLC_TOPN = 16
WINDOW = 512
SCALE = HEAD_DIM ** -0.5
NEG = -1e30
FORCE_INIT = 1e6
FORCE_LOCAL = 2e6
POOL_GROUPS = 4
POOL_WINDOWS = (2, 4, 8, 16)
POOL_WIDTH = 512
POOL_GW = POOL_WIDTH // POOL_GROUPS
POOL_HALO = 16
ATT_WIDTH = N_HEADS * HEAD_DIM
KV_WIDTH = 3 * 2 * N_KV * HEAD_DIM
GATE_NSA = 3 * N_HEADS
MERGE_GATES = 2 * D_MODEL
PEER_HEADS = 8
N_KEYS = 128
PEER_TOPK = 16
PEER_DK = 128
PEER_SEL = PEER_HEADS * PEER_TOPK
LN_EPS = 1e-5

LANES = 128
SUBLANES = 8
VMEM_LIMIT = 56 * 1024 * 1024

NT_DIMS = (((1,), (1,)), ((), ()))


def _cparams(sem):
    return pltpu.CompilerParams(dimension_semantics=sem, vmem_limit_bytes=VMEM_LIMIT)


def _ada_kernel(c_ref, w_ref, b_ref, o_ref):
    c = c_ref[...]
    ca = c * jax.nn.sigmoid(c)
    o_ref[...] = jnp.dot(ca, w_ref[...], precision=HI, preferred_element_type=F32) + b_ref[...]


def _ada_call(c_pad, w_ada, b_ada):
    depth = w_ada.shape[0]
    nblk = w_ada.shape[2] // D_MODEL
    rows = c_pad.shape[0]
    return pl.pallas_call(
        _ada_kernel,
        grid=(depth, nblk),
        in_specs=[
            pl.BlockSpec((rows, D_MODEL), lambda l, j: (0, 0)),
            pl.BlockSpec((None, D_MODEL, D_MODEL), lambda l, j: (l, 0, j)),
            pl.BlockSpec((None, 1, D_MODEL), lambda l, j: (l, 0, j)),
        ],
        out_specs=pl.BlockSpec((None, rows, D_MODEL), lambda l, j: (l, 0, j)),
        out_shape=jax.ShapeDtypeStruct((depth, rows, nblk * D_MODEL), F32),
        compiler_params=_cparams(("arbitrary", "arbitrary")),
        name="ada_mod",
    )(c_pad, w_ada, b_ada.reshape(depth, 1, -1))


HI_COLS = ATT_WIDTH + KV_WIDTH
LO_COLS = POOL_WIDTH + MERGE_GATES + LANES


def _rope_lanes(z, rc, rs1, rs2):
    return z * rc + pltpu.roll(z, ROT_HALF, 1) * rs1 + pltpu.roll(z, LANES - ROT_HALF, 1) * rs2


def _inproj_kernel(x_ref, sc_ref, sh_ref, whi_ref, wlo_ref, rc_ref, rs1_ref, rs2_ref,
                   q_ref, kv_ref, p_ref, mrg_ref, gn_ref):
    h = x_ref[...] * (1.0 + sc_ref[...]) + sh_ref[...]
    a = jnp.dot(h, whi_ref[...], precision=HI, preferred_element_type=F32)
    b = jnp.dot(h.astype(BF16), wlo_ref[...], preferred_element_type=F32)
    rc, rs1, rs2 = rc_ref[...], rs1_ref[...], rs2_ref[...]
    for j in range(ATT_WIDTH // LANES):
        q_ref[:, j * LANES:(j + 1) * LANES] = _rope_lanes(a[:, j * LANES:(j + 1) * LANES], rc, rs1, rs2)
    for br in range(3):
        c0 = ATT_WIDTH + br * 2 * LANES
        k = a[:, c0:c0 + LANES]
        if br > 0:
            k = _rope_lanes(k, rc, rs1, rs2)
        kv_ref[:, br * 2 * LANES:br * 2 * LANES + LANES] = k
        kv_ref[:, br * 2 * LANES + LANES:(br + 1) * 2 * LANES] = a[:, c0 + LANES:c0 + 2 * LANES]
    p_ref[...] = b[:, :POOL_WIDTH]
    mrg_ref[...] = b[:, POOL_WIDTH:POOL_WIDTH + MERGE_GATES]
    gn_ref[...] = b[:, POOL_WIDTH + MERGE_GATES:]


def _inproj_call(x2, sc, sh, w_hi, w_lo, rc, rs1, rs2, seq, tm):
    n = x2.shape[0]
    tpb = seq // tm
    tok = lambda i: (i, 0)
    bat = lambda i: (i // tpb, 0, 0)
    pos = lambda i: (i % tpb, 0)
    full = lambda i: (0, 0)
    return pl.pallas_call(
        _inproj_kernel,
        grid=(n // tm,),
        in_specs=[
            pl.BlockSpec((tm, D_MODEL), tok),
            pl.BlockSpec((None, 1, D_MODEL), bat),
            pl.BlockSpec((None, 1, D_MODEL), bat),
            pl.BlockSpec((D_MODEL, HI_COLS), full),
            pl.BlockSpec((D_MODEL, LO_COLS), full),
            pl.BlockSpec((tm, LANES), pos),
            pl.BlockSpec((tm, LANES), pos),
            pl.BlockSpec((tm, LANES), pos),
        ],
        out_specs=[
            pl.BlockSpec((tm, ATT_WIDTH), tok),
            pl.BlockSpec((tm, KV_WIDTH), tok),
            pl.BlockSpec((tm, POOL_WIDTH), tok),
            pl.BlockSpec((tm, MERGE_GATES), tok),
            pl.BlockSpec((tm, LANES), tok),
        ],
        out_shape=[
            jax.ShapeDtypeStruct((n, ATT_WIDTH), F32),
            jax.ShapeDtypeStruct((n, KV_WIDTH), F32),
            jax.ShapeDtypeStruct((n, POOL_WIDTH), F32),
            jax.ShapeDtypeStruct((n, MERGE_GATES), F32),
            jax.ShapeDtypeStruct((n, LANES), F32),
        ],
        compiler_params=_cparams(("arbitrary",)),
        name="in_proj",
    )(x2, sc, sh, w_hi, w_lo, rc, rs1, rs2)


def _compress_kernel(z_ref, pet_ref, peb_ref, w1t_ref, w1b_ref, w2_ref, rc_ref, rs1_ref, rs2_ref, o_ref):
    z = z_ref[...]
    rows = z.shape[0]
    top = jnp.dot(z + pet_ref[...], w1t_ref[...], precision=HI, preferred_element_type=F32)
    bot = jnp.dot(z + peb_ref[...], w1b_ref[...], precision=HI, preferred_element_type=F32)
    pre = top + pltpu.roll(bot, rows - 1, 0)
    y = jnp.dot(jax.nn.gelu(pre), w2_ref[...], precision=HI, preferred_element_type=F32)
    o_ref[...] = _rope_lanes(y, rc_ref[...], rs1_ref[...], rs2_ref[...])


def _compress_call(z, pet, peb, w1t, w1b, w2, rc, rs1, rs2):
    b, _, rows, width = z.shape
    kvsel = lambda i, j: (j, 0, 0)
    return pl.pallas_call(
        _compress_kernel,
        grid=(b, 2),
        in_specs=[
            pl.BlockSpec((None, None, rows, width), lambda i, j: (i, j, 0, 0)),
            pl.BlockSpec((None, 1, width), kvsel),
            pl.BlockSpec((None, 1, width), kvsel),
            pl.BlockSpec((None, width, LANES), kvsel),
            pl.BlockSpec((None, width, LANES), kvsel),
            pl.BlockSpec((None, LANES, LANES), kvsel),
            pl.BlockSpec((None, rows, LANES), kvsel),
            pl.BlockSpec((None, rows, LANES), kvsel),
            pl.BlockSpec((None, rows, LANES), kvsel),
        ],
        out_specs=pl.BlockSpec((None, None, rows, LANES), lambda i, j: (i, j, 0, 0)),
        out_shape=jax.ShapeDtypeStruct((b, 2, rows, LANES), F32),
        compiler_params=_cparams(("arbitrary", "arbitrary")),
        name="compress",
    )(z, pet, peb, w1t, w1b, w2, rc, rs1, rs2)


def _cmpsel_kernel(q_ref, kc_ref, vc_ref, c2s_ref, o_ref, sel_ref, *, tq):
    t0 = pl.program_id(1) * tq
    kc = kc_ref[...]
    vc = vc_ref[...]
    rows = kc.shape[0]
    trow = t0 + lax.broadcasted_iota(I32, (tq, rows), 0)
    cend = lax.broadcasted_iota(I32, (tq, rows), 1) * CMP_STRIDE + (CMP_LEN - 1)
    vis = cend <= trow
    anyv = (trow[:, :1] >= CMP_LEN - 1).astype(F32)
    imp = jnp.zeros((tq, LANES), F32)
    for g in range(N_KV):
        kg = kc[:, g * HEAD_DIM:(g + 1) * HEAD_DIM]
        vg = vc[:, g * HEAD_DIM:(g + 1) * HEAD_DIM].astype(BF16)
        psum = jnp.zeros((tq, rows), F32)
        for h in range(HPG):
            hd = g * HPG + h
            qh = q_ref[:, hd * HEAD_DIM:(hd + 1) * HEAD_DIM]
            s = lax.dot_general(qh, kg, NT_DIMS, precision=HI, preferred_element_type=F32) * SCALE
            s = jnp.where(vis, s, NEG)
            e = jnp.exp(s - jnp.max(s, axis=-1, keepdims=True))
            p = e / jnp.sum(e, axis=-1, keepdims=True) * anyv
            o_ref[:, hd * HEAD_DIM:(hd + 1) * HEAD_DIM] = jnp.dot(
                p.astype(BF16), vg, preferred_element_type=F32)
            psum = psum + p
        imp = imp + jnp.dot(psum, c2s_ref[g], precision=HI, preferred_element_type=F32)
    lane = lax.broadcasted_iota(I32, (tq, LANES), 1)
    blk = lane & (SLC_LEN - 1)
    cur = lax.shift_right_logical(t0 + lax.broadcasted_iota(I32, (tq, LANES), 0), 6)
    score = jnp.where(blk <= cur, imp, NEG)
    score = jnp.where(blk == 0, FORCE_INIT, score)
    score = jnp.where(blk == cur, FORCE_LOCAL, score)
    sc_t = score.T
    nblk = LANES // N_KV
    jrow = lax.broadcasted_iota(I32, (nblk, tq), 0)
    sel_parts = []
    for g in range(N_KV):
        sc = sc_t[g * nblk:(g + 1) * nblk]
        cnt = jnp.zeros((nblk, tq), I32)
        for k in range(nblk):
            rk = sc[k:k + 1, :]
            ge = (rk >= sc).astype(I32)
            gt = (rk > sc).astype(I32)
            cnt = cnt + jnp.where(jrow > k, ge, gt)
        sel_parts.append((cnt < SLC_TOPN).astype(F32))
    sel_ref[...] = jnp.concatenate(sel_parts, axis=0).T


def _cmpsel_call(q, kvc, c2s, seq, tq):
    n = q.shape[0]
    b = n // seq
    nq = seq // tq
    rows = kvc.shape[2]
    tok = lambda i, j: (i * nq + j, 0)
    return pl.pallas_call(
        functools.partial(_cmpsel_kernel, tq=tq),
        grid=(b, nq),
        in_specs=[
            pl.BlockSpec((tq, ATT_WIDTH), tok),
            pl.BlockSpec((None, None, rows, LANES), lambda i, j: (i, 0, 0, 0)),
            pl.BlockSpec((None, None, rows, LANES), lambda i, j: (i, 1, 0, 0)),
            pl.BlockSpec((N_KV, rows, LANES), lambda i, j: (0, 0, 0)),
        ],
        out_specs=[pl.BlockSpec((tq, ATT_WIDTH), tok), pl.BlockSpec((tq, LANES), tok)],
        out_shape=[jax.ShapeDtypeStruct((n, ATT_WIDTH), F32), jax.ShapeDtypeStruct((n, LANES), F32)],
        compiler_params=_cparams(("arbitrary", "arbitrary")),
        name="cmp_select",
    )(q, kvc, kvc, c2s)


def _slc_kernel(q_ref, k_ref, v_ref, sel_ref, o_ref, qs_sc, m_sc, l_sc, acc_sc, *, tq, tk):
    qi = pl.program_id(1)
    kt = pl.program_id(2)
    nk = pl.num_programs(2)

    @pl.when(kt == 0)
    def _init():
        for hd in range(N_HEADS):
            qs_sc[hd] = (q_ref[:, hd * HEAD_DIM:(hd + 1) * HEAD_DIM] * SCALE).astype(BF16)
        m_sc[...] = jnp.full(m_sc.shape, NEG, F32)
        l_sc[...] = jnp.zeros(l_sc.shape, F32)
        acc_sc[...] = jnp.zeros(acc_sc.shape, F32)

    @pl.when(kt * tk <= qi * tq + (tq - 1))
    def _step():
        t = qi * tq + lax.broadcasted_iota(I32, (tq, tk), 0)
        kp = kt * tk + lax.broadcasted_iota(I32, (tq, tk), 1)
        causal = kp <= t
        nblk = LANES // N_KV
        jb = lax.broadcasted_iota(I32, (nblk, tk), 0)
        kb = lax.shift_right_logical(kt * tk + lax.broadcasted_iota(I32, (nblk, tk), 1), 6)
        expand = (jb == kb).astype(BF16)
        for g in range(N_KV):
            selg = sel_ref[:, g * nblk:(g + 1) * nblk].astype(BF16)
            member = jnp.dot(selg, expand, preferred_element_type=F32)
            valid = jnp.logical_and(causal, member > 0.5)
            kg = k_ref[:, g * HEAD_DIM:(g + 1) * HEAD_DIM].astype(BF16)
            vg = v_ref[:, g * HEAD_DIM:(g + 1) * HEAD_DIM].astype(BF16)
            for h in range(HPG):
                hd = g * HPG + h
                s = lax.dot_general(qs_sc[hd], kg, NT_DIMS, preferred_element_type=F32)
                s = jnp.where(valid, s, NEG)
                m_old = m_sc[hd]
                m_new = jnp.maximum(m_old, jnp.max(s, axis=-1, keepdims=True))
                alpha = jnp.exp(m_old - m_new)
                p = jnp.exp(s - m_new)
                l_sc[hd] = alpha * l_sc[hd] + jnp.sum(p, axis=-1, keepdims=True)
                acc_sc[hd] = alpha * acc_sc[hd] + jnp.dot(p.astype(BF16), vg, preferred_element_type=F32)
                m_sc[hd] = m_new

    @pl.when(kt == nk - 1)
    def _fin():
        for hd in range(N_HEADS):
            o_ref[:, hd * HEAD_DIM:(hd + 1) * HEAD_DIM] = acc_sc[hd] / l_sc[hd]


def _slc_call(q, kv, sel, seq, tq, tk):
    n = q.shape[0]
    b = n // seq
    nq = seq // tq
    nk = seq // tk
    tok = lambda i, j, k: (i * nq + j, 0)

    def key_map(col):
        def f(i, j, k):
            last = (j * tq + tq - 1) // tk
            return (i * nk + jnp.minimum(k, last), col)
        return f

    return pl.pallas_call(
        functools.partial(_slc_kernel, tq=tq, tk=tk),
        grid=(b, nq, nk),
        in_specs=[
            pl.BlockSpec((tq, ATT_WIDTH), tok),
            pl.BlockSpec((tk, LANES), key_map(2)),
            pl.BlockSpec((tk, LANES), key_map(3)),
            pl.BlockSpec((tq, LANES), tok),
        ],
        out_specs=pl.BlockSpec((tq, ATT_WIDTH), tok),
        out_shape=jax.ShapeDtypeStruct((n, ATT_WIDTH), F32),
        scratch_shapes=[
            pltpu.VMEM((N_HEADS, tq, HEAD_DIM), BF16),
            pltpu.VMEM((N_HEADS, tq, 1), F32),
            pltpu.VMEM((N_HEADS, tq, 1), F32),
            pltpu.VMEM((N_HEADS, tq, HEAD_DIM), F32),
        ],
        compiler_params=_cparams(("arbitrary", "arbitrary", "arbitrary")),
        name="slc_attn",
    )(q, kv, kv, sel)


def _win_kernel(q_ref, *refs, tq, nkb):
    k_refs = refs[:nkb]
    v_refs = refs[nkb:2 * nkb]
    o_ref = refs[2 * nkb]
    qi = pl.program_id(1)
    t = qi * tq + lax.broadcasted_iota(I32, (tq, tq), 0)
    col = lax.broadcasted_iota(I32, (tq, tq), 1)
    oks = []
    for j in range(nkb):
        kp = (qi - (nkb - 1) + j) * tq + col
        diff = t - kp
        oks.append(jnp.logical_and(jnp.logical_and(diff >= 0, diff < WINDOW), kp >= 0))
    for g in range(N_KV):
        kgs = [k_refs[j][:, g * HEAD_DIM:(g + 1) * HEAD_DIM].astype(BF16) for j in range(nkb)]
        vgs = [v_refs[j][:, g * HEAD_DIM:(g + 1) * HEAD_DIM].astype(BF16) for j in range(nkb)]
        for h in range(HPG):
            hd = g * HPG + h
            qh = (q_ref[:, hd * HEAD_DIM:(hd + 1) * HEAD_DIM] * SCALE).astype(BF16)
            ss = [jnp.where(oks[j], lax.dot_general(qh, kgs[j], NT_DIMS, preferred_element_type=F32), NEG)
                  for j in range(nkb)]
            m = jnp.max(ss[0], axis=-1, keepdims=True)
            for j in range(1, nkb):
                m = jnp.maximum(m, jnp.max(ss[j], axis=-1, keepdims=True))
            ps = [jnp.exp(s - m) for s in ss]
            l = jnp.sum(ps[0], axis=-1, keepdims=True)
            o = jnp.dot(ps[0].astype(BF16), vgs[0], preferred_element_type=F32)
            for j in range(1, nkb):
                l = l + jnp.sum(ps[j], axis=-1, keepdims=True)
                o = o + jnp.dot(ps[j].astype(BF16), vgs[j], preferred_element_type=F32)
            o_ref[:, hd * HEAD_DIM:(hd + 1) * HEAD_DIM] = o / l


def _win_call(q, kv, seq, tq):
    n = q.shape[0]
    b = n // seq
    nq = seq // tq
    nkb = WINDOW // tq + 1
    tok = lambda i, j: (i * nq + j, 0)

    def key_map(col, back):
        return lambda i, j: (i * nq + jnp.maximum(j - back, 0), col)

    k_specs = [pl.BlockSpec((tq, LANES), key_map(4, nkb - 1 - jj)) for jj in range(nkb)]
    v_specs = [pl.BlockSpec((tq, LANES), key_map(5, nkb - 1 - jj)) for jj in range(nkb)]
    return pl.pallas_call(
        functools.partial(_win_kernel, tq=tq, nkb=nkb),
        grid=(b, nq),
        in_specs=[pl.BlockSpec((tq, ATT_WIDTH), tok)] + k_specs + v_specs,
        out_specs=pl.BlockSpec((tq, ATT_WIDTH), tok),
        out_shape=jax.ShapeDtypeStruct((n, ATT_WIDTH), F32),
        compiler_params=_cparams(("arbitrary", "arbitrary")),
        name="win_attn",
    )(q, *([kv] * (2 * nkb)))


def _pool_kernel(p_ref, prev_ref, w_ref, sc_ref, o_ref, *, ts):
    i = pl.program_id(1)
    x = p_ref[...]
    prev = prev_ref[...] * (i > 0).astype(F32)
    xe = jnp.concatenate([prev, x], axis=0)
    t1 = (i * ts + 1 + lax.broadcasted_iota(I32, (ts, POOL_GW), 0)).astype(F32)
    for g, w in enumerate(POOL_WINDOWS):
        a = xe[:, g * POOL_GW:(g + 1) * POOL_GW]
        off = POOL_HALO
        span = 1
        while span < w:
            a = a[span:] + a[:-span]
            off -= span
            span *= 2
        sums = a[off:off + ts]
        cnt = jnp.minimum(t1, float(w))
        pooled = sums / cnt - x[:, g * POOL_GW:(g + 1) * POOL_GW]
        y = jnp.dot(pooled.astype(BF16), w_ref[g], preferred_element_type=F32)
        o_ref[:, g * POOL_GW:(g + 1) * POOL_GW] = y * sc_ref[:, g * POOL_GW:(g + 1) * POOL_GW]


def _pool_call(p_in, w_pool, pool_scale, seq, ts):
    n = p_in.shape[0]
    b = n // seq
    nt = seq // ts
    hpt = ts // POOL_HALO
    tok = lambda i, j: (i * nt + j, 0)
    return pl.pallas_call(
        functools.partial(_pool_kernel, ts=ts),
        grid=(b, nt),
        in_specs=[
            pl.BlockSpec((ts, POOL_WIDTH), tok),
            pl.BlockSpec((POOL_HALO, POOL_WIDTH), lambda i, j: (i * nt * hpt + jnp.maximum(j * hpt - 1, 0), 0)),
            pl.BlockSpec((POOL_GROUPS, POOL_GW, POOL_GW), lambda i, j: (0, 0, 0)),
            pl.BlockSpec((1, POOL_WIDTH), lambda i, j: (0, 0)),
        ],
        out_specs=pl.BlockSpec((ts, POOL_WIDTH), tok),
        out_shape=jax.ShapeDtypeStruct((n, POOL_WIDTH), F32),
        compiler_params=_cparams(("arbitrary", "arbitrary")),
        name="pool_mix",
    )(p_in, p_in, w_pool, pool_scale)


def _layer_norm(z, g, b):
    mu = jnp.mean(z, axis=-1, keepdims=True)
    zc = z - mu
    var = jnp.mean(zc * zc, axis=-1, keepdims=True)
    return zc * lax.rsqrt(var + LN_EPS) * g + b


def _merge_kernel(oc_ref, os_ref, ow_ref, gn_ref, op_ref, gm_ref, x_ref, g1_ref, lng_ref, lnb_ref,
                  wl_ref, wo_ref, eb_ref, o_ref, *, alpha):
    gate = jax.nn.sigmoid(gn_ref[...])
    branches = (oc_ref, os_ref, ow_ref)
    oatt = None
    for br in range(3):
        gx = jnp.dot(gate, eb_ref[br], precision=HI, preferred_element_type=F32)
        term = gx * branches[br][...]
        oatt = term if oatt is None else oatt + term
    la = jnp.dot(oatt.astype(BF16), wl_ref[0], preferred_element_type=F32)
    lb = jnp.dot(op_ref[...].astype(BF16), wl_ref[1], preferred_element_type=F32)
    gm = jax.nn.sigmoid(gm_ref[...])
    merged = gm[:, :D_MODEL] * la + gm[:, D_MODEL:] * lb
    y = jnp.dot(merged.astype(BF16), wo_ref[...], preferred_element_type=F32)
    z = alpha * x_ref[...] + g1_ref[...] * y
    o_ref[...] = _layer_norm(z, lng_ref[...], lnb_ref[...])


def _merge_call(oc, osl, ow, gn, op, gm, x2, g1, lng, lnb, wl, wo, eb, seq, tm, alpha):
    n = x2.shape[0]
    tpb = seq // tm
    tok = lambda i: (i, 0)
    bat = lambda i: (i // tpb, 0, 0)
    return pl.pallas_call(
        functools.partial(_merge_kernel, alpha=alpha),
        grid=(n // tm,),
        in_specs=[
            pl.BlockSpec((tm, ATT_WIDTH), tok), pl.BlockSpec((tm, ATT_WIDTH), tok),
            pl.BlockSpec((tm, ATT_WIDTH), tok), pl.BlockSpec((tm, LANES), tok),
            pl.BlockSpec((tm, POOL_WIDTH), tok), pl.BlockSpec((tm, MERGE_GATES), tok),
            pl.BlockSpec((tm, D_MODEL), tok),
            pl.BlockSpec((None, 1, D_MODEL), bat),
            pl.BlockSpec((1, D_MODEL), lambda i: (0, 0)), pl.BlockSpec((1, D_MODEL), lambda i: (0, 0)),
            pl.BlockSpec((2, ATT_WIDTH, D_MODEL), lambda i: (0, 0, 0)),
            pl.BlockSpec((D_MODEL, D_MODEL), lambda i: (0, 0)),
            pl.BlockSpec((3, LANES, ATT_WIDTH), lambda i: (0, 0, 0)),
        ],
        out_specs=pl.BlockSpec((tm, D_MODEL), tok),
        out_shape=jax.ShapeDtypeStruct((n, D_MODEL), F32),
        compiler_params=_cparams(("arbitrary",)),
        name="merge_out",
    )(oc, osl, ow, gn, op, gm, x2, g1, lng, lnb, wl, wo, eb)


def _extract_top(cur, ids, n):
    rows = cur.shape[0]
    rio = lax.broadcasted_iota(I32, cur.shape, 0)
    vals, outs = [], []
    for _ in range(n):
        m = jnp.max(cur, axis=0, keepdims=True)
        pos = jnp.min(jnp.where(cur == m, rio, rows), axis=0, keepdims=True)
        hit = rio == pos
        vals.append(m)
        outs.append(pos if ids is None else jnp.max(jnp.where(hit, ids, -1), axis=0, keepdims=True))
        cur = jnp.where(hit, -jnp.inf, cur)
    return jnp.concatenate(vals, axis=0), jnp.concatenate(outs, axis=0)


def _route_kernel(x_ref, sc_ref, sh_ref, wq_ref, keys_ref, h_ref, e_ref, g_ref,
                  st_sc, ts_sc, ti_sc, eo_sc, go_sc):
    h = x_ref[...] * (1.0 + sc_ref[...]) + sh_ref[...]
    h_ref[...] = h
    qp = jnp.dot(h, wq_ref[...], precision=HI, preferred_element_type=F32)
    half = PEER_DK // 2
    for hp in range(2 * PEER_HEADS):
        st_sc[hp] = lax.dot_general(keys_ref[hp], qp[:, hp * half:(hp + 1) * half], NT_DIMS,
                                    precision=HI, preferred_element_type=F32)

    def half_body(hp, carry):
        vals, ids = _extract_top(st_sc[hp], None, PEER_TOPK)
        ts_sc[hp] = vals
        ti_sc[hp] = ids
        return carry

    lax.fori_loop(0, 2 * PEER_HEADS, half_body, 0)

    def head_body(hh, carry):
        s1, s2 = ts_sc[2 * hh], ts_sc[2 * hh + 1]
        i1, i2 = ti_sc[2 * hh], ti_sc[2 * hh + 1]
        cand = jnp.concatenate([s1[a:a + 1, :] + s2 for a in range(PEER_TOPK)], axis=0)
        cidx = jnp.concatenate([i1[a:a + 1, :] * N_KEYS + i2 for a in range(PEER_TOPK)], axis=0)
        sv, ei = _extract_top(cand, cidx, PEER_TOPK)
        ex = jnp.exp(sv - sv[0:1, :])
        go_sc[hh] = ex / jnp.sum(ex, axis=0, keepdims=True)
        eo_sc[hh] = ei.astype(F32)
        return carry

    lax.fori_loop(0, PEER_HEADS, head_body, 0)
    e_all = jnp.concatenate([eo_sc[hh] for hh in range(PEER_HEADS)], axis=0)
    g_all = jnp.concatenate([go_sc[hh] for hh in range(PEER_HEADS)], axis=0)
    e_ref[...] = e_all.T.astype(I32)
    g_ref[...] = g_all.T


def _route_call(x2, sc, sh, wq, keys, seq, tt):
    n = x2.shape[0]
    tpb = seq // tt
    tok = lambda i: (i, 0)
    bat = lambda i: (i // tpb, 0, 0)
    nhp = 2 * PEER_HEADS
    return pl.pallas_call(
        _route_kernel,
        grid=(n // tt,),
        in_specs=[
            pl.BlockSpec((tt, D_MODEL), tok),
            pl.BlockSpec((None, 1, D_MODEL), bat),
            pl.BlockSpec((None, 1, D_MODEL), bat),
            pl.BlockSpec((D_MODEL, PEER_HEADS * PEER_DK), lambda i: (0, 0)),
            pl.BlockSpec((nhp, N_KEYS, PEER_DK // 2), lambda i: (0, 0, 0)),
        ],
        out_specs=[pl.BlockSpec((tt, D_MODEL), tok), pl.BlockSpec((tt, PEER_SEL), tok),
                   pl.BlockSpec((tt, PEER_SEL), tok)],
        out_shape=[jax.ShapeDtypeStruct((n, D_MODEL), F32), jax.ShapeDtypeStruct((n, PEER_SEL), I32),
                   jax.ShapeDtypeStruct((n, PEER_SEL), F32)],
        scratch_shapes=[
            pltpu.VMEM((nhp, N_KEYS, tt), F32),
            pltpu.VMEM((nhp, PEER_TOPK, tt), F32),
            pltpu.VMEM((nhp, PEER_TOPK, tt), I32),
            pltpu.VMEM((PEER_HEADS, PEER_TOPK, tt), F32),
            pltpu.VMEM((PEER_HEADS, PEER_TOPK, tt), F32),
        ],
        compiler_params=_cparams(("arbitrary",)),
        name="peer_route",
    )(x2, sc, sh, wq, keys)


HALF_ROWS = SUBLANES // 2
HI_MASK = -65536


def _unpack_pair_tile(tab_ref, e):
    tile = pl.multiple_of((e >> 1) * SUBLANES, SUBLANES)
    w = tab_ref[pl.ds(tile, SUBLANES), :]
    lo = lax.bitcast_convert_type(w << 16, F32)
    hi = lax.bitcast_convert_type(w & HI_MASK, F32)
    return lo, hi, e & 1


def _fold_rows(vs):
    row = lax.broadcasted_iota(I32, (SUBLANES, LANES), 0)
    shift = HALF_ROWS
    while len(vs) > 1:
        low = (row & shift) == 0
        nxt = []
        for a, b in zip(vs[0::2], vs[1::2]):
            if 2 * shift == SUBLANES:
                nxt.append(jnp.where(low, a, b) + pltpu.roll(jnp.where(low, b, a), shift, 0))
            else:
                nxt.append(jnp.where(low, a + pltpu.roll(a, SUBLANES - shift, 0),
                                     b + pltpu.roll(b, shift, 0)))
        vs = nxt
        shift //= 2
    return vs[0]


def _fold_order():
    idx = [[i] * SUBLANES for i in range(SUBLANES)]
    shift = HALF_ROWS
    while len(idx) > 1:
        idx = [[a[r] if (r & shift) == 0 else b[r] for r in range(SUBLANES)]
               for a, b in zip(idx[0::2], idx[1::2])]
        shift //= 2
    return idx[0]


def _peer_u_kernel(e_sm, tab_ref, hv_ref, gate_ref, coef_ref, hvar_sc, *, tt):
    row = lax.broadcasted_iota(I32, (SUBLANES, LANES), 0)
    low = row < HALF_ROWS
    eye = (lax.broadcasted_iota(I32, (PEER_SEL, LANES), 0) ==
           lax.broadcasted_iota(I32, (PEER_SEL, LANES), 1))
    order = _fold_order()

    def token(t, carry):
        hv = hv_ref[t]
        hsw = pltpu.roll(hv, HALF_ROWS, 0)
        hvar_sc[0] = jnp.where(low, hv, 0.0)
        hvar_sc[1] = jnp.where(low, hsw, 0.0)
        hvar_sc[2] = jnp.where(low, 0.0, hsw)
        hvar_sc[3] = jnp.where(low, 0.0, hv)
        folded = []
        for j in range(PEER_SEL // SUBLANES):
            prods = [None] * SUBLANES
            for r, src in enumerate(order):
                lo, hi, par = _unpack_pair_tile(tab_ref, e_sm[t, j * SUBLANES + r])
                prods[src] = lo * hvar_sc[2 * par] + hi * hvar_sc[2 * par + 1]
            folded.append(_fold_rows(prods))
        part = jnp.concatenate(folded, axis=0)
        col = jnp.sum(part, axis=-1, keepdims=True)
        a_row = jnp.sum(jnp.where(eye, col, 0.0), axis=0, keepdims=True)
        coef_ref[t] = gate_ref[t] * jax.nn.gelu(a_row)
        return carry

    lax.fori_loop(0, tt, token, 0)


def _peer_v_kernel(e_sm, c_sm, tab_ref, y_ref, pm_sc, *, tt, nacc):
    row = lax.broadcasted_iota(I32, (SUBLANES, LANES), 0)
    low = row < HALF_ROWS
    pm_sc[0] = jnp.where(low, 1.0, 0.0)
    pm_sc[1] = jnp.where(low, 0.0, 1.0)

    def token(t, carry):
        acc_lo = [jnp.zeros((SUBLANES, LANES), F32) for _ in range(nacc)]
        acc_hi = [jnp.zeros((SUBLANES, LANES), F32) for _ in range(nacc)]
        for k in range(PEER_SEL):
            lo, hi, par = _unpack_pair_tile(tab_ref, e_sm[t, k])
            cv = c_sm[t, k] * pm_sc[par]
            acc_lo[k % nacc] = acc_lo[k % nacc] + cv * lo
            acc_hi[k % nacc] = acc_hi[k % nacc] + cv * hi
        a_lo, a_hi = acc_lo[0], acc_hi[0]
        for i in range(1, nacc):
            a_lo = a_lo + acc_lo[i]
            a_hi = a_hi + acc_hi[i]
        a_lo = a_lo + pltpu.roll(a_lo, HALF_ROWS, 0)
        a_hi = a_hi + pltpu.roll(a_hi, HALF_ROWS, 0)
        y_ref[t] = jnp.where(low, a_lo, a_hi)
        return carry

    lax.fori_loop(0, tt, token, 0)


def _resident_table_spec(tab):
    return pl.BlockSpec(tab.shape, lambda i: (0, 0), pipeline_mode=pl.Buffered(1))


def _peer_u_call(eidx, tab, hv, gate3, tt):
    n = eidx.shape[0]
    return pl.pallas_call(
        functools.partial(_peer_u_kernel, tt=tt),
        grid=(n // tt,),
        in_specs=[
            pl.BlockSpec((tt, PEER_SEL), lambda i: (i, 0), memory_space=pltpu.SMEM),
            _resident_table_spec(tab),
            pl.BlockSpec((tt, SUBLANES, LANES), lambda i: (i, 0, 0)),
            pl.BlockSpec((tt, 1, PEER_SEL), lambda i: (i, 0, 0)),
        ],
        out_specs=pl.BlockSpec((tt, 1, PEER_SEL), lambda i: (i, 0, 0)),
        out_shape=jax.ShapeDtypeStruct((n, 1, PEER_SEL), F32),
        scratch_shapes=[pltpu.VMEM((4, SUBLANES, LANES), F32)],
        compiler_params=_cparams(("arbitrary",)),
        name="peer_u",
    )(eidx, tab, hv, gate3)


def _peer_v_call(eidx, coef, tab, tt):
    n = eidx.shape[0]
    return pl.pallas_call(
        functools.partial(_peer_v_kernel, tt=tt, nacc=4),
        grid=(n // tt,),
        in_specs=[
            pl.BlockSpec((tt, PEER_SEL), lambda i: (i, 0), memory_space=pltpu.SMEM),
            pl.BlockSpec((tt, PEER_SEL), lambda i: (i, 0), memory_space=pltpu.SMEM),
            _resident_table_spec(tab),
        ],
        out_specs=pl.BlockSpec((tt, SUBLANES, LANES), lambda i: (i, 0, 0)),
        out_shape=jax.ShapeDtypeStruct((n, SUBLANES, LANES), F32),
        scratch_shapes=[pltpu.VMEM((2, SUBLANES, LANES), F32)],
        compiler_params=_cparams(("arbitrary",)),
        name="peer_v",
    )(eidx, coef, tab)


def _resln_kernel(x_ref, y_ref, g_ref, lng_ref, lnb_ref, o_ref, *, alpha):
    z = alpha * x_ref[...] + g_ref[...] * y_ref[...]
    o_ref[...] = _layer_norm(z, lng_ref[...], lnb_ref[...])


def _resln_call(x2, y2, g2, lng, lnb, seq, tm, alpha):
    n = x2.shape[0]
    tpb = seq // tm
    tok = lambda i: (i, 0)
    return pl.pallas_call(
        functools.partial(_resln_kernel, alpha=alpha),
        grid=(n // tm,),
        in_specs=[
            pl.BlockSpec((tm, D_MODEL), tok), pl.BlockSpec((tm, D_MODEL), tok),
            pl.BlockSpec((None, 1, D_MODEL), lambda i: (i // tpb, 0, 0)),
            pl.BlockSpec((1, D_MODEL), lambda i: (0, 0)), pl.BlockSpec((1, D_MODEL), lambda i: (0, 0)),
        ],
        out_specs=pl.BlockSpec((tm, D_MODEL), tok),
        out_shape=jax.ShapeDtypeStruct((n, D_MODEL), F32),
        compiler_params=_cparams(("arbitrary",)),
        name="res_ln",
    )(x2, y2, g2, lng, lnb)


def _rope_lane_tables(pos):
    inv = ROPE_THETA ** (-jnp.arange(0, ROT_DIM, 2, dtype=F32) / ROT_DIM)
    ang = pos.astype(F32)[:, None] * inv[None, :]
    cos, sin = jnp.cos(ang), jnp.sin(ang)
    lane = np.arange(LANES) % HEAD_DIM
    fidx = lane % ROT_HALF
    first = jnp.asarray(lane < ROT_HALF)
    second = jnp.asarray((lane >= ROT_HALF) & (lane < ROT_DIM))
    rot = jnp.asarray(lane < ROT_DIM)
    cl, sl = cos[:, fidx], sin[:, fidx]
    rc = jnp.where(rot, cl, 1.0)
    rs1 = jnp.where(second, sl, 0.0)
    rs2 = jnp.where(first, -sl, 0.0)
    return rc, rs1, rs2


def _pack_table(tab):
    e, d = tab.shape
    bits = lax.bitcast_convert_type(tab.astype(BF16), jnp.uint16).astype(jnp.uint32)
    word = bits[:, :d // 2] | (bits[:, d // 2:] << 16)
    return lax.bitcast_convert_type(word, I32).reshape(e * (d // 2) // LANES, LANES)


def _cmp_to_slc_wide(rows, n_slc):
    st = np.arange(rows) * CMP_STRIDE
    js = np.arange(n_slc) * SLC_LEN
    ov = np.minimum(st[:, None] + CMP_LEN, js[None, :] + SLC_LEN) - np.maximum(st[:, None], js[None, :])
    c2s = np.maximum(ov, 0).astype(np.float32) / CMP_STRIDE
    wide = np.zeros((N_KV, rows, LANES), np.float32)
    nblk = LANES // N_KV
    for g in range(N_KV):
        wide[g, :, g * nblk:g * nblk + n_slc] = c2s
    return jnp.asarray(wide)


def _gate_expanders():
    eb = np.zeros((3, LANES, ATT_WIDTH), np.float32)
    for hd in range(N_HEADS):
        for br in range(3):
            eb[br, hd * 3 + br, hd * HEAD_DIM:(hd + 1) * HEAD_DIM] = 1.0
    return jnp.asarray(eb)


class _Consts:
    def __init__(self, seq):
        self.rows = seq // CMP_STRIDE
        self.rope = _rope_lane_tables(jnp.arange(seq))
        cpos = jnp.arange(self.rows) * CMP_STRIDE + CMP_LEN - 1
        crope = _rope_lane_tables(cpos)
        ident = (jnp.ones_like(crope[0]), jnp.zeros_like(crope[0]), jnp.zeros_like(crope[0]))
        self.crope = tuple(jnp.stack([a, b]) for a, b in zip(crope, ident))
        self.c2s = _cmp_to_slc_wide(self.rows, seq // SLC_LEN)
        self.eb = _gate_expanders()


def _token_mixer_layer(x2, sc1, sh1, g1, w_in, cmp_pe, cmp_w1, cmp_w2, w_pool, pool_scale, w_lift, w_o,
                       lng, lnb, cst, bsz, seq, alpha):
    d = D_MODEL
    rows = cst.rows
    s1 = ATT_WIDTH + KV_WIDTH
    s2 = s1 + GATE_NSA
    s3 = s2 + POOL_WIDTH
    w_hi = w_in[:, :s1]
    w_gate = jnp.pad(w_in[:, s1:s2], ((0, 0), (0, LANES - GATE_NSA)))
    w_lo = jnp.concatenate([w_in[:, s2:s3], w_in[:, s3:], w_gate], axis=1).astype(BF16)
    q, kv, p_in, g_mrg, g_nsa = _inproj_call(x2, sc1, sh1, w_hi, w_lo, *cst.rope, seq, 256)

    eye_g = jnp.eye(N_KV, dtype=F32)
    zc = kv[:, :2 * LANES].reshape(bsz, rows, CMP_STRIDE, 2, LANES)
    zc = jnp.transpose(zc, (0, 3, 1, 2, 4)).reshape(bsz, 2, rows, CMP_STRIDE * LANES)
    w1x = jnp.einsum('klde,gh->klgdhe', cmp_w1.reshape(2, CMP_LEN, HEAD_DIM, HEAD_DIM), eye_g)
    w1x = w1x.reshape(2, CMP_LEN * LANES, LANES)
    half = CMP_STRIDE * LANES
    pex = jnp.broadcast_to(cmp_pe[:, :, None, :], (2, CMP_LEN, N_KV, HEAD_DIM)).reshape(2, 1, CMP_LEN * LANES)
    w2x = jnp.einsum('kef,gh->kgehf', cmp_w2, eye_g).reshape(2, LANES, LANES)
    kvc = _compress_call(zc, pex[:, :, :half], pex[:, :, half:], w1x[:, :half], w1x[:, half:], w2x,
                         *cst.crope)

    o_cmp, sel = _cmpsel_call(q, kvc, cst.c2s, seq, 256)
    o_slc = _slc_call(q, kv, sel, seq, 256, 256)
    o_win = _win_call(q, kv, seq, 256)
    o_pool = _pool_call(p_in, w_pool.astype(BF16), pool_scale.reshape(1, -1), seq, 512)
    return _merge_call(o_cmp, o_slc, o_win, g_nsa, o_pool, g_mrg, x2, g1,
                       lng.reshape(1, d), lnb.reshape(1, d),
                       w_lift.astype(BF16), w_o.astype(BF16), cst.eb, seq, 256, alpha)


def _peer_layer(x2, sc2, sh2, g2, peer_wq, peer_keys, peer_u, peer_v, lng, lnb, seq, alpha,
                tt_route=256, tt_gather=64, tm=512):
    n, d = x2.shape
    keys = peer_keys.reshape(2 * PEER_HEADS, N_KEYS, PEER_DK // 2)
    h2, eidx, gate = _route_call(x2, sc2, sh2, peer_wq, keys, seq, tt_route)
    hv = h2.reshape(n, SUBLANES, LANES)
    coef = _peer_u_call(eidx, _pack_table(peer_u), hv, gate.reshape(n, 1, PEER_SEL), tt_gather)
    y = _peer_v_call(eidx, coef.reshape(n, PEER_SEL), _pack_table(peer_v), tt_gather)
    return _resln_call(x2, y.reshape(n, d), g2, lng.reshape(1, d), lnb.reshape(1, d), seq, tm, alpha)


def kernel(x, c, w_ada, b_ada, w_in, cmp_pe, cmp_w1, cmp_w2, w_pool, pool_scale, w_lift, w_o,
           ln_g, ln_b, peer_wq, peer_keys, peer_u, peer_v):
    bsz, seq, d = x.shape
    depth = w_ada.shape[0]
    n = bsz * seq
    assert d == D_MODEL and seq % 512 == 0 and SLC_TOPN <= seq // SLC_LEN <= LANES // N_KV
    alpha = (2 * depth) ** 0.25

    c_pad = jnp.zeros((SUBLANES, d), F32).at[:bsz].set(c)
    mods = _ada_call(c_pad, w_ada, b_ada)[:, :bsz]
    cst = _Consts(seq)
    x2 = x.reshape(n, d)
    for l in range(depth):
        sh1, sc1, g1, sh2, sc2, g2 = (mods[l][:, i * d:(i + 1) * d].reshape(bsz, 1, d) for i in range(6))
        x2 = _token_mixer_layer(x2, sc1, sh1, g1, w_in[l], cmp_pe[l], cmp_w1[l], cmp_w2[l], w_pool[l],
                                pool_scale[l], w_lift[l], w_o[l], ln_g[l, 0], ln_b[l, 0], cst, bsz, seq, alpha)
        x2 = _peer_layer(x2, sc2, sh2, g2, peer_wq[l], peer_keys[l], peer_u[l], peer_v[l],
                         ln_g[l, 1], ln_b[l, 1], seq, alpha)
    return x2.reshape(bsz, seq, d)
```

```python
import functools

import jax
import jax.numpy as jnp
import numpy as np
from jax import lax
from jax.experimental import pallas as pl
from jax.experimental.pallas import tpu as pltpu

F32 = jnp.float32
BF16 = jnp.bfloat16
I32 = jnp.int32
HI = lax.Precision.HIGHEST

D_MODEL = 1024
N_HEADS = 8
HEAD_DIM = 64
N_KV = 2
HPG = N_HEADS // N_KV
ROT_DIM = HEAD_DIM // 4
ROT_HALF = ROT_DIM // 2
ROPE_THETA = 500000.0
CMP_LEN = 32
CMP_STRIDE = 16
SLC_LEN = 64
SLC_TOPN = 16
WINDOW = 512
SCALE = HEAD_DIM ** -0.5
NEG = -1e30
FORCE_INIT = 1e6
FORCE_LOCAL = 2e6
POOL_GROUPS = 4
POOL_WINDOWS = (2, 4, 8, 16)
POOL_WIDTH = 512
POOL_GW = POOL_WIDTH // POOL_GROUPS
POOL_HALO = 16
ATT_WIDTH = N_HEADS * HEAD_DIM
KV_WIDTH = 3 * 2 * N_KV * HEAD_DIM
GATE_NSA = 3 * N_HEADS
MERGE_GATES = 2 * D_MODEL
PEER_HEADS = 8
N_KEYS = 128
PEER_TOPK = 16
PEER_DK = 128
PEER_SEL = PEER_HEADS * PEER_TOPK
LN_EPS = 1e-5

LANES = 128
SUBLANES = 8
VMEM_LIMIT = 56 * 1024 * 1024

NT_DIMS = (((1,), (1,)), ((), ()))


def _cparams(sem):
    return pltpu.CompilerParams(dimension_semantics=sem, vmem_limit_bytes=VMEM_LIMIT)


def _ada_kernel(c_ref, w_ref, b_ref, o_ref):
    c = c_ref[...]
    ca = c * jax.nn.sigmoid(c)
    o_ref[...] = jnp.dot(ca, w_ref[...], precision=HI, preferred_element_type=F32) + b_ref[...]


def _ada_call(c_pad, w_ada, b_ada):
    depth = w_ada.shape[0]
    nblk = w_ada.shape[2] // D_MODEL
    rows = c_pad.shape[0]
    return pl.pallas_call(
        _ada_kernel,
        grid=(depth, nblk),
        in_specs=[
            pl.BlockSpec((rows, D_MODEL), lambda l, j: (0, 0)),
            pl.BlockSpec((None, D_MODEL, D_MODEL), lambda l, j: (l, 0, j)),
            pl.BlockSpec((None, 1, D_MODEL), lambda l, j: (l, 0, j)),
        ],
        out_specs=pl.BlockSpec((None, rows, D_MODEL), lambda l, j: (l, 0, j)),
        out_shape=jax.ShapeDtypeStruct((depth, rows, nblk * D_MODEL), F32),
        compiler_params=_cparams(("arbitrary", "arbitrary")),
        name="ada_mod",
    )(c_pad, w_ada, b_ada.reshape(depth, 1, -1))


IN_COLS = ATT_WIDTH + KV_WIDTH + POOL_WIDTH + MERGE_GATES + LANES


def _rope_lanes(z, rc, rs1, rs2):
    return z * rc + pltpu.roll(z, ROT_HALF, 1) * rs1 + pltpu.roll(z, LANES - ROT_HALF, 1) * rs2


def _inproj_kernel(x_ref, sc_ref, sh_ref, w_ref, rc_ref, rs1_ref, rs2_ref,
                   q_ref, kv_ref, p_ref, mrg_ref, gn_ref):
    h = x_ref[...] * (1.0 + sc_ref[...]) + sh_ref[...]
    a = jnp.dot(h.astype(BF16), w_ref[...], preferred_element_type=F32)
    rc, rs1, rs2 = rc_ref[...], rs1_ref[...], rs2_ref[...]
    for j in range(ATT_WIDTH // LANES):
        q_ref[:, j * LANES:(j + 1) * LANES] = _rope_lanes(a[:, j * LANES:(j + 1) * LANES], rc, rs1, rs2)
    for br in range(3):
        c0 = ATT_WIDTH + br * 2 * LANES
        k = a[:, c0:c0 + LANES]
        if br > 0:
            k = _rope_lanes(k, rc, rs1, rs2)
        kv_ref[:, br * 2 * LANES:br * 2 * LANES + LANES] = k
        kv_ref[:, br * 2 * LANES + LANES:(br + 1) * 2 * LANES] = a[:, c0 + LANES:c0 + 2 * LANES]
    c1 = ATT_WIDTH + KV_WIDTH
    p_ref[...] = a[:, c1:c1 + POOL_WIDTH]
    mrg_ref[...] = a[:, c1 + POOL_WIDTH:c1 + POOL_WIDTH + MERGE_GATES]
    gn_ref[...] = a[:, c1 + POOL_WIDTH + MERGE_GATES:]


def _inproj_call(x2, sc, sh, w, rc, rs1, rs2, seq, tm):
    n = x2.shape[0]
    tpb = seq // tm
    tok = lambda i: (i, 0)
    bat = lambda i: (i // tpb, 0, 0)
    pos = lambda i: (i % tpb, 0)
    full = lambda i: (0, 0)
    return pl.pallas_call(
        _inproj_kernel,
        grid=(n // tm,),
        in_specs=[
            pl.BlockSpec((tm, D_MODEL), tok),
            pl.BlockSpec((None, 1, D_MODEL), bat),
            pl.BlockSpec((None, 1, D_MODEL), bat),
            pl.BlockSpec((D_MODEL, IN_COLS), full),
            pl.BlockSpec((tm, LANES), pos),
            pl.BlockSpec((tm, LANES), pos),
            pl.BlockSpec((tm, LANES), pos),
        ],
        out_specs=[
            pl.BlockSpec((tm, ATT_WIDTH), tok),
            pl.BlockSpec((tm, KV_WIDTH), tok),
            pl.BlockSpec((tm, POOL_WIDTH), tok),
            pl.BlockSpec((tm, MERGE_GATES), tok),
            pl.BlockSpec((tm, LANES), tok),
        ],
        out_shape=[
            jax.ShapeDtypeStruct((n, ATT_WIDTH), F32),
            jax.ShapeDtypeStruct((n, KV_WIDTH), F32),
            jax.ShapeDtypeStruct((n, POOL_WIDTH), F32),
            jax.ShapeDtypeStruct((n, MERGE_GATES), F32),
            jax.ShapeDtypeStruct((n, LANES), F32),
        ],
        compiler_params=_cparams(("arbitrary",)),
        name="in_proj",
    )(x2, sc, sh, w, rc, rs1, rs2)


def _compress_kernel(z_ref, pet_ref, peb_ref, w1t_ref, w1b_ref, w2_ref, rc_ref, rs1_ref, rs2_ref, o_ref):
    z = z_ref[...]
    rows = z.shape[0]
    top = jnp.dot(z + pet_ref[...], w1t_ref[...], precision=HI, preferred_element_type=F32)
    bot = jnp.dot(z + peb_ref[...], w1b_ref[...], precision=HI, preferred_element_type=F32)
    pre = top + pltpu.roll(bot, rows - 1, 0)
    y = jnp.dot(jax.nn.gelu(pre), w2_ref[...], precision=HI, preferred_element_type=F32)
    o_ref[...] = _rope_lanes(y, rc_ref[...], rs1_ref[...], rs2_ref[...])


def _compress_call(z, pet, peb, w1t, w1b, w2, rc, rs1, rs2):
    b, _, rows, width = z.shape
    kvsel = lambda i, j: (j, 0, 0)
    return pl.pallas_call(
        _compress_kernel,
        grid=(b, 2),
        in_specs=[
            pl.BlockSpec((None, None, rows, width), lambda i, j: (i, j, 0, 0)),
            pl.BlockSpec((None, 1, width), kvsel),
            pl.BlockSpec((None, 1, width), kvsel),
            pl.BlockSpec((None, width, LANES), kvsel),
            pl.BlockSpec((None, width, LANES), kvsel),
            pl.BlockSpec((None, LANES, LANES), kvsel),
            pl.BlockSpec((None, rows, LANES), kvsel),
            pl.BlockSpec((None, rows, LANES), kvsel),
            pl.BlockSpec((None, rows, LANES), kvsel),
        ],
        out_specs=pl.BlockSpec((None, None, rows, LANES), lambda i, j: (i, j, 0, 0)),
        out_shape=jax.ShapeDtypeStruct((b, 2, rows, LANES), F32),
        compiler_params=_cparams(("arbitrary", "arbitrary")),
        name="compress",
    )(z, pet, peb, w1t, w1b, w2, rc, rs1, rs2)


def _cmpsel_kernel(q_ref, kc_ref, vc_ref, c2s_ref, o_ref, sel_ref, *, tq):
    t0 = pl.program_id(1) * tq
    kc = kc_ref[...]
    vc = vc_ref[...]
    rows = kc.shape[0]
    trow = t0 + lax.broadcasted_iota(I32, (tq, rows), 0)
    cend = lax.broadcasted_iota(I32, (tq, rows), 1) * CMP_STRIDE + (CMP_LEN - 1)
    vis = cend <= trow
    anyv = (trow[:, :1] >= CMP_LEN - 1).astype(F32)
    imp = jnp.zeros((tq, LANES), F32)
    for g in range(N_KV):
        kg = kc[:, g * HEAD_DIM:(g + 1) * HEAD_DIM]
        vg = vc[:, g * HEAD_DIM:(g + 1) * HEAD_DIM].astype(BF16)
        psum = jnp.zeros((tq, rows), F32)
        for h in range(HPG):
            hd = g * HPG + h
            qh = q_ref[:, hd * HEAD_DIM:(hd + 1) * HEAD_DIM]
            s = lax.dot_general(qh, kg, NT_DIMS, precision=HI, preferred_element_type=F32) * SCALE
            s = jnp.where(vis, s, NEG)
            e = jnp.exp(s - jnp.max(s, axis=-1, keepdims=True))
            p = e / jnp.sum(e, axis=-1, keepdims=True) * anyv
            o_ref[:, hd * HEAD_DIM:(hd + 1) * HEAD_DIM] = jnp.dot(
                p.astype(BF16), vg, preferred_element_type=F32)
            psum = psum + p
        imp = imp + jnp.dot(psum, c2s_ref[g], precision=HI, preferred_element_type=F32)
    lane = lax.broadcasted_iota(I32, (tq, LANES), 1)
    blk = lane & (SLC_LEN - 1)
    cur = lax.shift_right_logical(t0 + lax.broadcasted_iota(I32, (tq, LANES), 0), 6)
    score = jnp.where(blk <= cur, imp, NEG)
    score = jnp.where(blk == 0, FORCE_INIT, score)
    score = jnp.where(blk == cur, FORCE_LOCAL, score)
    sc_t = score.T
    nblk = LANES // N_KV
    jrow = lax.broadcasted_iota(I32, (nblk, tq), 0)
    sel_parts = []
    for g in range(N_KV):
        sc = sc_t[g * nblk:(g + 1) * nblk]
        cnt = jnp.zeros((nblk, tq), I32)
        for k in range(nblk):
            rk = sc[k:k + 1, :]
            ge = (rk >= sc).astype(I32)
            gt = (rk > sc).astype(I32)
            cnt = cnt + jnp.where(jrow > k, ge, gt)
        sel_parts.append((cnt < SLC_TOPN).astype(F32))
    sel_ref[...] = jnp.concatenate(sel_parts, axis=0).T


def _cmpsel_call(q, kvc, c2s, seq, tq):
    n = q.shape[0]
    b = n // seq
    nq = seq // tq
    rows = kvc.shape[2]
    tok = lambda i, j: (i * nq + j, 0)
    return pl.pallas_call(
        functools.partial(_cmpsel_kernel, tq=tq),
        grid=(b, nq),
        in_specs=[
            pl.BlockSpec((tq, ATT_WIDTH), tok),
            pl.BlockSpec((None, None, rows, LANES), lambda i, j: (i, 0, 0, 0)),
            pl.BlockSpec((None, None, rows, LANES), lambda i, j: (i, 1, 0, 0)),
            pl.BlockSpec((N_KV, rows, LANES), lambda i, j: (0, 0, 0)),
        ],
        out_specs=[pl.BlockSpec((tq, ATT_WIDTH), tok), pl.BlockSpec((tq, LANES), tok)],
        out_shape=[jax.ShapeDtypeStruct((n, ATT_WIDTH), F32), jax.ShapeDtypeStruct((n, LANES), F32)],
        compiler_params=_cparams(("arbitrary", "arbitrary")),
        name="cmp_select",
    )(q, kvc, kvc, c2s)


def _slc_kernel(q_ref, k_ref, v_ref, sel_ref, o_ref, qs_sc, m_sc, l_sc, acc_sc, *, tq, tk):
    qi = pl.program_id(1)
    kt = pl.program_id(2)
    nk = pl.num_programs(2)

    @pl.when(kt == 0)
    def _init():
        for hd in range(N_HEADS):
            g, h = divmod(hd, HPG)
            qs_sc[g, h * tq:(h + 1) * tq, :] = (q_ref[:, hd * HEAD_DIM:(hd + 1) * HEAD_DIM] * SCALE).astype(BF16)
        m_sc[...] = jnp.full(m_sc.shape, NEG, F32)
        l_sc[...] = jnp.zeros(l_sc.shape, F32)
        acc_sc[...] = jnp.zeros(acc_sc.shape, F32)

    @pl.when(kt * tk <= qi * tq + (tq - 1))
    def _step():
        t = qi * tq + lax.broadcasted_iota(I32, (tq, tk), 0)
        kp = kt * tk + lax.broadcasted_iota(I32, (tq, tk), 1)
        causal = kp <= t
        nblk = LANES // N_KV
        jb = lax.broadcasted_iota(I32, (nblk, tk), 0)
        kb = lax.shift_right_logical(kt * tk + lax.broadcasted_iota(I32, (nblk, tk), 1), 6)
        expand = (jb == kb).astype(BF16)
        for g in range(N_KV):
            selg = sel_ref[:, g * nblk:(g + 1) * nblk].astype(BF16)
            member = jnp.dot(selg, expand, preferred_element_type=F32)
            bias = jnp.where(jnp.logical_and(causal, member > 0.5), 0.0, NEG)
            bias = jnp.concatenate([bias] * HPG, axis=0)
            kg = k_ref[:, g * HEAD_DIM:(g + 1) * HEAD_DIM].astype(BF16)
            vg = v_ref[:, g * HEAD_DIM:(g + 1) * HEAD_DIM].astype(BF16)
            s = lax.dot_general(qs_sc[g], kg, NT_DIMS, preferred_element_type=F32) + bias
            chunks = [s[:, c * LANES:(c + 1) * LANES] for c in range(tk // LANES)]
            mc = chunks[0]
            for x in chunks[1:]:
                mc = jnp.maximum(mc, x)
            m_old = m_sc[g]
            m_new = jnp.maximum(m_old, jnp.max(mc, axis=-1, keepdims=True))
            alpha = jnp.exp(m_old - m_new)
            ps = [jnp.exp(x - m_new) for x in chunks]
            lsum = ps[0]
            for x in ps[1:]:
                lsum = lsum + x
            l_sc[g] = alpha * l_sc[g] + lsum
            p = jnp.concatenate(ps, axis=1).astype(BF16)
            acc_sc[g] = alpha[:, :HEAD_DIM] * acc_sc[g] + jnp.dot(p, vg, preferred_element_type=F32)
            m_sc[g] = m_new

    @pl.when(kt == nk - 1)
    def _fin():
        for hd in range(N_HEADS):
            g, h = divmod(hd, HPG)
            l = jnp.sum(l_sc[g, h * tq:(h + 1) * tq, :], axis=-1, keepdims=True)
            o_ref[:, hd * HEAD_DIM:(hd + 1) * HEAD_DIM] = acc_sc[g, h * tq:(h + 1) * tq, :] / l


def _slc_call(q, kv, sel, seq, tq, tk):
    n = q.shape[0]
    b = n // seq
    nq = seq // tq
    nk = seq // tk
    tok = lambda i, j, k: (i * nq + j, 0)

    def key_map(col):
        def f(i, j, k):
            last = (j * tq + tq - 1) // tk
            return (i * nk + jnp.minimum(k, last), col)
        return f

    return pl.pallas_call(
        functools.partial(_slc_kernel, tq=tq, tk=tk),
        grid=(b, nq, nk),
        in_specs=[
            pl.BlockSpec((tq, ATT_WIDTH), tok),
            pl.BlockSpec((tk, LANES), key_map(2)),
            pl.BlockSpec((tk, LANES), key_map(3)),
            pl.BlockSpec((tq, LANES), tok),
        ],
        out_specs=pl.BlockSpec((tq, ATT_WIDTH), tok),
        out_shape=jax.ShapeDtypeStruct((n, ATT_WIDTH), F32),
        scratch_shapes=[
            pltpu.VMEM((N_KV, HPG * tq, HEAD_DIM), BF16),
            pltpu.VMEM((N_KV, HPG * tq, LANES), F32),
            pltpu.VMEM((N_KV, HPG * tq, LANES), F32),
            pltpu.VMEM((N_KV, HPG * tq, HEAD_DIM), F32),
        ],
        compiler_params=_cparams(("arbitrary", "arbitrary", "arbitrary")),
        name="slc_attn",
    )(q, kv, kv, sel)


def _win_kernel(q_ref, *refs, tq, nkb):
    k_refs = refs[:nkb]
    v_refs = refs[nkb:2 * nkb]
    o_ref = refs[2 * nkb]
    qi = pl.program_id(1)
    t = qi * tq + lax.broadcasted_iota(I32, (tq, tq), 0)
    col = lax.broadcasted_iota(I32, (tq, tq), 1)
    oks = []
    for j in range(nkb):
        kp = (qi - (nkb - 1) + j) * tq + col
        diff = t - kp
        oks.append(jnp.logical_and(jnp.logical_and(diff >= 0, diff < WINDOW), kp >= 0))
    for g in range(N_KV):
        kgs = [k_refs[j][:, g * HEAD_DIM:(g + 1) * HEAD_DIM].astype(BF16) for j in range(nkb)]
        vgs = [v_refs[j][:, g * HEAD_DIM:(g + 1) * HEAD_DIM].astype(BF16) for j in range(nkb)]
        for h in range(HPG):
            hd = g * HPG + h
            qh = (q_ref[:, hd * HEAD_DIM:(hd + 1) * HEAD_DIM] * SCALE).astype(BF16)
            ss = [jnp.where(oks[j], lax.dot_general(qh, kgs[j], NT_DIMS, preferred_element_type=F32), NEG)
                  for j in range(nkb)]
            m = jnp.max(ss[0], axis=-1, keepdims=True)
            for j in range(1, nkb):
                m = jnp.maximum(m, jnp.max(ss[j], axis=-1, keepdims=True))
            ps = [jnp.exp(s - m) for s in ss]
            l = jnp.sum(ps[0], axis=-1, keepdims=True)
            o = jnp.dot(ps[0].astype(BF16), vgs[0], preferred_element_type=F32)
            for j in range(1, nkb):
                l = l + jnp.sum(ps[j], axis=-1, keepdims=True)
                o = o + jnp.dot(ps[j].astype(BF16), vgs[j], preferred_element_type=F32)
            o_ref[:, hd * HEAD_DIM:(hd + 1) * HEAD_DIM] = o / l


def _win_call(q, kv, seq, tq):
    n = q.shape[0]
    b = n // seq
    nq = seq // tq
    nkb = WINDOW // tq + 1
    tok = lambda i, j: (i * nq + j, 0)

    def key_map(col, back):
        return lambda i, j: (i * nq + jnp.maximum(j - back, 0), col)

    k_specs = [pl.BlockSpec((tq, LANES), key_map(4, nkb - 1 - jj)) for jj in range(nkb)]
    v_specs = [pl.BlockSpec((tq, LANES), key_map(5, nkb - 1 - jj)) for jj in range(nkb)]
    return pl.pallas_call(
        functools.partial(_win_kernel, tq=tq, nkb=nkb),
        grid=(b, nq),
        in_specs=[pl.BlockSpec((tq, ATT_WIDTH), tok)] + k_specs + v_specs,
        out_specs=pl.BlockSpec((tq, ATT_WIDTH), tok),
        out_shape=jax.ShapeDtypeStruct((n, ATT_WIDTH), F32),
        compiler_params=_cparams(("arbitrary", "arbitrary")),
        name="win_attn",
    )(q, *([kv] * (2 * nkb)))


def _pool_kernel(p_ref, prev_ref, w_ref, sc_ref, o_ref, *, ts):
    i = pl.program_id(1)
    x = p_ref[...]
    prev = prev_ref[...] * (i > 0).astype(F32)
    xe = jnp.concatenate([prev, x], axis=0)
    t1 = (i * ts + 1 + lax.broadcasted_iota(I32, (ts, POOL_GW), 0)).astype(F32)
    for g, w in enumerate(POOL_WINDOWS):
        a = xe[:, g * POOL_GW:(g + 1) * POOL_GW]
        off = POOL_HALO
        span = 1
        while span < w:
            a = a[span:] + a[:-span]
            off -= span
            span *= 2
        sums = a[off:off + ts]
        cnt = jnp.minimum(t1, float(w))
        pooled = sums / cnt - x[:, g * POOL_GW:(g + 1) * POOL_GW]
        y = jnp.dot(pooled.astype(BF16), w_ref[g], preferred_element_type=F32)
        o_ref[:, g * POOL_GW:(g + 1) * POOL_GW] = y * sc_ref[:, g * POOL_GW:(g + 1) * POOL_GW]


def _pool_call(p_in, w_pool, pool_scale, seq, ts):
    n = p_in.shape[0]
    b = n // seq
    nt = seq // ts
    hpt = ts // POOL_HALO
    tok = lambda i, j: (i * nt + j, 0)
    return pl.pallas_call(
        functools.partial(_pool_kernel, ts=ts),
        grid=(b, nt),
        in_specs=[
            pl.BlockSpec((ts, POOL_WIDTH), tok),
            pl.BlockSpec((POOL_HALO, POOL_WIDTH), lambda i, j: (i * nt * hpt + jnp.maximum(j * hpt - 1, 0), 0)),
            pl.BlockSpec((POOL_GROUPS, POOL_GW, POOL_GW), lambda i, j: (0, 0, 0)),
            pl.BlockSpec((1, POOL_WIDTH), lambda i, j: (0, 0)),
        ],
        out_specs=pl.BlockSpec((ts, POOL_WIDTH), tok),
        out_shape=jax.ShapeDtypeStruct((n, POOL_WIDTH), F32),
        compiler_params=_cparams(("arbitrary", "arbitrary")),
        name="pool_mix",
    )(p_in, p_in, w_pool, pool_scale)


def _layer_norm(z, g, b):
    mu = jnp.mean(z, axis=-1, keepdims=True)
    zc = z - mu
    var = jnp.mean(zc * zc, axis=-1, keepdims=True)
    return zc * lax.rsqrt(var + LN_EPS) * g + b


def _merge_kernel(oc_ref, os_ref, ow_ref, gn_ref, op_ref, gm_ref, x_ref, g1_ref, lng_ref, lnb_ref,
                  wl_ref, wo_ref, eb_ref, o_ref, *, alpha):
    gate = jax.nn.sigmoid(gn_ref[...])
    branches = (oc_ref, os_ref, ow_ref)
    oatt = None
    for br in range(3):
        gx = jnp.dot(gate, eb_ref[br], precision=HI, preferred_element_type=F32)
        term = gx * branches[br][...]
        oatt = term if oatt is None else oatt + term
    la = jnp.dot(oatt.astype(BF16), wl_ref[0], preferred_element_type=F32)
    lb = jnp.dot(op_ref[...].astype(BF16), wl_ref[1], preferred_element_type=F32)
    gm = jax.nn.sigmoid(gm_ref[...])
    merged = gm[:, :D_MODEL] * la + gm[:, D_MODEL:] * lb
    y = jnp.dot(merged.astype(BF16), wo_ref[...], preferred_element_type=F32)
    z = alpha * x_ref[...] + g1_ref[...] * y
    o_ref[...] = _layer_norm(z, lng_ref[...], lnb_ref[...])


def _merge_call(oc, osl, ow, gn, op, gm, x2, g1, lng, lnb, wl, wo, eb, seq, tm, alpha):
    n = x2.shape[0]
    tpb = seq // tm
    tok = lambda i: (i, 0)
    bat = lambda i: (i // tpb, 0, 0)
    return pl.pallas_call(
        functools.partial(_merge_kernel, alpha=alpha),
        grid=(n // tm,),
        in_specs=[
            pl.BlockSpec((tm, ATT_WIDTH), tok), pl.BlockSpec((tm, ATT_WIDTH), tok),
            pl.BlockSpec((tm, ATT_WIDTH), tok), pl.BlockSpec((tm, LANES), tok),
            pl.BlockSpec((tm, POOL_WIDTH), tok), pl.BlockSpec((tm, MERGE_GATES), tok),
            pl.BlockSpec((tm, D_MODEL), tok),
            pl.BlockSpec((None, 1, D_MODEL), bat),
            pl.BlockSpec((1, D_MODEL), lambda i: (0, 0)), pl.BlockSpec((1, D_MODEL), lambda i: (0, 0)),
            pl.BlockSpec((2, ATT_WIDTH, D_MODEL), lambda i: (0, 0, 0)),
            pl.BlockSpec((D_MODEL, D_MODEL), lambda i: (0, 0)),
            pl.BlockSpec((3, LANES, ATT_WIDTH), lambda i: (0, 0, 0)),
        ],
        out_specs=pl.BlockSpec((tm, D_MODEL), tok),
        out_shape=jax.ShapeDtypeStruct((n, D_MODEL), F32),
        compiler_params=_cparams(("arbitrary",)),
        name="merge_out",
    )(oc, osl, ow, gn, op, gm, x2, g1, lng, lnb, wl, wo, eb)


def _extract_top(cur, ids, n):
    rows = cur.shape[0]
    rio = lax.broadcasted_iota(I32, cur.shape, 0)
    vals, outs = [], []
    for _ in range(n):
        m = jnp.max(cur, axis=0, keepdims=True)
        pos = jnp.min(jnp.where(cur == m, rio, rows), axis=0, keepdims=True)
        hit = rio == pos
        vals.append(m)
        outs.append(pos if ids is None else jnp.max(jnp.where(hit, ids, -1), axis=0, keepdims=True))
        cur = jnp.where(hit, -jnp.inf, cur)
    return jnp.concatenate(vals, axis=0), jnp.concatenate(outs, axis=0)


def _route_kernel(x_ref, sc_ref, sh_ref, wq_ref, keys_ref, h_ref, e_ref, g_ref,
                  st_sc, ts_sc, ti_sc, eo_sc, go_sc):
    h = x_ref[...] * (1.0 + sc_ref[...]) + sh_ref[...]
    h_ref[...] = h
    qp = jnp.dot(h.astype(BF16), wq_ref[...], preferred_element_type=F32).astype(BF16)
    half = PEER_DK // 2
    for hp in range(2 * PEER_HEADS):
        st_sc[hp] = lax.dot_general(keys_ref[hp], qp[:, hp * half:(hp + 1) * half], NT_DIMS,
                                    preferred_element_type=F32)

    def half_body(hp, carry):
        vals, ids = _extract_top(st_sc[hp], None, PEER_TOPK)
        ts_sc[hp] = vals
        ti_sc[hp] = ids
        return carry

    lax.fori_loop(0, 2 * PEER_HEADS, half_body, 0)

    def head_body(hh, carry):
        s1, s2 = ts_sc[2 * hh], ts_sc[2 * hh + 1]
        i1, i2 = ti_sc[2 * hh], ti_sc[2 * hh + 1]
        brow = lax.broadcasted_iota(I32, (SUBLANES, s1.shape[1]), 0)
        cands = [s1[0:1, :] + s2]
        cidxs = [i1[0:1, :] * N_KEYS + i2]
        for a in range(1, SUBLANES):
            ok = brow < PEER_TOPK // (a + 1)
            cands.append(jnp.where(ok, s1[a:a + 1, :] + s2[:SUBLANES], -jnp.inf))
            cidxs.append(i1[a:a + 1, :] * N_KEYS + i2[:SUBLANES])
        cands.append(s1[SUBLANES:] + s2[0:1, :])
        cidxs.append(i1[SUBLANES:] * N_KEYS + i2[0:1, :])
        sv, ei = _extract_top(jnp.concatenate(cands, axis=0), jnp.concatenate(cidxs, axis=0), PEER_TOPK)
        ex = jnp.exp(sv - sv[0:1, :])
        go_sc[hh] = ex / jnp.sum(ex, axis=0, keepdims=True)
        eo_sc[hh] = ei.astype(F32)
        return carry

    lax.fori_loop(0, PEER_HEADS, head_body, 0)
    e_all = jnp.concatenate([eo_sc[hh] for hh in range(PEER_HEADS)], axis=0)
    g_all = jnp.concatenate([go_sc[hh] for hh in range(PEER_HEADS)], axis=0)
    e_ref[...] = e_all.T.astype(I32) * (D_MODEL // 2 // LANES)
    g_ref[...] = g_all.T


def _route_call(x2, sc, sh, wq, keys, seq, tt):
    n = x2.shape[0]
    tpb = seq // tt
    tok = lambda i: (i, 0)
    bat = lambda i: (i // tpb, 0, 0)
    nhp = 2 * PEER_HEADS
    return pl.pallas_call(
        _route_kernel,
        grid=(n // tt,),
        in_specs=[
            pl.BlockSpec((tt, D_MODEL), tok),
            pl.BlockSpec((None, 1, D_MODEL), bat),
            pl.BlockSpec((None, 1, D_MODEL), bat),
            pl.BlockSpec((D_MODEL, PEER_HEADS * PEER_DK), lambda i: (0, 0)),
            pl.BlockSpec((nhp, N_KEYS, PEER_DK // 2), lambda i: (0, 0, 0)),
        ],
        out_specs=[pl.BlockSpec((tt, D_MODEL), tok), pl.BlockSpec((tt, PEER_SEL), tok),
                   pl.BlockSpec((tt, PEER_SEL), tok)],
        out_shape=[jax.ShapeDtypeStruct((n, D_MODEL), F32), jax.ShapeDtypeStruct((n, PEER_SEL), I32),
                   jax.ShapeDtypeStruct((n, PEER_SEL), F32)],
        scratch_shapes=[
            pltpu.VMEM((nhp, N_KEYS, tt), F32),
            pltpu.VMEM((nhp, PEER_TOPK, tt), F32),
            pltpu.VMEM((nhp, PEER_TOPK, tt), I32),
            pltpu.VMEM((PEER_HEADS, PEER_TOPK, tt), F32),
            pltpu.VMEM((PEER_HEADS, PEER_TOPK, tt), F32),
        ],
        compiler_params=_cparams(("arbitrary",)),
        name="peer_route",
    )(x2, sc, sh, wq, keys)


HALF_ROWS = SUBLANES // 2
HI_MASK = -65536
PAIR_TILES = PEER_SEL // 2
PAIR_ROWS = PAIR_TILES * SUBLANES


def _load_two_experts(tab_ref, ra, rb):
    wa = tab_ref[pl.ds(pl.multiple_of(ra, HALF_ROWS), HALF_ROWS), :]
    wb = tab_ref[pl.ds(pl.multiple_of(rb, HALF_ROWS), HALF_ROWS), :]
    w2 = jnp.concatenate([wa, wb], axis=0)
    return lax.bitcast_convert_type(w2 << 16, F32), lax.bitcast_convert_type(w2 & HI_MASK, F32)


def _fold_pairs(vs):
    row = lax.broadcasted_iota(I32, (SUBLANES, LANES), 0)
    shift = HALF_ROWS // 2
    while len(vs) > 1:
        low = (row & shift) == 0
        vs = [jnp.where(low, a + pltpu.roll(a, SUBLANES - shift, 0), b + pltpu.roll(b, shift, 0))
              for a, b in zip(vs[0::2], vs[1::2])]
        shift //= 2
    return vs[0]


def _fold_order():
    idx = [[2 * i if r < HALF_ROWS else 2 * i + 1 for r in range(SUBLANES)] for i in range(HALF_ROWS)]
    shift = HALF_ROWS // 2
    while len(idx) > 1:
        idx = [[a[r] if (r & shift) == 0 else b[r] for r in range(SUBLANES)]
               for a, b in zip(idx[0::2], idx[1::2])]
        shift //= 2
    return idx[0]


def _peer_u_kernel(e_sm, tab_ref, hv_ref, gate_ref, coef_ref, *, tt):
    row = lax.broadcasted_iota(I32, (SUBLANES, LANES), 0)
    low = row < HALF_ROWS
    eye = (lax.broadcasted_iota(I32, (PEER_SEL, LANES), 0) ==
           lax.broadcasted_iota(I32, (PEER_SEL, LANES), 1))
    order = _fold_order()

    def finish(t, part):
        col = jnp.sum(part, axis=-1, keepdims=True)
        a_row = jnp.sum(jnp.where(eye, col, 0.0), axis=0, keepdims=True)
        coef_ref[t] = gate_ref[t] * jax.nn.gelu(a_row)

    def token(t, part_prev):
        finish(jnp.maximum(t - 1, 0), part_prev)
        hv = hv_ref[t]
        hsw = pltpu.roll(hv, HALF_ROWS, 0)
        h_lo = jnp.where(low, hv, hsw)
        h_hi = jnp.where(low, hsw, hv)
        folded = []
        for j in range(PEER_SEL // SUBLANES):
            prods = []
            for i in range(HALF_ROWS):
                ka = j * SUBLANES + order.index(2 * i)
                kb = j * SUBLANES + order.index(2 * i + 1)
                lo, hi = _load_two_experts(tab_ref, e_sm[t, ka], e_sm[t, kb])
                prods.append(lo * h_lo + hi * h_hi)
            folded.append(_fold_pairs(prods))
        return jnp.concatenate(folded, axis=0)

    last = lax.fori_loop(0, tt, token, jnp.zeros((PEER_SEL, LANES), F32))
    finish(tt - 1, last)


def _peer_v_kernel(e_sm, coef_ref, tab_ref, y_ref, cv_sc, *, tt, nacc):
    row = lax.broadcasted_iota(I32, (SUBLANES, LANES), 0)
    low = row < HALF_ROWS
    rr = lax.broadcasted_iota(I32, (PAIR_ROWS, LANES), 0)
    kk = lax.broadcasted_iota(I32, (PAIR_ROWS, LANES), 1)
    onehot = (kk == 2 * (rr >> 3) + ((rr >> 2) & 1)).astype(F32)
    ones = jnp.ones((LANES, LANES), BF16)

    def expand(t, slot):
        lhs = (onehot * coef_ref[t]).astype(BF16)
        cv_sc[slot] = jnp.dot(lhs, ones, preferred_element_type=F32)

    def process(t, slot):
        acc_lo = [jnp.zeros((SUBLANES, LANES), F32) for _ in range(nacc)]
        acc_hi = [jnp.zeros((SUBLANES, LANES), F32) for _ in range(nacc)]
        for j in range(PAIR_TILES):
            lo, hi = _load_two_experts(tab_ref, e_sm[t, 2 * j], e_sm[t, 2 * j + 1])
            cv = cv_sc[slot, j * SUBLANES:(j + 1) * SUBLANES, :]
            acc_lo[j % nacc] = acc_lo[j % nacc] + cv * lo
            acc_hi[j % nacc] = acc_hi[j % nacc] + cv * hi
        a_lo, a_hi = acc_lo[0], acc_hi[0]
        for i in range(1, nacc):
            a_lo = a_lo + acc_lo[i]
            a_hi = a_hi + acc_hi[i]
        a_lo = a_lo + pltpu.roll(a_lo, HALF_ROWS, 0)
        a_hi = a_hi + pltpu.roll(a_hi, HALF_ROWS, 0)
        y_ref[t] = jnp.where(low, a_lo, a_hi)

    expand(0, 0)

    def two_tokens(i, carry):
        t = 2 * i
        expand(t + 1, 1)
        process(t, 0)
        expand(jnp.minimum(t + 2, tt - 1), 0)
        process(t + 1, 1)
        return carry

    lax.fori_loop(0, tt // 2, two_tokens, 0)


def _resident_table_spec(tab):
    return pl.BlockSpec(tab.shape, lambda i: (0, 0), pipeline_mode=pl.Buffered(1))


def _peer_u_call(erow, tab, hv, gate3, tt):
    n = erow.shape[0]
    return pl.pallas_call(
        functools.partial(_peer_u_kernel, tt=tt),
        grid=(n // tt,),
        in_specs=[
            pl.BlockSpec((tt, PEER_SEL), lambda i: (i, 0), memory_space=pltpu.SMEM),
            _resident_table_spec(tab),
            pl.BlockSpec((tt, SUBLANES, LANES), lambda i: (i, 0, 0)),
            pl.BlockSpec((tt, 1, PEER_SEL), lambda i: (i, 0, 0)),
        ],
        out_specs=pl.BlockSpec((tt, 1, PEER_SEL), lambda i: (i, 0, 0)),
        out_shape=jax.ShapeDtypeStruct((n, 1, PEER_SEL), F32),
        compiler_params=_cparams(("arbitrary",)),
        name="peer_u",
    )(erow, tab, hv, gate3)


def _peer_v_call(erow, coef3, tab, tt):
    n = erow.shape[0]
    assert tt % 2 == 0
    return pl.pallas_call(
        functools.partial(_peer_v_kernel, tt=tt, nacc=4),
        grid=(n // tt,),
        in_specs=[
            pl.BlockSpec((tt, PEER_SEL), lambda i: (i, 0), memory_space=pltpu.SMEM),
            pl.BlockSpec((tt, 1, PEER_SEL), lambda i: (i, 0, 0)),
            _resident_table_spec(tab),
        ],
        out_specs=pl.BlockSpec((tt, SUBLANES, LANES), lambda i: (i, 0, 0)),
        out_shape=jax.ShapeDtypeStruct((n, SUBLANES, LANES), F32),
        scratch_shapes=[pltpu.VMEM((2, PAIR_ROWS, LANES), F32)],
        compiler_params=_cparams(("arbitrary",)),
        name="peer_v",
    )(erow, coef3, tab)


def _resln_kernel(x_ref, y_ref, g_ref, lng_ref, lnb_ref, o_ref, *, alpha):
    z = alpha * x_ref[...] + g_ref[...] * y_ref[...]
    o_ref[...] = _layer_norm(z, lng_ref[...], lnb_ref[...])


def _resln_call(x2, y2, g2, lng, lnb, seq, tm, alpha):
    n = x2.shape[0]
    tpb = seq // tm
    tok = lambda i: (i, 0)
    return pl.pallas_call(
        functools.partial(_resln_kernel, alpha=alpha),
        grid=(n // tm,),
        in_specs=[
            pl.BlockSpec((tm, D_MODEL), tok), pl.BlockSpec((tm, D_MODEL), tok),
            pl.BlockSpec((None, 1, D_MODEL), lambda i: (i // tpb, 0, 0)),
            pl.BlockSpec((1, D_MODEL), lambda i: (0, 0)), pl.BlockSpec((1, D_MODEL), lambda i: (0, 0)),
        ],
        out_specs=pl.BlockSpec((tm, D_MODEL), tok),
        out_shape=jax.ShapeDtypeStruct((n, D_MODEL), F32),
        compiler_params=_cparams(("arbitrary",)),
        name="res_ln",
    )(x2, y2, g2, lng, lnb)


def _rope_lane_tables(pos):
    inv = ROPE_THETA ** (-jnp.arange(0, ROT_DIM, 2, dtype=F32) / ROT_DIM)
    ang = pos.astype(F32)[:, None] * inv[None, :]
    cos, sin = jnp.cos(ang), jnp.sin(ang)
    lane = np.arange(LANES) % HEAD_DIM
    fidx = lane % ROT_HALF
    first = jnp.asarray(lane < ROT_HALF)
    second = jnp.asarray((lane >= ROT_HALF) & (lane < ROT_DIM))
    rot = jnp.asarray(lane < ROT_DIM)
    cl, sl = cos[:, fidx], sin[:, fidx]
    rc = jnp.where(rot, cl, 1.0)
    rs1 = jnp.where(second, sl, 0.0)
    rs2 = jnp.where(first, -sl, 0.0)
    return rc, rs1, rs2


def _pack_table(tab):
    e, d = tab.shape
    bits = lax.bitcast_convert_type(tab.astype(BF16), jnp.uint16).astype(jnp.uint32)
    word = bits[:, :d // 2] | (bits[:, d // 2:] << 16)
    return lax.bitcast_convert_type(word, I32).reshape(e * (d // 2) // LANES, LANES)


def _cmp_to_slc_wide(rows, n_slc):
    st = np.arange(rows) * CMP_STRIDE
    js = np.arange(n_slc) * SLC_LEN
    ov = np.minimum(st[:, None] + CMP_LEN, js[None, :] + SLC_LEN) - np.maximum(st[:, None], js[None, :])
    c2s = np.maximum(ov, 0).astype(np.float32) / CMP_STRIDE
    wide = np.zeros((N_KV, rows, LANES), np.float32)
    nblk = LANES // N_KV
    for g in range(N_KV):
        wide[g, :, g * nblk:g * nblk + n_slc] = c2s
    return jnp.asarray(wide)


def _gate_expanders():
    eb = np.zeros((3, LANES, ATT_WIDTH), np.float32)
    for hd in range(N_HEADS):
        for br in range(3):
            eb[br, hd * 3 + br, hd * HEAD_DIM:(hd + 1) * HEAD_DIM] = 1.0
    return jnp.asarray(eb)


class _Consts:
    def __init__(self, seq):
        self.rows = seq // CMP_STRIDE
        self.rope = _rope_lane_tables(jnp.arange(seq))
        cpos = jnp.arange(self.rows) * CMP_STRIDE + CMP_LEN - 1
        crope = _rope_lane_tables(cpos)
        ident = (jnp.ones_like(crope[0]), jnp.zeros_like(crope[0]), jnp.zeros_like(crope[0]))
        self.crope = tuple(jnp.stack([a, b]) for a, b in zip(crope, ident))
        self.c2s = _cmp_to_slc_wide(self.rows, seq // SLC_LEN)
        self.eb = _gate_expanders()


def _token_mixer_layer(x2, sc1, sh1, g1, w_in, cmp_pe, cmp_w1, cmp_w2, w_pool, pool_scale, w_lift, w_o,
                       lng, lnb, cst, bsz, seq, alpha):
    d = D_MODEL
    rows = cst.rows
    s1 = ATT_WIDTH + KV_WIDTH
    s2 = s1 + GATE_NSA
    s3 = s2 + POOL_WIDTH
    w_gate = jnp.pad(w_in[:, s1:s2], ((0, 0), (0, LANES - GATE_NSA)))
    w_all = jnp.concatenate([w_in[:, :s1], w_in[:, s2:s3], w_in[:, s3:], w_gate], axis=1).astype(BF16)
    q, kv, p_in, g_mrg, g_nsa = _inproj_call(x2, sc1, sh1, w_all, *cst.rope, seq, 256)

    eye_g = jnp.eye(N_KV, dtype=F32)
    zc = kv[:, :2 * LANES].reshape(bsz, rows, CMP_STRIDE, 2, LANES)
    zc = jnp.transpose(zc, (0, 3, 1, 2, 4)).reshape(bsz, 2, rows, CMP_STRIDE * LANES)
    w1x = jnp.einsum('klde,gh->klgdhe', cmp_w1.reshape(2, CMP_LEN, HEAD_DIM, HEAD_DIM), eye_g)
    w1x = w1x.reshape(2, CMP_LEN * LANES, LANES)
    half = CMP_STRIDE * LANES
    pex = jnp.broadcast_to(cmp_pe[:, :, None, :], (2, CMP_LEN, N_KV, HEAD_DIM)).reshape(2, 1, CMP_LEN * LANES)
    w2x = jnp.einsum('kef,gh->kgehf', cmp_w2, eye_g).reshape(2, LANES, LANES)
    kvc = _compress_call(zc, pex[:, :, :half], pex[:, :, half:], w1x[:, :half], w1x[:, half:], w2x,
                         *cst.crope)

    o_cmp, sel = _cmpsel_call(q, kvc, cst.c2s, seq, 256)
    o_slc = _slc_call(q, kv, sel, seq, 256, 512)
    o_win = _win_call(q, kv, seq, 256)
    o_pool = _pool_call(p_in, w_pool.astype(BF16), pool_scale.reshape(1, -1), seq, 512)
    return _merge_call(o_cmp, o_slc, o_win, g_nsa, o_pool, g_mrg, x2, g1,
                       lng.reshape(1, d), lnb.reshape(1, d),
                       w_lift.astype(BF16), w_o.astype(BF16), cst.eb, seq, 256, alpha)


def _peer_layer(x2, sc2, sh2, g2, peer_wq, peer_keys, peer_u, peer_v, lng, lnb, seq, alpha,
                tt_route=256, tt_gather=64, tm=512):
    n, d = x2.shape
    keys = peer_keys.reshape(2 * PEER_HEADS, N_KEYS, PEER_DK // 2)
    h2, erow, gate = _route_call(x2, sc2, sh2, peer_wq.astype(BF16), keys.astype(BF16), seq, tt_route)
    hv = h2.reshape(n, SUBLANES, LANES)
    coef = _peer_u_call(erow, _pack_table(peer_u), hv, gate.reshape(n, 1, PEER_SEL), tt_gather)
    y = _peer_v_call(erow, coef, _pack_table(peer_v), tt_gather)
    return _resln_call(x2, y.reshape(n, d), g2, lng.reshape(1, d), lnb.reshape(1, d), seq, tm, alpha)


def kernel(x, c, w_ada, b_ada, w_in, cmp_pe, cmp_w1, cmp_w2, w_pool, pool_scale, w_lift, w_o,
           ln_g, ln_b, peer_wq, peer_keys, peer_u, peer_v):
    bsz, seq, d = x.shape
    depth = w_ada.shape[0]
    n = bsz * seq
    assert d == D_MODEL and seq % 512 == 0 and SLC_TOPN <= seq // SLC_LEN <= LANES // N_KV
    alpha = (2 * depth) ** 0.25

    c_pad = jnp.zeros((SUBLANES, d), F32).at[:bsz].set(c)
    mods = _ada_call(c_pad, w_ada, b_ada)[:, :bsz]
    cst = _Consts(seq)
    x2 = x.reshape(n, d)
    for l in range(depth):
        sh1, sc1, g1, sh2, sc2, g2 = (mods[l][:, i * d:(i + 1) * d].reshape(bsz, 1, d) for i in range(6))
        x2 = _token_mixer_layer(x2, sc1, sh1, g1, w_in[l], cmp_pe[l], cmp_w1[l], cmp_w2[l], w_pool[l],
                                pool_scale[l], w_lift[l], w_o[l], ln_g[l, 0], ln_b[l, 0], cst, bsz, seq, alpha)
        x2 = _peer_layer(x2, sc2, sh2, g2, peer_wq[l], peer_keys[l], peer_u[l], peer_v[l],
                         ln_g[l, 1], ln_b[l, 1], seq, alpha)
    return x2.reshape(bsz, seq, d)
```

```python
import dataclasses
import functools

import jax
import jax.numpy as jnp
import numpy as np
from jax import lax
from jax.experimental import pallas as pl
from jax.experimental.pallas import tpu as pltpu
from jax.experimental.pallas import tpu_sc as plsc

F32 = jnp.float32
BF16 = jnp.bfloat16
I32 = jnp.int32
HI = lax.Precision.HIGHEST

D_MODEL = 1024
N_HEADS = 8
HEAD_DIM = 64
N_KV = 2
HPG = N_HEADS // N_KV
ROT_DIM = HEAD_DIM // 4
ROT_HALF = ROT_DIM // 2
ROPE_THETA = 500000.0
CMP_LEN = 32
CMP_STRIDE = 16
SLC_LEN = 64
SLC_TOPN = 16
WINDOW = 512
SCALE = HEAD_DIM ** -0.5
NEG = -1e30
FORCE_INIT = 1e6
FORCE_LOCAL = 2e6
POOL_GROUPS = 4
POOL_WINDOWS = (2, 4, 8, 16)
POOL_WIDTH = 512
POOL_GW = POOL_WIDTH // POOL_GROUPS
POOL_HALO = 16
ATT_WIDTH = N_HEADS * HEAD_DIM
KV_WIDTH = 3 * 2 * N_KV * HEAD_DIM
GATE_NSA = 3 * N_HEADS
MERGE_GATES = 2 * D_MODEL
PEER_HEADS = 8
N_KEYS = 128
PEER_TOPK = 16
PEER_DK = 128
PEER_SEL = PEER_HEADS * PEER_TOPK
LN_EPS = 1e-5

LANES = 128
SUBLANES = 8
VMEM_LIMIT = 56 * 1024 * 1024

NT_DIMS = (((1,), (1,)), ((), ()))


def _cparams(sem):
    return pltpu.CompilerParams(dimension_semantics=sem, vmem_limit_bytes=VMEM_LIMIT)


def _ada_kernel(c_ref, w_ref, b_ref, o_ref):
    c = c_ref[...]
    ca = c * jax.nn.sigmoid(c)
    o_ref[...] = jnp.dot(ca, w_ref[...], precision=HI, preferred_element_type=F32) + b_ref[...]


def _ada_call(c_pad, w_ada, b_ada):
    depth = w_ada.shape[0]
    nblk = w_ada.shape[2] // D_MODEL
    rows = c_pad.shape[0]
    return pl.pallas_call(
        _ada_kernel,
        grid=(depth, nblk),
        in_specs=[
            pl.BlockSpec((rows, D_MODEL), lambda l, j: (0, 0)),
            pl.BlockSpec((None, D_MODEL, D_MODEL), lambda l, j: (l, 0, j)),
            pl.BlockSpec((None, 1, D_MODEL), lambda l, j: (l, 0, j)),
        ],
        out_specs=pl.BlockSpec((None, rows, D_MODEL), lambda l, j: (l, 0, j)),
        out_shape=jax.ShapeDtypeStruct((depth, rows, nblk * D_MODEL), F32),
        compiler_params=_cparams(("arbitrary", "arbitrary")),
        name="ada_mod",
    )(c_pad, w_ada, b_ada.reshape(depth, 1, -1))


IN_COLS = ATT_WIDTH + KV_WIDTH + POOL_WIDTH + MERGE_GATES + LANES


def _rope_lanes(z, rc, rs1, rs2):
    return z * rc + pltpu.roll(z, ROT_HALF, 1) * rs1 + pltpu.roll(z, LANES - ROT_HALF, 1) * rs2


def _inproj_kernel(x_ref, sc_ref, sh_ref, w_ref, rc_ref, rs1_ref, rs2_ref,
                   q_ref, kv_ref, p_ref, mrg_ref, gn_ref):
    h = x_ref[...] * (1.0 + sc_ref[...]) + sh_ref[...]
    a = jnp.dot(h.astype(BF16), w_ref[...], preferred_element_type=F32)
    rc, rs1, rs2 = rc_ref[...], rs1_ref[...], rs2_ref[...]
    for j in range(ATT_WIDTH // LANES):
        q_ref[:, j * LANES:(j + 1) * LANES] = _rope_lanes(a[:, j * LANES:(j + 1) * LANES], rc, rs1, rs2)
    for br in range(3):
        c0 = ATT_WIDTH + br * 2 * LANES
        k = a[:, c0:c0 + LANES]
        if br > 0:
            k = _rope_lanes(k, rc, rs1, rs2)
        kv_ref[:, br * 2 * LANES:br * 2 * LANES + LANES] = k
        kv_ref[:, br * 2 * LANES + LANES:(br + 1) * 2 * LANES] = a[:, c0 + LANES:c0 + 2 * LANES]
    c1 = ATT_WIDTH + KV_WIDTH
    p_ref[...] = a[:, c1:c1 + POOL_WIDTH]
    mrg_ref[...] = a[:, c1 + POOL_WIDTH:c1 + POOL_WIDTH + MERGE_GATES]
    gn_ref[...] = a[:, c1 + POOL_WIDTH + MERGE_GATES:]


def _inproj_call(x2, sc, sh, w, rc, rs1, rs2, seq, tm):
    n = x2.shape[0]
    tpb = seq // tm
    tok = lambda i: (i, 0)
    bat = lambda i: (i // tpb, 0, 0)
    pos = lambda i: (i % tpb, 0)
    full = lambda i: (0, 0)
    return pl.pallas_call(
        _inproj_kernel,
        grid=(n // tm,),
        in_specs=[
            pl.BlockSpec((tm, D_MODEL), tok),
            pl.BlockSpec((None, 1, D_MODEL), bat),
            pl.BlockSpec((None, 1, D_MODEL), bat),
            pl.BlockSpec((D_MODEL, IN_COLS), full),
            pl.BlockSpec((tm, LANES), pos),
            pl.BlockSpec((tm, LANES), pos),
            pl.BlockSpec((tm, LANES), pos),
        ],
        out_specs=[
            pl.BlockSpec((tm, ATT_WIDTH), tok),
            pl.BlockSpec((tm, KV_WIDTH), tok),
            pl.BlockSpec((tm, POOL_WIDTH), tok),
            pl.BlockSpec((tm, MERGE_GATES), tok),
            pl.BlockSpec((tm, LANES), tok),
        ],
        out_shape=[
            jax.ShapeDtypeStruct((n, ATT_WIDTH), F32),
            jax.ShapeDtypeStruct((n, KV_WIDTH), F32),
            jax.ShapeDtypeStruct((n, POOL_WIDTH), F32),
            jax.ShapeDtypeStruct((n, MERGE_GATES), F32),
            jax.ShapeDtypeStruct((n, LANES), F32),
        ],
        compiler_params=_cparams(("arbitrary",)),
        name="in_proj",
    )(x2, sc, sh, w, rc, rs1, rs2)


def _compress_kernel(z_ref, pet_ref, peb_ref, w1t_ref, w1b_ref, w2_ref, rc_ref, rs1_ref, rs2_ref, o_ref):
    z = z_ref[...]
    rows = z.shape[0]
    top = jnp.dot(z + pet_ref[...], w1t_ref[...], precision=HI, preferred_element_type=F32)
    bot = jnp.dot(z + peb_ref[...], w1b_ref[...], precision=HI, preferred_element_type=F32)
    pre = top + pltpu.roll(bot, rows - 1, 0)
    y = jnp.dot(jax.nn.gelu(pre), w2_ref[...], precision=HI, preferred_element_type=F32)
    o_ref[...] = _rope_lanes(y, rc_ref[...], rs1_ref[...], rs2_ref[...])


def _compress_call(z, pet, peb, w1t, w1b, w2, rc, rs1, rs2):
    b, _, rows, width = z.shape
    kvsel = lambda i, j: (j, 0, 0)
    return pl.pallas_call(
        _compress_kernel,
        grid=(b, 2),
        in_specs=[
            pl.BlockSpec((None, None, rows, width), lambda i, j: (i, j, 0, 0)),
            pl.BlockSpec((None, 1, width), kvsel),
            pl.BlockSpec((None, 1, width), kvsel),
            pl.BlockSpec((None, width, LANES), kvsel),
            pl.BlockSpec((None, width, LANES), kvsel),
            pl.BlockSpec((None, LANES, LANES), kvsel),
            pl.BlockSpec((None, rows, LANES), kvsel),
            pl.BlockSpec((None, rows, LANES), kvsel),
            pl.BlockSpec((None, rows, LANES), kvsel),
        ],
        out_specs=pl.BlockSpec((None, None, rows, LANES), lambda i, j: (i, j, 0, 0)),
        out_shape=jax.ShapeDtypeStruct((b, 2, rows, LANES), F32),
        compiler_params=_cparams(("arbitrary", "arbitrary")),
        name="compress",
    )(z, pet, peb, w1t, w1b, w2, rc, rs1, rs2)


def _cmpsel_kernel(q_ref, kc_ref, vc_ref, c2s_ref, o_ref, sel_ref, *, tq):
    t0 = pl.program_id(1) * tq
    kc = kc_ref[...]
    vc = vc_ref[...]
    rows = kc.shape[0]
    trow = t0 + lax.broadcasted_iota(I32, (tq, rows), 0)
    cend = lax.broadcasted_iota(I32, (tq, rows), 1) * CMP_STRIDE + (CMP_LEN - 1)
    vis = cend <= trow
    anyv = (trow[:, :1] >= CMP_LEN - 1).astype(F32)
    imp = jnp.zeros((tq, LANES), F32)
    for g in range(N_KV):
        kg = kc[:, g * HEAD_DIM:(g + 1) * HEAD_DIM]
        vg = vc[:, g * HEAD_DIM:(g + 1) * HEAD_DIM].astype(BF16)
        psum = jnp.zeros((tq, rows), F32)
        for h in range(HPG):
            hd = g * HPG + h
            qh = q_ref[:, hd * HEAD_DIM:(hd + 1) * HEAD_DIM]
            s = lax.dot_general(qh, kg, NT_DIMS, precision=HI, preferred_element_type=F32) * SCALE
            s = jnp.where(vis, s, NEG)
            e = jnp.exp(s - jnp.max(s, axis=-1, keepdims=True))
            p = e / jnp.sum(e, axis=-1, keepdims=True) * anyv
            o_ref[:, hd * HEAD_DIM:(hd + 1) * HEAD_DIM] = jnp.dot(
                p.astype(BF16), vg, preferred_element_type=F32)
            psum = psum + p
        imp = imp + jnp.dot(psum, c2s_ref[g], precision=HI, preferred_element_type=F32)
    lane = lax.broadcasted_iota(I32, (tq, LANES), 1)
    blk = lane & (SLC_LEN - 1)
    cur = lax.shift_right_logical(t0 + lax.broadcasted_iota(I32, (tq, LANES), 0), 6)
    score = jnp.where(blk <= cur, imp, NEG)
    score = jnp.where(blk == 0, FORCE_INIT, score)
    score = jnp.where(blk == cur, FORCE_LOCAL, score)
    sc_t = score.T
    nblk = LANES // N_KV
    jrow = lax.broadcasted_iota(I32, (nblk, tq), 0)
    sel_parts = []
    for g in range(N_KV):
        sc = sc_t[g * nblk:(g + 1) * nblk]
        cnt = jnp.zeros((nblk, tq), I32)
        for k in range(nblk):
            rk = sc[k:k + 1, :]
            ge = (rk >= sc).astype(I32)
            gt = (rk > sc).astype(I32)
            cnt = cnt + jnp.where(jrow > k, ge, gt)
        sel_parts.append((cnt < SLC_TOPN).astype(F32))
    sel_ref[...] = jnp.concatenate(sel_parts, axis=0).T


def _cmpsel_call(q, kvc, c2s, seq, tq):
    n = q.shape[0]
    b = n // seq
    nq = seq // tq
    rows = kvc.shape[2]
    tok = lambda i, j: (i * nq + j, 0)
    return pl.pallas_call(
        functools.partial(_cmpsel_kernel, tq=tq),
        grid=(b, nq),
        in_specs=[
            pl.BlockSpec((tq, ATT_WIDTH), tok),
            pl.BlockSpec((None, None, rows, LANES), lambda i, j: (i, 0, 0, 0)),
            pl.BlockSpec((None, None, rows, LANES), lambda i, j: (i, 1, 0, 0)),
            pl.BlockSpec((N_KV, rows, LANES), lambda i, j: (0, 0, 0)),
        ],
        out_specs=[pl.BlockSpec((tq, ATT_WIDTH), tok), pl.BlockSpec((tq, LANES), tok)],
        out_shape=[jax.ShapeDtypeStruct((n, ATT_WIDTH), F32), jax.ShapeDtypeStruct((n, LANES), F32)],
        compiler_params=_cparams(("arbitrary", "arbitrary")),
        name="cmp_select",
    )(q, kvc, kvc, c2s)


def _slc_kernel(q_ref, k_ref, v_ref, sel_ref, o_ref, qs_sc, m_sc, l_sc, acc_sc, *, tq, tk):
    qi = pl.program_id(1)
    kt = pl.program_id(2)
    nk = pl.num_programs(2)

    @pl.when(kt == 0)
    def _init():
        for hd in range(N_HEADS):
            g, h = divmod(hd, HPG)
            qs_sc[g, h * tq:(h + 1) * tq, :] = (q_ref[:, hd * HEAD_DIM:(hd + 1) * HEAD_DIM] * SCALE).astype(BF16)
        m_sc[...] = jnp.full(m_sc.shape, NEG, F32)
        l_sc[...] = jnp.zeros(l_sc.shape, F32)
        acc_sc[...] = jnp.zeros(acc_sc.shape, F32)

    @pl.when(kt * tk <= qi * tq + (tq - 1))
    def _step():
        t = qi * tq + lax.broadcasted_iota(I32, (tq, tk), 0)
        kp = kt * tk + lax.broadcasted_iota(I32, (tq, tk), 1)
        causal = kp <= t
        nblk = LANES // N_KV
        jb = lax.broadcasted_iota(I32, (nblk, tk), 0)
        kb = lax.shift_right_logical(kt * tk + lax.broadcasted_iota(I32, (nblk, tk), 1), 6)
        expand = (jb == kb).astype(BF16)
        for g in range(N_KV):
            selg = sel_ref[:, g * nblk:(g + 1) * nblk].astype(BF16)
            member = jnp.dot(selg, expand, preferred_element_type=F32)
            bias = jnp.where(jnp.logical_and(causal, member > 0.5), 0.0, NEG)
            bias = jnp.concatenate([bias] * HPG, axis=0)
            kg = k_ref[:, g * HEAD_DIM:(g + 1) * HEAD_DIM].astype(BF16)
            vg = v_ref[:, g * HEAD_DIM:(g + 1) * HEAD_DIM].astype(BF16)
            s = lax.dot_general(qs_sc[g], kg, NT_DIMS, preferred_element_type=F32) + bias
            chunks = [s[:, c * LANES:(c + 1) * LANES] for c in range(tk // LANES)]
            mc = chunks[0]
            for x in chunks[1:]:
                mc = jnp.maximum(mc, x)
            m_old = m_sc[g]
            m_new = jnp.maximum(m_old, jnp.max(mc, axis=-1, keepdims=True))
            alpha = jnp.exp(m_old - m_new)
            ps = [jnp.exp(x - m_new) for x in chunks]
            lsum = ps[0]
            for x in ps[1:]:
                lsum = lsum + x
            l_sc[g] = alpha * l_sc[g] + lsum
            p = jnp.concatenate(ps, axis=1).astype(BF16)
            acc_sc[g] = alpha[:, :HEAD_DIM] * acc_sc[g] + jnp.dot(p, vg, preferred_element_type=F32)
            m_sc[g] = m_new

    @pl.when(kt == nk - 1)
    def _fin():
        for hd in range(N_HEADS):
            g, h = divmod(hd, HPG)
            l = jnp.sum(l_sc[g, h * tq:(h + 1) * tq, :], axis=-1, keepdims=True)
            o_ref[:, hd * HEAD_DIM:(hd + 1) * HEAD_DIM] = acc_sc[g, h * tq:(h + 1) * tq, :] / l


def _slc_call(q, kv, sel, seq, tq, tk):
    n = q.shape[0]
    b = n // seq
    nq = seq // tq
    nk = seq // tk
    tok = lambda i, j, k: (i * nq + j, 0)

    def key_map(col):
        def f(i, j, k):
            last = (j * tq + tq - 1) // tk
            return (i * nk + jnp.minimum(k, last), col)
        return f

    return pl.pallas_call(
        functools.partial(_slc_kernel, tq=tq, tk=tk),
        grid=(b, nq, nk),
        in_specs=[
            pl.BlockSpec((tq, ATT_WIDTH), tok),
            pl.BlockSpec((tk, LANES), key_map(2)),
            pl.BlockSpec((tk, LANES), key_map(3)),
            pl.BlockSpec((tq, LANES), tok),
        ],
        out_specs=pl.BlockSpec((tq, ATT_WIDTH), tok),
        out_shape=jax.ShapeDtypeStruct((n, ATT_WIDTH), F32),
        scratch_shapes=[
            pltpu.VMEM((N_KV, HPG * tq, HEAD_DIM), BF16),
            pltpu.VMEM((N_KV, HPG * tq, LANES), F32),
            pltpu.VMEM((N_KV, HPG * tq, LANES), F32),
            pltpu.VMEM((N_KV, HPG * tq, HEAD_DIM), F32),
        ],
        compiler_params=_cparams(("arbitrary", "arbitrary", "arbitrary")),
        name="slc_attn",
    )(q, kv, kv, sel)


def _win_kernel(q_ref, *refs, tq, nkb):
    k_refs = refs[:nkb]
    v_refs = refs[nkb:2 * nkb]
    o_ref = refs[2 * nkb]
    qi = pl.program_id(1)
    t = qi * tq + lax.broadcasted_iota(I32, (tq, tq), 0)
    col = lax.broadcasted_iota(I32, (tq, tq), 1)
    oks = []
    for j in range(nkb):
        kp = (qi - (nkb - 1) + j) * tq + col
        diff = t - kp
        oks.append(jnp.logical_and(jnp.logical_and(diff >= 0, diff < WINDOW), kp >= 0))
    for g in range(N_KV):
        kgs = [k_refs[j][:, g * HEAD_DIM:(g + 1) * HEAD_DIM].astype(BF16) for j in range(nkb)]
        vgs = [v_refs[j][:, g * HEAD_DIM:(g + 1) * HEAD_DIM].astype(BF16) for j in range(nkb)]
        for h in range(HPG):
            hd = g * HPG + h
            qh = (q_ref[:, hd * HEAD_DIM:(hd + 1) * HEAD_DIM] * SCALE).astype(BF16)
            ss = [jnp.where(oks[j], lax.dot_general(qh, kgs[j], NT_DIMS, preferred_element_type=F32), NEG)
                  for j in range(nkb)]
            m = jnp.max(ss[0], axis=-1, keepdims=True)
            for j in range(1, nkb):
                m = jnp.maximum(m, jnp.max(ss[j], axis=-1, keepdims=True))
            ps = [jnp.exp(s - m) for s in ss]
            l = jnp.sum(ps[0], axis=-1, keepdims=True)
            o = jnp.dot(ps[0].astype(BF16), vgs[0], preferred_element_type=F32)
            for j in range(1, nkb):
                l = l + jnp.sum(ps[j], axis=-1, keepdims=True)
                o = o + jnp.dot(ps[j].astype(BF16), vgs[j], preferred_element_type=F32)
            o_ref[:, hd * HEAD_DIM:(hd + 1) * HEAD_DIM] = o / l


def _win_call(q, kv, seq, tq):
    n = q.shape[0]
    b = n // seq
    nq = seq // tq
    nkb = WINDOW // tq + 1
    tok = lambda i, j: (i * nq + j, 0)

    def key_map(col, back):
        return lambda i, j: (i * nq + jnp.maximum(j - back, 0), col)

    k_specs = [pl.BlockSpec((tq, LANES), key_map(4, nkb - 1 - jj)) for jj in range(nkb)]
    v_specs = [pl.BlockSpec((tq, LANES), key_map(5, nkb - 1 - jj)) for jj in range(nkb)]
    return pl.pallas_call(
        functools.partial(_win_kernel, tq=tq, nkb=nkb),
        grid=(b, nq),
        in_specs=[pl.BlockSpec((tq, ATT_WIDTH), tok)] + k_specs + v_specs,
        out_specs=pl.BlockSpec((tq, ATT_WIDTH), tok),
        out_shape=jax.ShapeDtypeStruct((n, ATT_WIDTH), F32),
        compiler_params=_cparams(("arbitrary", "arbitrary")),
        name="win_attn",
    )(q, *([kv] * (2 * nkb)))


def _pool_kernel(p_ref, prev_ref, w_ref, sc_ref, o_ref, *, ts):
    i = pl.program_id(1)
    x = p_ref[...]
    prev = prev_ref[...] * (i > 0).astype(F32)
    xe = jnp.concatenate([prev, x], axis=0)
    t1 = (i * ts + 1 + lax.broadcasted_iota(I32, (ts, POOL_GW), 0)).astype(F32)
    for g, w in enumerate(POOL_WINDOWS):
        a = xe[:, g * POOL_GW:(g + 1) * POOL_GW]
        off = POOL_HALO
        span = 1
        while span < w:
            a = a[span:] + a[:-span]
            off -= span
            span *= 2
        sums = a[off:off + ts]
        cnt = jnp.minimum(t1, float(w))
        pooled = sums / cnt - x[:, g * POOL_GW:(g + 1) * POOL_GW]
        y = jnp.dot(pooled.astype(BF16), w_ref[g], preferred_element_type=F32)
        o_ref[:, g * POOL_GW:(g + 1) * POOL_GW] = y * sc_ref[:, g * POOL_GW:(g + 1) * POOL_GW]


def _pool_call(p_in, w_pool, pool_scale, seq, ts):
    n = p_in.shape[0]
    b = n // seq
    nt = seq // ts
    hpt = ts // POOL_HALO
    tok = lambda i, j: (i * nt + j, 0)
    return pl.pallas_call(
        functools.partial(_pool_kernel, ts=ts),
        grid=(b, nt),
        in_specs=[
            pl.BlockSpec((ts, POOL_WIDTH), tok),
            pl.BlockSpec((POOL_HALO, POOL_WIDTH), lambda i, j: (i * nt * hpt + jnp.maximum(j * hpt - 1, 0), 0)),
            pl.BlockSpec((POOL_GROUPS, POOL_GW, POOL_GW), lambda i, j: (0, 0, 0)),
            pl.BlockSpec((1, POOL_WIDTH), lambda i, j: (0, 0)),
        ],
        out_specs=pl.BlockSpec((ts, POOL_WIDTH), tok),
        out_shape=jax.ShapeDtypeStruct((n, POOL_WIDTH), F32),
        compiler_params=_cparams(("arbitrary", "arbitrary")),
        name="pool_mix",
    )(p_in, p_in, w_pool, pool_scale)


def _layer_norm(z, g, b):
    mu = jnp.mean(z, axis=-1, keepdims=True)
    zc = z - mu
    var = jnp.mean(zc * zc, axis=-1, keepdims=True)
    return zc * lax.rsqrt(var + LN_EPS) * g + b


def _merge_kernel(oc_ref, os_ref, ow_ref, gn_ref, op_ref, gm_ref, x_ref, g1_ref, lng_ref, lnb_ref,
                  wl_ref, wo_ref, eb_ref, o_ref, *, alpha):
    gate = jax.nn.sigmoid(gn_ref[...])
    branches = (oc_ref, os_ref, ow_ref)
    oatt = None
    for br in range(3):
        gx = jnp.dot(gate, eb_ref[br], precision=HI, preferred_element_type=F32)
        term = gx * branches[br][...]
        oatt = term if oatt is None else oatt + term
    la = jnp.dot(oatt.astype(BF16), wl_ref[0], preferred_element_type=F32)
    lb = jnp.dot(op_ref[...].astype(BF16), wl_ref[1], preferred_element_type=F32)
    gm = jax.nn.sigmoid(gm_ref[...])
    merged = gm[:, :D_MODEL] * la + gm[:, D_MODEL:] * lb
    y = jnp.dot(merged.astype(BF16), wo_ref[...], preferred_element_type=F32)
    z = alpha * x_ref[...] + g1_ref[...] * y
    o_ref[...] = _layer_norm(z, lng_ref[...], lnb_ref[...])


def _merge_call(oc, osl, ow, gn, op, gm, x2, g1, lng, lnb, wl, wo, eb, seq, tm, alpha):
    n = x2.shape[0]
    tpb = seq // tm
    tok = lambda i: (i, 0)
    bat = lambda i: (i // tpb, 0, 0)
    return pl.pallas_call(
        functools.partial(_merge_kernel, alpha=alpha),
        grid=(n // tm,),
        in_specs=[
            pl.BlockSpec((tm, ATT_WIDTH), tok), pl.BlockSpec((tm, ATT_WIDTH), tok),
            pl.BlockSpec((tm, ATT_WIDTH), tok), pl.BlockSpec((tm, LANES), tok),
            pl.BlockSpec((tm, POOL_WIDTH), tok), pl.BlockSpec((tm, MERGE_GATES), tok),
            pl.BlockSpec((tm, D_MODEL), tok),
            pl.BlockSpec((None, 1, D_MODEL), bat),
            pl.BlockSpec((1, D_MODEL), lambda i: (0, 0)), pl.BlockSpec((1, D_MODEL), lambda i: (0, 0)),
            pl.BlockSpec((2, ATT_WIDTH, D_MODEL), lambda i: (0, 0, 0)),
            pl.BlockSpec((D_MODEL, D_MODEL), lambda i: (0, 0)),
            pl.BlockSpec((3, LANES, ATT_WIDTH), lambda i: (0, 0, 0)),
        ],
        out_specs=pl.BlockSpec((tm, D_MODEL), tok),
        out_shape=jax.ShapeDtypeStruct((n, D_MODEL), F32),
        compiler_params=_cparams(("arbitrary",)),
        name="merge_out",
    )(oc, osl, ow, gn, op, gm, x2, g1, lng, lnb, wl, wo, eb)


def _extract_top(cur, ids, n):
    rows = cur.shape[0]
    rio = lax.broadcasted_iota(I32, cur.shape, 0)
    vals, outs = [], []
    for _ in range(n):
        m = jnp.max(cur, axis=0, keepdims=True)
        pos = jnp.min(jnp.where(cur == m, rio, rows), axis=0, keepdims=True)
        hit = rio == pos
        vals.append(m)
        outs.append(pos if ids is None else jnp.max(jnp.where(hit, ids, -1), axis=0, keepdims=True))
        cur = jnp.where(hit, -jnp.inf, cur)
    return jnp.concatenate(vals, axis=0), jnp.concatenate(outs, axis=0)


def _route_kernel(x_ref, sc_ref, sh_ref, wq_ref, keys_ref, h_ref, e_ref, g_ref,
                  st_sc, ts_sc, ti_sc, eo_sc, go_sc):
    h = x_ref[...] * (1.0 + sc_ref[...]) + sh_ref[...]
    h_ref[...] = h
    qp = jnp.dot(h.astype(BF16), wq_ref[...], preferred_element_type=F32).astype(BF16)
    half = PEER_DK // 2
    for hp in range(2 * PEER_HEADS):
        st_sc[hp] = lax.dot_general(keys_ref[hp], qp[:, hp * half:(hp + 1) * half], NT_DIMS,
                                    preferred_element_type=F32)

    def half_body(hp, carry):
        vals, ids = _extract_top(st_sc[hp], None, PEER_TOPK)
        ts_sc[hp] = vals
        ti_sc[hp] = ids
        return carry

    lax.fori_loop(0, 2 * PEER_HEADS, half_body, 0)

    def head_body(hh, carry):
        s1, s2 = ts_sc[2 * hh], ts_sc[2 * hh + 1]
        i1, i2 = ti_sc[2 * hh], ti_sc[2 * hh + 1]
        brow = lax.broadcasted_iota(I32, (SUBLANES, s1.shape[1]), 0)
        cands = [s1[0:1, :] + s2]
        cidxs = [i1[0:1, :] * N_KEYS + i2]
        for a in range(1, SUBLANES):
            ok = brow < PEER_TOPK // (a + 1)
            cands.append(jnp.where(ok, s1[a:a + 1, :] + s2[:SUBLANES], -jnp.inf))
            cidxs.append(i1[a:a + 1, :] * N_KEYS + i2[:SUBLANES])
        cands.append(s1[SUBLANES:] + s2[0:1, :])
        cidxs.append(i1[SUBLANES:] * N_KEYS + i2[0:1, :])
        sv, ei = _extract_top(jnp.concatenate(cands, axis=0), jnp.concatenate(cidxs, axis=0), PEER_TOPK)
        ex = jnp.exp(sv - sv[0:1, :])
        go_sc[hh] = ex / jnp.sum(ex, axis=0, keepdims=True)
        eo_sc[hh] = ei.astype(F32)
        return carry

    lax.fori_loop(0, PEER_HEADS, head_body, 0)
    e_all = jnp.concatenate([eo_sc[hh] for hh in range(PEER_HEADS)], axis=0)
    g_all = jnp.concatenate([go_sc[hh] for hh in range(PEER_HEADS)], axis=0)
    e_ref[...] = e_all.T.astype(I32) * (D_MODEL // 2 // LANES)
    g_ref[...] = g_all.T


def _route_call(x2, sc, sh, wq, keys, seq, tt):
    n = x2.shape[0]
    tpb = seq // tt
    tok = lambda i: (i, 0)
    bat = lambda i: (i // tpb, 0, 0)
    nhp = 2 * PEER_HEADS
    return pl.pallas_call(
        _route_kernel,
        grid=(n // tt,),
        in_specs=[
            pl.BlockSpec((tt, D_MODEL), tok),
            pl.BlockSpec((None, 1, D_MODEL), bat),
            pl.BlockSpec((None, 1, D_MODEL), bat),
            pl.BlockSpec((D_MODEL, PEER_HEADS * PEER_DK), lambda i: (0, 0)),
            pl.BlockSpec((nhp, N_KEYS, PEER_DK // 2), lambda i: (0, 0, 0)),
        ],
        out_specs=[pl.BlockSpec((tt, D_MODEL), tok), pl.BlockSpec((tt, PEER_SEL), tok),
                   pl.BlockSpec((tt, PEER_SEL), tok)],
        out_shape=[jax.ShapeDtypeStruct((n, D_MODEL), F32), jax.ShapeDtypeStruct((n, PEER_SEL), I32),
                   jax.ShapeDtypeStruct((n, PEER_SEL), F32)],
        scratch_shapes=[
            pltpu.VMEM((nhp, N_KEYS, tt), F32),
            pltpu.VMEM((nhp, PEER_TOPK, tt), F32),
            pltpu.VMEM((nhp, PEER_TOPK, tt), I32),
            pltpu.VMEM((PEER_HEADS, PEER_TOPK, tt), F32),
            pltpu.VMEM((PEER_HEADS, PEER_TOPK, tt), F32),
        ],
        compiler_params=_cparams(("arbitrary",)),
        name="peer_route",
    )(x2, sc, sh, wq, keys)


HALF_ROWS = SUBLANES // 2
HI_MASK = -65536
PAIR_TILES = PEER_SEL // 2
PAIR_ROWS = PAIR_TILES * SUBLANES


def _load_two_experts(tab_ref, ra, rb):
    wa = tab_ref[pl.ds(pl.multiple_of(ra, HALF_ROWS), HALF_ROWS), :]
    wb = tab_ref[pl.ds(pl.multiple_of(rb, HALF_ROWS), HALF_ROWS), :]
    w2 = jnp.concatenate([wa, wb], axis=0)
    return lax.bitcast_convert_type(w2 << 16, F32), lax.bitcast_convert_type(w2 & HI_MASK, F32)


def _fold_pairs(vs):
    row = lax.broadcasted_iota(I32, (SUBLANES, LANES), 0)
    shift = HALF_ROWS // 2
    while len(vs) > 1:
        low = (row & shift) == 0
        vs = [jnp.where(low, a + pltpu.roll(a, SUBLANES - shift, 0), b + pltpu.roll(b, shift, 0))
              for a, b in zip(vs[0::2], vs[1::2])]
        shift //= 2
    return vs[0]


def _fold_order():
    idx = [[2 * i if r < HALF_ROWS else 2 * i + 1 for r in range(SUBLANES)] for i in range(HALF_ROWS)]
    shift = HALF_ROWS // 2
    while len(idx) > 1:
        idx = [[a[r] if (r & shift) == 0 else b[r] for r in range(SUBLANES)]
               for a, b in zip(idx[0::2], idx[1::2])]
        shift //= 2
    return idx[0]


def _peer_u_kernel(e_sm, tab_ref, hv_ref, gate_ref, coef_ref, *, tt):
    row = lax.broadcasted_iota(I32, (SUBLANES, LANES), 0)
    low = row < HALF_ROWS
    eye = (lax.broadcasted_iota(I32, (PEER_SEL, LANES), 0) ==
           lax.broadcasted_iota(I32, (PEER_SEL, LANES), 1))
    order = _fold_order()

    def finish(t, part):
        col = jnp.sum(part, axis=-1, keepdims=True)
        a_row = jnp.sum(jnp.where(eye, col, 0.0), axis=0, keepdims=True)
        coef_ref[t] = gate_ref[t] * jax.nn.gelu(a_row)

    def token(t, part_prev):
        finish(jnp.maximum(t - 1, 0), part_prev)
        hv = hv_ref[t]
        hsw = pltpu.roll(hv, HALF_ROWS, 0)
        h_lo = jnp.where(low, hv, hsw)
        h_hi = jnp.where(low, hsw, hv)
        folded = []
        for j in range(PEER_SEL // SUBLANES):
            prods = []
            for i in range(HALF_ROWS):
                ka = j * SUBLANES + order.index(2 * i)
                kb = j * SUBLANES + order.index(2 * i + 1)
                lo, hi = _load_two_experts(tab_ref, e_sm[t, ka], e_sm[t, kb])
                prods.append(lo * h_lo + hi * h_hi)
            folded.append(_fold_pairs(prods))
        return jnp.concatenate(folded, axis=0)

    last = lax.fori_loop(0, tt, token, jnp.zeros((PEER_SEL, LANES), F32))
    finish(tt - 1, last)


def _peer_v_kernel(e_sm, coef_ref, tab_ref, y_ref, cv_sc, *, tt, nacc):
    row = lax.broadcasted_iota(I32, (SUBLANES, LANES), 0)
    low = row < HALF_ROWS
    rr = lax.broadcasted_iota(I32, (PAIR_ROWS, LANES), 0)
    kk = lax.broadcasted_iota(I32, (PAIR_ROWS, LANES), 1)
    onehot = (kk == 2 * (rr >> 3) + ((rr >> 2) & 1)).astype(F32)
    ones = jnp.ones((LANES, LANES), BF16)

    def expand(t, slot):
        lhs = (onehot * coef_ref[t]).astype(BF16)
        cv_sc[slot] = jnp.dot(lhs, ones, preferred_element_type=F32)

    def process(t, slot):
        acc_lo = [jnp.zeros((SUBLANES, LANES), F32) for _ in range(nacc)]
        acc_hi = [jnp.zeros((SUBLANES, LANES), F32) for _ in range(nacc)]
        for j in range(PAIR_TILES):
            lo, hi = _load_two_experts(tab_ref, e_sm[t, 2 * j], e_sm[t, 2 * j + 1])
            cv = cv_sc[slot, j * SUBLANES:(j + 1) * SUBLANES, :]
            acc_lo[j % nacc] = acc_lo[j % nacc] + cv * lo
            acc_hi[j % nacc] = acc_hi[j % nacc] + cv * hi
        a_lo, a_hi = acc_lo[0], acc_hi[0]
        for i in range(1, nacc):
            a_lo = a_lo + acc_lo[i]
            a_hi = a_hi + acc_hi[i]
        a_lo = a_lo + pltpu.roll(a_lo, HALF_ROWS, 0)
        a_hi = a_hi + pltpu.roll(a_hi, HALF_ROWS, 0)
        y_ref[t] = jnp.where(low, a_lo, a_hi)

    expand(0, 0)

    def two_tokens(i, carry):
        t = 2 * i
        expand(t + 1, 1)
        process(t, 0)
        expand(jnp.minimum(t + 2, tt - 1), 0)
        process(t + 1, 1)
        return carry

    lax.fori_loop(0, tt // 2, two_tokens, 0)


def _resident_table_spec(tab):
    return pl.BlockSpec(tab.shape, lambda i: (0, 0), pipeline_mode=pl.Buffered(1))


def _peer_u_call(erow, tab, hv, gate3, tt):
    n = erow.shape[0]
    return pl.pallas_call(
        functools.partial(_peer_u_kernel, tt=tt),
        grid=(n // tt,),
        in_specs=[
            pl.BlockSpec((tt, PEER_SEL), lambda i: (i, 0), memory_space=pltpu.SMEM),
            _resident_table_spec(tab),
            pl.BlockSpec((tt, SUBLANES, LANES), lambda i: (i, 0, 0)),
            pl.BlockSpec((tt, 1, PEER_SEL), lambda i: (i, 0, 0)),
        ],
        out_specs=pl.BlockSpec((tt, 1, PEER_SEL), lambda i: (i, 0, 0)),
        out_shape=jax.ShapeDtypeStruct((n, 1, PEER_SEL), F32),
        compiler_params=_cparams(("arbitrary",)),
        name="peer_u",
    )(erow, tab, hv, gate3)


def _peer_v_call(erow, coef3, tab, tt):
    n = erow.shape[0]
    assert tt % 2 == 0
    return pl.pallas_call(
        functools.partial(_peer_v_kernel, tt=tt, nacc=4),
        grid=(n // tt,),
        in_specs=[
            pl.BlockSpec((tt, PEER_SEL), lambda i: (i, 0), memory_space=pltpu.SMEM),
            pl.BlockSpec((tt, 1, PEER_SEL), lambda i: (i, 0, 0)),
            _resident_table_spec(tab),
        ],
        out_specs=pl.BlockSpec((tt, SUBLANES, LANES), lambda i: (i, 0, 0)),
        out_shape=jax.ShapeDtypeStruct((n, SUBLANES, LANES), F32),
        scratch_shapes=[pltpu.VMEM((2, PAIR_ROWS, LANES), F32)],
        compiler_params=_cparams(("arbitrary",)),
        name="peer_v",
    )(erow, coef3, tab)


def _resln_kernel(x_ref, y_ref, g_ref, lng_ref, lnb_ref, o_ref, *, alpha):
    z = alpha * x_ref[...] + g_ref[...] * y_ref[...]
    o_ref[...] = _layer_norm(z, lng_ref[...], lnb_ref[...])


def _resln_call(x2, y2, g2, lng, lnb, seq, tm, alpha):
    n = x2.shape[0]
    tpb = seq // tm
    tok = lambda i: (i, 0)
    return pl.pallas_call(
        functools.partial(_resln_kernel, alpha=alpha),
        grid=(n // tm,),
        in_specs=[
            pl.BlockSpec((tm, D_MODEL), tok), pl.BlockSpec((tm, D_MODEL), tok),
            pl.BlockSpec((None, 1, D_MODEL), lambda i: (i // tpb, 0, 0)),
            pl.BlockSpec((1, D_MODEL), lambda i: (0, 0)), pl.BlockSpec((1, D_MODEL), lambda i: (0, 0)),
        ],
        out_specs=pl.BlockSpec((tm, D_MODEL), tok),
        out_shape=jax.ShapeDtypeStruct((n, D_MODEL), F32),
        compiler_params=_cparams(("arbitrary",)),
        name="res_ln",
    )(x2, y2, g2, lng, lnb)


def _rope_lane_tables(pos):
    inv = ROPE_THETA ** (-jnp.arange(0, ROT_DIM, 2, dtype=F32) / ROT_DIM)
    ang = pos.astype(F32)[:, None] * inv[None, :]
    cos, sin = jnp.cos(ang), jnp.sin(ang)
    lane = np.arange(LANES) % HEAD_DIM
    fidx = lane % ROT_HALF
    first = jnp.asarray(lane < ROT_HALF)
    second = jnp.asarray((lane >= ROT_HALF) & (lane < ROT_DIM))
    rot = jnp.asarray(lane < ROT_DIM)
    cl, sl = cos[:, fidx], sin[:, fidx]
    rc = jnp.where(rot, cl, 1.0)
    rs1 = jnp.where(second, sl, 0.0)
    rs2 = jnp.where(first, -sl, 0.0)
    return rc, rs1, rs2


def _pack_table(tab):
    e, d = tab.shape
    bits = lax.bitcast_convert_type(tab.astype(BF16), jnp.uint16).astype(jnp.uint32)
    word = bits[:, :d // 2] | (bits[:, d // 2:] << 16)
    return lax.bitcast_convert_type(word, I32).reshape(e * (d // 2) // LANES, LANES)


def _cmp_to_slc_wide(rows, n_slc):
    st = np.arange(rows) * CMP_STRIDE
    js = np.arange(n_slc) * SLC_LEN
    ov = np.minimum(st[:, None] + CMP_LEN, js[None, :] + SLC_LEN) - np.maximum(st[:, None], js[None, :])
    c2s = np.maximum(ov, 0).astype(np.float32) / CMP_STRIDE
    wide = np.zeros((N_KV, rows, LANES), np.float32)
    nblk = LANES // N_KV
    for g in range(N_KV):
        wide[g, :, g * nblk:g * nblk + n_slc] = c2s
    return jnp.asarray(wide)


def _gate_expanders():
    eb = np.zeros((3, LANES, ATT_WIDTH), np.float32)
    for hd in range(N_HEADS):
        for br in range(3):
            eb[br, hd * 3 + br, hd * HEAD_DIM:(hd + 1) * HEAD_DIM] = 1.0
    return jnp.asarray(eb)


class _Consts:
    def __init__(self, seq):
        self.rows = seq // CMP_STRIDE
        self.rope = _rope_lane_tables(jnp.arange(seq))
        cpos = jnp.arange(self.rows) * CMP_STRIDE + CMP_LEN - 1
        crope = _rope_lane_tables(cpos)
        ident = (jnp.ones_like(crope[0]), jnp.zeros_like(crope[0]), jnp.zeros_like(crope[0]))
        self.crope = tuple(jnp.stack([a, b]) for a, b in zip(crope, ident))
        self.c2s = _cmp_to_slc_wide(self.rows, seq // SLC_LEN)
        self.eb = _gate_expanders()


def _token_mixer_layer(x2, sc1, sh1, g1, w_in, cmp_pe, cmp_w1, cmp_w2, w_pool, pool_scale, w_lift, w_o,
                       lng, lnb, cst, bsz, seq, alpha):
    d = D_MODEL
    rows = cst.rows
    s1 = ATT_WIDTH + KV_WIDTH
    s2 = s1 + GATE_NSA
    s3 = s2 + POOL_WIDTH
    w_gate = jnp.pad(w_in[:, s1:s2], ((0, 0), (0, LANES - GATE_NSA)))
    w_all = jnp.concatenate([w_in[:, :s1], w_in[:, s2:s3], w_in[:, s3:], w_gate], axis=1).astype(BF16)
    q, kv, p_in, g_mrg, g_nsa = _inproj_call(x2, sc1, sh1, w_all, *cst.rope, seq, 256)

    eye_g = jnp.eye(N_KV, dtype=F32)
    zc = kv[:, :2 * LANES].reshape(bsz, rows, CMP_STRIDE, 2, LANES)
    zc = jnp.transpose(zc, (0, 3, 1, 2, 4)).reshape(bsz, 2, rows, CMP_STRIDE * LANES)
    w1x = jnp.einsum('klde,gh->klgdhe', cmp_w1.reshape(2, CMP_LEN, HEAD_DIM, HEAD_DIM), eye_g)
    w1x = w1x.reshape(2, CMP_LEN * LANES, LANES)
    half = CMP_STRIDE * LANES
    pex = jnp.broadcast_to(cmp_pe[:, :, None, :], (2, CMP_LEN, N_KV, HEAD_DIM)).reshape(2, 1, CMP_LEN * LANES)
    w2x = jnp.einsum('kef,gh->kgehf', cmp_w2, eye_g).reshape(2, LANES, LANES)
    kvc = _compress_call(zc, pex[:, :, :half], pex[:, :, half:], w1x[:, :half], w1x[:, half:], w2x,
                         *cst.crope)

    o_cmp, sel = _cmpsel_call(q, kvc, cst.c2s, seq, 256)
    o_slc = _slc_call(q, kv, sel, seq, 256, 512)
    o_win = _win_call(q, kv, seq, 256)
    o_pool = _pool_call(p_in, w_pool.astype(BF16), pool_scale.reshape(1, -1), seq, 512)
    return _merge_call(o_cmp, o_slc, o_win, g_nsa, o_pool, g_mrg, x2, g1,
                       lng.reshape(1, d), lnb.reshape(1, d),
                       w_lift.astype(BF16), w_o.astype(BF16), cst.eb, seq, 256, alpha)


SC_CORES = 2
SC_SUBCORES = 16
SC_LANES = 16
SC_WORKERS = SC_CORES * SC_SUBCORES
SC_ROWS = 32


def _sc_params():
    cp = pltpu.CompilerParams()
    if "needs_layout_passes" in pltpu.CompilerParams.__dataclass_fields__:
        cp = dataclasses.replace(cp, needs_layout_passes=False)
    return cp


def _sc_v_call(eidx, coef, vtab):
    m = eidx.shape[0]
    d = vtab.shape[1]
    assert m % SC_WORKERS == 0 and PEER_SEL % SC_ROWS == 0 and d % SC_LANES == 0
    tpw = m // SC_WORKERS
    nch = PEER_SEL // SC_ROWS
    mesh = plsc.VectorSubcoreMesh(core_axis_name="c", subcore_axis_name="s")

    @functools.partial(
        pl.kernel, mesh=mesh, out_type=jax.ShapeDtypeStruct((m, d), F32),
        scratch_types=[pltpu.VMEM((PEER_SEL,), I32), pltpu.VMEM((PEER_SEL,), F32),
                       pltpu.VMEM((2, SC_ROWS, d), F32), pltpu.VMEM((d,), F32),
                       pltpu.SemaphoreType.DMA((2,))],
        compiler_params=_sc_params(), name="sc_peer_v")
    def run(tab_hbm, idx_hbm, coef_hbm, out_hbm, idx_v, coef_v, rows_v, acc_v, sems):
        wid = lax.axis_index("s") * SC_CORES + lax.axis_index("c")
        base = wid * tpw
        zero = jnp.zeros((SC_LANES,), F32)

        @pl.loop(0, tpw)
        def _(i):
            t = base + i
            pltpu.sync_copy(idx_hbm.at[t], idx_v)
            pltpu.sync_copy(coef_hbm.at[t], coef_v)
            for j in range(d // SC_LANES):
                acc_v[pl.ds(j * SC_LANES, SC_LANES)] = zero

            def gather(c):
                return pltpu.async_copy(tab_hbm.at[idx_v.at[pl.ds(c * SC_ROWS, SC_ROWS)]],
                                        rows_v.at[c % 2], sems.at[c % 2])

            pending = {0: gather(0)}
            for c in range(nch):
                if c + 1 < nch:
                    pending[c + 1] = gather(c + 1)
                pending.pop(c).wait()
                buf = rows_v.at[c % 2]

                @pl.loop(0, SC_ROWS)
                def _(r):
                    ck = plsc.load_gather(coef_v, [jnp.full((SC_LANES,), c * SC_ROWS, I32) + r])

                    @plsc.parallel_loop(0, d, SC_LANES, unroll=8)
                    def _(off):
                        sl = pl.ds(pl.multiple_of(off, SC_LANES), SC_LANES)
                        plsc.addupdate(acc_v.at[sl], ck * buf[r, sl])
            pltpu.sync_copy(acc_v, out_hbm.at[t])

    return run(vtab, eidx, coef)


def _peer_layer(x2, sc2, sh2, g2, peer_wq, peer_keys, peer_u, peer_v, lng, lnb, seq, alpha,
                tt_route=256, tt_gather=64, tm=512, sc_tokens=0):
    n, d = x2.shape
    keys = peer_keys.reshape(2 * PEER_HEADS, N_KEYS, PEER_DK // 2)
    h2, erow, gate = _route_call(x2, sc2, sh2, peer_wq.astype(BF16), keys.astype(BF16), seq, tt_route)
    hv = h2.reshape(n, SUBLANES, LANES)
    coef = _peer_u_call(erow, _pack_table(peer_u), hv, gate.reshape(n, 1, PEER_SEL), tt_gather)
    n_tc = n - sc_tokens
    y = _peer_v_call(erow[:n_tc], coef[:n_tc], _pack_table(peer_v), tt_gather).reshape(n_tc, d)
    if sc_tokens:
        eidx = erow[n_tc:] // (D_MODEL // 2 // LANES)
        y_sc = _sc_v_call(eidx, coef[n_tc:].reshape(sc_tokens, PEER_SEL), peer_v)
        y = jnp.concatenate([y, y_sc], axis=0)
    return _resln_call(x2, y, g2, lng.reshape(1, d), lnb.reshape(1, d), seq, tm, alpha)


def kernel(x, c, w_ada, b_ada, w_in, cmp_pe, cmp_w1, cmp_w2, w_pool, pool_scale, w_lift, w_o,
           ln_g, ln_b, peer_wq, peer_keys, peer_u, peer_v):
    bsz, seq, d = x.shape
    depth = w_ada.shape[0]
    n = bsz * seq
    assert d == D_MODEL and seq % 512 == 0 and SLC_TOPN <= seq // SLC_LEN <= LANES // N_KV
    alpha = (2 * depth) ** 0.25

    c_pad = jnp.zeros((SUBLANES, d), F32).at[:bsz].set(c)
    mods = _ada_call(c_pad, w_ada, b_ada)[:, :bsz]
    cst = _Consts(seq)
    x2 = x.reshape(n, d)
    for l in range(depth):
        sh1, sc1, g1, sh2, sc2, g2 = (mods[l][:, i * d:(i + 1) * d].reshape(bsz, 1, d) for i in range(6))
        x2 = _token_mixer_layer(x2, sc1, sh1, g1, w_in[l], cmp_pe[l], cmp_w1[l], cmp_w2[l], w_pool[l],
                                pool_scale[l], w_lift[l], w_o[l], ln_g[l, 0], ln_b[l, 0], cst, bsz, seq, alpha)
        x2 = _peer_layer(x2, sc2, sh2, g2, peer_wq[l], peer_keys[l], peer_u[l], peer_v[l],
                         ln_g[l, 1], ln_b[l, 1], seq, alpha, sc_tokens=n // 4)
    return x2.reshape(bsz, seq, d)
```

```python
import dataclasses
import functools

import jax
import jax.numpy as jnp
import numpy as np
from jax import lax
from jax.experimental import pallas as pl
from jax.experimental.pallas import tpu as pltpu
from jax.experimental.pallas import tpu_sc as plsc

F32 = jnp.float32
BF16 = jnp.bfloat16
I32 = jnp.int32
HI = lax.Precision.HIGHEST

D_MODEL = 1024
N_HEADS = 8
HEAD_DIM = 64
N_KV = 2
HPG = N_HEADS // N_KV
ROT_DIM = HEAD_DIM // 4
ROT_HALF = ROT_DIM // 2
ROPE_THETA = 500000.0
CMP_LEN = 32
CMP_STRIDE = 16
SLC_LEN = 64
SLC_TOPN = 16
WINDOW = 512
SCALE = HEAD_DIM ** -0.5
NEG = -1e30
FORCE_INIT = 1e6
FORCE_LOCAL = 2e6
POOL_GROUPS = 4
POOL_WINDOWS = (2, 4, 8, 16)
POOL_WIDTH = 512
POOL_GW = POOL_WIDTH // POOL_GROUPS
POOL_HALO = 16
ATT_WIDTH = N_HEADS * HEAD_DIM
KV_WIDTH = 3 * 2 * N_KV * HEAD_DIM
GATE_NSA = 3 * N_HEADS
MERGE_GATES = 2 * D_MODEL
PEER_HEADS = 8
N_KEYS = 128
PEER_TOPK = 16
PEER_DK = 128
PEER_SEL = PEER_HEADS * PEER_TOPK
LN_EPS = 1e-5

LANES = 128
SUBLANES = 8
VMEM_LIMIT = 56 * 1024 * 1024

NT_DIMS = (((1,), (1,)), ((), ()))


def _cparams(sem):
    return pltpu.CompilerParams(dimension_semantics=sem, vmem_limit_bytes=VMEM_LIMIT)


def _ada_kernel(c_ref, w_ref, b_ref, o_ref):
    c = c_ref[...]
    ca = c * jax.nn.sigmoid(c)
    o_ref[...] = jnp.dot(ca, w_ref[...], precision=HI, preferred_element_type=F32) + b_ref[...]


def _ada_call(c_pad, w_ada, b_ada):
    depth = w_ada.shape[0]
    nblk = w_ada.shape[2] // D_MODEL
    rows = c_pad.shape[0]
    return pl.pallas_call(
        _ada_kernel,
        grid=(depth, nblk),
        in_specs=[
            pl.BlockSpec((rows, D_MODEL), lambda l, j: (0, 0)),
            pl.BlockSpec((None, D_MODEL, D_MODEL), lambda l, j: (l, 0, j)),
            pl.BlockSpec((None, 1, D_MODEL), lambda l, j: (l, 0, j)),
        ],
        out_specs=pl.BlockSpec((None, rows, D_MODEL), lambda l, j: (l, 0, j)),
        out_shape=jax.ShapeDtypeStruct((depth, rows, nblk * D_MODEL), F32),
        compiler_params=_cparams(("arbitrary", "arbitrary")),
        name="ada_mod",
    )(c_pad, w_ada, b_ada.reshape(depth, 1, -1))


IN_COLS = ATT_WIDTH + KV_WIDTH + POOL_WIDTH + MERGE_GATES + LANES


def _rope_lanes(z, rc, rs1, rs2):
    return z * rc + pltpu.roll(z, ROT_HALF, 1) * rs1 + pltpu.roll(z, LANES - ROT_HALF, 1) * rs2


def _inproj_kernel(x_ref, sc_ref, sh_ref, w_ref, rc_ref, rs1_ref, rs2_ref,
                   q_ref, kv_ref, p_ref, mrg_ref, gn_ref):
    h = x_ref[...] * (1.0 + sc_ref[...]) + sh_ref[...]
    a = jnp.dot(h.astype(BF16), w_ref[...], preferred_element_type=F32)
    rc, rs1, rs2 = rc_ref[...], rs1_ref[...], rs2_ref[...]
    for j in range(ATT_WIDTH // LANES):
        q_ref[:, j * LANES:(j + 1) * LANES] = _rope_lanes(a[:, j * LANES:(j + 1) * LANES], rc, rs1, rs2)
    for br in range(3):
        c0 = ATT_WIDTH + br * 2 * LANES
        k = a[:, c0:c0 + LANES]
        if br > 0:
            k = _rope_lanes(k, rc, rs1, rs2)
        kv_ref[:, br * 2 * LANES:br * 2 * LANES + LANES] = k
        kv_ref[:, br * 2 * LANES + LANES:(br + 1) * 2 * LANES] = a[:, c0 + LANES:c0 + 2 * LANES]
    c1 = ATT_WIDTH + KV_WIDTH
    p_ref[...] = a[:, c1:c1 + POOL_WIDTH]
    mrg_ref[...] = a[:, c1 + POOL_WIDTH:c1 + POOL_WIDTH + MERGE_GATES]
    gn_ref[...] = a[:, c1 + POOL_WIDTH + MERGE_GATES:]


def _inproj_call(x2, sc, sh, w, rc, rs1, rs2, seq, tm):
    n = x2.shape[0]
    tpb = seq // tm
    tok = lambda i: (i, 0)
    bat = lambda i: (i // tpb, 0, 0)
    pos = lambda i: (i % tpb, 0)
    full = lambda i: (0, 0)
    return pl.pallas_call(
        _inproj_kernel,
        grid=(n // tm,),
        in_specs=[
            pl.BlockSpec((tm, D_MODEL), tok),
            pl.BlockSpec((None, 1, D_MODEL), bat),
            pl.BlockSpec((None, 1, D_MODEL), bat),
            pl.BlockSpec((D_MODEL, IN_COLS), full),
            pl.BlockSpec((tm, LANES), pos),
            pl.BlockSpec((tm, LANES), pos),
            pl.BlockSpec((tm, LANES), pos),
        ],
        out_specs=[
            pl.BlockSpec((tm, ATT_WIDTH), tok),
            pl.BlockSpec((tm, KV_WIDTH), tok),
            pl.BlockSpec((tm, POOL_WIDTH), tok),
            pl.BlockSpec((tm, MERGE_GATES), tok),
            pl.BlockSpec((tm, LANES), tok),
        ],
        out_shape=[
            jax.ShapeDtypeStruct((n, ATT_WIDTH), F32),
            jax.ShapeDtypeStruct((n, KV_WIDTH), F32),
            jax.ShapeDtypeStruct((n, POOL_WIDTH), F32),
            jax.ShapeDtypeStruct((n, MERGE_GATES), F32),
            jax.ShapeDtypeStruct((n, LANES), F32),
        ],
        compiler_params=_cparams(("arbitrary",)),
        name="in_proj",
    )(x2, sc, sh, w, rc, rs1, rs2)


def _compress_kernel(z_ref, pet_ref, peb_ref, w1t_ref, w1b_ref, w2_ref, rc_ref, rs1_ref, rs2_ref, o_ref):
    z = z_ref[...]
    rows = z.shape[0]
    top = jnp.dot(z + pet_ref[...], w1t_ref[...], precision=HI, preferred_element_type=F32)
    bot = jnp.dot(z + peb_ref[...], w1b_ref[...], precision=HI, preferred_element_type=F32)
    pre = top + pltpu.roll(bot, rows - 1, 0)
    y = jnp.dot(jax.nn.gelu(pre), w2_ref[...], precision=HI, preferred_element_type=F32)
    o_ref[...] = _rope_lanes(y, rc_ref[...], rs1_ref[...], rs2_ref[...])


def _compress_call(z, pet, peb, w1t, w1b, w2, rc, rs1, rs2):
    b, _, rows, width = z.shape
    kvsel = lambda i, j: (j, 0, 0)
    return pl.pallas_call(
        _compress_kernel,
        grid=(b, 2),
        in_specs=[
            pl.BlockSpec((None, None, rows, width), lambda i, j: (i, j, 0, 0)),
            pl.BlockSpec((None, 1, width), kvsel),
            pl.BlockSpec((None, 1, width), kvsel),
            pl.BlockSpec((None, width, LANES), kvsel),
            pl.BlockSpec((None, width, LANES), kvsel),
            pl.BlockSpec((None, LANES, LANES), kvsel),
            pl.BlockSpec((None, rows, LANES), kvsel),
            pl.BlockSpec((None, rows, LANES), kvsel),
            pl.BlockSpec((None, rows, LANES), kvsel),
        ],
        out_specs=pl.BlockSpec((None, None, rows, LANES), lambda i, j: (i, j, 0, 0)),
        out_shape=jax.ShapeDtypeStruct((b, 2, rows, LANES), F32),
        compiler_params=_cparams(("arbitrary", "arbitrary")),
        name="compress",
    )(z, pet, peb, w1t, w1b, w2, rc, rs1, rs2)


def _cmpsel_kernel(q_ref, kc_ref, vc_ref, c2s_ref, o_ref, sel_ref, *, tq):
    t0 = pl.program_id(1) * tq
    kc = kc_ref[...]
    vc = vc_ref[...]
    rows = kc.shape[0]
    trow = t0 + lax.broadcasted_iota(I32, (tq, rows), 0)
    cend = lax.broadcasted_iota(I32, (tq, rows), 1) * CMP_STRIDE + (CMP_LEN - 1)
    vis = cend <= trow
    anyv = (trow[:, :1] >= CMP_LEN - 1).astype(F32)
    imp = jnp.zeros((tq, LANES), F32)
    for g in range(N_KV):
        kg = kc[:, g * HEAD_DIM:(g + 1) * HEAD_DIM]
        vg = vc[:, g * HEAD_DIM:(g + 1) * HEAD_DIM].astype(BF16)
        psum = jnp.zeros((tq, rows), F32)
        for h in range(HPG):
            hd = g * HPG + h
            qh = q_ref[:, hd * HEAD_DIM:(hd + 1) * HEAD_DIM]
            s = lax.dot_general(qh, kg, NT_DIMS, precision=HI, preferred_element_type=F32) * SCALE
            s = jnp.where(vis, s, NEG)
            e = jnp.exp(s - jnp.max(s, axis=-1, keepdims=True))
            p = e / jnp.sum(e, axis=-1, keepdims=True) * anyv
            o_ref[:, hd * HEAD_DIM:(hd + 1) * HEAD_DIM] = jnp.dot(
                p.astype(BF16), vg, preferred_element_type=F32)
            psum = psum + p
        imp = imp + jnp.dot(psum, c2s_ref[g], precision=HI, preferred_element_type=F32)
    lane = lax.broadcasted_iota(I32, (tq, LANES), 1)
    blk = lane & (SLC_LEN - 1)
    cur = lax.shift_right_logical(t0 + lax.broadcasted_iota(I32, (tq, LANES), 0), 6)
    score = jnp.where(blk <= cur, imp, NEG)
    score = jnp.where(blk == 0, FORCE_INIT, score)
    score = jnp.where(blk == cur, FORCE_LOCAL, score)
    sc_t = score.T
    nblk = LANES // N_KV
    jrow = lax.broadcasted_iota(I32, (nblk, tq), 0)
    sel_parts = []
    for g in range(N_KV):
        sc = sc_t[g * nblk:(g + 1) * nblk]
        cnt = jnp.zeros((nblk, tq), I32)
        for k in range(nblk):
            rk = sc[k:k + 1, :]
            ge = (rk >= sc).astype(I32)
            gt = (rk > sc).astype(I32)
            cnt = cnt + jnp.where(jrow > k, ge, gt)
        sel_parts.append((cnt < SLC_TOPN).astype(F32))
    sel_ref[...] = jnp.concatenate(sel_parts, axis=0).T


def _cmpsel_call(q, kvc, c2s, seq, tq):
    n = q.shape[0]
    b = n // seq
    nq = seq // tq
    rows = kvc.shape[2]
    tok = lambda i, j: (i * nq + j, 0)
    return pl.pallas_call(
        functools.partial(_cmpsel_kernel, tq=tq),
        grid=(b, nq),
        in_specs=[
            pl.BlockSpec((tq, ATT_WIDTH), tok),
            pl.BlockSpec((None, None, rows, LANES), lambda i, j: (i, 0, 0, 0)),
            pl.BlockSpec((None, None, rows, LANES), lambda i, j: (i, 1, 0, 0)),
            pl.BlockSpec((N_KV, rows, LANES), lambda i, j: (0, 0, 0)),
        ],
        out_specs=[pl.BlockSpec((tq, ATT_WIDTH), tok), pl.BlockSpec((tq, LANES), tok)],
        out_shape=[jax.ShapeDtypeStruct((n, ATT_WIDTH), F32), jax.ShapeDtypeStruct((n, LANES), F32)],
        compiler_params=_cparams(("arbitrary", "arbitrary")),
        name="cmp_select",
    )(q, kvc, kvc, c2s)


def _slc_kernel(q_ref, k_ref, v_ref, sel_ref, o_ref, qs_sc, m_sc, l_sc, acc_sc, *, tq, tk):
    qi = pl.program_id(1)
    kt = pl.program_id(2)
    nk = pl.num_programs(2)

    @pl.when(kt == 0)
    def _init():
        for hd in range(N_HEADS):
            g, h = divmod(hd, HPG)
            qs_sc[g, h * tq:(h + 1) * tq, :] = (q_ref[:, hd * HEAD_DIM:(hd + 1) * HEAD_DIM] * SCALE).astype(BF16)
        m_sc[...] = jnp.full(m_sc.shape, NEG, F32)
        l_sc[...] = jnp.zeros(l_sc.shape, F32)
        acc_sc[...] = jnp.zeros(acc_sc.shape, F32)

    @pl.when(kt * tk <= qi * tq + (tq - 1))
    def _step():
        t = qi * tq + lax.broadcasted_iota(I32, (tq, tk), 0)
        kp = kt * tk + lax.broadcasted_iota(I32, (tq, tk), 1)
        causal = kp <= t
        nblk = LANES // N_KV
        jb = lax.broadcasted_iota(I32, (nblk, tk), 0)
        kb = lax.shift_right_logical(kt * tk + lax.broadcasted_iota(I32, (nblk, tk), 1), 6)
        expand = (jb == kb).astype(BF16)
        for g in range(N_KV):
            selg = sel_ref[:, g * nblk:(g + 1) * nblk].astype(BF16)
            member = jnp.dot(selg, expand, preferred_element_type=F32)
            bias = jnp.where(jnp.logical_and(causal, member > 0.5), 0.0, NEG)
            bias = jnp.concatenate([bias] * HPG, axis=0)
            kg = k_ref[:, g * HEAD_DIM:(g + 1) * HEAD_DIM].astype(BF16)
            vg = v_ref[:, g * HEAD_DIM:(g + 1) * HEAD_DIM].astype(BF16)
            s = lax.dot_general(qs_sc[g], kg, NT_DIMS, preferred_element_type=F32) + bias
            chunks = [s[:, c * LANES:(c + 1) * LANES] for c in range(tk // LANES)]
            mc = chunks[0]
            for x in chunks[1:]:
                mc = jnp.maximum(mc, x)
            m_old = m_sc[g]
            m_new = jnp.maximum(m_old, jnp.max(mc, axis=-1, keepdims=True))
            alpha = jnp.exp(m_old - m_new)
            ps = [jnp.exp(x - m_new) for x in chunks]
            lsum = ps[0]
            for x in ps[1:]:
                lsum = lsum + x
            l_sc[g] = alpha * l_sc[g] + lsum
            p = jnp.concatenate(ps, axis=1).astype(BF16)
            acc_sc[g] = alpha[:, :HEAD_DIM] * acc_sc[g] + jnp.dot(p, vg, preferred_element_type=F32)
            m_sc[g] = m_new

    @pl.when(kt == nk - 1)
    def _fin():
        for hd in range(N_HEADS):
            g, h = divmod(hd, HPG)
            l = jnp.sum(l_sc[g, h * tq:(h + 1) * tq, :], axis=-1, keepdims=True)
            o_ref[:, hd * HEAD_DIM:(hd + 1) * HEAD_DIM] = acc_sc[g, h * tq:(h + 1) * tq, :] / l


def _slc_call(q, kv, sel, seq, tq, tk):
    n = q.shape[0]
    b = n // seq
    nq = seq // tq
    nk = seq // tk
    tok = lambda i, j, k: (i * nq + j, 0)

    def key_map(col):
        def f(i, j, k):
            last = (j * tq + tq - 1) // tk
            return (i * nk + jnp.minimum(k, last), col)
        return f

    return pl.pallas_call(
        functools.partial(_slc_kernel, tq=tq, tk=tk),
        grid=(b, nq, nk),
        in_specs=[
            pl.BlockSpec((tq, ATT_WIDTH), tok),
            pl.BlockSpec((tk, LANES), key_map(2)),
            pl.BlockSpec((tk, LANES), key_map(3)),
            pl.BlockSpec((tq, LANES), tok),
        ],
        out_specs=pl.BlockSpec((tq, ATT_WIDTH), tok),
        out_shape=jax.ShapeDtypeStruct((n, ATT_WIDTH), F32),
        scratch_shapes=[
            pltpu.VMEM((N_KV, HPG * tq, HEAD_DIM), BF16),
            pltpu.VMEM((N_KV, HPG * tq, LANES), F32),
            pltpu.VMEM((N_KV, HPG * tq, LANES), F32),
            pltpu.VMEM((N_KV, HPG * tq, HEAD_DIM), F32),
        ],
        compiler_params=_cparams(("arbitrary", "arbitrary", "arbitrary")),
        name="slc_attn",
    )(q, kv, kv, sel)


def _win_kernel(q_ref, *refs, tq, nkb):
    k_refs = refs[:nkb]
    v_refs = refs[nkb:2 * nkb]
    o_ref = refs[2 * nkb]
    qi = pl.program_id(1)
    t = qi * tq + lax.broadcasted_iota(I32, (tq, tq), 0)
    col = lax.broadcasted_iota(I32, (tq, tq), 1)
    oks = []
    for j in range(nkb):
        kp = (qi - (nkb - 1) + j) * tq + col
        diff = t - kp
        oks.append(jnp.logical_and(jnp.logical_and(diff >= 0, diff < WINDOW), kp >= 0))
    for g in range(N_KV):
        kgs = [k_refs[j][:, g * HEAD_DIM:(g + 1) * HEAD_DIM].astype(BF16) for j in range(nkb)]
        vgs = [v_refs[j][:, g * HEAD_DIM:(g + 1) * HEAD_DIM].astype(BF16) for j in range(nkb)]
        for h in range(HPG):
            hd = g * HPG + h
            qh = (q_ref[:, hd * HEAD_DIM:(hd + 1) * HEAD_DIM] * SCALE).astype(BF16)
            ss = [jnp.where(oks[j], lax.dot_general(qh, kgs[j], NT_DIMS, preferred_element_type=F32), NEG)
                  for j in range(nkb)]
            m = jnp.max(ss[0], axis=-1, keepdims=True)
            for j in range(1, nkb):
                m = jnp.maximum(m, jnp.max(ss[j], axis=-1, keepdims=True))
            ps = [jnp.exp(s - m) for s in ss]
            l = jnp.sum(ps[0], axis=-1, keepdims=True)
            o = jnp.dot(ps[0].astype(BF16), vgs[0], preferred_element_type=F32)
            for j in range(1, nkb):
                l = l + jnp.sum(ps[j], axis=-1, keepdims=True)
                o = o + jnp.dot(ps[j].astype(BF16), vgs[j], preferred_element_type=F32)
            o_ref[:, hd * HEAD_DIM:(hd + 1) * HEAD_DIM] = o / l


def _win_call(q, kv, seq, tq):
    n = q.shape[0]
    b = n // seq
    nq = seq // tq
    nkb = WINDOW // tq + 1
    tok = lambda i, j: (i * nq + j, 0)

    def key_map(col, back):
        return lambda i, j: (i * nq + jnp.maximum(j - back, 0), col)

    k_specs = [pl.BlockSpec((tq, LANES), key_map(4, nkb - 1 - jj)) for jj in range(nkb)]
    v_specs = [pl.BlockSpec((tq, LANES), key_map(5, nkb - 1 - jj)) for jj in range(nkb)]
    return pl.pallas_call(
        functools.partial(_win_kernel, tq=tq, nkb=nkb),
        grid=(b, nq),
        in_specs=[pl.BlockSpec((tq, ATT_WIDTH), tok)] + k_specs + v_specs,
        out_specs=pl.BlockSpec((tq, ATT_WIDTH), tok),
        out_shape=jax.ShapeDtypeStruct((n, ATT_WIDTH), F32),
        compiler_params=_cparams(("arbitrary", "arbitrary")),
        name="win_attn",
    )(q, *([kv] * (2 * nkb)))


def _pool_kernel(p_ref, prev_ref, w_ref, sc_ref, o_ref, *, ts):
    i = pl.program_id(1)
    x = p_ref[...]
    prev = prev_ref[...] * (i > 0).astype(F32)
    xe = jnp.concatenate([prev, x], axis=0)
    t1 = (i * ts + 1 + lax.broadcasted_iota(I32, (ts, POOL_GW), 0)).astype(F32)
    for g, w in enumerate(POOL_WINDOWS):
        a = xe[:, g * POOL_GW:(g + 1) * POOL_GW]
        off = POOL_HALO
        span = 1
        while span < w:
            a = a[span:] + a[:-span]
            off -= span
            span *= 2
        sums = a[off:off + ts]
        cnt = jnp.minimum(t1, float(w))
        pooled = sums / cnt - x[:, g * POOL_GW:(g + 1) * POOL_GW]
        y = jnp.dot(pooled.astype(BF16), w_ref[g], preferred_element_type=F32)
        o_ref[:, g * POOL_GW:(g + 1) * POOL_GW] = y * sc_ref[:, g * POOL_GW:(g + 1) * POOL_GW]


def _pool_call(p_in, w_pool, pool_scale, seq, ts):
    n = p_in.shape[0]
    b = n // seq
    nt = seq // ts
    hpt = ts // POOL_HALO
    tok = lambda i, j: (i * nt + j, 0)
    return pl.pallas_call(
        functools.partial(_pool_kernel, ts=ts),
        grid=(b, nt),
        in_specs=[
            pl.BlockSpec((ts, POOL_WIDTH), tok),
            pl.BlockSpec((POOL_HALO, POOL_WIDTH), lambda i, j: (i * nt * hpt + jnp.maximum(j * hpt - 1, 0), 0)),
            pl.BlockSpec((POOL_GROUPS, POOL_GW, POOL_GW), lambda i, j: (0, 0, 0)),
            pl.BlockSpec((1, POOL_WIDTH), lambda i, j: (0, 0)),
        ],
        out_specs=pl.BlockSpec((ts, POOL_WIDTH), tok),
        out_shape=jax.ShapeDtypeStruct((n, POOL_WIDTH), F32),
        compiler_params=_cparams(("arbitrary", "arbitrary")),
        name="pool_mix",
    )(p_in, p_in, w_pool, pool_scale)


def _layer_norm(z, g, b):
    mu = jnp.mean(z, axis=-1, keepdims=True)
    zc = z - mu
    var = jnp.mean(zc * zc, axis=-1, keepdims=True)
    return zc * lax.rsqrt(var + LN_EPS) * g + b


def _merge_kernel(oc_ref, os_ref, ow_ref, gn_ref, op_ref, gm_ref, x_ref, g1_ref, lng_ref, lnb_ref,
                  wl_ref, wo_ref, eb_ref, o_ref, *, alpha):
    gate = jax.nn.sigmoid(gn_ref[...])
    branches = (oc_ref, os_ref, ow_ref)
    oatt = None
    for br in range(3):
        gx = jnp.dot(gate, eb_ref[br], precision=HI, preferred_element_type=F32)
        term = gx * branches[br][...]
        oatt = term if oatt is None else oatt + term
    la = jnp.dot(oatt.astype(BF16), wl_ref[0], preferred_element_type=F32)
    lb = jnp.dot(op_ref[...].astype(BF16), wl_ref[1], preferred_element_type=F32)
    gm = jax.nn.sigmoid(gm_ref[...])
    merged = gm[:, :D_MODEL] * la + gm[:, D_MODEL:] * lb
    y = jnp.dot(merged.astype(BF16), wo_ref[...], preferred_element_type=F32)
    z = alpha * x_ref[...] + g1_ref[...] * y
    o_ref[...] = _layer_norm(z, lng_ref[...], lnb_ref[...])


def _merge_call(oc, osl, ow, gn, op, gm, x2, g1, lng, lnb, wl, wo, eb, seq, tm, alpha):
    n = x2.shape[0]
    tpb = seq // tm
    tok = lambda i: (i, 0)
    bat = lambda i: (i // tpb, 0, 0)
    return pl.pallas_call(
        functools.partial(_merge_kernel, alpha=alpha),
        grid=(n // tm,),
        in_specs=[
            pl.BlockSpec((tm, ATT_WIDTH), tok), pl.BlockSpec((tm, ATT_WIDTH), tok),
            pl.BlockSpec((tm, ATT_WIDTH), tok), pl.BlockSpec((tm, LANES), tok),
            pl.BlockSpec((tm, POOL_WIDTH), tok), pl.BlockSpec((tm, MERGE_GATES), tok),
            pl.BlockSpec((tm, D_MODEL), tok),
            pl.BlockSpec((None, 1, D_MODEL), bat),
            pl.BlockSpec((1, D_MODEL), lambda i: (0, 0)), pl.BlockSpec((1, D_MODEL), lambda i: (0, 0)),
            pl.BlockSpec((2, ATT_WIDTH, D_MODEL), lambda i: (0, 0, 0)),
            pl.BlockSpec((D_MODEL, D_MODEL), lambda i: (0, 0)),
            pl.BlockSpec((3, LANES, ATT_WIDTH), lambda i: (0, 0, 0)),
        ],
        out_specs=pl.BlockSpec((tm, D_MODEL), tok),
        out_shape=jax.ShapeDtypeStruct((n, D_MODEL), F32),
        compiler_params=_cparams(("arbitrary",)),
        name="merge_out",
    )(oc, osl, ow, gn, op, gm, x2, g1, lng, lnb, wl, wo, eb)


def _extract_top(cur, ids, n):
    rows = cur.shape[0]
    rio = lax.broadcasted_iota(I32, cur.shape, 0)
    vals, outs = [], []
    for _ in range(n):
        m = jnp.max(cur, axis=0, keepdims=True)
        pos = jnp.min(jnp.where(cur == m, rio, rows), axis=0, keepdims=True)
        hit = rio == pos
        vals.append(m)
        outs.append(pos if ids is None else jnp.max(jnp.where(hit, ids, -1), axis=0, keepdims=True))
        cur = jnp.where(hit, -jnp.inf, cur)
    return jnp.concatenate(vals, axis=0), jnp.concatenate(outs, axis=0)


def _route_kernel(x_ref, sc_ref, sh_ref, wq_ref, keys_ref, h_ref, e_ref, g_ref,
                  st_sc, ts_sc, ti_sc, eo_sc, go_sc):
    h = x_ref[...] * (1.0 + sc_ref[...]) + sh_ref[...]
    h_ref[...] = h
    qp = jnp.dot(h.astype(BF16), wq_ref[...], preferred_element_type=F32).astype(BF16)
    half = PEER_DK // 2
    for hp in range(2 * PEER_HEADS):
        st_sc[hp] = lax.dot_general(keys_ref[hp], qp[:, hp * half:(hp + 1) * half], NT_DIMS,
                                    preferred_element_type=F32)

    def half_body(hp, carry):
        vals, ids = _extract_top(st_sc[hp], None, PEER_TOPK)
        ts_sc[hp] = vals
        ti_sc[hp] = ids
        return carry

    lax.fori_loop(0, 2 * PEER_HEADS, half_body, 0)

    def head_body(hh, carry):
        s1, s2 = ts_sc[2 * hh], ts_sc[2 * hh + 1]
        i1, i2 = ti_sc[2 * hh], ti_sc[2 * hh + 1]
        brow = lax.broadcasted_iota(I32, (SUBLANES, s1.shape[1]), 0)
        cands = [s1[0:1, :] + s2]
        cidxs = [i1[0:1, :] * N_KEYS + i2]
        for a in range(1, SUBLANES):
            ok = brow < PEER_TOPK // (a + 1)
            cands.append(jnp.where(ok, s1[a:a + 1, :] + s2[:SUBLANES], -jnp.inf))
            cidxs.append(i1[a:a + 1, :] * N_KEYS + i2[:SUBLANES])
        cands.append(s1[SUBLANES:] + s2[0:1, :])
        cidxs.append(i1[SUBLANES:] * N_KEYS + i2[0:1, :])
        sv, ei = _extract_top(jnp.concatenate(cands, axis=0), jnp.concatenate(cidxs, axis=0), PEER_TOPK)
        ex = jnp.exp(sv - sv[0:1, :])
        go_sc[hh] = ex / jnp.sum(ex, axis=0, keepdims=True)
        eo_sc[hh] = ei.astype(F32)
        return carry

    lax.fori_loop(0, PEER_HEADS, head_body, 0)
    e_all = jnp.concatenate([eo_sc[hh] for hh in range(PEER_HEADS)], axis=0)
    g_all = jnp.concatenate([go_sc[hh] for hh in range(PEER_HEADS)], axis=0)
    e_ref[...] = e_all.T.astype(I32) * (D_MODEL // 2 // LANES)
    g_ref[...] = g_all.T


def _route_call(x2, sc, sh, wq, keys, seq, tt):
    n = x2.shape[0]
    tpb = seq // tt
    tok = lambda i: (i, 0)
    bat = lambda i: (i // tpb, 0, 0)
    nhp = 2 * PEER_HEADS
    return pl.pallas_call(
        _route_kernel,
        grid=(n // tt,),
        in_specs=[
            pl.BlockSpec((tt, D_MODEL), tok),
            pl.BlockSpec((None, 1, D_MODEL), bat),
            pl.BlockSpec((None, 1, D_MODEL), bat),
            pl.BlockSpec((D_MODEL, PEER_HEADS * PEER_DK), lambda i: (0, 0)),
            pl.BlockSpec((nhp, N_KEYS, PEER_DK // 2), lambda i: (0, 0, 0)),
        ],
        out_specs=[pl.BlockSpec((tt, D_MODEL), tok), pl.BlockSpec((tt, PEER_SEL), tok),
                   pl.BlockSpec((tt, PEER_SEL), tok)],
        out_shape=[jax.ShapeDtypeStruct((n, D_MODEL), F32), jax.ShapeDtypeStruct((n, PEER_SEL), I32),
                   jax.ShapeDtypeStruct((n, PEER_SEL), F32)],
        scratch_shapes=[
            pltpu.VMEM((nhp, N_KEYS, tt), F32),
            pltpu.VMEM((nhp, PEER_TOPK, tt), F32),
            pltpu.VMEM((nhp, PEER_TOPK, tt), I32),
            pltpu.VMEM((PEER_HEADS, PEER_TOPK, tt), F32),
            pltpu.VMEM((PEER_HEADS, PEER_TOPK, tt), F32),
        ],
        compiler_params=_cparams(("arbitrary",)),
        name="peer_route",
    )(x2, sc, sh, wq, keys)


HALF_ROWS = SUBLANES // 2
HI_MASK = -65536
PAIR_TILES = PEER_SEL // 2
PAIR_ROWS = PAIR_TILES * SUBLANES


def _load_two_experts(tab_ref, ra, rb):
    wa = tab_ref[pl.ds(pl.multiple_of(ra, HALF_ROWS), HALF_ROWS), :]
    wb = tab_ref[pl.ds(pl.multiple_of(rb, HALF_ROWS), HALF_ROWS), :]
    w2 = jnp.concatenate([wa, wb], axis=0)
    return lax.bitcast_convert_type(w2 << 16, F32), lax.bitcast_convert_type(w2 & HI_MASK, F32)


def _fold_pairs(vs):
    row = lax.broadcasted_iota(I32, (SUBLANES, LANES), 0)
    shift = HALF_ROWS // 2
    while len(vs) > 1:
        low = (row & shift) == 0
        vs = [jnp.where(low, a + pltpu.roll(a, SUBLANES - shift, 0), b + pltpu.roll(b, shift, 0))
              for a, b in zip(vs[0::2], vs[1::2])]
        shift //= 2
    return vs[0]


def _fold_order():
    idx = [[2 * i if r < HALF_ROWS else 2 * i + 1 for r in range(SUBLANES)] for i in range(HALF_ROWS)]
    shift = HALF_ROWS // 2
    while len(idx) > 1:
        idx = [[a[r] if (r & shift) == 0 else b[r] for r in range(SUBLANES)]
               for a, b in zip(idx[0::2], idx[1::2])]
        shift //= 2
    return idx[0]


def _peer_u_kernel(e_sm, tab_ref, hv_ref, gate_ref, coef_ref, *, tt):
    row = lax.broadcasted_iota(I32, (SUBLANES, LANES), 0)
    low = row < HALF_ROWS
    eye = (lax.broadcasted_iota(I32, (PEER_SEL, LANES), 0) ==
           lax.broadcasted_iota(I32, (PEER_SEL, LANES), 1))
    order = _fold_order()

    def finish(t, part):
        col = jnp.sum(part, axis=-1, keepdims=True)
        a_row = jnp.sum(jnp.where(eye, col, 0.0), axis=0, keepdims=True)
        coef_ref[t] = gate_ref[t] * jax.nn.gelu(a_row)

    def token(t, part_prev):
        finish(jnp.maximum(t - 1, 0), part_prev)
        hv = hv_ref[t]
        hsw = pltpu.roll(hv, HALF_ROWS, 0)
        h_lo = jnp.where(low, hv, hsw)
        h_hi = jnp.where(low, hsw, hv)
        folded = []
        for j in range(PEER_SEL // SUBLANES):
            prods = []
            for i in range(HALF_ROWS):
                ka = j * SUBLANES + order.index(2 * i)
                kb = j * SUBLANES + order.index(2 * i + 1)
                lo, hi = _load_two_experts(tab_ref, e_sm[t, ka], e_sm[t, kb])
                prods.append(lo * h_lo + hi * h_hi)
            folded.append(_fold_pairs(prods))
        return jnp.concatenate(folded, axis=0)

    last = lax.fori_loop(0, tt, token, jnp.zeros((PEER_SEL, LANES), F32))
    finish(tt - 1, last)


def _peer_v_kernel(e_sm, coef_ref, tab_ref, y_ref, cv_sc, *, tt, nacc):
    row = lax.broadcasted_iota(I32, (SUBLANES, LANES), 0)
    low = row < HALF_ROWS
    rr = lax.broadcasted_iota(I32, (PAIR_ROWS, LANES), 0)
    kk = lax.broadcasted_iota(I32, (PAIR_ROWS, LANES), 1)
    onehot = (kk == 2 * (rr >> 3) + ((rr >> 2) & 1)).astype(F32)
    ones = jnp.ones((LANES, LANES), BF16)

    def expand(t, slot):
        lhs = (onehot * coef_ref[t]).astype(BF16)
        cv_sc[slot] = jnp.dot(lhs, ones, preferred_element_type=F32)

    def process(t, slot):
        acc_lo = [jnp.zeros((SUBLANES, LANES), F32) for _ in range(nacc)]
        acc_hi = [jnp.zeros((SUBLANES, LANES), F32) for _ in range(nacc)]
        for j in range(PAIR_TILES):
            lo, hi = _load_two_experts(tab_ref, e_sm[t, 2 * j], e_sm[t, 2 * j + 1])
            cv = cv_sc[slot, j * SUBLANES:(j + 1) * SUBLANES, :]
            acc_lo[j % nacc] = acc_lo[j % nacc] + cv * lo
            acc_hi[j % nacc] = acc_hi[j % nacc] + cv * hi
        a_lo, a_hi = acc_lo[0], acc_hi[0]
        for i in range(1, nacc):
            a_lo = a_lo + acc_lo[i]
            a_hi = a_hi + acc_hi[i]
        a_lo = a_lo + pltpu.roll(a_lo, HALF_ROWS, 0)
        a_hi = a_hi + pltpu.roll(a_hi, HALF_ROWS, 0)
        y_ref[t] = jnp.where(low, a_lo, a_hi)

    expand(0, 0)

    def two_tokens(i, carry):
        t = 2 * i
        expand(t + 1, 1)
        process(t, 0)
        expand(jnp.minimum(t + 2, tt - 1), 0)
        process(t + 1, 1)
        return carry

    lax.fori_loop(0, tt // 2, two_tokens, 0)


def _resident_table_spec(tab):
    return pl.BlockSpec(tab.shape, lambda i: (0, 0), pipeline_mode=pl.Buffered(1))


def _peer_u_call(erow, tab, hv, gate3, tt):
    n = erow.shape[0]
    return pl.pallas_call(
        functools.partial(_peer_u_kernel, tt=tt),
        grid=(n // tt,),
        in_specs=[
            pl.BlockSpec((tt, PEER_SEL), lambda i: (i, 0), memory_space=pltpu.SMEM),
            _resident_table_spec(tab),
            pl.BlockSpec((tt, SUBLANES, LANES), lambda i: (i, 0, 0)),
            pl.BlockSpec((tt, 1, PEER_SEL), lambda i: (i, 0, 0)),
        ],
        out_specs=pl.BlockSpec((tt, 1, PEER_SEL), lambda i: (i, 0, 0)),
        out_shape=jax.ShapeDtypeStruct((n, 1, PEER_SEL), F32),
        compiler_params=_cparams(("arbitrary",)),
        name="peer_u",
    )(erow, tab, hv, gate3)


def _peer_v_call(erow, coef3, tab, tt):
    n = erow.shape[0]
    assert tt % 2 == 0
    return pl.pallas_call(
        functools.partial(_peer_v_kernel, tt=tt, nacc=4),
        grid=(n // tt,),
        in_specs=[
            pl.BlockSpec((tt, PEER_SEL), lambda i: (i, 0), memory_space=pltpu.SMEM),
            pl.BlockSpec((tt, 1, PEER_SEL), lambda i: (i, 0, 0)),
            _resident_table_spec(tab),
        ],
        out_specs=pl.BlockSpec((tt, SUBLANES, LANES), lambda i: (i, 0, 0)),
        out_shape=jax.ShapeDtypeStruct((n, SUBLANES, LANES), F32),
        scratch_shapes=[pltpu.VMEM((2, PAIR_ROWS, LANES), F32)],
        compiler_params=_cparams(("arbitrary",)),
        name="peer_v",
    )(erow, coef3, tab)


def _resln_kernel(x_ref, y_ref, g_ref, lng_ref, lnb_ref, o_ref, *, alpha):
    z = alpha * x_ref[...] + g_ref[...] * y_ref[...]
    o_ref[...] = _layer_norm(z, lng_ref[...], lnb_ref[...])


def _resln_call(x2, y2, g2, lng, lnb, seq, tm, alpha):
    n = x2.shape[0]
    tpb = seq // tm
    tok = lambda i: (i, 0)
    return pl.pallas_call(
        functools.partial(_resln_kernel, alpha=alpha),
        grid=(n // tm,),
        in_specs=[
            pl.BlockSpec((tm, D_MODEL), tok), pl.BlockSpec((tm, D_MODEL), tok),
            pl.BlockSpec((None, 1, D_MODEL), lambda i: (i // tpb, 0, 0)),
            pl.BlockSpec((1, D_MODEL), lambda i: (0, 0)), pl.BlockSpec((1, D_MODEL), lambda i: (0, 0)),
        ],
        out_specs=pl.BlockSpec((tm, D_MODEL), tok),
        out_shape=jax.ShapeDtypeStruct((n, D_MODEL), F32),
        compiler_params=_cparams(("arbitrary",)),
        name="res_ln",
    )(x2, y2, g2, lng, lnb)


def _rope_lane_tables(pos):
    inv = ROPE_THETA ** (-jnp.arange(0, ROT_DIM, 2, dtype=F32) / ROT_DIM)
    ang = pos.astype(F32)[:, None] * inv[None, :]
    cos, sin = jnp.cos(ang), jnp.sin(ang)
    lane = np.arange(LANES) % HEAD_DIM
    fidx = lane % ROT_HALF
    first = jnp.asarray(lane < ROT_HALF)
    second = jnp.asarray((lane >= ROT_HALF) & (lane < ROT_DIM))
    rot = jnp.asarray(lane < ROT_DIM)
    cl, sl = cos[:, fidx], sin[:, fidx]
    rc = jnp.where(rot, cl, 1.0)
    rs1 = jnp.where(second, sl, 0.0)
    rs2 = jnp.where(first, -sl, 0.0)
    return rc, rs1, rs2


def _pack_table(tab):
    e, d = tab.shape
    bits = lax.bitcast_convert_type(tab.astype(BF16), jnp.uint16).astype(jnp.uint32)
    word = bits[:, :d // 2] | (bits[:, d // 2:] << 16)
    return lax.bitcast_convert_type(word, I32).reshape(e * (d // 2) // LANES, LANES)


def _cmp_to_slc_wide(rows, n_slc):
    st = np.arange(rows) * CMP_STRIDE
    js = np.arange(n_slc) * SLC_LEN
    ov = np.minimum(st[:, None] + CMP_LEN, js[None, :] + SLC_LEN) - np.maximum(st[:, None], js[None, :])
    c2s = np.maximum(ov, 0).astype(np.float32) / CMP_STRIDE
    wide = np.zeros((N_KV, rows, LANES), np.float32)
    nblk = LANES // N_KV
    for g in range(N_KV):
        wide[g, :, g * nblk:g * nblk + n_slc] = c2s
    return jnp.asarray(wide)


def _gate_expanders():
    eb = np.zeros((3, LANES, ATT_WIDTH), np.float32)
    for hd in range(N_HEADS):
        for br in range(3):
            eb[br, hd * 3 + br, hd * HEAD_DIM:(hd + 1) * HEAD_DIM] = 1.0
    return jnp.asarray(eb)


class _Consts:
    def __init__(self, seq):
        self.rows = seq // CMP_STRIDE
        self.rope = _rope_lane_tables(jnp.arange(seq))
        cpos = jnp.arange(self.rows) * CMP_STRIDE + CMP_LEN - 1
        crope = _rope_lane_tables(cpos)
        ident = (jnp.ones_like(crope[0]), jnp.zeros_like(crope[0]), jnp.zeros_like(crope[0]))
        self.crope = tuple(jnp.stack([a, b]) for a, b in zip(crope, ident))
        self.c2s = _cmp_to_slc_wide(self.rows, seq // SLC_LEN)
        self.eb = _gate_expanders()


def _token_mixer_layer(x2, sc1, sh1, g1, w_in, cmp_pe, cmp_w1, cmp_w2, w_pool, pool_scale, w_lift, w_o,
                       lng, lnb, cst, bsz, seq, alpha):
    d = D_MODEL
    rows = cst.rows
    s1 = ATT_WIDTH + KV_WIDTH
    s2 = s1 + GATE_NSA
    s3 = s2 + POOL_WIDTH
    w_gate = jnp.pad(w_in[:, s1:s2], ((0, 0), (0, LANES - GATE_NSA)))
    w_all = jnp.concatenate([w_in[:, :s1], w_in[:, s2:s3], w_in[:, s3:], w_gate], axis=1).astype(BF16)
    q, kv, p_in, g_mrg, g_nsa = _inproj_call(x2, sc1, sh1, w_all, *cst.rope, seq, 256)

    eye_g = jnp.eye(N_KV, dtype=F32)
    zc = kv[:, :2 * LANES].reshape(bsz, rows, CMP_STRIDE, 2, LANES)
    zc = jnp.transpose(zc, (0, 3, 1, 2, 4)).reshape(bsz, 2, rows, CMP_STRIDE * LANES)
    w1x = jnp.einsum('klde,gh->klgdhe', cmp_w1.reshape(2, CMP_LEN, HEAD_DIM, HEAD_DIM), eye_g)
    w1x = w1x.reshape(2, CMP_LEN * LANES, LANES)
    half = CMP_STRIDE * LANES
    pex = jnp.broadcast_to(cmp_pe[:, :, None, :], (2, CMP_LEN, N_KV, HEAD_DIM)).reshape(2, 1, CMP_LEN * LANES)
    w2x = jnp.einsum('kef,gh->kgehf', cmp_w2, eye_g).reshape(2, LANES, LANES)
    kvc = _compress_call(zc, pex[:, :, :half], pex[:, :, half:], w1x[:, :half], w1x[:, half:], w2x,
                         *cst.crope)

    o_cmp, sel = _cmpsel_call(q, kvc, cst.c2s, seq, 256)
    o_slc = _slc_call(q, kv, sel, seq, 256, 512)
    o_win = _win_call(q, kv, seq, 256)
    o_pool = _pool_call(p_in, w_pool.astype(BF16), pool_scale.reshape(1, -1), seq, 512)
    return _merge_call(o_cmp, o_slc, o_win, g_nsa, o_pool, g_mrg, x2, g1,
                       lng.reshape(1, d), lnb.reshape(1, d),
                       w_lift.astype(BF16), w_o.astype(BF16), cst.eb, seq, 256, alpha)


SC_CORES = 2
SC_SUBCORES = 16
SC_LANES = 16
SC_WORKERS = SC_CORES * SC_SUBCORES
SC_ROWS = 32
SC_TOKEN_SHARE = 5
SC_BLOCK = 256


def _sc_params():
    cp = pltpu.CompilerParams()
    if "needs_layout_passes" in pltpu.CompilerParams.__dataclass_fields__:
        cp = dataclasses.replace(cp, needs_layout_passes=False)
    return cp


def _sc_gather_pipeline(tab_hbm, idx_v, rows_v, sems, nch, chunk_fn):
    def gather(c):
        return pltpu.async_copy(tab_hbm.at[idx_v.at[pl.ds(c * SC_ROWS, SC_ROWS)]],
                                rows_v.at[c % 2], sems.at[c % 2])

    pending = {0: gather(0)}
    for c in range(nch):
        if c + 1 < nch:
            pending[c + 1] = gather(c + 1)
        pending.pop(c).wait()
        chunk_fn(c, rows_v.at[c % 2])


def _sc_v_call(eidx, coef, vtab):
    m = eidx.shape[0]
    d = vtab.shape[1]
    assert m % SC_WORKERS == 0 and PEER_SEL % SC_ROWS == 0 and d % SC_BLOCK == 0
    tpw = m // SC_WORKERS
    nch = PEER_SEL // SC_ROWS
    nvec = SC_BLOCK // SC_LANES
    mesh = plsc.VectorSubcoreMesh(core_axis_name="c", subcore_axis_name="s")

    @functools.partial(
        pl.kernel, mesh=mesh, out_type=jax.ShapeDtypeStruct((m, d), F32),
        scratch_types=[pltpu.VMEM((PEER_SEL,), I32), pltpu.VMEM((PEER_SEL,), F32),
                       pltpu.VMEM((2, SC_ROWS, d), F32), pltpu.VMEM((d,), F32),
                       pltpu.SemaphoreType.DMA((2,))],
        compiler_params=_sc_params(), name="sc_peer_v")
    def run(tab_hbm, idx_hbm, coef_hbm, out_hbm, idx_v, coef_v, rows_v, acc_v, sems):
        wid = lax.axis_index("s") * SC_CORES + lax.axis_index("c")
        base = wid * tpw
        zero = jnp.zeros((SC_LANES,), F32)

        @pl.loop(0, tpw)
        def _(i):
            t = base + i
            pltpu.sync_copy(idx_hbm.at[t], idx_v)
            pltpu.sync_copy(coef_hbm.at[t], coef_v)
            for j in range(d // SC_LANES):
                acc_v[pl.ds(j * SC_LANES, SC_LANES)] = zero

            def chunk(c, buf):
                @pl.loop(0, d // SC_BLOCK)
                def _(lb):
                    lane0 = pl.multiple_of(lb * SC_BLOCK, SC_BLOCK)
                    accs = tuple(acc_v[pl.ds(lane0 + q * SC_LANES, SC_LANES)] for q in range(nvec))

                    @plsc.parallel_loop(0, SC_ROWS, unroll=2, carry=accs)
                    def accs(r, acc):
                        ck = plsc.load_gather(coef_v, [jnp.full((SC_LANES,), c * SC_ROWS, I32) + r])
                        return tuple(acc[q] + ck * buf[r, pl.ds(lane0 + q * SC_LANES, SC_LANES)]
                                     for q in range(nvec))

                    for q in range(nvec):
                        acc_v[pl.ds(lane0 + q * SC_LANES, SC_LANES)] = accs[q]

            _sc_gather_pipeline(tab_hbm, idx_v, rows_v, sems, nch, chunk)
            pltpu.sync_copy(acc_v, out_hbm.at[t])

    return run(vtab, eidx, coef)


def _sc_u_call(eidx, hrows, utab):
    m = eidx.shape[0]
    d = utab.shape[1]
    assert m % SC_WORKERS == 0 and PEER_SEL % SC_ROWS == 0 and d % SC_BLOCK == 0
    tpw = m // SC_WORKERS
    nch = PEER_SEL // SC_ROWS
    nvec = SC_BLOCK // SC_LANES
    mesh = plsc.VectorSubcoreMesh(core_axis_name="c", subcore_axis_name="s")

    @functools.partial(
        pl.kernel, mesh=mesh, out_type=jax.ShapeDtypeStruct((m, PEER_SEL), F32),
        scratch_types=[pltpu.VMEM((PEER_SEL,), I32), pltpu.VMEM((d,), F32),
                       pltpu.VMEM((2, SC_ROWS, d), F32), pltpu.VMEM((PEER_SEL * SC_LANES,), F32),
                       pltpu.VMEM((PEER_SEL,), F32), pltpu.SemaphoreType.DMA((2,))],
        compiler_params=_sc_params(), name="sc_peer_u")
    def run(tab_hbm, idx_hbm, h_hbm, out_hbm, idx_v, h_v, rows_v, part_v, a_v, sems):
        wid = lax.axis_index("s") * SC_CORES + lax.axis_index("c")
        base = wid * tpw
        zero = jnp.zeros((SC_LANES,), F32)
        lane_iota = lax.iota(I32, SC_LANES)

        @pl.loop(0, tpw)
        def _(i):
            t = base + i
            pltpu.sync_copy(idx_hbm.at[t], idx_v)
            pltpu.sync_copy(h_hbm.at[t], h_v)
            for j in range(PEER_SEL):
                part_v[pl.ds(j * SC_LANES, SC_LANES)] = zero

            def chunk(c, buf):
                @pl.loop(0, d // SC_BLOCK)
                def _(lb):
                    lane0 = pl.multiple_of(lb * SC_BLOCK, SC_BLOCK)
                    hs = [h_v[pl.ds(lane0 + q * SC_LANES, SC_LANES)] for q in range(nvec)]

                    @plsc.parallel_loop(0, SC_ROWS, unroll=2)
                    def _(r):
                        ps = [hs[q] * buf[r, pl.ds(lane0 + q * SC_LANES, SC_LANES)] for q in range(nvec)]
                        while len(ps) > 1:
                            ps = [x + y for x, y in zip(ps[0::2], ps[1::2])]
                        row = pl.multiple_of((c * SC_ROWS + r) * SC_LANES, SC_LANES)
                        plsc.addupdate(part_v.at[pl.ds(row, SC_LANES)], ps[0])

            _sc_gather_pipeline(tab_hbm, idx_v, rows_v, sems, nch, chunk)
            for g in range(PEER_SEL // SC_LANES):
                rowbase = (g * SC_LANES + lane_iota) * SC_LANES
                acc = plsc.load_gather(part_v, [rowbase])
                for l in range(1, SC_LANES):
                    acc = acc + plsc.load_gather(part_v, [rowbase + l])
                a_v[pl.ds(g * SC_LANES, SC_LANES)] = acc
            pltpu.sync_copy(a_v, out_hbm.at[t])

    return run(utab, eidx, hrows)


def _coef_kernel(a_ref, g_ref, o_ref):
    o_ref[...] = g_ref[...] * jax.nn.gelu(a_ref[...])


def _coef_call(a, gate, tm):
    m = a.shape[0]
    spec = pl.BlockSpec((tm, PEER_SEL), lambda i: (i, 0))
    return pl.pallas_call(
        _coef_kernel, grid=(m // tm,), in_specs=[spec, spec], out_specs=spec,
        out_shape=jax.ShapeDtypeStruct((m, PEER_SEL), F32),
        compiler_params=_cparams(("arbitrary",)), name="peer_coef",
    )(a, gate)


def _peer_layer(x2, sc2, sh2, g2, peer_wq, peer_keys, peer_u, peer_v, lng, lnb, seq, alpha,
                tt_route=256, tt_gather=64, tm=512, sc_tokens=0):
    n, d = x2.shape
    keys = peer_keys.reshape(2 * PEER_HEADS, N_KEYS, PEER_DK // 2)
    h2, erow, gate = _route_call(x2, sc2, sh2, peer_wq.astype(BF16), keys.astype(BF16), seq, tt_route)
    n_tc = n - sc_tokens
    hv = h2[:n_tc].reshape(n_tc, SUBLANES, LANES)
    coef = _peer_u_call(erow[:n_tc], _pack_table(peer_u), hv, gate[:n_tc].reshape(n_tc, 1, PEER_SEL), tt_gather)
    if sc_tokens:
        eidx = erow[n_tc:] // (D_MODEL // 2 // LANES)
        coef_sc = _coef_call(_sc_u_call(eidx, h2[n_tc:], peer_u), gate[n_tc:], SC_WORKERS)
    y = _peer_v_call(erow[:n_tc], coef, _pack_table(peer_v), tt_gather).reshape(n_tc, d)
    if sc_tokens:
        y = jnp.concatenate([y, _sc_v_call(eidx, coef_sc, peer_v)], axis=0)
    return _resln_call(x2, y, g2, lng.reshape(1, d), lnb.reshape(1, d), seq, tm, alpha)


def kernel(x, c, w_ada, b_ada, w_in, cmp_pe, cmp_w1, cmp_w2, w_pool, pool_scale, w_lift, w_o,
           ln_g, ln_b, peer_wq, peer_keys, peer_u, peer_v):
    bsz, seq, d = x.shape
    depth = w_ada.shape[0]
    n = bsz * seq
    assert d == D_MODEL and seq % 512 == 0 and SLC_TOPN <= seq // SLC_LEN <= LANES // N_KV
    alpha = (2 * depth) ** 0.25

    c_pad = jnp.zeros((SUBLANES, d), F32).at[:bsz].set(c)
    mods = _ada_call(c_pad, w_ada, b_ada)[:, :bsz]
    cst = _Consts(seq)
    x2 = x.reshape(n, d)
    for l in range(depth):
        sh1, sc1, g1, sh2, sc2, g2 = (mods[l][:, i * d:(i + 1) * d].reshape(bsz, 1, d) for i in range(6))
        x2 = _token_mixer_layer(x2, sc1, sh1, g1, w_in[l], cmp_pe[l], cmp_w1[l], cmp_w2[l], w_pool[l],
                                pool_scale[l], w_lift[l], w_o[l], ln_g[l, 0], ln_b[l, 0], cst, bsz, seq, alpha)
        x2 = _peer_layer(x2, sc2, sh2, g2, peer_wq[l], peer_keys[l], peer_u[l], peer_v[l],
                         ln_g[l, 1], ln_b[l, 1], seq, alpha, sc_tokens=SC_TOKEN_SHARE * n // 16)
    return x2.reshape(bsz, seq, d)
```

```python
import dataclasses
import functools

import jax
import jax.numpy as jnp
import numpy as np
from jax import lax
from jax.experimental import pallas as pl
from jax.experimental.pallas import tpu as pltpu
from jax.experimental.pallas import tpu_sc as plsc

F32 = jnp.float32
BF16 = jnp.bfloat16
I32 = jnp.int32
HI = lax.Precision.HIGHEST

D_MODEL = 1024
N_HEADS = 8
HEAD_DIM = 64
N_KV = 2
HPG = N_HEADS // N_KV
ROT_DIM = HEAD_DIM // 4
ROT_HALF = ROT_DIM // 2
ROPE_THETA = 500000.0
CMP_LEN = 32
CMP_STRIDE = 16
SLC_LEN = 64
SLC_TOPN = 16
WINDOW = 512
SCALE = HEAD_DIM ** -0.5
NEG = -1e30
FORCE_INIT = 1e6
FORCE_LOCAL = 2e6
POOL_GROUPS = 4
POOL_WINDOWS = (2, 4, 8, 16)
POOL_WIDTH = 512
POOL_GW = POOL_WIDTH // POOL_GROUPS
POOL_HALO = 16
ATT_WIDTH = N_HEADS * HEAD_DIM
KV_WIDTH = 3 * 2 * N_KV * HEAD_DIM
GATE_NSA = 3 * N_HEADS
MERGE_GATES = 2 * D_MODEL
PEER_HEADS = 8
N_KEYS = 128
PEER_TOPK = 16
PEER_DK = 128
PEER_SEL = PEER_HEADS * PEER_TOPK
LN_EPS = 1e-5

LANES = 128
SUBLANES = 8
VMEM_LIMIT = 56 * 1024 * 1024

NT_DIMS = (((1,), (1,)), ((), ()))


def _cparams(sem):
    return pltpu.CompilerParams(dimension_semantics=sem, vmem_limit_bytes=VMEM_LIMIT)


def _ada_kernel(c_ref, w_ref, b_ref, o_ref):
    c = c_ref[...]
    ca = c * jax.nn.sigmoid(c)
    o_ref[...] = jnp.dot(ca, w_ref[...], precision=HI, preferred_element_type=F32) + b_ref[...]


def _ada_call(c_pad, w_ada, b_ada):
    depth = w_ada.shape[0]
    nblk = w_ada.shape[2] // D_MODEL
    rows = c_pad.shape[0]
    return pl.pallas_call(
        _ada_kernel,
        grid=(depth, nblk),
        in_specs=[
            pl.BlockSpec((rows, D_MODEL), lambda l, j: (0, 0)),
            pl.BlockSpec((None, D_MODEL, D_MODEL), lambda l, j: (l, 0, j)),
            pl.BlockSpec((None, 1, D_MODEL), lambda l, j: (l, 0, j)),
        ],
        out_specs=pl.BlockSpec((None, rows, D_MODEL), lambda l, j: (l, 0, j)),
        out_shape=jax.ShapeDtypeStruct((depth, rows, nblk * D_MODEL), F32),
        compiler_params=_cparams(("arbitrary", "arbitrary")),
        name="ada_mod",
    )(c_pad, w_ada, b_ada.reshape(depth, 1, -1))


IN_COLS = ATT_WIDTH + KV_WIDTH + POOL_WIDTH + MERGE_GATES + LANES


def _rope_lanes(z, rc, rs1, rs2):
    return z * rc + pltpu.roll(z, ROT_HALF, 1) * rs1 + pltpu.roll(z, LANES - ROT_HALF, 1) * rs2


def _inproj_kernel(x_ref, sc_ref, sh_ref, w_ref, rc_ref, rs1_ref, rs2_ref,
                   q_ref, kv_ref, p_ref, mrg_ref, gn_ref):
    h = x_ref[...] * (1.0 + sc_ref[...]) + sh_ref[...]
    a = jnp.dot(h.astype(BF16), w_ref[...], preferred_element_type=F32)
    rc, rs1, rs2 = rc_ref[...], rs1_ref[...], rs2_ref[...]
    for j in range(ATT_WIDTH // LANES):
        q_ref[:, j * LANES:(j + 1) * LANES] = _rope_lanes(a[:, j * LANES:(j + 1) * LANES], rc, rs1, rs2)
    for br in range(3):
        c0 = ATT_WIDTH + br * 2 * LANES
        k = a[:, c0:c0 + LANES]
        if br > 0:
            k = _rope_lanes(k, rc, rs1, rs2)
        kv_ref[:, br * 2 * LANES:br * 2 * LANES + LANES] = k
        kv_ref[:, br * 2 * LANES + LANES:(br + 1) * 2 * LANES] = a[:, c0 + LANES:c0 + 2 * LANES]
    c1 = ATT_WIDTH + KV_WIDTH
    p_ref[...] = a[:, c1:c1 + POOL_WIDTH]
    mrg_ref[...] = a[:, c1 + POOL_WIDTH:c1 + POOL_WIDTH + MERGE_GATES]
    gn_ref[...] = a[:, c1 + POOL_WIDTH + MERGE_GATES:]


def _inproj_call(x2, sc, sh, w, rc, rs1, rs2, seq, tm):
    n = x2.shape[0]
    tpb = seq // tm
    tok = lambda i: (i, 0)
    bat = lambda i: (i // tpb, 0, 0)
    pos = lambda i: (i % tpb, 0)
    full = lambda i: (0, 0)
    return pl.pallas_call(
        _inproj_kernel,
        grid=(n // tm,),
        in_specs=[
            pl.BlockSpec((tm, D_MODEL), tok),
            pl.BlockSpec((None, 1, D_MODEL), bat),
            pl.BlockSpec((None, 1, D_MODEL), bat),
            pl.BlockSpec((D_MODEL, IN_COLS), full),
            pl.BlockSpec((tm, LANES), pos),
            pl.BlockSpec((tm, LANES), pos),
            pl.BlockSpec((tm, LANES), pos),
        ],
        out_specs=[
            pl.BlockSpec((tm, ATT_WIDTH), tok),
            pl.BlockSpec((tm, KV_WIDTH), tok),
            pl.BlockSpec((tm, POOL_WIDTH), tok),
            pl.BlockSpec((tm, MERGE_GATES), tok),
            pl.BlockSpec((tm, LANES), tok),
        ],
        out_shape=[
            jax.ShapeDtypeStruct((n, ATT_WIDTH), F32),
            jax.ShapeDtypeStruct((n, KV_WIDTH), F32),
            jax.ShapeDtypeStruct((n, POOL_WIDTH), F32),
            jax.ShapeDtypeStruct((n, MERGE_GATES), F32),
            jax.ShapeDtypeStruct((n, LANES), F32),
        ],
        compiler_params=_cparams(("arbitrary",)),
        name="in_proj",
    )(x2, sc, sh, w, rc, rs1, rs2)


def _compress_kernel(z_ref, pet_ref, peb_ref, w1t_ref, w1b_ref, w2_ref, rc_ref, rs1_ref, rs2_ref, o_ref):
    z = z_ref[...]
    rows = z.shape[0]
    top = jnp.dot(z + pet_ref[...], w1t_ref[...], precision=HI, preferred_element_type=F32)
    bot = jnp.dot(z + peb_ref[...], w1b_ref[...], precision=HI, preferred_element_type=F32)
    pre = top + pltpu.roll(bot, rows - 1, 0)
    y = jnp.dot(jax.nn.gelu(pre), w2_ref[...], precision=HI, preferred_element_type=F32)
    o_ref[...] = _rope_lanes(y, rc_ref[...], rs1_ref[...], rs2_ref[...])


def _compress_call(z, pet, peb, w1t, w1b, w2, rc, rs1, rs2):
    b, _, rows, width = z.shape
    kvsel = lambda i, j: (j, 0, 0)
    return pl.pallas_call(
        _compress_kernel,
        grid=(b, 2),
        in_specs=[
            pl.BlockSpec((None, None, rows, width), lambda i, j: (i, j, 0, 0)),
            pl.BlockSpec((None, 1, width), kvsel),
            pl.BlockSpec((None, 1, width), kvsel),
            pl.BlockSpec((None, width, LANES), kvsel),
            pl.BlockSpec((None, width, LANES), kvsel),
            pl.BlockSpec((None, LANES, LANES), kvsel),
            pl.BlockSpec((None, rows, LANES), kvsel),
            pl.BlockSpec((None, rows, LANES), kvsel),
            pl.BlockSpec((None, rows, LANES), kvsel),
        ],
        out_specs=pl.BlockSpec((None, None, rows, LANES), lambda i, j: (i, j, 0, 0)),
        out_shape=jax.ShapeDtypeStruct((b, 2, rows, LANES), F32),
        compiler_params=_cparams(("arbitrary", "arbitrary")),
        name="compress",
    )(z, pet, peb, w1t, w1b, w2, rc, rs1, rs2)


def _cmpsel_kernel(q_ref, kc_ref, vc_ref, c2s_ref, o_ref, sel_ref, *, tq):
    t0 = pl.program_id(1) * tq
    kc = kc_ref[...]
    vc = vc_ref[...]
    rows = kc.shape[0]
    trow = t0 + lax.broadcasted_iota(I32, (tq, rows), 0)
    cend = lax.broadcasted_iota(I32, (tq, rows), 1) * CMP_STRIDE + (CMP_LEN - 1)
    vis = cend <= trow
    anyv = (trow[:, :1] >= CMP_LEN - 1).astype(F32)
    imp = jnp.zeros((tq, LANES), F32)
    for g in range(N_KV):
        kg = kc[:, g * HEAD_DIM:(g + 1) * HEAD_DIM]
        vg = vc[:, g * HEAD_DIM:(g + 1) * HEAD_DIM].astype(BF16)
        psum = jnp.zeros((tq, rows), F32)
        for h in range(HPG):
            hd = g * HPG + h
            qh = q_ref[:, hd * HEAD_DIM:(hd + 1) * HEAD_DIM]
            s = lax.dot_general(qh, kg, NT_DIMS, precision=HI, preferred_element_type=F32) * SCALE
            s = jnp.where(vis, s, NEG)
            e = jnp.exp(s - jnp.max(s, axis=-1, keepdims=True))
            p = e / jnp.sum(e, axis=-1, keepdims=True) * anyv
            o_ref[:, hd * HEAD_DIM:(hd + 1) * HEAD_DIM] = jnp.dot(
                p.astype(BF16), vg, preferred_element_type=F32)
            psum = psum + p
        imp = imp + jnp.dot(psum, c2s_ref[g], precision=HI, preferred_element_type=F32)
    lane = lax.broadcasted_iota(I32, (tq, LANES), 1)
    blk = lane & (SLC_LEN - 1)
    cur = lax.shift_right_logical(t0 + lax.broadcasted_iota(I32, (tq, LANES), 0), 6)
    score = jnp.where(blk <= cur, imp, NEG)
    score = jnp.where(blk == 0, FORCE_INIT, score)
    score = jnp.where(blk == cur, FORCE_LOCAL, score)
    sc_t = score.T
    nblk = LANES // N_KV
    jrow = lax.broadcasted_iota(I32, (nblk, tq), 0)
    sel_parts = []
    for g in range(N_KV):
        sc = sc_t[g * nblk:(g + 1) * nblk]
        cnt = jnp.zeros((nblk, tq), I32)
        for k in range(nblk):
            rk = sc[k:k + 1, :]
            ge = (rk >= sc).astype(I32)
            gt = (rk > sc).astype(I32)
            cnt = cnt + jnp.where(jrow > k, ge, gt)
        sel_parts.append((cnt < SLC_TOPN).astype(F32))
    sel_ref[...] = jnp.concatenate(sel_parts, axis=0).T


def _cmpsel_call(q, kvc, c2s, seq, tq):
    n = q.shape[0]
    b = n // seq
    nq = seq // tq
    rows = kvc.shape[2]
    tok = lambda i, j: (i * nq + j, 0)
    return pl.pallas_call(
        functools.partial(_cmpsel_kernel, tq=tq),
        grid=(b, nq),
        in_specs=[
            pl.BlockSpec((tq, ATT_WIDTH), tok),
            pl.BlockSpec((None, None, rows, LANES), lambda i, j: (i, 0, 0, 0)),
            pl.BlockSpec((None, None, rows, LANES), lambda i, j: (i, 1, 0, 0)),
            pl.BlockSpec((N_KV, rows, LANES), lambda i, j: (0, 0, 0)),
        ],
        out_specs=[pl.BlockSpec((tq, ATT_WIDTH), tok), pl.BlockSpec((tq, LANES), tok)],
        out_shape=[jax.ShapeDtypeStruct((n, ATT_WIDTH), F32), jax.ShapeDtypeStruct((n, LANES), F32)],
        compiler_params=_cparams(("arbitrary", "arbitrary")),
        name="cmp_select",
    )(q, kvc, kvc, c2s)


def _slc_kernel(q_ref, k_ref, v_ref, sel_ref, o_ref, qs_sc, m_sc, l_sc, acc_sc, *, tq, tk):
    qi = pl.program_id(1)
    kt = pl.program_id(2)
    nk = pl.num_programs(2)

    @pl.when(kt == 0)
    def _init():
        for hd in range(N_HEADS):
            g, h = divmod(hd, HPG)
            qs_sc[g, h * tq:(h + 1) * tq, :] = (q_ref[:, hd * HEAD_DIM:(hd + 1) * HEAD_DIM] * SCALE).astype(BF16)
        m_sc[...] = jnp.full(m_sc.shape, NEG, F32)
        l_sc[...] = jnp.zeros(l_sc.shape, F32)
        acc_sc[...] = jnp.zeros(acc_sc.shape, F32)

    @pl.when(kt * tk <= qi * tq + (tq - 1))
    def _step():
        t = qi * tq + lax.broadcasted_iota(I32, (tq, tk), 0)
        kp = kt * tk + lax.broadcasted_iota(I32, (tq, tk), 1)
        causal = kp <= t
        nblk = LANES // N_KV
        jb = lax.broadcasted_iota(I32, (nblk, tk), 0)
        kb = lax.shift_right_logical(kt * tk + lax.broadcasted_iota(I32, (nblk, tk), 1), 6)
        expand = (jb == kb).astype(BF16)
        for g in range(N_KV):
            selg = sel_ref[:, g * nblk:(g + 1) * nblk].astype(BF16)
            member = jnp.dot(selg, expand, preferred_element_type=F32)
            bias = jnp.where(jnp.logical_and(causal, member > 0.5), 0.0, NEG)
            bias = jnp.concatenate([bias] * HPG, axis=0)
            kg = k_ref[:, g * HEAD_DIM:(g + 1) * HEAD_DIM].astype(BF16)
            vg = v_ref[:, g * HEAD_DIM:(g + 1) * HEAD_DIM].astype(BF16)
            s = lax.dot_general(qs_sc[g], kg, NT_DIMS, preferred_element_type=F32) + bias
            chunks = [s[:, c * LANES:(c + 1) * LANES] for c in range(tk // LANES)]
            mc = chunks[0]
            for x in chunks[1:]:
                mc = jnp.maximum(mc, x)
            m_old = m_sc[g]
            m_new = jnp.maximum(m_old, jnp.max(mc, axis=-1, keepdims=True))
            alpha = jnp.exp(m_old - m_new)
            ps = [jnp.exp(x - m_new) for x in chunks]
            lsum = ps[0]
            for x in ps[1:]:
                lsum = lsum + x
            l_sc[g] = alpha * l_sc[g] + lsum
            p = jnp.concatenate(ps, axis=1).astype(BF16)
            acc_sc[g] = alpha[:, :HEAD_DIM] * acc_sc[g] + jnp.dot(p, vg, preferred_element_type=F32)
            m_sc[g] = m_new

    @pl.when(kt == nk - 1)
    def _fin():
        for hd in range(N_HEADS):
            g, h = divmod(hd, HPG)
            l = jnp.sum(l_sc[g, h * tq:(h + 1) * tq, :], axis=-1, keepdims=True)
            o_ref[:, hd * HEAD_DIM:(hd + 1) * HEAD_DIM] = acc_sc[g, h * tq:(h + 1) * tq, :] / l


def _slc_call(q, kv, sel, seq, tq, tk):
    n = q.shape[0]
    b = n // seq
    nq = seq // tq
    nk = seq // tk
    tok = lambda i, j, k: (i * nq + j, 0)

    def key_map(col):
        def f(i, j, k):
            last = (j * tq + tq - 1) // tk
            return (i * nk + jnp.minimum(k, last), col)
        return f

    return pl.pallas_call(
        functools.partial(_slc_kernel, tq=tq, tk=tk),
        grid=(b, nq, nk),
        in_specs=[
            pl.BlockSpec((tq, ATT_WIDTH), tok),
            pl.BlockSpec((tk, LANES), key_map(2)),
            pl.BlockSpec((tk, LANES), key_map(3)),
            pl.BlockSpec((tq, LANES), tok),
        ],
        out_specs=pl.BlockSpec((tq, ATT_WIDTH), tok),
        out_shape=jax.ShapeDtypeStruct((n, ATT_WIDTH), F32),
        scratch_shapes=[
            pltpu.VMEM((N_KV, HPG * tq, HEAD_DIM), BF16),
            pltpu.VMEM((N_KV, HPG * tq, LANES), F32),
            pltpu.VMEM((N_KV, HPG * tq, LANES), F32),
            pltpu.VMEM((N_KV, HPG * tq, HEAD_DIM), F32),
        ],
        compiler_params=_cparams(("arbitrary", "arbitrary", "arbitrary")),
        name="slc_attn",
    )(q, kv, kv, sel)


def _win_kernel(q_ref, *refs, tq, nkb):
    k_refs = refs[:nkb]
    v_refs = refs[nkb:2 * nkb]
    o_ref = refs[2 * nkb]
    qi = pl.program_id(1)
    t = qi * tq + lax.broadcasted_iota(I32, (tq, tq), 0)
    col = lax.broadcasted_iota(I32, (tq, tq), 1)
    oks = []
    for j in range(nkb):
        kp = (qi - (nkb - 1) + j) * tq + col
        diff = t - kp
        oks.append(jnp.logical_and(jnp.logical_and(diff >= 0, diff < WINDOW), kp >= 0))
    for g in range(N_KV):
        kgs = [k_refs[j][:, g * HEAD_DIM:(g + 1) * HEAD_DIM].astype(BF16) for j in range(nkb)]
        vgs = [v_refs[j][:, g * HEAD_DIM:(g + 1) * HEAD_DIM].astype(BF16) for j in range(nkb)]
        for h in range(HPG):
            hd = g * HPG + h
            qh = (q_ref[:, hd * HEAD_DIM:(hd + 1) * HEAD_DIM] * SCALE).astype(BF16)
            ss = [jnp.where(oks[j], lax.dot_general(qh, kgs[j], NT_DIMS, preferred_element_type=F32), NEG)
                  for j in range(nkb)]
            m = jnp.max(ss[0], axis=-1, keepdims=True)
            for j in range(1, nkb):
                m = jnp.maximum(m, jnp.max(ss[j], axis=-1, keepdims=True))
            ps = [jnp.exp(s - m) for s in ss]
            l = jnp.sum(ps[0], axis=-1, keepdims=True)
            o = jnp.dot(ps[0].astype(BF16), vgs[0], preferred_element_type=F32)
            for j in range(1, nkb):
                l = l + jnp.sum(ps[j], axis=-1, keepdims=True)
                o = o + jnp.dot(ps[j].astype(BF16), vgs[j], preferred_element_type=F32)
            o_ref[:, hd * HEAD_DIM:(hd + 1) * HEAD_DIM] = o / l


def _win_call(q, kv, seq, tq):
    n = q.shape[0]
    b = n // seq
    nq = seq // tq
    nkb = WINDOW // tq + 1
    tok = lambda i, j: (i * nq + j, 0)

    def key_map(col, back):
        return lambda i, j: (i * nq + jnp.maximum(j - back, 0), col)

    k_specs = [pl.BlockSpec((tq, LANES), key_map(4, nkb - 1 - jj)) for jj in range(nkb)]
    v_specs = [pl.BlockSpec((tq, LANES), key_map(5, nkb - 1 - jj)) for jj in range(nkb)]
    return pl.pallas_call(
        functools.partial(_win_kernel, tq=tq, nkb=nkb),
        grid=(b, nq),
        in_specs=[pl.BlockSpec((tq, ATT_WIDTH), tok)] + k_specs + v_specs,
        out_specs=pl.BlockSpec((tq, ATT_WIDTH), tok),
        out_shape=jax.ShapeDtypeStruct((n, ATT_WIDTH), F32),
        compiler_params=_cparams(("arbitrary", "arbitrary")),
        name="win_attn",
    )(q, *([kv] * (2 * nkb)))


def _pool_kernel(p_ref, prev_ref, w_ref, sc_ref, o_ref, *, ts):
    i = pl.program_id(1)
    x = p_ref[...]
    prev = prev_ref[...] * (i > 0).astype(F32)
    xe = jnp.concatenate([prev, x], axis=0)
    t1 = (i * ts + 1 + lax.broadcasted_iota(I32, (ts, POOL_GW), 0)).astype(F32)
    for g, w in enumerate(POOL_WINDOWS):
        a = xe[:, g * POOL_GW:(g + 1) * POOL_GW]
        off = POOL_HALO
        span = 1
        while span < w:
            a = a[span:] + a[:-span]
            off -= span
            span *= 2
        sums = a[off:off + ts]
        cnt = jnp.minimum(t1, float(w))
        pooled = sums / cnt - x[:, g * POOL_GW:(g + 1) * POOL_GW]
        y = jnp.dot(pooled.astype(BF16), w_ref[g], preferred_element_type=F32)
        o_ref[:, g * POOL_GW:(g + 1) * POOL_GW] = y * sc_ref[:, g * POOL_GW:(g + 1) * POOL_GW]


def _pool_call(p_in, w_pool, pool_scale, seq, ts):
    n = p_in.shape[0]
    b = n // seq
    nt = seq // ts
    hpt = ts // POOL_HALO
    tok = lambda i, j: (i * nt + j, 0)
    return pl.pallas_call(
        functools.partial(_pool_kernel, ts=ts),
        grid=(b, nt),
        in_specs=[
            pl.BlockSpec((ts, POOL_WIDTH), tok),
            pl.BlockSpec((POOL_HALO, POOL_WIDTH), lambda i, j: (i * nt * hpt + jnp.maximum(j * hpt - 1, 0), 0)),
            pl.BlockSpec((POOL_GROUPS, POOL_GW, POOL_GW), lambda i, j: (0, 0, 0)),
            pl.BlockSpec((1, POOL_WIDTH), lambda i, j: (0, 0)),
        ],
        out_specs=pl.BlockSpec((ts, POOL_WIDTH), tok),
        out_shape=jax.ShapeDtypeStruct((n, POOL_WIDTH), F32),
        compiler_params=_cparams(("arbitrary", "arbitrary")),
        name="pool_mix",
    )(p_in, p_in, w_pool, pool_scale)


def _layer_norm(z, g, b):
    mu = jnp.mean(z, axis=-1, keepdims=True)
    zc = z - mu
    var = jnp.mean(zc * zc, axis=-1, keepdims=True)
    return zc * lax.rsqrt(var + LN_EPS) * g + b


def _merge_kernel(oc_ref, os_ref, ow_ref, gn_ref, op_ref, gm_ref, x_ref, g1_ref, lng_ref, lnb_ref,
                  wl_ref, wo_ref, eb_ref, o_ref, *, alpha):
    gate = jax.nn.sigmoid(gn_ref[...])
    branches = (oc_ref, os_ref, ow_ref)
    oatt = None
    for br in range(3):
        gx = jnp.dot(gate, eb_ref[br], precision=HI, preferred_element_type=F32)
        term = gx * branches[br][...]
        oatt = term if oatt is None else oatt + term
    la = jnp.dot(oatt.astype(BF16), wl_ref[0], preferred_element_type=F32)
    lb = jnp.dot(op_ref[...].astype(BF16), wl_ref[1], preferred_element_type=F32)
    gm = jax.nn.sigmoid(gm_ref[...])
    merged = gm[:, :D_MODEL] * la + gm[:, D_MODEL:] * lb
    y = jnp.dot(merged.astype(BF16), wo_ref[...], preferred_element_type=F32)
    z = alpha * x_ref[...] + g1_ref[...] * y
    o_ref[...] = _layer_norm(z, lng_ref[...], lnb_ref[...])


def _merge_call(oc, osl, ow, gn, op, gm, x2, g1, lng, lnb, wl, wo, eb, seq, tm, alpha):
    n = x2.shape[0]
    tpb = seq // tm
    tok = lambda i: (i, 0)
    bat = lambda i: (i // tpb, 0, 0)
    return pl.pallas_call(
        functools.partial(_merge_kernel, alpha=alpha),
        grid=(n // tm,),
        in_specs=[
            pl.BlockSpec((tm, ATT_WIDTH), tok), pl.BlockSpec((tm, ATT_WIDTH), tok),
            pl.BlockSpec((tm, ATT_WIDTH), tok), pl.BlockSpec((tm, LANES), tok),
            pl.BlockSpec((tm, POOL_WIDTH), tok), pl.BlockSpec((tm, MERGE_GATES), tok),
            pl.BlockSpec((tm, D_MODEL), tok),
            pl.BlockSpec((None, 1, D_MODEL), bat),
            pl.BlockSpec((1, D_MODEL), lambda i: (0, 0)), pl.BlockSpec((1, D_MODEL), lambda i: (0, 0)),
            pl.BlockSpec((2, ATT_WIDTH, D_MODEL), lambda i: (0, 0, 0)),
            pl.BlockSpec((D_MODEL, D_MODEL), lambda i: (0, 0)),
            pl.BlockSpec((3, LANES, ATT_WIDTH), lambda i: (0, 0, 0)),
        ],
        out_specs=pl.BlockSpec((tm, D_MODEL), tok),
        out_shape=jax.ShapeDtypeStruct((n, D_MODEL), F32),
        compiler_params=_cparams(("arbitrary",)),
        name="merge_out",
    )(oc, osl, ow, gn, op, gm, x2, g1, lng, lnb, wl, wo, eb)


def _extract_top(cur, ids, n):
    rows = cur.shape[0]
    rio = lax.broadcasted_iota(I32, cur.shape, 0)
    vals, outs = [], []
    for _ in range(n):
        m = jnp.max(cur, axis=0, keepdims=True)
        pos = jnp.min(jnp.where(cur == m, rio, rows), axis=0, keepdims=True)
        hit = rio == pos
        vals.append(m)
        outs.append(pos if ids is None else jnp.max(jnp.where(hit, ids, -1), axis=0, keepdims=True))
        cur = jnp.where(hit, -jnp.inf, cur)
    return jnp.concatenate(vals, axis=0), jnp.concatenate(outs, axis=0)


def _route_kernel(x_ref, sc_ref, sh_ref, wq_ref, keys_ref, h_ref, e_ref, g_ref,
                  st_sc, ts_sc, ti_sc, eo_sc, go_sc):
    h = x_ref[...] * (1.0 + sc_ref[...]) + sh_ref[...]
    h_ref[...] = h
    qp = jnp.dot(h.astype(BF16), wq_ref[...], preferred_element_type=F32).astype(BF16)
    half = PEER_DK // 2
    for hp in range(2 * PEER_HEADS):
        st_sc[hp] = lax.dot_general(keys_ref[hp], qp[:, hp * half:(hp + 1) * half], NT_DIMS,
                                    preferred_element_type=F32)

    def half_body(hp, carry):
        vals, ids = _extract_top(st_sc[hp], None, PEER_TOPK)
        ts_sc[hp] = vals
        ti_sc[hp] = ids
        return carry

    lax.fori_loop(0, 2 * PEER_HEADS, half_body, 0)

    def head_body(hh, carry):
        s1, s2 = ts_sc[2 * hh], ts_sc[2 * hh + 1]
        i1, i2 = ti_sc[2 * hh], ti_sc[2 * hh + 1]
        brow = lax.broadcasted_iota(I32, (SUBLANES, s1.shape[1]), 0)
        cands = [s1[0:1, :] + s2]
        cidxs = [i1[0:1, :] * N_KEYS + i2]
        for a in range(1, SUBLANES):
            ok = brow < PEER_TOPK // (a + 1)
            cands.append(jnp.where(ok, s1[a:a + 1, :] + s2[:SUBLANES], -jnp.inf))
            cidxs.append(i1[a:a + 1, :] * N_KEYS + i2[:SUBLANES])
        cands.append(s1[SUBLANES:] + s2[0:1, :])
        cidxs.append(i1[SUBLANES:] * N_KEYS + i2[0:1, :])
        sv, ei = _extract_top(jnp.concatenate(cands, axis=0), jnp.concatenate(cidxs, axis=0), PEER_TOPK)
        ex = jnp.exp(sv - sv[0:1, :])
        go_sc[hh] = ex / jnp.sum(ex, axis=0, keepdims=True)
        eo_sc[hh] = ei.astype(F32)
        return carry

    lax.fori_loop(0, PEER_HEADS, head_body, 0)
    e_all = jnp.concatenate([eo_sc[hh] for hh in range(PEER_HEADS)], axis=0)
    g_all = jnp.concatenate([go_sc[hh] for hh in range(PEER_HEADS)], axis=0)
    e_ref[...] = e_all.T.astype(I32) * (D_MODEL // 2 // LANES)
    g_ref[...] = g_all.T


def _route_call(x2, sc, sh, wq, keys, seq, tt):
    n = x2.shape[0]
    tpb = seq // tt
    tok = lambda i: (i, 0)
    bat = lambda i: (i // tpb, 0, 0)
    nhp = 2 * PEER_HEADS
    return pl.pallas_call(
        _route_kernel,
        grid=(n // tt,),
        in_specs=[
            pl.BlockSpec((tt, D_MODEL), tok),
            pl.BlockSpec((None, 1, D_MODEL), bat),
            pl.BlockSpec((None, 1, D_MODEL), bat),
            pl.BlockSpec((D_MODEL, PEER_HEADS * PEER_DK), lambda i: (0, 0)),
            pl.BlockSpec((nhp, N_KEYS, PEER_DK // 2), lambda i: (0, 0, 0)),
        ],
        out_specs=[pl.BlockSpec((tt, D_MODEL), tok), pl.BlockSpec((tt, PEER_SEL), tok),
                   pl.BlockSpec((tt, PEER_SEL), tok)],
        out_shape=[jax.ShapeDtypeStruct((n, D_MODEL), F32), jax.ShapeDtypeStruct((n, PEER_SEL), I32),
                   jax.ShapeDtypeStruct((n, PEER_SEL), F32)],
        scratch_shapes=[
            pltpu.VMEM((nhp, N_KEYS, tt), F32),
            pltpu.VMEM((nhp, PEER_TOPK, tt), F32),
            pltpu.VMEM((nhp, PEER_TOPK, tt), I32),
            pltpu.VMEM((PEER_HEADS, PEER_TOPK, tt), F32),
            pltpu.VMEM((PEER_HEADS, PEER_TOPK, tt), F32),
        ],
        compiler_params=_cparams(("arbitrary",)),
        name="peer_route",
    )(x2, sc, sh, wq, keys)


HALF_ROWS = SUBLANES // 2
HI_MASK = -65536
PAIR_TILES = PEER_SEL // 2
PAIR_ROWS = PAIR_TILES * SUBLANES


def _load_two_experts(tab_ref, ra, rb):
    wa = tab_ref[pl.ds(pl.multiple_of(ra, HALF_ROWS), HALF_ROWS), :]
    wb = tab_ref[pl.ds(pl.multiple_of(rb, HALF_ROWS), HALF_ROWS), :]
    w2 = jnp.concatenate([wa, wb], axis=0)
    return lax.bitcast_convert_type(w2 << 16, F32), lax.bitcast_convert_type(w2 & HI_MASK, F32)


def _fold_pairs(vs):
    row = lax.broadcasted_iota(I32, (SUBLANES, LANES), 0)
    shift = HALF_ROWS // 2
    while len(vs) > 1:
        low = (row & shift) == 0
        vs = [jnp.where(low, a + pltpu.roll(a, SUBLANES - shift, 0), b + pltpu.roll(b, shift, 0))
              for a, b in zip(vs[0::2], vs[1::2])]
        shift //= 2
    return vs[0]


def _fold_order():
    idx = [[2 * i if r < HALF_ROWS else 2 * i + 1 for r in range(SUBLANES)] for i in range(HALF_ROWS)]
    shift = HALF_ROWS // 2
    while len(idx) > 1:
        idx = [[a[r] if (r & shift) == 0 else b[r] for r in range(SUBLANES)]
               for a, b in zip(idx[0::2], idx[1::2])]
        shift //= 2
    return idx[0]


def _peer_u_kernel(e_sm, tab_ref, hv_ref, gate_ref, coef_ref, *, tt):
    row = lax.broadcasted_iota(I32, (SUBLANES, LANES), 0)
    low = row < HALF_ROWS
    eye = (lax.broadcasted_iota(I32, (PEER_SEL, LANES), 0) ==
           lax.broadcasted_iota(I32, (PEER_SEL, LANES), 1))
    order = _fold_order()

    def finish(t, part):
        col = jnp.sum(part, axis=-1, keepdims=True)
        a_row = jnp.sum(jnp.where(eye, col, 0.0), axis=0, keepdims=True)
        coef_ref[t] = gate_ref[t] * jax.nn.gelu(a_row)

    def token(t, part_prev):
        finish(jnp.maximum(t - 1, 0), part_prev)
        hv = hv_ref[t]
        hsw = pltpu.roll(hv, HALF_ROWS, 0)
        h_lo = jnp.where(low, hv, hsw)
        h_hi = jnp.where(low, hsw, hv)
        folded = []
        for j in range(PEER_SEL // SUBLANES):
            prods = []
            for i in range(HALF_ROWS):
                ka = j * SUBLANES + order.index(2 * i)
                kb = j * SUBLANES + order.index(2 * i + 1)
                lo, hi = _load_two_experts(tab_ref, e_sm[t, ka], e_sm[t, kb])
                prods.append(lo * h_lo + hi * h_hi)
            folded.append(_fold_pairs(prods))
        return jnp.concatenate(folded, axis=0)

    last = lax.fori_loop(0, tt, token, jnp.zeros((PEER_SEL, LANES), F32))
    finish(tt - 1, last)


def _peer_v_kernel(e_sm, coef_ref, tab_ref, y_ref, cv_sc, *, tt, nacc):
    row = lax.broadcasted_iota(I32, (SUBLANES, LANES), 0)
    low = row < HALF_ROWS
    rr = lax.broadcasted_iota(I32, (PAIR_ROWS, LANES), 0)
    kk = lax.broadcasted_iota(I32, (PAIR_ROWS, LANES), 1)
    onehot = (kk == 2 * (rr >> 3) + ((rr >> 2) & 1)).astype(F32)
    ones = jnp.ones((LANES, LANES), BF16)

    def expand(t, slot):
        lhs = (onehot * coef_ref[t]).astype(BF16)
        cv_sc[slot] = jnp.dot(lhs, ones, preferred_element_type=F32)

    def process(t, slot):
        acc_lo = [jnp.zeros((SUBLANES, LANES), F32) for _ in range(nacc)]
        acc_hi = [jnp.zeros((SUBLANES, LANES), F32) for _ in range(nacc)]
        for j in range(PAIR_TILES):
            lo, hi = _load_two_experts(tab_ref, e_sm[t, 2 * j], e_sm[t, 2 * j + 1])
            cv = cv_sc[slot, j * SUBLANES:(j + 1) * SUBLANES, :]
            acc_lo[j % nacc] = acc_lo[j % nacc] + cv * lo
            acc_hi[j % nacc] = acc_hi[j % nacc] + cv * hi
        a_lo, a_hi = acc_lo[0], acc_hi[0]
        for i in range(1, nacc):
            a_lo = a_lo + acc_lo[i]
            a_hi = a_hi + acc_hi[i]
        a_lo = a_lo + pltpu.roll(a_lo, HALF_ROWS, 0)
        a_hi = a_hi + pltpu.roll(a_hi, HALF_ROWS, 0)
        y_ref[t] = jnp.where(low, a_lo, a_hi)

    expand(0, 0)

    def two_tokens(i, carry):
        t = 2 * i
        expand(t + 1, 1)
        process(t, 0)
        expand(jnp.minimum(t + 2, tt - 1), 0)
        process(t + 1, 1)
        return carry

    lax.fori_loop(0, tt // 2, two_tokens, 0)


def _resident_table_spec(tab):
    return pl.BlockSpec(tab.shape, lambda i: (0, 0), pipeline_mode=pl.Buffered(1))


def _peer_u_call(erow, tab, hv, gate3, tt):
    n = erow.shape[0]
    return pl.pallas_call(
        functools.partial(_peer_u_kernel, tt=tt),
        grid=(n // tt,),
        in_specs=[
            pl.BlockSpec((tt, PEER_SEL), lambda i: (i, 0), memory_space=pltpu.SMEM),
            _resident_table_spec(tab),
            pl.BlockSpec((tt, SUBLANES, LANES), lambda i: (i, 0, 0)),
            pl.BlockSpec((tt, 1, PEER_SEL), lambda i: (i, 0, 0)),
        ],
        out_specs=pl.BlockSpec((tt, 1, PEER_SEL), lambda i: (i, 0, 0)),
        out_shape=jax.ShapeDtypeStruct((n, 1, PEER_SEL), F32),
        compiler_params=_cparams(("arbitrary",)),
        name="peer_u",
    )(erow, tab, hv, gate3)


def _peer_v_call(erow, coef3, tab, tt):
    n = erow.shape[0]
    assert tt % 2 == 0
    return pl.pallas_call(
        functools.partial(_peer_v_kernel, tt=tt, nacc=4),
        grid=(n // tt,),
        in_specs=[
            pl.BlockSpec((tt, PEER_SEL), lambda i: (i, 0), memory_space=pltpu.SMEM),
            pl.BlockSpec((tt, 1, PEER_SEL), lambda i: (i, 0, 0)),
            _resident_table_spec(tab),
        ],
        out_specs=pl.BlockSpec((tt, SUBLANES, LANES), lambda i: (i, 0, 0)),
        out_shape=jax.ShapeDtypeStruct((n, SUBLANES, LANES), F32),
        scratch_shapes=[pltpu.VMEM((2, PAIR_ROWS, LANES), F32)],
        compiler_params=_cparams(("arbitrary",)),
        name="peer_v",
    )(erow, coef3, tab)


def _resln_kernel(x_ref, y_ref, g_ref, lng_ref, lnb_ref, o_ref, *, alpha):
    z = alpha * x_ref[...] + g_ref[...] * y_ref[...]
    o_ref[...] = _layer_norm(z, lng_ref[...], lnb_ref[...])


def _resln_call(x2, y2, g2, lng, lnb, seq, tm, alpha):
    n = x2.shape[0]
    tpb = seq // tm
    tok = lambda i: (i, 0)
    return pl.pallas_call(
        functools.partial(_resln_kernel, alpha=alpha),
        grid=(n // tm,),
        in_specs=[
            pl.BlockSpec((tm, D_MODEL), tok), pl.BlockSpec((tm, D_MODEL), tok),
            pl.BlockSpec((None, 1, D_MODEL), lambda i: (i // tpb, 0, 0)),
            pl.BlockSpec((1, D_MODEL), lambda i: (0, 0)), pl.BlockSpec((1, D_MODEL), lambda i: (0, 0)),
        ],
        out_specs=pl.BlockSpec((tm, D_MODEL), tok),
        out_shape=jax.ShapeDtypeStruct((n, D_MODEL), F32),
        compiler_params=_cparams(("arbitrary",)),
        name="res_ln",
    )(x2, y2, g2, lng, lnb)


def _rope_lane_tables(pos):
    inv = ROPE_THETA ** (-jnp.arange(0, ROT_DIM, 2, dtype=F32) / ROT_DIM)
    ang = pos.astype(F32)[:, None] * inv[None, :]
    cos, sin = jnp.cos(ang), jnp.sin(ang)
    lane = np.arange(LANES) % HEAD_DIM
    fidx = lane % ROT_HALF
    first = jnp.asarray(lane < ROT_HALF)
    second = jnp.asarray((lane >= ROT_HALF) & (lane < ROT_DIM))
    rot = jnp.asarray(lane < ROT_DIM)
    cl, sl = cos[:, fidx], sin[:, fidx]
    rc = jnp.where(rot, cl, 1.0)
    rs1 = jnp.where(second, sl, 0.0)
    rs2 = jnp.where(first, -sl, 0.0)
    return rc, rs1, rs2


def _pack_table(tab):
    e, d = tab.shape
    bits = lax.bitcast_convert_type(tab.astype(BF16), jnp.uint16).astype(jnp.uint32)
    word = bits[:, :d // 2] | (bits[:, d // 2:] << 16)
    return lax.bitcast_convert_type(word, I32).reshape(e * (d // 2) // LANES, LANES)


def _cmp_to_slc_wide(rows, n_slc):
    st = np.arange(rows) * CMP_STRIDE
    js = np.arange(n_slc) * SLC_LEN
    ov = np.minimum(st[:, None] + CMP_LEN, js[None, :] + SLC_LEN) - np.maximum(st[:, None], js[None, :])
    c2s = np.maximum(ov, 0).astype(np.float32) / CMP_STRIDE
    wide = np.zeros((N_KV, rows, LANES), np.float32)
    nblk = LANES // N_KV
    for g in range(N_KV):
        wide[g, :, g * nblk:g * nblk + n_slc] = c2s
    return jnp.asarray(wide)


def _gate_expanders():
    eb = np.zeros((3, LANES, ATT_WIDTH), np.float32)
    for hd in range(N_HEADS):
        for br in range(3):
            eb[br, hd * 3 + br, hd * HEAD_DIM:(hd + 1) * HEAD_DIM] = 1.0
    return jnp.asarray(eb)


class _Consts:
    def __init__(self, seq):
        self.rows = seq // CMP_STRIDE
        self.rope = _rope_lane_tables(jnp.arange(seq))
        cpos = jnp.arange(self.rows) * CMP_STRIDE + CMP_LEN - 1
        crope = _rope_lane_tables(cpos)
        ident = (jnp.ones_like(crope[0]), jnp.zeros_like(crope[0]), jnp.zeros_like(crope[0]))
        self.crope = tuple(jnp.stack([a, b]) for a, b in zip(crope, ident))
        self.c2s = _cmp_to_slc_wide(self.rows, seq // SLC_LEN)
        self.eb = _gate_expanders()


def _token_mixer_layer(x2, sc1, sh1, g1, w_in, cmp_pe, cmp_w1, cmp_w2, w_pool, pool_scale, w_lift, w_o,
                       lng, lnb, cst, bsz, seq, alpha):
    d = D_MODEL
    rows = cst.rows
    s1 = ATT_WIDTH + KV_WIDTH
    s2 = s1 + GATE_NSA
    s3 = s2 + POOL_WIDTH
    w_gate = jnp.pad(w_in[:, s1:s2], ((0, 0), (0, LANES - GATE_NSA)))
    w_all = jnp.concatenate([w_in[:, :s1], w_in[:, s2:s3], w_in[:, s3:], w_gate], axis=1).astype(BF16)
    q, kv, p_in, g_mrg, g_nsa = _inproj_call(x2, sc1, sh1, w_all, *cst.rope, seq, 256)

    eye_g = jnp.eye(N_KV, dtype=F32)
    zc = kv[:, :2 * LANES].reshape(bsz, rows, CMP_STRIDE, 2, LANES)
    zc = jnp.transpose(zc, (0, 3, 1, 2, 4)).reshape(bsz, 2, rows, CMP_STRIDE * LANES)
    w1x = jnp.einsum('klde,gh->klgdhe', cmp_w1.reshape(2, CMP_LEN, HEAD_DIM, HEAD_DIM), eye_g)
    w1x = w1x.reshape(2, CMP_LEN * LANES, LANES)
    half = CMP_STRIDE * LANES
    pex = jnp.broadcast_to(cmp_pe[:, :, None, :], (2, CMP_LEN, N_KV, HEAD_DIM)).reshape(2, 1, CMP_LEN * LANES)
    w2x = jnp.einsum('kef,gh->kgehf', cmp_w2, eye_g).reshape(2, LANES, LANES)
    kvc = _compress_call(zc, pex[:, :, :half], pex[:, :, half:], w1x[:, :half], w1x[:, half:], w2x,
                         *cst.crope)

    o_cmp, sel = _cmpsel_call(q, kvc, cst.c2s, seq, 256)
    o_slc = _slc_call(q, kv, sel, seq, 256, 512)
    o_win = _win_call(q, kv, seq, 256)
    o_pool = _pool_call(p_in, w_pool.astype(BF16), pool_scale.reshape(1, -1), seq, 512)
    return _merge_call(o_cmp, o_slc, o_win, g_nsa, o_pool, g_mrg, x2, g1,
                       lng.reshape(1, d), lnb.reshape(1, d),
                       w_lift.astype(BF16), w_o.astype(BF16), cst.eb, seq, 256, alpha)


SC_CORES = 2
SC_SUBCORES = 16
SC_LANES = 16
SC_WORKERS = SC_CORES * SC_SUBCORES
SC_ROWS = 32
SC_TOKEN_SHARE = 5
SC_BLOCK = 256


def _sc_params():
    cp = pltpu.CompilerParams()
    if "needs_layout_passes" in pltpu.CompilerParams.__dataclass_fields__:
        cp = dataclasses.replace(cp, needs_layout_passes=False)
    return cp


def _sc_gather_pipeline(tab_hbm, idx_v, rows_v, sems, nch, chunk_fn):
    def gather(c):
        return pltpu.async_copy(tab_hbm.at[idx_v.at[pl.ds(c * SC_ROWS, SC_ROWS)]],
                                rows_v.at[c % 2], sems.at[c % 2])

    pending = {0: gather(0)}
    for c in range(nch):
        if c + 1 < nch:
            pending[c + 1] = gather(c + 1)
        pending.pop(c).wait()
        chunk_fn(c, rows_v.at[c % 2])


def _sc_v_call(eidx, coef, vtab):
    m = eidx.shape[0]
    d = vtab.shape[1]
    assert m % SC_WORKERS == 0 and PEER_SEL % SC_ROWS == 0 and d % SC_BLOCK == 0
    tpw = m // SC_WORKERS
    nch = PEER_SEL // SC_ROWS
    nvec = SC_BLOCK // SC_LANES
    mesh = plsc.VectorSubcoreMesh(core_axis_name="c", subcore_axis_name="s")

    @functools.partial(
        pl.kernel, mesh=mesh, out_type=jax.ShapeDtypeStruct((m, d), F32),
        scratch_types=[pltpu.VMEM((PEER_SEL,), I32), pltpu.VMEM((PEER_SEL,), F32),
                       pltpu.VMEM((2, SC_ROWS, d), F32), pltpu.VMEM((d,), F32),
                       pltpu.SemaphoreType.DMA((2,))],
        compiler_params=_sc_params(), name="sc_peer_v")
    def run(tab_hbm, idx_hbm, coef_hbm, out_hbm, idx_v, coef_v, rows_v, acc_v, sems):
        wid = lax.axis_index("s") * SC_CORES + lax.axis_index("c")
        base = wid * tpw
        zero = jnp.zeros((SC_LANES,), F32)

        @pl.loop(0, tpw)
        def _(i):
            t = base + i
            pltpu.sync_copy(idx_hbm.at[t], idx_v)
            pltpu.sync_copy(coef_hbm.at[t], coef_v)
            for j in range(d // SC_LANES):
                acc_v[pl.ds(j * SC_LANES, SC_LANES)] = zero

            def chunk(c, buf):
                @pl.loop(0, d // SC_BLOCK)
                def _(lb):
                    lane0 = pl.multiple_of(lb * SC_BLOCK, SC_BLOCK)
                    accs = tuple(acc_v[pl.ds(lane0 + q * SC_LANES, SC_LANES)] for q in range(nvec))

                    @plsc.parallel_loop(0, SC_ROWS, unroll=2, carry=accs)
                    def accs(r, acc):
                        ck = plsc.load_gather(coef_v, [jnp.full((SC_LANES,), c * SC_ROWS, I32) + r])
                        return tuple(acc[q] + ck * buf[r, pl.ds(lane0 + q * SC_LANES, SC_LANES)]
                                     for q in range(nvec))

                    for q in range(nvec):
                        acc_v[pl.ds(lane0 + q * SC_LANES, SC_LANES)] = accs[q]

            _sc_gather_pipeline(tab_hbm, idx_v, rows_v, sems, nch, chunk)
            pltpu.sync_copy(acc_v, out_hbm.at[t])

    return run(vtab, eidx, coef)


def _sc_u_call(eidx, hrows, utab):
    m = eidx.shape[0]
    d = utab.shape[1]
    assert m % SC_WORKERS == 0 and PEER_SEL % SC_ROWS == 0 and d % SC_BLOCK == 0
    tpw = m // SC_WORKERS
    nch = PEER_SEL // SC_ROWS
    nvec = SC_BLOCK // SC_LANES
    mesh = plsc.VectorSubcoreMesh(core_axis_name="c", subcore_axis_name="s")

    @functools.partial(
        pl.kernel, mesh=mesh, out_type=jax.ShapeDtypeStruct((m, PEER_SEL), F32),
        scratch_types=[pltpu.VMEM((PEER_SEL,), I32), pltpu.VMEM((d,), F32),
                       pltpu.VMEM((2, SC_ROWS, d), F32), pltpu.VMEM((PEER_SEL * SC_LANES,), F32),
                       pltpu.VMEM((PEER_SEL,), F32), pltpu.SemaphoreType.DMA((2,))],
        compiler_params=_sc_params(), name="sc_peer_u")
    def run(tab_hbm, idx_hbm, h_hbm, out_hbm, idx_v, h_v, rows_v, part_v, a_v, sems):
        wid = lax.axis_index("s") * SC_CORES + lax.axis_index("c")
        base = wid * tpw
        zero = jnp.zeros((SC_LANES,), F32)
        lane_iota = lax.iota(I32, SC_LANES)

        @pl.loop(0, tpw)
        def _(i):
            t = base + i
            pltpu.sync_copy(idx_hbm.at[t], idx_v)
            pltpu.sync_copy(h_hbm.at[t], h_v)
            for j in range(PEER_SEL):
                part_v[pl.ds(j * SC_LANES, SC_LANES)] = zero

            def chunk(c, buf):
                @pl.loop(0, d // SC_BLOCK)
                def _(lb):
                    lane0 = pl.multiple_of(lb * SC_BLOCK, SC_BLOCK)
                    hs = [h_v[pl.ds(lane0 + q * SC_LANES, SC_LANES)] for q in range(nvec)]

                    @plsc.parallel_loop(0, SC_ROWS, unroll=2)
                    def _(r):
                        ps = [hs[q] * buf[r, pl.ds(lane0 + q * SC_LANES, SC_LANES)] for q in range(nvec)]
                        while len(ps) > 1:
                            ps = [x + y for x, y in zip(ps[0::2], ps[1::2])]
                        row = pl.multiple_of((c * SC_ROWS + r) * SC_LANES, SC_LANES)
                        plsc.addupdate(part_v.at[pl.ds(row, SC_LANES)], ps[0])

            _sc_gather_pipeline(tab_hbm, idx_v, rows_v, sems, nch, chunk)
            for g in range(PEER_SEL // SC_LANES):
                rowbase = (g * SC_LANES + lane_iota) * SC_LANES
                acc = plsc.load_gather(part_v, [rowbase])
                for l in range(1, SC_LANES):
                    acc = acc + plsc.load_gather(part_v, [rowbase + l])
                a_v[pl.ds(g * SC_LANES, SC_LANES)] = acc
            pltpu.sync_copy(a_v, out_hbm.at[t])

    return run(utab, eidx, hrows)


def _coef_kernel(a_ref, g_ref, o_ref):
    o_ref[...] = g_ref[...] * jax.nn.gelu(a_ref[...])


def _coef_call(a, gate, tm):
    m = a.shape[0]
    spec = pl.BlockSpec((tm, PEER_SEL), lambda i: (i, 0))
    return pl.pallas_call(
        _coef_kernel, grid=(m // tm,), in_specs=[spec, spec], out_specs=spec,
        out_shape=jax.ShapeDtypeStruct((m, PEER_SEL), F32),
        compiler_params=_cparams(("arbitrary",)), name="peer_coef",
    )(a, gate)


def _peer_layer(x2, sc2, sh2, g2, peer_wq, peer_keys, peer_u, peer_v, lng, lnb, seq, alpha,
                tt_route=256, tt_gather=64, tm=512, sc_tokens=0):
    n, d = x2.shape
    keys = peer_keys.reshape(2 * PEER_HEADS, N_KEYS, PEER_DK // 2)
    h2, erow, gate = _route_call(x2, sc2, sh2, peer_wq.astype(BF16), keys.astype(BF16), seq, tt_route)
    n_tc = n - sc_tokens
    hv = h2[:n_tc].reshape(n_tc, SUBLANES, LANES)
    coef = _peer_u_call(erow[:n_tc], _pack_table(peer_u), hv, gate[:n_tc].reshape(n_tc, 1, PEER_SEL), tt_gather)
    if sc_tokens:
        eidx = erow[n_tc:] // (D_MODEL // 2 // LANES)
        a_sc = _sc_u_call(eidx, h2[n_tc:], peer_u)
        a_sc, coef = lax.optimization_barrier((a_sc, coef))
        coef_sc = _coef_call(a_sc, gate[n_tc:], tm)
    y = _peer_v_call(erow[:n_tc], coef, _pack_table(peer_v), tt_gather).reshape(n_tc, d)
    if sc_tokens:
        y = jnp.concatenate([y, _sc_v_call(eidx, coef_sc, peer_v)], axis=0)
    return _resln_call(x2, y, g2, lng.reshape(1, d), lnb.reshape(1, d), seq, tm, alpha)


def kernel(x, c, w_ada, b_ada, w_in, cmp_pe, cmp_w1, cmp_w2, w_pool, pool_scale, w_lift, w_o,
           ln_g, ln_b, peer_wq, peer_keys, peer_u, peer_v):
    bsz, seq, d = x.shape
    depth = w_ada.shape[0]
    n = bsz * seq
    assert d == D_MODEL and seq % 512 == 0 and SLC_TOPN <= seq // SLC_LEN <= LANES // N_KV
    alpha = (2 * depth) ** 0.25

    c_pad = jnp.zeros((SUBLANES, d), F32).at[:bsz].set(c)
    mods = _ada_call(c_pad, w_ada, b_ada)[:, :bsz]
    cst = _Consts(seq)
    x2 = x.reshape(n, d)
    for l in range(depth):
        sh1, sc1, g1, sh2, sc2, g2 = (mods[l][:, i * d:(i + 1) * d].reshape(bsz, 1, d) for i in range(6))
        x2 = _token_mixer_layer(x2, sc1, sh1, g1, w_in[l], cmp_pe[l], cmp_w1[l], cmp_w2[l], w_pool[l],
                                pool_scale[l], w_lift[l], w_o[l], ln_g[l, 0], ln_b[l, 0], cst, bsz, seq, alpha)
        x2 = _peer_layer(x2, sc2, sh2, g2, peer_wq[l], peer_keys[l], peer_u[l], peer_v[l],
                         ln_g[l, 1], ln_b[l, 1], seq, alpha, sc_tokens=SC_TOKEN_SHARE * n // 16)
    return x2.reshape(bsz, seq, d)
```

```python
import dataclasses
import functools

import jax
import jax.numpy as jnp
import numpy as np
from jax import lax
from jax.experimental import pallas as pl
from jax.experimental.pallas import tpu as pltpu
from jax.experimental.pallas import tpu_sc as plsc

F32 = jnp.float32
BF16 = jnp.bfloat16
I32 = jnp.int32
HI = lax.Precision.HIGHEST

D_MODEL = 1024
N_HEADS = 8
HEAD_DIM = 64
N_KV = 2
HPG = N_HEADS // N_KV
ROT_DIM = HEAD_DIM // 4
ROT_HALF = ROT_DIM // 2
ROPE_THETA = 500000.0
CMP_LEN = 32
CMP_STRIDE = 16
SLC_LEN = 64
SLC_TOPN = 16
WINDOW = 512
SCALE = HEAD_DIM ** -0.5
NEG = -1e30
FORCE_INIT = 1e6
FORCE_LOCAL = 2e6
POOL_GROUPS = 4
POOL_WINDOWS = (2, 4, 8, 16)
POOL_WIDTH = 512
POOL_GW = POOL_WIDTH // POOL_GROUPS
POOL_HALO = 16
ATT_WIDTH = N_HEADS * HEAD_DIM
KV_WIDTH = 3 * 2 * N_KV * HEAD_DIM
GATE_NSA = 3 * N_HEADS
MERGE_GATES = 2 * D_MODEL
PEER_HEADS = 8
N_KEYS = 128
PEER_TOPK = 16
PEER_DK = 128
PEER_SEL = PEER_HEADS * PEER_TOPK
LN_EPS = 1e-5

LANES = 128
SUBLANES = 8
VMEM_LIMIT = 56 * 1024 * 1024

NT_DIMS = (((1,), (1,)), ((), ()))


def _cparams(sem):
    return pltpu.CompilerParams(dimension_semantics=sem, vmem_limit_bytes=VMEM_LIMIT)


def _ada_kernel(c_ref, w_ref, b_ref, o_ref):
    c = c_ref[...]
    ca = c * jax.nn.sigmoid(c)
    o_ref[...] = jnp.dot(ca, w_ref[...], precision=HI, preferred_element_type=F32) + b_ref[...]


def _ada_call(c_pad, w_ada, b_ada):
    depth = w_ada.shape[0]
    nblk = w_ada.shape[2] // D_MODEL
    rows = c_pad.shape[0]
    return pl.pallas_call(
        _ada_kernel,
        grid=(depth, nblk),
        in_specs=[
            pl.BlockSpec((rows, D_MODEL), lambda l, j: (0, 0)),
            pl.BlockSpec((None, D_MODEL, D_MODEL), lambda l, j: (l, 0, j)),
            pl.BlockSpec((None, 1, D_MODEL), lambda l, j: (l, 0, j)),
        ],
        out_specs=pl.BlockSpec((None, rows, D_MODEL), lambda l, j: (l, 0, j)),
        out_shape=jax.ShapeDtypeStruct((depth, rows, nblk * D_MODEL), F32),
        compiler_params=_cparams(("arbitrary", "arbitrary")),
        name="ada_mod",
    )(c_pad, w_ada, b_ada.reshape(depth, 1, -1))


IN_COLS = ATT_WIDTH + KV_WIDTH + POOL_WIDTH + MERGE_GATES + LANES


def _rope_lanes(z, rc, rs1, rs2):
    return z * rc + pltpu.roll(z, ROT_HALF, 1) * rs1 + pltpu.roll(z, LANES - ROT_HALF, 1) * rs2


def _inproj_kernel(x_ref, sc_ref, sh_ref, w_ref, rc_ref, rs1_ref, rs2_ref,
                   q_ref, kv_ref, p_ref, mrg_ref, gn_ref):
    h = x_ref[...] * (1.0 + sc_ref[...]) + sh_ref[...]
    a = jnp.dot(h.astype(BF16), w_ref[...], preferred_element_type=F32)
    rc, rs1, rs2 = rc_ref[...], rs1_ref[...], rs2_ref[...]
    for j in range(ATT_WIDTH // LANES):
        q_ref[:, j * LANES:(j + 1) * LANES] = _rope_lanes(a[:, j * LANES:(j + 1) * LANES], rc, rs1, rs2)
    for br in range(3):
        c0 = ATT_WIDTH + br * 2 * LANES
        k = a[:, c0:c0 + LANES]
        if br > 0:
            k = _rope_lanes(k, rc, rs1, rs2)
        kv_ref[:, br * 2 * LANES:br * 2 * LANES + LANES] = k
        kv_ref[:, br * 2 * LANES + LANES:(br + 1) * 2 * LANES] = a[:, c0 + LANES:c0 + 2 * LANES]
    c1 = ATT_WIDTH + KV_WIDTH
    p_ref[...] = a[:, c1:c1 + POOL_WIDTH]
    mrg_ref[...] = a[:, c1 + POOL_WIDTH:c1 + POOL_WIDTH + MERGE_GATES]
    gn_ref[...] = a[:, c1 + POOL_WIDTH + MERGE_GATES:]


def _inproj_call(x2, sc, sh, w, rc, rs1, rs2, seq, tm):
    n = x2.shape[0]
    tpb = seq // tm
    tok = lambda i: (i, 0)
    bat = lambda i: (i // tpb, 0, 0)
    pos = lambda i: (i % tpb, 0)
    full = lambda i: (0, 0)
    return pl.pallas_call(
        _inproj_kernel,
        grid=(n // tm,),
        in_specs=[
            pl.BlockSpec((tm, D_MODEL), tok),
            pl.BlockSpec((None, 1, D_MODEL), bat),
            pl.BlockSpec((None, 1, D_MODEL), bat),
            pl.BlockSpec((D_MODEL, IN_COLS), full),
            pl.BlockSpec((tm, LANES), pos),
            pl.BlockSpec((tm, LANES), pos),
            pl.BlockSpec((tm, LANES), pos),
        ],
        out_specs=[
            pl.BlockSpec((tm, ATT_WIDTH), tok),
            pl.BlockSpec((tm, KV_WIDTH), tok),
            pl.BlockSpec((tm, POOL_WIDTH), tok),
            pl.BlockSpec((tm, MERGE_GATES), tok),
            pl.BlockSpec((tm, LANES), tok),
        ],
        out_shape=[
            jax.ShapeDtypeStruct((n, ATT_WIDTH), F32),
            jax.ShapeDtypeStruct((n, KV_WIDTH), F32),
            jax.ShapeDtypeStruct((n, POOL_WIDTH), F32),
            jax.ShapeDtypeStruct((n, MERGE_GATES), F32),
            jax.ShapeDtypeStruct((n, LANES), F32),
        ],
        compiler_params=_cparams(("arbitrary",)),
        name="in_proj",
    )(x2, sc, sh, w, rc, rs1, rs2)


def _compress_kernel(z_ref, pet_ref, peb_ref, w1t_ref, w1b_ref, w2_ref, rc_ref, rs1_ref, rs2_ref, o_ref):
    z = z_ref[...]
    rows = z.shape[0]
    top = jnp.dot(z + pet_ref[...], w1t_ref[...], precision=HI, preferred_element_type=F32)
    bot = jnp.dot(z + peb_ref[...], w1b_ref[...], precision=HI, preferred_element_type=F32)
    pre = top + pltpu.roll(bot, rows - 1, 0)
    y = jnp.dot(jax.nn.gelu(pre), w2_ref[...], precision=HI, preferred_element_type=F32)
    o_ref[...] = _rope_lanes(y, rc_ref[...], rs1_ref[...], rs2_ref[...])


def _compress_call(z, pet, peb, w1t, w1b, w2, rc, rs1, rs2):
    b, _, rows, width = z.shape
    kvsel = lambda i, j: (j, 0, 0)
    return pl.pallas_call(
        _compress_kernel,
        grid=(b, 2),
        in_specs=[
            pl.BlockSpec((None, None, rows, width), lambda i, j: (i, j, 0, 0)),
            pl.BlockSpec((None, 1, width), kvsel),
            pl.BlockSpec((None, 1, width), kvsel),
            pl.BlockSpec((None, width, LANES), kvsel),
            pl.BlockSpec((None, width, LANES), kvsel),
            pl.BlockSpec((None, LANES, LANES), kvsel),
            pl.BlockSpec((None, rows, LANES), kvsel),
            pl.BlockSpec((None, rows, LANES), kvsel),
            pl.BlockSpec((None, rows, LANES), kvsel),
        ],
        out_specs=pl.BlockSpec((None, None, rows, LANES), lambda i, j: (i, j, 0, 0)),
        out_shape=jax.ShapeDtypeStruct((b, 2, rows, LANES), F32),
        compiler_params=_cparams(("arbitrary", "arbitrary")),
        name="compress",
    )(z, pet, peb, w1t, w1b, w2, rc, rs1, rs2)


def _cmpsel_kernel(q_ref, kc_ref, vc_ref, c2s_ref, o_ref, sel_ref, *, tq):
    t0 = pl.program_id(1) * tq
    kc = kc_ref[...]
    vc = vc_ref[...]
    rows = kc.shape[0]
    trow = t0 + lax.broadcasted_iota(I32, (tq, rows), 0)
    cend = lax.broadcasted_iota(I32, (tq, rows), 1) * CMP_STRIDE + (CMP_LEN - 1)
    vis = cend <= trow
    anyv = (trow[:, :1] >= CMP_LEN - 1).astype(F32)
    imp = jnp.zeros((tq, LANES), F32)
    for g in range(N_KV):
        kg = kc[:, g * HEAD_DIM:(g + 1) * HEAD_DIM]
        vg = vc[:, g * HEAD_DIM:(g + 1) * HEAD_DIM].astype(BF16)
        psum = jnp.zeros((tq, rows), F32)
        for h in range(HPG):
            hd = g * HPG + h
            qh = q_ref[:, hd * HEAD_DIM:(hd + 1) * HEAD_DIM]
            s = lax.dot_general(qh, kg, NT_DIMS, precision=HI, preferred_element_type=F32) * SCALE
            s = jnp.where(vis, s, NEG)
            e = jnp.exp(s - jnp.max(s, axis=-1, keepdims=True))
            p = e / jnp.sum(e, axis=-1, keepdims=True) * anyv
            o_ref[:, hd * HEAD_DIM:(hd + 1) * HEAD_DIM] = jnp.dot(
                p.astype(BF16), vg, preferred_element_type=F32)
            psum = psum + p
        imp = imp + jnp.dot(psum, c2s_ref[g], precision=HI, preferred_element_type=F32)
    lane = lax.broadcasted_iota(I32, (tq, LANES), 1)
    blk = lane & (SLC_LEN - 1)
    cur = lax.shift_right_logical(t0 + lax.broadcasted_iota(I32, (tq, LANES), 0), 6)
    score = jnp.where(blk <= cur, imp, NEG)
    score = jnp.where(blk == 0, FORCE_INIT, score)
    score = jnp.where(blk == cur, FORCE_LOCAL, score)
    sc_t = score.T
    nblk = LANES // N_KV
    jrow = lax.broadcasted_iota(I32, (nblk, tq), 0)
    sel_parts = []
    for g in range(N_KV):
        sc = sc_t[g * nblk:(g + 1) * nblk]
        cnt = jnp.zeros((nblk, tq), I32)
        for k in range(nblk):
            rk = sc[k:k + 1, :]
            ge = (rk >= sc).astype(I32)
            gt = (rk > sc).astype(I32)
            cnt = cnt + jnp.where(jrow > k, ge, gt)
        sel_parts.append((cnt < SLC_TOPN).astype(F32))
    sel_ref[...] = jnp.concatenate(sel_parts, axis=0).T


def _cmpsel_call(q, kvc, c2s, seq, tq):
    n = q.shape[0]
    b = n // seq
    nq = seq // tq
    rows = kvc.shape[2]
    tok = lambda i, j: (i * nq + j, 0)
    return pl.pallas_call(
        functools.partial(_cmpsel_kernel, tq=tq),
        grid=(b, nq),
        in_specs=[
            pl.BlockSpec((tq, ATT_WIDTH), tok),
            pl.BlockSpec((None, None, rows, LANES), lambda i, j: (i, 0, 0, 0)),
            pl.BlockSpec((None, None, rows, LANES), lambda i, j: (i, 1, 0, 0)),
            pl.BlockSpec((N_KV, rows, LANES), lambda i, j: (0, 0, 0)),
        ],
        out_specs=[pl.BlockSpec((tq, ATT_WIDTH), tok), pl.BlockSpec((tq, LANES), tok)],
        out_shape=[jax.ShapeDtypeStruct((n, ATT_WIDTH), F32), jax.ShapeDtypeStruct((n, LANES), F32)],
        compiler_params=_cparams(("arbitrary", "arbitrary")),
        name="cmp_select",
    )(q, kvc, kvc, c2s)


def _slc_kernel(q_ref, k_ref, v_ref, sel_ref, o_ref, qs_sc, m_sc, l_sc, acc_sc, *, tq, tk):
    qi = pl.program_id(1)
    kt = pl.program_id(2)
    nk = pl.num_programs(2)

    @pl.when(kt == 0)
    def _init():
        for hd in range(N_HEADS):
            g, h = divmod(hd, HPG)
            qs_sc[g, h * tq:(h + 1) * tq, :] = (q_ref[:, hd * HEAD_DIM:(hd + 1) * HEAD_DIM] * SCALE).astype(BF16)
        m_sc[...] = jnp.full(m_sc.shape, NEG, F32)
        l_sc[...] = jnp.zeros(l_sc.shape, F32)
        acc_sc[...] = jnp.zeros(acc_sc.shape, F32)

    @pl.when(kt * tk <= qi * tq + (tq - 1))
    def _step():
        t = qi * tq + lax.broadcasted_iota(I32, (tq, tk), 0)
        kp = kt * tk + lax.broadcasted_iota(I32, (tq, tk), 1)
        causal = kp <= t
        nblk = LANES // N_KV
        jb = lax.broadcasted_iota(I32, (nblk, tk), 0)
        kb = lax.shift_right_logical(kt * tk + lax.broadcasted_iota(I32, (nblk, tk), 1), 6)
        expand = (jb == kb).astype(BF16)
        for g in range(N_KV):
            selg = sel_ref[:, g * nblk:(g + 1) * nblk].astype(BF16)
            member = jnp.dot(selg, expand, preferred_element_type=F32)
            bias = jnp.where(jnp.logical_and(causal, member > 0.5), 0.0, NEG)
            bias = jnp.concatenate([bias] * HPG, axis=0)
            kg = k_ref[:, g * HEAD_DIM:(g + 1) * HEAD_DIM].astype(BF16)
            vg = v_ref[:, g * HEAD_DIM:(g + 1) * HEAD_DIM].astype(BF16)
            s = lax.dot_general(qs_sc[g], kg, NT_DIMS, preferred_element_type=F32) + bias
            chunks = [s[:, c * LANES:(c + 1) * LANES] for c in range(tk // LANES)]
            mc = chunks[0]
            for x in chunks[1:]:
                mc = jnp.maximum(mc, x)
            m_old = m_sc[g]
            m_new = jnp.maximum(m_old, jnp.max(mc, axis=-1, keepdims=True))
            alpha = jnp.exp(m_old - m_new)
            ps = [jnp.exp(x - m_new) for x in chunks]
            lsum = ps[0]
            for x in ps[1:]:
                lsum = lsum + x
            l_sc[g] = alpha * l_sc[g] + lsum
            p = jnp.concatenate(ps, axis=1).astype(BF16)
            acc_sc[g] = alpha[:, :HEAD_DIM] * acc_sc[g] + jnp.dot(p, vg, preferred_element_type=F32)
            m_sc[g] = m_new

    @pl.when(kt == nk - 1)
    def _fin():
        for hd in range(N_HEADS):
            g, h = divmod(hd, HPG)
            l = jnp.sum(l_sc[g, h * tq:(h + 1) * tq, :], axis=-1, keepdims=True)
            o_ref[:, hd * HEAD_DIM:(hd + 1) * HEAD_DIM] = acc_sc[g, h * tq:(h + 1) * tq, :] / l


def _slc_call(q, kv, sel, seq, tq, tk):
    n = q.shape[0]
    b = n // seq
    nq = seq // tq
    nk = seq // tk
    tok = lambda i, j, k: (i * nq + j, 0)

    def key_map(col):
        def f(i, j, k):
            last = (j * tq + tq - 1) // tk
            return (i * nk + jnp.minimum(k, last), col)
        return f

    return pl.pallas_call(
        functools.partial(_slc_kernel, tq=tq, tk=tk),
        grid=(b, nq, nk),
        in_specs=[
            pl.BlockSpec((tq, ATT_WIDTH), tok),
            pl.BlockSpec((tk, LANES), key_map(2)),
            pl.BlockSpec((tk, LANES), key_map(3)),
            pl.BlockSpec((tq, LANES), tok),
        ],
        out_specs=pl.BlockSpec((tq, ATT_WIDTH), tok),
        out_shape=jax.ShapeDtypeStruct((n, ATT_WIDTH), F32),
        scratch_shapes=[
            pltpu.VMEM((N_KV, HPG * tq, HEAD_DIM), BF16),
            pltpu.VMEM((N_KV, HPG * tq, LANES), F32),
            pltpu.VMEM((N_KV, HPG * tq, LANES), F32),
            pltpu.VMEM((N_KV, HPG * tq, HEAD_DIM), F32),
        ],
        compiler_params=_cparams(("arbitrary", "arbitrary", "arbitrary")),
        name="slc_attn",
    )(q, kv, kv, sel)


def _win_kernel(q_ref, *refs, tq, nkb):
    k_refs = refs[:nkb]
    v_refs = refs[nkb:2 * nkb]
    o_ref = refs[2 * nkb]
    qi = pl.program_id(1)
    t = qi * tq + lax.broadcasted_iota(I32, (tq, tq), 0)
    col = lax.broadcasted_iota(I32, (tq, tq), 1)
    oks = []
    for j in range(nkb):
        kp = (qi - (nkb - 1) + j) * tq + col
        diff = t - kp
        oks.append(jnp.logical_and(jnp.logical_and(diff >= 0, diff < WINDOW), kp >= 0))
    for g in range(N_KV):
        kgs = [k_refs[j][:, g * HEAD_DIM:(g + 1) * HEAD_DIM].astype(BF16) for j in range(nkb)]
        vgs = [v_refs[j][:, g * HEAD_DIM:(g + 1) * HEAD_DIM].astype(BF16) for j in range(nkb)]
        for h in range(HPG):
            hd = g * HPG + h
            qh = (q_ref[:, hd * HEAD_DIM:(hd + 1) * HEAD_DIM] * SCALE).astype(BF16)
            ss = [jnp.where(oks[j], lax.dot_general(qh, kgs[j], NT_DIMS, preferred_element_type=F32), NEG)
                  for j in range(nkb)]
            m = jnp.max(ss[0], axis=-1, keepdims=True)
            for j in range(1, nkb):
                m = jnp.maximum(m, jnp.max(ss[j], axis=-1, keepdims=True))
            ps = [jnp.exp(s - m) for s in ss]
            l = jnp.sum(ps[0], axis=-1, keepdims=True)
            o = jnp.dot(ps[0].astype(BF16), vgs[0], preferred_element_type=F32)
            for j in range(1, nkb):
                l = l + jnp.sum(ps[j], axis=-1, keepdims=True)
                o = o + jnp.dot(ps[j].astype(BF16), vgs[j], preferred_element_type=F32)
            o_ref[:, hd * HEAD_DIM:(hd + 1) * HEAD_DIM] = o / l


def _win_call(q, kv, seq, tq):
    n = q.shape[0]
    b = n // seq
    nq = seq // tq
    nkb = WINDOW // tq + 1
    tok = lambda i, j: (i * nq + j, 0)

    def key_map(col, back):
        return lambda i, j: (i * nq + jnp.maximum(j - back, 0), col)

    k_specs = [pl.BlockSpec((tq, LANES), key_map(4, nkb - 1 - jj)) for jj in range(nkb)]
    v_specs = [pl.BlockSpec((tq, LANES), key_map(5, nkb - 1 - jj)) for jj in range(nkb)]
    return pl.pallas_call(
        functools.partial(_win_kernel, tq=tq, nkb=nkb),
        grid=(b, nq),
        in_specs=[pl.BlockSpec((tq, ATT_WIDTH), tok)] + k_specs + v_specs,
        out_specs=pl.BlockSpec((tq, ATT_WIDTH), tok),
        out_shape=jax.ShapeDtypeStruct((n, ATT_WIDTH), F32),
        compiler_params=_cparams(("arbitrary", "arbitrary")),
        name="win_attn",
    )(q, *([kv] * (2 * nkb)))


def _pool_kernel(p_ref, prev_ref, w_ref, sc_ref, o_ref, *, ts):
    i = pl.program_id(1)
    x = p_ref[...]
    prev = prev_ref[...] * (i > 0).astype(F32)
    xe = jnp.concatenate([prev, x], axis=0)
    t1 = (i * ts + 1 + lax.broadcasted_iota(I32, (ts, POOL_GW), 0)).astype(F32)
    for g, w in enumerate(POOL_WINDOWS):
        a = xe[:, g * POOL_GW:(g + 1) * POOL_GW]
        off = POOL_HALO
        span = 1
        while span < w:
            a = a[span:] + a[:-span]
            off -= span
            span *= 2
        sums = a[off:off + ts]
        cnt = jnp.minimum(t1, float(w))
        pooled = sums / cnt - x[:, g * POOL_GW:(g + 1) * POOL_GW]
        y = jnp.dot(pooled.astype(BF16), w_ref[g], preferred_element_type=F32)
        o_ref[:, g * POOL_GW:(g + 1) * POOL_GW] = y * sc_ref[:, g * POOL_GW:(g + 1) * POOL_GW]


def _pool_call(p_in, w_pool, pool_scale, seq, ts):
    n = p_in.shape[0]
    b = n // seq
    nt = seq // ts
    hpt = ts // POOL_HALO
    tok = lambda i, j: (i * nt + j, 0)
    return pl.pallas_call(
        functools.partial(_pool_kernel, ts=ts),
        grid=(b, nt),
        in_specs=[
            pl.BlockSpec((ts, POOL_WIDTH), tok),
            pl.BlockSpec((POOL_HALO, POOL_WIDTH), lambda i, j: (i * nt * hpt + jnp.maximum(j * hpt - 1, 0), 0)),
            pl.BlockSpec((POOL_GROUPS, POOL_GW, POOL_GW), lambda i, j: (0, 0, 0)),
            pl.BlockSpec((1, POOL_WIDTH), lambda i, j: (0, 0)),
        ],
        out_specs=pl.BlockSpec((ts, POOL_WIDTH), tok),
        out_shape=jax.ShapeDtypeStruct((n, POOL_WIDTH), F32),
        compiler_params=_cparams(("arbitrary", "arbitrary")),
        name="pool_mix",
    )(p_in, p_in, w_pool, pool_scale)


def _layer_norm(z, g, b):
    mu = jnp.mean(z, axis=-1, keepdims=True)
    zc = z - mu
    var = jnp.mean(zc * zc, axis=-1, keepdims=True)
    return zc * lax.rsqrt(var + LN_EPS) * g + b


def _merge_kernel(oc_ref, os_ref, ow_ref, gn_ref, op_ref, gm_ref, x_ref, g1_ref, lng_ref, lnb_ref,
                  wl_ref, wo_ref, eb_ref, o_ref, *, alpha):
    gate = jax.nn.sigmoid(gn_ref[...])
    branches = (oc_ref, os_ref, ow_ref)
    oatt = None
    for br in range(3):
        gx = jnp.dot(gate, eb_ref[br], precision=HI, preferred_element_type=F32)
        term = gx * branches[br][...]
        oatt = term if oatt is None else oatt + term
    la = jnp.dot(oatt.astype(BF16), wl_ref[0], preferred_element_type=F32)
    lb = jnp.dot(op_ref[...].astype(BF16), wl_ref[1], preferred_element_type=F32)
    gm = jax.nn.sigmoid(gm_ref[...])
    merged = gm[:, :D_MODEL] * la + gm[:, D_MODEL:] * lb
    y = jnp.dot(merged.astype(BF16), wo_ref[...], preferred_element_type=F32)
    z = alpha * x_ref[...] + g1_ref[...] * y
    o_ref[...] = _layer_norm(z, lng_ref[...], lnb_ref[...])


def _merge_call(oc, osl, ow, gn, op, gm, x2, g1, lng, lnb, wl, wo, eb, seq, tm, alpha):
    n = x2.shape[0]
    tpb = seq // tm
    tok = lambda i: (i, 0)
    bat = lambda i: (i // tpb, 0, 0)
    return pl.pallas_call(
        functools.partial(_merge_kernel, alpha=alpha),
        grid=(n // tm,),
        in_specs=[
            pl.BlockSpec((tm, ATT_WIDTH), tok), pl.BlockSpec((tm, ATT_WIDTH), tok),
            pl.BlockSpec((tm, ATT_WIDTH), tok), pl.BlockSpec((tm, LANES), tok),
            pl.BlockSpec((tm, POOL_WIDTH), tok), pl.BlockSpec((tm, MERGE_GATES), tok),
            pl.BlockSpec((tm, D_MODEL), tok),
            pl.BlockSpec((None, 1, D_MODEL), bat),
            pl.BlockSpec((1, D_MODEL), lambda i: (0, 0)), pl.BlockSpec((1, D_MODEL), lambda i: (0, 0)),
            pl.BlockSpec((2, ATT_WIDTH, D_MODEL), lambda i: (0, 0, 0)),
            pl.BlockSpec((D_MODEL, D_MODEL), lambda i: (0, 0)),
            pl.BlockSpec((3, LANES, ATT_WIDTH), lambda i: (0, 0, 0)),
        ],
        out_specs=pl.BlockSpec((tm, D_MODEL), tok),
        out_shape=jax.ShapeDtypeStruct((n, D_MODEL), F32),
        compiler_params=_cparams(("arbitrary",)),
        name="merge_out",
    )(oc, osl, ow, gn, op, gm, x2, g1, lng, lnb, wl, wo, eb)


def _extract_top(cur, ids, n):
    rows = cur.shape[0]
    rio = lax.broadcasted_iota(I32, cur.shape, 0)
    vals, outs = [], []
    for _ in range(n):
        m = jnp.max(cur, axis=0, keepdims=True)
        pos = jnp.min(jnp.where(cur == m, rio, rows), axis=0, keepdims=True)
        hit = rio == pos
        vals.append(m)
        outs.append(pos if ids is None else jnp.max(jnp.where(hit, ids, -1), axis=0, keepdims=True))
        cur = jnp.where(hit, -jnp.inf, cur)
    return jnp.concatenate(vals, axis=0), jnp.concatenate(outs, axis=0)


def _route_kernel(x_ref, sc_ref, sh_ref, wq_ref, keys_ref, h_ref, e_ref, g_ref,
                  st_sc, ts_sc, ti_sc, eo_sc, go_sc):
    h = x_ref[...] * (1.0 + sc_ref[...]) + sh_ref[...]
    h_ref[...] = h
    qp = jnp.dot(h.astype(BF16), wq_ref[...], preferred_element_type=F32).astype(BF16)
    half = PEER_DK // 2
    for hp in range(2 * PEER_HEADS):
        st_sc[hp] = lax.dot_general(keys_ref[hp], qp[:, hp * half:(hp + 1) * half], NT_DIMS,
                                    preferred_element_type=F32)

    def half_body(hp, carry):
        vals, ids = _extract_top(st_sc[hp], None, PEER_TOPK)
        ts_sc[hp] = vals
        ti_sc[hp] = ids
        return carry

    lax.fori_loop(0, 2 * PEER_HEADS, half_body, 0)

    def head_body(hh, carry):
        s1, s2 = ts_sc[2 * hh], ts_sc[2 * hh + 1]
        i1, i2 = ti_sc[2 * hh], ti_sc[2 * hh + 1]
        brow = lax.broadcasted_iota(I32, (SUBLANES, s1.shape[1]), 0)
        cands = [s1[0:1, :] + s2]
        cidxs = [i1[0:1, :] * N_KEYS + i2]
        for a in range(1, SUBLANES):
            ok = brow < PEER_TOPK // (a + 1)
            cands.append(jnp.where(ok, s1[a:a + 1, :] + s2[:SUBLANES], -jnp.inf))
            cidxs.append(i1[a:a + 1, :] * N_KEYS + i2[:SUBLANES])
        cands.append(s1[SUBLANES:] + s2[0:1, :])
        cidxs.append(i1[SUBLANES:] * N_KEYS + i2[0:1, :])
        sv, ei = _extract_top(jnp.concatenate(cands, axis=0), jnp.concatenate(cidxs, axis=0), PEER_TOPK)
        ex = jnp.exp(sv - sv[0:1, :])
        go_sc[hh] = ex / jnp.sum(ex, axis=0, keepdims=True)
        eo_sc[hh] = ei.astype(F32)
        return carry

    lax.fori_loop(0, PEER_HEADS, head_body, 0)
    e_all = jnp.concatenate([eo_sc[hh] for hh in range(PEER_HEADS)], axis=0)
    g_all = jnp.concatenate([go_sc[hh] for hh in range(PEER_HEADS)], axis=0)
    e_ref[...] = e_all.T.astype(I32) * (D_MODEL // 2 // LANES)
    g_ref[...] = g_all.T


def _route_call(x2, sc, sh, wq, keys, seq, tt):
    n = x2.shape[0]
    tpb = seq // tt
    tok = lambda i: (i, 0)
    bat = lambda i: (i // tpb, 0, 0)
    nhp = 2 * PEER_HEADS
    return pl.pallas_call(
        _route_kernel,
        grid=(n // tt,),
        in_specs=[
            pl.BlockSpec((tt, D_MODEL), tok),
            pl.BlockSpec((None, 1, D_MODEL), bat),
            pl.BlockSpec((None, 1, D_MODEL), bat),
            pl.BlockSpec((D_MODEL, PEER_HEADS * PEER_DK), lambda i: (0, 0)),
            pl.BlockSpec((nhp, N_KEYS, PEER_DK // 2), lambda i: (0, 0, 0)),
        ],
        out_specs=[pl.BlockSpec((tt, D_MODEL), tok), pl.BlockSpec((tt, PEER_SEL), tok),
                   pl.BlockSpec((tt, PEER_SEL), tok)],
        out_shape=[jax.ShapeDtypeStruct((n, D_MODEL), F32), jax.ShapeDtypeStruct((n, PEER_SEL), I32),
                   jax.ShapeDtypeStruct((n, PEER_SEL), F32)],
        scratch_shapes=[
            pltpu.VMEM((nhp, N_KEYS, tt), F32),
            pltpu.VMEM((nhp, PEER_TOPK, tt), F32),
            pltpu.VMEM((nhp, PEER_TOPK, tt), I32),
            pltpu.VMEM((PEER_HEADS, PEER_TOPK, tt), F32),
            pltpu.VMEM((PEER_HEADS, PEER_TOPK, tt), F32),
        ],
        compiler_params=_cparams(("arbitrary",)),
        name="peer_route",
    )(x2, sc, sh, wq, keys)


HALF_ROWS = SUBLANES // 2
HI_MASK = -65536
PAIR_TILES = PEER_SEL // 2
PAIR_ROWS = PAIR_TILES * SUBLANES


def _load_two_experts(tab_ref, ra, rb):
    wa = tab_ref[pl.ds(pl.multiple_of(ra, HALF_ROWS), HALF_ROWS), :]
    wb = tab_ref[pl.ds(pl.multiple_of(rb, HALF_ROWS), HALF_ROWS), :]
    w2 = jnp.concatenate([wa, wb], axis=0)
    return lax.bitcast_convert_type(w2 << 16, F32), lax.bitcast_convert_type(w2 & HI_MASK, F32)


def _fold_pairs(vs):
    row = lax.broadcasted_iota(I32, (SUBLANES, LANES), 0)
    shift = HALF_ROWS // 2
    while len(vs) > 1:
        low = (row & shift) == 0
        vs = [jnp.where(low, a + pltpu.roll(a, SUBLANES - shift, 0), b + pltpu.roll(b, shift, 0))
              for a, b in zip(vs[0::2], vs[1::2])]
        shift //= 2
    return vs[0]


def _fold_order():
    idx = [[2 * i if r < HALF_ROWS else 2 * i + 1 for r in range(SUBLANES)] for i in range(HALF_ROWS)]
    shift = HALF_ROWS // 2
    while len(idx) > 1:
        idx = [[a[r] if (r & shift) == 0 else b[r] for r in range(SUBLANES)]
               for a, b in zip(idx[0::2], idx[1::2])]
        shift //= 2
    return idx[0]


def _peer_u_kernel(e_sm, tab_ref, hv_ref, gate_ref, coef_ref, *, tt):
    row = lax.broadcasted_iota(I32, (SUBLANES, LANES), 0)
    low = row < HALF_ROWS
    eye = (lax.broadcasted_iota(I32, (PEER_SEL, LANES), 0) ==
           lax.broadcasted_iota(I32, (PEER_SEL, LANES), 1))
    order = _fold_order()

    def finish(t, part):
        col = jnp.sum(part, axis=-1, keepdims=True)
        a_row = jnp.sum(jnp.where(eye, col, 0.0), axis=0, keepdims=True)
        coef_ref[t] = gate_ref[t] * jax.nn.gelu(a_row)

    def token(t, part_prev):
        finish(jnp.maximum(t - 1, 0), part_prev)
        hv = hv_ref[t]
        hsw = pltpu.roll(hv, HALF_ROWS, 0)
        h_lo = jnp.where(low, hv, hsw)
        h_hi = jnp.where(low, hsw, hv)
        folded = []
        for j in range(PEER_SEL // SUBLANES):
            prods = []
            for i in range(HALF_ROWS):
                ka = j * SUBLANES + order.index(2 * i)
                kb = j * SUBLANES + order.index(2 * i + 1)
                lo, hi = _load_two_experts(tab_ref, e_sm[t, ka], e_sm[t, kb])
                prods.append(lo * h_lo + hi * h_hi)
            folded.append(_fold_pairs(prods))
        return jnp.concatenate(folded, axis=0)

    last = lax.fori_loop(0, tt, token, jnp.zeros((PEER_SEL, LANES), F32))
    finish(tt - 1, last)


def _peer_v_kernel(e_sm, coef_ref, tab_ref, y_ref, cv_sc, *, tt, nacc):
    row = lax.broadcasted_iota(I32, (SUBLANES, LANES), 0)
    low = row < HALF_ROWS
    rr = lax.broadcasted_iota(I32, (PAIR_ROWS, LANES), 0)
    kk = lax.broadcasted_iota(I32, (PAIR_ROWS, LANES), 1)
    onehot = (kk == 2 * (rr >> 3) + ((rr >> 2) & 1)).astype(F32)
    ones = jnp.ones((LANES, LANES), BF16)

    def expand(t, slot):
        lhs = (onehot * coef_ref[t]).astype(BF16)
        cv_sc[slot] = jnp.dot(lhs, ones, preferred_element_type=F32)

    def process(t, slot):
        acc_lo = [jnp.zeros((SUBLANES, LANES), F32) for _ in range(nacc)]
        acc_hi = [jnp.zeros((SUBLANES, LANES), F32) for _ in range(nacc)]
        for j in range(PAIR_TILES):
            lo, hi = _load_two_experts(tab_ref, e_sm[t, 2 * j], e_sm[t, 2 * j + 1])
            cv = cv_sc[slot, j * SUBLANES:(j + 1) * SUBLANES, :]
            acc_lo[j % nacc] = acc_lo[j % nacc] + cv * lo
            acc_hi[j % nacc] = acc_hi[j % nacc] + cv * hi
        a_lo, a_hi = acc_lo[0], acc_hi[0]
        for i in range(1, nacc):
            a_lo = a_lo + acc_lo[i]
            a_hi = a_hi + acc_hi[i]
        a_lo = a_lo + pltpu.roll(a_lo, HALF_ROWS, 0)
        a_hi = a_hi + pltpu.roll(a_hi, HALF_ROWS, 0)
        y_ref[t] = jnp.where(low, a_lo, a_hi)

    expand(0, 0)

    def two_tokens(i, carry):
        t = 2 * i
        expand(t + 1, 1)
        process(t, 0)
        expand(jnp.minimum(t + 2, tt - 1), 0)
        process(t + 1, 1)
        return carry

    lax.fori_loop(0, tt // 2, two_tokens, 0)


def _resident_table_spec(tab):
    return pl.BlockSpec(tab.shape, lambda i: (0, 0), pipeline_mode=pl.Buffered(1))


def _peer_u_call(erow, tab, hv, gate3, tt, n):
    return pl.pallas_call(
        functools.partial(_peer_u_kernel, tt=tt),
        grid=(n // tt,),
        in_specs=[
            pl.BlockSpec((tt, PEER_SEL), lambda i: (i, 0), memory_space=pltpu.SMEM),
            _resident_table_spec(tab),
            pl.BlockSpec((tt, SUBLANES, LANES), lambda i: (i, 0, 0)),
            pl.BlockSpec((tt, 1, PEER_SEL), lambda i: (i, 0, 0)),
        ],
        out_specs=pl.BlockSpec((tt, 1, PEER_SEL), lambda i: (i, 0, 0)),
        out_shape=jax.ShapeDtypeStruct((n, 1, PEER_SEL), F32),
        compiler_params=_cparams(("arbitrary",)),
        name="peer_u",
    )(erow, tab, hv, gate3)


def _peer_v_call(erow, coef3, tab, tt, n):
    assert tt % 2 == 0
    return pl.pallas_call(
        functools.partial(_peer_v_kernel, tt=tt, nacc=4),
        grid=(n // tt,),
        in_specs=[
            pl.BlockSpec((tt, PEER_SEL), lambda i: (i, 0), memory_space=pltpu.SMEM),
            pl.BlockSpec((tt, 1, PEER_SEL), lambda i: (i, 0, 0)),
            _resident_table_spec(tab),
        ],
        out_specs=pl.BlockSpec((tt, SUBLANES, LANES), lambda i: (i, 0, 0)),
        out_shape=jax.ShapeDtypeStruct((n, SUBLANES, LANES), F32),
        scratch_shapes=[pltpu.VMEM((2, PAIR_ROWS, LANES), F32)],
        compiler_params=_cparams(("arbitrary",)),
        name="peer_v",
    )(erow, coef3, tab)


def _resln_kernel(x_ref, y_ref, g_ref, lng_ref, lnb_ref, o_ref, *, alpha):
    z = alpha * x_ref[...] + g_ref[...] * y_ref[...]
    o_ref[...] = _layer_norm(z, lng_ref[...], lnb_ref[...])


def _resln_call(x2, y2, g2, lng, lnb, seq, tm, alpha):
    n = x2.shape[0]
    tpb = seq // tm
    tok = lambda i: (i, 0)
    return pl.pallas_call(
        functools.partial(_resln_kernel, alpha=alpha),
        grid=(n // tm,),
        in_specs=[
            pl.BlockSpec((tm, D_MODEL), tok), pl.BlockSpec((tm, D_MODEL), tok),
            pl.BlockSpec((None, 1, D_MODEL), lambda i: (i // tpb, 0, 0)),
            pl.BlockSpec((1, D_MODEL), lambda i: (0, 0)), pl.BlockSpec((1, D_MODEL), lambda i: (0, 0)),
        ],
        out_specs=pl.BlockSpec((tm, D_MODEL), tok),
        out_shape=jax.ShapeDtypeStruct((n, D_MODEL), F32),
        compiler_params=_cparams(("arbitrary",)),
        name="res_ln",
    )(x2, y2, g2, lng, lnb)


def _rope_lane_tables(pos):
    inv = ROPE_THETA ** (-jnp.arange(0, ROT_DIM, 2, dtype=F32) / ROT_DIM)
    ang = pos.astype(F32)[:, None] * inv[None, :]
    cos, sin = jnp.cos(ang), jnp.sin(ang)
    lane = np.arange(LANES) % HEAD_DIM
    fidx = lane % ROT_HALF
    first = jnp.asarray(lane < ROT_HALF)
    second = jnp.asarray((lane >= ROT_HALF) & (lane < ROT_DIM))
    rot = jnp.asarray(lane < ROT_DIM)
    cl, sl = cos[:, fidx], sin[:, fidx]
    rc = jnp.where(rot, cl, 1.0)
    rs1 = jnp.where(second, sl, 0.0)
    rs2 = jnp.where(first, -sl, 0.0)
    return rc, rs1, rs2


def _pack_kernel(t_ref, o_ref):
    x = t_ref[...]
    half = x.shape[1] // 2
    lo = lax.bitcast_convert_type(x[:, :half].astype(BF16).astype(F32), I32)
    hi = lax.bitcast_convert_type(x[:, half:].astype(BF16).astype(F32), I32)
    o_ref[...] = lax.shift_right_logical(lo, jnp.full_like(lo, 16)) | hi


def _pack_table(tab, te=512):
    e, d = tab.shape
    words = pl.pallas_call(
        _pack_kernel, grid=(e // te,),
        in_specs=[pl.BlockSpec((te, d), lambda i: (i, 0))],
        out_specs=pl.BlockSpec((te, d // 2), lambda i: (i, 0)),
        out_shape=jax.ShapeDtypeStruct((e, d // 2), I32),
        compiler_params=_cparams(("arbitrary",)), name="pack_table",
    )(tab)
    return words.reshape(e * (d // 2) // LANES, LANES)


def _cmp_to_slc_wide(rows, n_slc):
    st = np.arange(rows) * CMP_STRIDE
    js = np.arange(n_slc) * SLC_LEN
    ov = np.minimum(st[:, None] + CMP_LEN, js[None, :] + SLC_LEN) - np.maximum(st[:, None], js[None, :])
    c2s = np.maximum(ov, 0).astype(np.float32) / CMP_STRIDE
    wide = np.zeros((N_KV, rows, LANES), np.float32)
    nblk = LANES // N_KV
    for g in range(N_KV):
        wide[g, :, g * nblk:g * nblk + n_slc] = c2s
    return jnp.asarray(wide)


def _gate_expanders():
    eb = np.zeros((3, LANES, ATT_WIDTH), np.float32)
    for hd in range(N_HEADS):
        for br in range(3):
            eb[br, hd * 3 + br, hd * HEAD_DIM:(hd + 1) * HEAD_DIM] = 1.0
    return jnp.asarray(eb)


class _Consts:
    def __init__(self, seq):
        self.rows = seq // CMP_STRIDE
        self.rope = _rope_lane_tables(jnp.arange(seq))
        cpos = jnp.arange(self.rows) * CMP_STRIDE + CMP_LEN - 1
        crope = _rope_lane_tables(cpos)
        ident = (jnp.ones_like(crope[0]), jnp.zeros_like(crope[0]), jnp.zeros_like(crope[0]))
        self.crope = tuple(jnp.stack([a, b]) for a, b in zip(crope, ident))
        self.c2s = _cmp_to_slc_wide(self.rows, seq // SLC_LEN)
        self.eb = _gate_expanders()


def _token_mixer_layer(x2, sc1, sh1, g1, w_in, cmp_pe, cmp_w1, cmp_w2, w_pool, pool_scale, w_lift, w_o,
                       lng, lnb, cst, bsz, seq, alpha):
    d = D_MODEL
    rows = cst.rows
    s1 = ATT_WIDTH + KV_WIDTH
    s2 = s1 + GATE_NSA
    s3 = s2 + POOL_WIDTH
    w_gate = jnp.pad(w_in[:, s1:s2], ((0, 0), (0, LANES - GATE_NSA)))
    w_all = jnp.concatenate([w_in[:, :s1], w_in[:, s2:s3], w_in[:, s3:], w_gate], axis=1).astype(BF16)
    q, kv, p_in, g_mrg, g_nsa = _inproj_call(x2, sc1, sh1, w_all, *cst.rope, seq, 256)

    eye_g = jnp.eye(N_KV, dtype=F32)
    zc = kv[:, :2 * LANES].reshape(bsz, rows, CMP_STRIDE, 2, LANES)
    zc = jnp.transpose(zc, (0, 3, 1, 2, 4)).reshape(bsz, 2, rows, CMP_STRIDE * LANES)
    w1x = jnp.einsum('klde,gh->klgdhe', cmp_w1.reshape(2, CMP_LEN, HEAD_DIM, HEAD_DIM), eye_g)
    w1x = w1x.reshape(2, CMP_LEN * LANES, LANES)
    half = CMP_STRIDE * LANES
    pex = jnp.broadcast_to(cmp_pe[:, :, None, :], (2, CMP_LEN, N_KV, HEAD_DIM)).reshape(2, 1, CMP_LEN * LANES)
    w2x = jnp.einsum('kef,gh->kgehf', cmp_w2, eye_g).reshape(2, LANES, LANES)
    kvc = _compress_call(zc, pex[:, :, :half], pex[:, :, half:], w1x[:, :half], w1x[:, half:], w2x,
                         *cst.crope)

    o_cmp, sel = _cmpsel_call(q, kvc, cst.c2s, seq, 256)
    o_slc = _slc_call(q, kv, sel, seq, 256, 512)
    o_win = _win_call(q, kv, seq, 256)
    o_pool = _pool_call(p_in, w_pool.astype(BF16), pool_scale.reshape(1, -1), seq, 512)
    return _merge_call(o_cmp, o_slc, o_win, g_nsa, o_pool, g_mrg, x2, g1,
                       lng.reshape(1, d), lnb.reshape(1, d),
                       w_lift.astype(BF16), w_o.astype(BF16), cst.eb, seq, 256, alpha)


SC_CORES = 2
SC_SUBCORES = 16
SC_LANES = 16
SC_WORKERS = SC_CORES * SC_SUBCORES
SC_ROWS = 32
SC_TOKEN_SHARE = 5
SC_BLOCK = 256


def _sc_params():
    cp = pltpu.CompilerParams()
    if "needs_layout_passes" in pltpu.CompilerParams.__dataclass_fields__:
        cp = dataclasses.replace(cp, needs_layout_passes=False)
    return cp


def _sc_load_experts(erow_hbm, t, idx_v):
    pltpu.sync_copy(erow_hbm.at[t], idx_v)
    shift = jnp.full((SC_LANES,), (D_MODEL // 2 // LANES).bit_length() - 1, I32)
    for j in range(PEER_SEL // SC_LANES):
        sl = pl.ds(j * SC_LANES, SC_LANES)
        idx_v[sl] = lax.shift_right_logical(idx_v[sl], shift)


def _sc_gather_pipeline(tab_hbm, idx_v, rows_v, sems, nch, chunk_fn):
    def gather(c):
        return pltpu.async_copy(tab_hbm.at[idx_v.at[pl.ds(c * SC_ROWS, SC_ROWS)]],
                                rows_v.at[c % 2], sems.at[c % 2])

    pending = {0: gather(0)}
    for c in range(nch):
        if c + 1 < nch:
            pending[c + 1] = gather(c + 1)
        pending.pop(c).wait()
        chunk_fn(c, rows_v.at[c % 2])


def _sc_v_call(erow, coef, vtab, tok0):
    m = coef.shape[0]
    d = vtab.shape[1]
    assert m % SC_WORKERS == 0 and PEER_SEL % SC_ROWS == 0 and d % SC_BLOCK == 0
    tpw = m // SC_WORKERS
    nch = PEER_SEL // SC_ROWS
    nvec = SC_BLOCK // SC_LANES
    mesh = plsc.VectorSubcoreMesh(core_axis_name="c", subcore_axis_name="s")

    @functools.partial(
        pl.kernel, mesh=mesh, out_type=jax.ShapeDtypeStruct((m, d), F32),
        scratch_types=[pltpu.VMEM((PEER_SEL,), I32), pltpu.VMEM((PEER_SEL,), F32),
                       pltpu.VMEM((2, SC_ROWS, d), F32), pltpu.VMEM((d,), F32),
                       pltpu.SemaphoreType.DMA((2,))],
        compiler_params=_sc_params(), name="sc_peer_v")
    def run(tab_hbm, idx_hbm, coef_hbm, out_hbm, idx_v, coef_v, rows_v, acc_v, sems):
        wid = lax.axis_index("s") * SC_CORES + lax.axis_index("c")
        base = wid * tpw
        zero = jnp.zeros((SC_LANES,), F32)

        @pl.loop(0, tpw)
        def _(i):
            t = base + i
            _sc_load_experts(idx_hbm, tok0 + t, idx_v)
            pltpu.sync_copy(coef_hbm.at[t], coef_v)
            for j in range(d // SC_LANES):
                acc_v[pl.ds(j * SC_LANES, SC_LANES)] = zero

            def chunk(c, buf):
                @pl.loop(0, d // SC_BLOCK)
                def _(lb):
                    lane0 = pl.multiple_of(lb * SC_BLOCK, SC_BLOCK)
                    accs = tuple(acc_v[pl.ds(lane0 + q * SC_LANES, SC_LANES)] for q in range(nvec))

                    @plsc.parallel_loop(0, SC_ROWS, unroll=2, carry=accs)
                    def accs(r, acc):
                        ck = plsc.load_gather(coef_v, [jnp.full((SC_LANES,), c * SC_ROWS, I32) + r])
                        return tuple(acc[q] + ck * buf[r, pl.ds(lane0 + q * SC_LANES, SC_LANES)]
                                     for q in range(nvec))

                    for q in range(nvec):
                        acc_v[pl.ds(lane0 + q * SC_LANES, SC_LANES)] = accs[q]

            _sc_gather_pipeline(tab_hbm, idx_v, rows_v, sems, nch, chunk)
            pltpu.sync_copy(acc_v, out_hbm.at[t])

    return run(vtab, erow, coef)


def _sc_u_call(erow, hrows, utab, tok0, m):
    d = utab.shape[1]
    assert m % SC_WORKERS == 0 and PEER_SEL % SC_ROWS == 0 and d % SC_BLOCK == 0
    tpw = m // SC_WORKERS
    nch = PEER_SEL // SC_ROWS
    nvec = SC_BLOCK // SC_LANES
    mesh = plsc.VectorSubcoreMesh(core_axis_name="c", subcore_axis_name="s")

    @functools.partial(
        pl.kernel, mesh=mesh, out_type=jax.ShapeDtypeStruct((m, PEER_SEL), F32),
        scratch_types=[pltpu.VMEM((PEER_SEL,), I32), pltpu.VMEM((d,), F32),
                       pltpu.VMEM((2, SC_ROWS, d), F32), pltpu.VMEM((PEER_SEL * SC_LANES,), F32),
                       pltpu.VMEM((PEER_SEL,), F32), pltpu.SemaphoreType.DMA((2,))],
        compiler_params=_sc_params(), name="sc_peer_u")
    def run(tab_hbm, idx_hbm, h_hbm, out_hbm, idx_v, h_v, rows_v, part_v, a_v, sems):
        wid = lax.axis_index("s") * SC_CORES + lax.axis_index("c")
        base = wid * tpw
        zero = jnp.zeros((SC_LANES,), F32)
        lane_iota = lax.iota(I32, SC_LANES)

        @pl.loop(0, tpw)
        def _(i):
            t = base + i
            _sc_load_experts(idx_hbm, tok0 + t, idx_v)
            pltpu.sync_copy(h_hbm.at[tok0 + t], h_v)
            for j in range(PEER_SEL):
                part_v[pl.ds(j * SC_LANES, SC_LANES)] = zero

            def chunk(c, buf):
                @pl.loop(0, d // SC_BLOCK)
                def _(lb):
                    lane0 = pl.multiple_of(lb * SC_BLOCK, SC_BLOCK)
                    hs = [h_v[pl.ds(lane0 + q * SC_LANES, SC_LANES)] for q in range(nvec)]

                    @plsc.parallel_loop(0, SC_ROWS, unroll=2)
                    def _(r):
                        ps = [hs[q] * buf[r, pl.ds(lane0 + q * SC_LANES, SC_LANES)] for q in range(nvec)]
                        while len(ps) > 1:
                            ps = [x + y for x, y in zip(ps[0::2], ps[1::2])]
                        row = pl.multiple_of((c * SC_ROWS + r) * SC_LANES, SC_LANES)
                        plsc.addupdate(part_v.at[pl.ds(row, SC_LANES)], ps[0])

            _sc_gather_pipeline(tab_hbm, idx_v, rows_v, sems, nch, chunk)
            for g in range(PEER_SEL // SC_LANES):
                rowbase = (g * SC_LANES + lane_iota) * SC_LANES
                acc = plsc.load_gather(part_v, [rowbase])
                for l in range(1, SC_LANES):
                    acc = acc + plsc.load_gather(part_v, [rowbase + l])
                a_v[pl.ds(g * SC_LANES, SC_LANES)] = acc
            pltpu.sync_copy(a_v, out_hbm.at[t])

    return run(utab, erow, hrows)


def _coef_kernel(a_ref, g_ref, o_ref):
    o_ref[...] = g_ref[...] * jax.nn.gelu(a_ref[...])


def _coef_call(a, gate, tm, tok0):
    m = a.shape[0]
    assert tok0 % tm == 0
    spec = pl.BlockSpec((tm, PEER_SEL), lambda i: (i, 0))
    gspec = pl.BlockSpec((tm, PEER_SEL), lambda i: (i + tok0 // tm, 0))
    return pl.pallas_call(
        _coef_kernel, grid=(m // tm,), in_specs=[spec, gspec], out_specs=spec,
        out_shape=jax.ShapeDtypeStruct((m, PEER_SEL), F32),
        compiler_params=_cparams(("arbitrary",)), name="peer_coef",
    )(a, gate)


def _peer_layer(x2, sc2, sh2, g2, peer_wq, peer_keys, peer_u, peer_v, lng, lnb, seq, alpha,
                tt_route=256, tt_gather=64, tm=512, sc_tokens=0):
    n, d = x2.shape
    keys = peer_keys.reshape(2 * PEER_HEADS, N_KEYS, PEER_DK // 2)
    h2, erow, gate = _route_call(x2, sc2, sh2, peer_wq.astype(BF16), keys.astype(BF16), seq, tt_route)
    n_tc = n - sc_tokens
    hv = h2.reshape(n, SUBLANES, LANES)
    coef = _peer_u_call(erow, _pack_table(peer_u), hv, gate.reshape(n, 1, PEER_SEL), tt_gather, n_tc)
    if sc_tokens:
        a_sc = _sc_u_call(erow, h2, peer_u, n_tc, sc_tokens)
        a_sc, coef = lax.optimization_barrier((a_sc, coef))
        coef_sc = _coef_call(a_sc, gate, tm, n_tc)
    y = _peer_v_call(erow, coef, _pack_table(peer_v), tt_gather, n_tc).reshape(n_tc, d)
    if sc_tokens:
        y = jnp.concatenate([y, _sc_v_call(erow, coef_sc, peer_v, n_tc)], axis=0)
    return _resln_call(x2, y, g2, lng.reshape(1, d), lnb.reshape(1, d), seq, tm, alpha)


def kernel(x, c, w_ada, b_ada, w_in, cmp_pe, cmp_w1, cmp_w2, w_pool, pool_scale, w_lift, w_o,
           ln_g, ln_b, peer_wq, peer_keys, peer_u, peer_v):
    bsz, seq, d = x.shape
    depth = w_ada.shape[0]
    n = bsz * seq
    assert d == D_MODEL and seq % 512 == 0 and SLC_TOPN <= seq // SLC_LEN <= LANES // N_KV
    alpha = (2 * depth) ** 0.25

    c_pad = jnp.zeros((SUBLANES, d), F32).at[:bsz].set(c)
    mods = _ada_call(c_pad, w_ada, b_ada)[:, :bsz]
    cst = _Consts(seq)
    x2 = x.reshape(n, d)
    for l in range(depth):
        sh1, sc1, g1, sh2, sc2, g2 = (mods[l][:, i * d:(i + 1) * d].reshape(bsz, 1, d) for i in range(6))
        x2 = _token_mixer_layer(x2, sc1, sh1, g1, w_in[l], cmp_pe[l], cmp_w1[l], cmp_w2[l], w_pool[l],
                                pool_scale[l], w_lift[l], w_o[l], ln_g[l, 0], ln_b[l, 0], cst, bsz, seq, alpha)
        x2 = _peer_layer(x2, sc2, sh2, g2, peer_wq[l], peer_keys[l], peer_u[l], peer_v[l],
                         ln_g[l, 1], ln_b[l, 1], seq, alpha, sc_tokens=SC_TOKEN_SHARE * n // 16)
    return x2.reshape(bsz, seq, d)
```

```python
import dataclasses
import functools

import jax
import jax.numpy as jnp
import numpy as np
from jax import lax
from jax.experimental import pallas as pl
from jax.experimental.pallas import tpu as pltpu
from jax.experimental.pallas import tpu_sc as plsc

F32 = jnp.float32
BF16 = jnp.bfloat16
I32 = jnp.int32
HI = lax.Precision.HIGHEST

D_MODEL = 1024
N_HEADS = 8
HEAD_DIM = 64
N_KV = 2
HPG = N_HEADS // N_KV
ROT_DIM = HEAD_DIM // 4
ROT_HALF = ROT_DIM // 2
ROPE_THETA = 500000.0
CMP_LEN = 32
CMP_STRIDE = 16
SLC_LEN = 64
SLC_TOPN = 16
WINDOW = 512
SCALE = HEAD_DIM ** -0.5
NEG = -1e30
FORCE_INIT = 1e6
FORCE_LOCAL = 2e6
POOL_GROUPS = 4
POOL_WINDOWS = (2, 4, 8, 16)
POOL_WIDTH = 512
POOL_GW = POOL_WIDTH // POOL_GROUPS
POOL_HALO = 16
ATT_WIDTH = N_HEADS * HEAD_DIM
KV_WIDTH = 3 * 2 * N_KV * HEAD_DIM
GATE_NSA = 3 * N_HEADS
MERGE_GATES = 2 * D_MODEL
PEER_HEADS = 8
N_KEYS = 128
PEER_TOPK = 16
PEER_DK = 128
PEER_SEL = PEER_HEADS * PEER_TOPK
LN_EPS = 1e-5

LANES = 128
SUBLANES = 8
VMEM_LIMIT = 56 * 1024 * 1024

NT_DIMS = (((1,), (1,)), ((), ()))


def _cparams(sem):
    return pltpu.CompilerParams(dimension_semantics=sem, vmem_limit_bytes=VMEM_LIMIT)


def _ada_kernel(c_ref, w_ref, b_ref, o_ref):
    c = c_ref[...]
    ca = c * jax.nn.sigmoid(c)
    o_ref[...] = jnp.dot(ca, w_ref[...], precision=HI, preferred_element_type=F32) + b_ref[...]


def _ada_call(c_pad, w_ada, b_ada):
    depth = w_ada.shape[0]
    nblk = w_ada.shape[2] // D_MODEL
    rows = c_pad.shape[0]
    return pl.pallas_call(
        _ada_kernel,
        grid=(depth, nblk),
        in_specs=[
            pl.BlockSpec((rows, D_MODEL), lambda l, j: (0, 0)),
            pl.BlockSpec((None, D_MODEL, D_MODEL), lambda l, j: (l, 0, j)),
            pl.BlockSpec((None, 1, D_MODEL), lambda l, j: (l, 0, j)),
        ],
        out_specs=pl.BlockSpec((None, rows, D_MODEL), lambda l, j: (l, 0, j)),
        out_shape=jax.ShapeDtypeStruct((depth, rows, nblk * D_MODEL), F32),
        compiler_params=_cparams(("arbitrary", "arbitrary")),
        name="ada_mod",
    )(c_pad, w_ada, b_ada.reshape(depth, 1, -1))


IN_COLS = ATT_WIDTH + KV_WIDTH + POOL_WIDTH + MERGE_GATES + LANES


def _rope_lanes(z, rc, rs1, rs2):
    return z * rc + pltpu.roll(z, ROT_HALF, 1) * rs1 + pltpu.roll(z, LANES - ROT_HALF, 1) * rs2


def _inproj_kernel(x_ref, sc_ref, sh_ref, w_ref, rc_ref, rs1_ref, rs2_ref,
                   q_ref, kv_ref, p_ref, mrg_ref, gn_ref):
    h = x_ref[...] * (1.0 + sc_ref[...]) + sh_ref[...]
    a = jnp.dot(h.astype(BF16), w_ref[...], preferred_element_type=F32)
    rc, rs1, rs2 = rc_ref[...], rs1_ref[...], rs2_ref[...]
    for j in range(ATT_WIDTH // LANES):
        q_ref[:, j * LANES:(j + 1) * LANES] = _rope_lanes(a[:, j * LANES:(j + 1) * LANES], rc, rs1, rs2)
    for br in range(3):
        c0 = ATT_WIDTH + br * 2 * LANES
        k = a[:, c0:c0 + LANES]
        if br > 0:
            k = _rope_lanes(k, rc, rs1, rs2)
        kv_ref[:, br * 2 * LANES:br * 2 * LANES + LANES] = k
        kv_ref[:, br * 2 * LANES + LANES:(br + 1) * 2 * LANES] = a[:, c0 + LANES:c0 + 2 * LANES]
    c1 = ATT_WIDTH + KV_WIDTH
    p_ref[...] = a[:, c1:c1 + POOL_WIDTH]
    mrg_ref[...] = a[:, c1 + POOL_WIDTH:c1 + POOL_WIDTH + MERGE_GATES]
    gn_ref[...] = a[:, c1 + POOL_WIDTH + MERGE_GATES:]


def _inproj_call(x2, sc, sh, w, rc, rs1, rs2, seq, tm):
    n = x2.shape[0]
    tpb = seq // tm
    tok = lambda i: (i, 0)
    bat = lambda i: (i // tpb, 0, 0)
    pos = lambda i: (i % tpb, 0)
    full = lambda i: (0, 0)
    return pl.pallas_call(
        _inproj_kernel,
        grid=(n // tm,),
        in_specs=[
            pl.BlockSpec((tm, D_MODEL), tok),
            pl.BlockSpec((None, 1, D_MODEL), bat),
            pl.BlockSpec((None, 1, D_MODEL), bat),
            pl.BlockSpec((D_MODEL, IN_COLS), full),
            pl.BlockSpec((tm, LANES), pos),
            pl.BlockSpec((tm, LANES), pos),
            pl.BlockSpec((tm, LANES), pos),
        ],
        out_specs=[
            pl.BlockSpec((tm, ATT_WIDTH), tok),
            pl.BlockSpec((tm, KV_WIDTH), tok),
            pl.BlockSpec((tm, POOL_WIDTH), tok),
            pl.BlockSpec((tm, MERGE_GATES), tok),
            pl.BlockSpec((tm, LANES), tok),
        ],
        out_shape=[
            jax.ShapeDtypeStruct((n, ATT_WIDTH), F32),
            jax.ShapeDtypeStruct((n, KV_WIDTH), F32),
            jax.ShapeDtypeStruct((n, POOL_WIDTH), F32),
            jax.ShapeDtypeStruct((n, MERGE_GATES), F32),
            jax.ShapeDtypeStruct((n, LANES), F32),
        ],
        compiler_params=_cparams(("arbitrary",)),
        name="in_proj",
    )(x2, sc, sh, w, rc, rs1, rs2)


def _compress_kernel(z_ref, pet_ref, peb_ref, w1t_ref, w1b_ref, w2_ref, rc_ref, rs1_ref, rs2_ref, o_ref):
    z = z_ref[...]
    rows = z.shape[0]
    top = jnp.dot(z + pet_ref[...], w1t_ref[...], precision=HI, preferred_element_type=F32)
    bot = jnp.dot(z + peb_ref[...], w1b_ref[...], precision=HI, preferred_element_type=F32)
    pre = top + pltpu.roll(bot, rows - 1, 0)
    y = jnp.dot(jax.nn.gelu(pre), w2_ref[...], precision=HI, preferred_element_type=F32)
    o_ref[...] = _rope_lanes(y, rc_ref[...], rs1_ref[...], rs2_ref[...])


def _compress_call(z, pet, peb, w1t, w1b, w2, rc, rs1, rs2):
    b, _, rows, width = z.shape
    kvsel = lambda i, j: (j, 0, 0)
    return pl.pallas_call(
        _compress_kernel,
        grid=(b, 2),
        in_specs=[
            pl.BlockSpec((None, None, rows, width), lambda i, j: (i, j, 0, 0)),
            pl.BlockSpec((None, 1, width), kvsel),
            pl.BlockSpec((None, 1, width), kvsel),
            pl.BlockSpec((None, width, LANES), kvsel),
            pl.BlockSpec((None, width, LANES), kvsel),
            pl.BlockSpec((None, LANES, LANES), kvsel),
            pl.BlockSpec((None, rows, LANES), kvsel),
            pl.BlockSpec((None, rows, LANES), kvsel),
            pl.BlockSpec((None, rows, LANES), kvsel),
        ],
        out_specs=pl.BlockSpec((None, None, rows, LANES), lambda i, j: (i, j, 0, 0)),
        out_shape=jax.ShapeDtypeStruct((b, 2, rows, LANES), F32),
        compiler_params=_cparams(("arbitrary", "arbitrary")),
        name="compress",
    )(z, pet, peb, w1t, w1b, w2, rc, rs1, rs2)


def _cmpsel_kernel(q_ref, kc_ref, vc_ref, c2s_ref, o_ref, sel_ref, *, tq):
    t0 = pl.program_id(1) * tq
    kc = kc_ref[...]
    vc = vc_ref[...]
    rows = kc.shape[0]
    trow = t0 + lax.broadcasted_iota(I32, (tq, rows), 0)
    cend = lax.broadcasted_iota(I32, (tq, rows), 1) * CMP_STRIDE + (CMP_LEN - 1)
    vis = cend <= trow
    anyv = (trow[:, :1] >= CMP_LEN - 1).astype(F32)
    imp = jnp.zeros((tq, LANES), F32)
    for g in range(N_KV):
        kg = kc[:, g * HEAD_DIM:(g + 1) * HEAD_DIM].astype(BF16)
        vg = vc[:, g * HEAD_DIM:(g + 1) * HEAD_DIM].astype(BF16)
        psum = jnp.zeros((tq, rows), F32)
        for h in range(HPG):
            hd = g * HPG + h
            qh = q_ref[:, hd * HEAD_DIM:(hd + 1) * HEAD_DIM].astype(BF16)
            s = lax.dot_general(qh, kg, NT_DIMS, preferred_element_type=F32) * SCALE
            s = jnp.where(vis, s, NEG)
            e = jnp.exp(s - jnp.max(s, axis=-1, keepdims=True))
            p = e / jnp.sum(e, axis=-1, keepdims=True) * anyv
            o_ref[:, hd * HEAD_DIM:(hd + 1) * HEAD_DIM] = jnp.dot(
                p.astype(BF16), vg, preferred_element_type=F32)
            psum = psum + p
        imp = imp + jnp.dot(psum.astype(BF16), c2s_ref[g], preferred_element_type=F32)
    lane = lax.broadcasted_iota(I32, (tq, LANES), 1)
    blk = lane & (SLC_LEN - 1)
    cur = lax.shift_right_logical(t0 + lax.broadcasted_iota(I32, (tq, LANES), 0), 6)
    score = jnp.where(blk <= cur, imp, NEG)
    score = jnp.where(blk == 0, FORCE_INIT, score)
    score = jnp.where(blk == cur, FORCE_LOCAL, score)
    sc_t = score.T
    nblk = LANES // N_KV
    jrow = lax.broadcasted_iota(I32, (nblk, tq), 0)
    sel_parts = []
    for g in range(N_KV):
        sc = sc_t[g * nblk:(g + 1) * nblk]
        cnt = jnp.zeros((nblk, tq), I32)
        for k in range(nblk):
            rk = sc[k:k + 1, :]
            ge = (rk >= sc).astype(I32)
            gt = (rk > sc).astype(I32)
            cnt = cnt + jnp.where(jrow > k, ge, gt)
        sel_parts.append((cnt < SLC_TOPN).astype(F32))
    sel_ref[...] = jnp.concatenate(sel_parts, axis=0).T


def _cmpsel_call(q, kvc, c2s, seq, tq):
    n = q.shape[0]
    b = n // seq
    nq = seq // tq
    rows = kvc.shape[2]
    tok = lambda i, j: (i * nq + j, 0)
    return pl.pallas_call(
        functools.partial(_cmpsel_kernel, tq=tq),
        grid=(b, nq),
        in_specs=[
            pl.BlockSpec((tq, ATT_WIDTH), tok),
            pl.BlockSpec((None, None, rows, LANES), lambda i, j: (i, 0, 0, 0)),
            pl.BlockSpec((None, None, rows, LANES), lambda i, j: (i, 1, 0, 0)),
            pl.BlockSpec((N_KV, rows, LANES), lambda i, j: (0, 0, 0)),
        ],
        out_specs=[pl.BlockSpec((tq, ATT_WIDTH), tok), pl.BlockSpec((tq, LANES), tok)],
        out_shape=[jax.ShapeDtypeStruct((n, ATT_WIDTH), F32), jax.ShapeDtypeStruct((n, LANES), F32)],
        compiler_params=_cparams(("arbitrary", "arbitrary")),
        name="cmp_select",
    )(q, kvc, kvc, c2s)


def _slc_kernel(q_ref, k_ref, v_ref, sel_ref, o_ref, qs_sc, m_sc, l_sc, acc_sc, *, tq, tk):
    qi = pl.program_id(1)
    kt = pl.program_id(2)
    nk = pl.num_programs(2)

    @pl.when(kt == 0)
    def _init():
        for hd in range(N_HEADS):
            g, h = divmod(hd, HPG)
            qs_sc[g, h * tq:(h + 1) * tq, :] = (q_ref[:, hd * HEAD_DIM:(hd + 1) * HEAD_DIM] * SCALE).astype(BF16)
        m_sc[...] = jnp.full(m_sc.shape, NEG, F32)
        l_sc[...] = jnp.zeros(l_sc.shape, F32)
        acc_sc[...] = jnp.zeros(acc_sc.shape, F32)

    @pl.when(kt * tk <= qi * tq + (tq - 1))
    def _step():
        t = qi * tq + lax.broadcasted_iota(I32, (tq, tk), 0)
        kp = kt * tk + lax.broadcasted_iota(I32, (tq, tk), 1)
        causal = kp <= t
        nblk = LANES // N_KV
        jb = lax.broadcasted_iota(I32, (nblk, tk), 0)
        kb = lax.shift_right_logical(kt * tk + lax.broadcasted_iota(I32, (nblk, tk), 1), 6)
        expand = (jb == kb).astype(BF16)
        for g in range(N_KV):
            selg = sel_ref[:, g * nblk:(g + 1) * nblk].astype(BF16)
            member = jnp.dot(selg, expand, preferred_element_type=F32)
            bias = jnp.where(jnp.logical_and(causal, member > 0.5), 0.0, NEG)
            bias = jnp.concatenate([bias] * HPG, axis=0)
            kg = k_ref[:, g * HEAD_DIM:(g + 1) * HEAD_DIM].astype(BF16)
            vg = v_ref[:, g * HEAD_DIM:(g + 1) * HEAD_DIM].astype(BF16)
            s = lax.dot_general(qs_sc[g], kg, NT_DIMS, preferred_element_type=F32) + bias
            chunks = [s[:, c * LANES:(c + 1) * LANES] for c in range(tk // LANES)]
            mc = chunks[0]
            for x in chunks[1:]:
                mc = jnp.maximum(mc, x)
            m_old = m_sc[g]
            m_new = jnp.maximum(m_old, jnp.max(mc, axis=-1, keepdims=True))
            alpha = jnp.exp(m_old - m_new)
            ps = [jnp.exp(x - m_new) for x in chunks]
            lsum = ps[0]
            for x in ps[1:]:
                lsum = lsum + x
            l_sc[g] = alpha * l_sc[g] + lsum
            p = jnp.concatenate(ps, axis=1).astype(BF16)
            acc_sc[g] = alpha[:, :HEAD_DIM] * acc_sc[g] + jnp.dot(p, vg, preferred_element_type=F32)
            m_sc[g] = m_new

    @pl.when(kt == nk - 1)
    def _fin():
        for hd in range(N_HEADS):
            g, h = divmod(hd, HPG)
            l = jnp.sum(l_sc[g, h * tq:(h + 1) * tq, :], axis=-1, keepdims=True)
            o_ref[:, hd * HEAD_DIM:(hd + 1) * HEAD_DIM] = acc_sc[g, h * tq:(h + 1) * tq, :] / l


def _slc_call(q, kv, sel, seq, tq, tk):
    n = q.shape[0]
    b = n // seq
    nq = seq // tq
    nk = seq // tk
    tok = lambda i, j, k: (i * nq + j, 0)

    def key_map(col):
        def f(i, j, k):
            last = (j * tq + tq - 1) // tk
            return (i * nk + jnp.minimum(k, last), col)
        return f

    return pl.pallas_call(
        functools.partial(_slc_kernel, tq=tq, tk=tk),
        grid=(b, nq, nk),
        in_specs=[
            pl.BlockSpec((tq, ATT_WIDTH), tok),
            pl.BlockSpec((tk, LANES), key_map(2)),
            pl.BlockSpec((tk, LANES), key_map(3)),
            pl.BlockSpec((tq, LANES), tok),
        ],
        out_specs=pl.BlockSpec((tq, ATT_WIDTH), tok),
        out_shape=jax.ShapeDtypeStruct((n, ATT_WIDTH), F32),
        scratch_shapes=[
            pltpu.VMEM((N_KV, HPG * tq, HEAD_DIM), BF16),
            pltpu.VMEM((N_KV, HPG * tq, LANES), F32),
            pltpu.VMEM((N_KV, HPG * tq, LANES), F32),
            pltpu.VMEM((N_KV, HPG * tq, HEAD_DIM), F32),
        ],
        compiler_params=_cparams(("arbitrary", "arbitrary", "arbitrary")),
        name="slc_attn",
    )(q, kv, kv, sel)


def _win_kernel(q_ref, *refs, tq, nkb):
    k_refs = refs[:nkb]
    v_refs = refs[nkb:2 * nkb]
    o_ref = refs[2 * nkb]
    qi = pl.program_id(1)
    t = qi * tq + lax.broadcasted_iota(I32, (tq, tq), 0)
    col = lax.broadcasted_iota(I32, (tq, tq), 1)
    biases = []
    for j in range(nkb):
        kp = (qi - (nkb - 1) + j) * tq + col
        diff = t - kp
        ok = jnp.logical_and(jnp.logical_and(diff >= 0, diff < WINDOW), kp >= 0)
        biases.append(jnp.concatenate([jnp.where(ok, 0.0, NEG)] * HPG, axis=0))
    for g in range(N_KV):
        qs = jnp.concatenate(
            [(q_ref[:, (g * HPG + h) * HEAD_DIM:(g * HPG + h + 1) * HEAD_DIM] * SCALE).astype(BF16)
             for h in range(HPG)], axis=0)
        chunks = []
        for j in range(nkb):
            kg = k_refs[j][:, g * HEAD_DIM:(g + 1) * HEAD_DIM].astype(BF16)
            s = lax.dot_general(qs, kg, NT_DIMS, preferred_element_type=F32) + biases[j]
            chunks += [s[:, c * LANES:(c + 1) * LANES] for c in range(tq // LANES)]
        mc = chunks[0]
        for x in chunks[1:]:
            mc = jnp.maximum(mc, x)
        m = jnp.broadcast_to(jnp.max(mc, axis=-1, keepdims=True), mc.shape)
        ps = [jnp.exp(x - m) for x in chunks]
        lsum = ps[0]
        for x in ps[1:]:
            lsum = lsum + x
        l = jnp.sum(lsum, axis=-1, keepdims=True)
        per = tq // LANES
        o = None
        for j in range(nkb):
            vg = v_refs[j][:, g * HEAD_DIM:(g + 1) * HEAD_DIM].astype(BF16)
            pj = jnp.concatenate(ps[j * per:(j + 1) * per], axis=1).astype(BF16)
            oj = jnp.dot(pj, vg, preferred_element_type=F32)
            o = oj if o is None else o + oj
        o = o / l
        for h in range(HPG):
            hd = g * HPG + h
            o_ref[:, hd * HEAD_DIM:(hd + 1) * HEAD_DIM] = o[h * tq:(h + 1) * tq]


def _win_call(q, kv, seq, tq):
    n = q.shape[0]
    b = n // seq
    nq = seq // tq
    nkb = WINDOW // tq + 1
    tok = lambda i, j: (i * nq + j, 0)

    def key_map(col, back):
        return lambda i, j: (i * nq + jnp.maximum(j - back, 0), col)

    k_specs = [pl.BlockSpec((tq, LANES), key_map(4, nkb - 1 - jj)) for jj in range(nkb)]
    v_specs = [pl.BlockSpec((tq, LANES), key_map(5, nkb - 1 - jj)) for jj in range(nkb)]
    return pl.pallas_call(
        functools.partial(_win_kernel, tq=tq, nkb=nkb),
        grid=(b, nq),
        in_specs=[pl.BlockSpec((tq, ATT_WIDTH), tok)] + k_specs + v_specs,
        out_specs=pl.BlockSpec((tq, ATT_WIDTH), tok),
        out_shape=jax.ShapeDtypeStruct((n, ATT_WIDTH), F32),
        compiler_params=_cparams(("arbitrary", "arbitrary")),
        name="win_attn",
    )(q, *([kv] * (2 * nkb)))


def _pool_kernel(p_ref, prev_ref, w_ref, sc_ref, o_ref, *, ts):
    i = pl.program_id(1)
    x = p_ref[...]
    prev = prev_ref[...] * (i > 0).astype(F32)
    xe = jnp.concatenate([prev, x], axis=0)
    t1 = (i * ts + 1 + lax.broadcasted_iota(I32, (ts, POOL_GW), 0)).astype(F32)
    for g, w in enumerate(POOL_WINDOWS):
        a = xe[:, g * POOL_GW:(g + 1) * POOL_GW]
        off = POOL_HALO
        span = 1
        while span < w:
            a = a[span:] + a[:-span]
            off -= span
            span *= 2
        sums = a[off:off + ts]
        cnt = jnp.minimum(t1, float(w))
        pooled = sums / cnt - x[:, g * POOL_GW:(g + 1) * POOL_GW]
        y = jnp.dot(pooled.astype(BF16), w_ref[g], preferred_element_type=F32)
        o_ref[:, g * POOL_GW:(g + 1) * POOL_GW] = y * sc_ref[:, g * POOL_GW:(g + 1) * POOL_GW]


def _pool_call(p_in, w_pool, pool_scale, seq, ts):
    n = p_in.shape[0]
    b = n // seq
    nt = seq // ts
    hpt = ts // POOL_HALO
    tok = lambda i, j: (i * nt + j, 0)
    return pl.pallas_call(
        functools.partial(_pool_kernel, ts=ts),
        grid=(b, nt),
        in_specs=[
            pl.BlockSpec((ts, POOL_WIDTH), tok),
            pl.BlockSpec((POOL_HALO, POOL_WIDTH), lambda i, j: (i * nt * hpt + jnp.maximum(j * hpt - 1, 0), 0)),
            pl.BlockSpec((POOL_GROUPS, POOL_GW, POOL_GW), lambda i, j: (0, 0, 0)),
            pl.BlockSpec((1, POOL_WIDTH), lambda i, j: (0, 0)),
        ],
        out_specs=pl.BlockSpec((ts, POOL_WIDTH), tok),
        out_shape=jax.ShapeDtypeStruct((n, POOL_WIDTH), F32),
        compiler_params=_cparams(("arbitrary", "arbitrary")),
        name="pool_mix",
    )(p_in, p_in, w_pool, pool_scale)


def _layer_norm(z, g, b):
    mu = jnp.mean(z, axis=-1, keepdims=True)
    zc = z - mu
    var = jnp.mean(zc * zc, axis=-1, keepdims=True)
    return zc * lax.rsqrt(var + LN_EPS) * g + b


def _merge_kernel(oc_ref, os_ref, ow_ref, gn_ref, op_ref, gm_ref, x_ref, g1_ref, lng_ref, lnb_ref,
                  wl_ref, wo_ref, eb_ref, o_ref, *, alpha):
    gate = jax.nn.sigmoid(gn_ref[...])
    branches = (oc_ref, os_ref, ow_ref)
    oatt = None
    for br in range(3):
        gx = jnp.dot(gate, eb_ref[br], precision=HI, preferred_element_type=F32)
        term = gx * branches[br][...]
        oatt = term if oatt is None else oatt + term
    la = jnp.dot(oatt.astype(BF16), wl_ref[0], preferred_element_type=F32)
    lb = jnp.dot(op_ref[...].astype(BF16), wl_ref[1], preferred_element_type=F32)
    gm = jax.nn.sigmoid(gm_ref[...])
    merged = gm[:, :D_MODEL] * la + gm[:, D_MODEL:] * lb
    y = jnp.dot(merged.astype(BF16), wo_ref[...], preferred_element_type=F32)
    z = alpha * x_ref[...] + g1_ref[...] * y
    o_ref[...] = _layer_norm(z, lng_ref[...], lnb_ref[...])


def _merge_call(oc, osl, ow, gn, op, gm, x2, g1, lng, lnb, wl, wo, eb, seq, tm, alpha):
    n = x2.shape[0]
    tpb = seq // tm
    tok = lambda i: (i, 0)
    bat = lambda i: (i // tpb, 0, 0)
    return pl.pallas_call(
        functools.partial(_merge_kernel, alpha=alpha),
        grid=(n // tm,),
        in_specs=[
            pl.BlockSpec((tm, ATT_WIDTH), tok), pl.BlockSpec((tm, ATT_WIDTH), tok),
            pl.BlockSpec((tm, ATT_WIDTH), tok), pl.BlockSpec((tm, LANES), tok),
            pl.BlockSpec((tm, POOL_WIDTH), tok), pl.BlockSpec((tm, MERGE_GATES), tok),
            pl.BlockSpec((tm, D_MODEL), tok),
            pl.BlockSpec((None, 1, D_MODEL), bat),
            pl.BlockSpec((1, D_MODEL), lambda i: (0, 0)), pl.BlockSpec((1, D_MODEL), lambda i: (0, 0)),
            pl.BlockSpec((2, ATT_WIDTH, D_MODEL), lambda i: (0, 0, 0)),
            pl.BlockSpec((D_MODEL, D_MODEL), lambda i: (0, 0)),
            pl.BlockSpec((3, LANES, ATT_WIDTH), lambda i: (0, 0, 0)),
        ],
        out_specs=pl.BlockSpec((tm, D_MODEL), tok),
        out_shape=jax.ShapeDtypeStruct((n, D_MODEL), F32),
        compiler_params=_cparams(("arbitrary",)),
        name="merge_out",
    )(oc, osl, ow, gn, op, gm, x2, g1, lng, lnb, wl, wo, eb)


def _extract_top(cur, ids, n):
    rows = cur.shape[0]
    rio = lax.broadcasted_iota(I32, cur.shape, 0)
    vals, outs = [], []
    for _ in range(n):
        m = jnp.max(cur, axis=0, keepdims=True)
        pos = jnp.min(jnp.where(cur == m, rio, rows), axis=0, keepdims=True)
        hit = rio == pos
        vals.append(m)
        outs.append(pos if ids is None else jnp.max(jnp.where(hit, ids, -1), axis=0, keepdims=True))
        cur = jnp.where(hit, -jnp.inf, cur)
    return jnp.concatenate(vals, axis=0), jnp.concatenate(outs, axis=0)


def _route_kernel(x_ref, sc_ref, sh_ref, wq_ref, keys_ref, h_ref, hv_ref, e_ref, g_ref,
                  st_sc, ts_sc, ti_sc, eo_sc, go_sc):
    h = x_ref[...] * (1.0 + sc_ref[...]) + sh_ref[...]
    h_ref[...] = h
    for r in range(SUBLANES):
        hv_ref[:, r, :] = h[:, r * LANES:(r + 1) * LANES]
    qp = jnp.dot(h.astype(BF16), wq_ref[...], preferred_element_type=F32).astype(BF16)
    half = PEER_DK // 2
    for hp in range(2 * PEER_HEADS):
        st_sc[hp] = lax.dot_general(keys_ref[hp], qp[:, hp * half:(hp + 1) * half], NT_DIMS,
                                    preferred_element_type=F32)

    def half_body(hp, carry):
        vals, ids = _extract_top(st_sc[hp], None, PEER_TOPK)
        ts_sc[hp] = vals
        ti_sc[hp] = ids
        return carry

    lax.fori_loop(0, 2 * PEER_HEADS, half_body, 0)

    def head_body(hh, carry):
        s1, s2 = ts_sc[2 * hh], ts_sc[2 * hh + 1]
        i1, i2 = ti_sc[2 * hh], ti_sc[2 * hh + 1]
        brow = lax.broadcasted_iota(I32, (SUBLANES, s1.shape[1]), 0)
        cands = [s1[0:1, :] + s2]
        cidxs = [i1[0:1, :] * N_KEYS + i2]
        for a in range(1, SUBLANES):
            ok = brow < PEER_TOPK // (a + 1)
            cands.append(jnp.where(ok, s1[a:a + 1, :] + s2[:SUBLANES], -jnp.inf))
            cidxs.append(i1[a:a + 1, :] * N_KEYS + i2[:SUBLANES])
        cands.append(s1[SUBLANES:] + s2[0:1, :])
        cidxs.append(i1[SUBLANES:] * N_KEYS + i2[0:1, :])
        sv, ei = _extract_top(jnp.concatenate(cands, axis=0), jnp.concatenate(cidxs, axis=0), PEER_TOPK)
        ex = jnp.exp(sv - sv[0:1, :])
        go_sc[hh] = ex / jnp.sum(ex, axis=0, keepdims=True)
        eo_sc[hh] = ei.astype(F32)
        return carry

    lax.fori_loop(0, PEER_HEADS, head_body, 0)
    e_all = jnp.concatenate([eo_sc[hh] for hh in range(PEER_HEADS)], axis=0)
    g_all = jnp.concatenate([go_sc[hh] for hh in range(PEER_HEADS)], axis=0)
    e_ref[...] = e_all.T.astype(I32) * (D_MODEL // 2 // LANES)
    g_ref[...] = g_all.T


def _route_call(x2, sc, sh, wq, keys, seq, tt):
    n = x2.shape[0]
    tpb = seq // tt
    tok = lambda i: (i, 0)
    bat = lambda i: (i // tpb, 0, 0)
    nhp = 2 * PEER_HEADS
    return pl.pallas_call(
        _route_kernel,
        grid=(n // tt,),
        in_specs=[
            pl.BlockSpec((tt, D_MODEL), tok),
            pl.BlockSpec((None, 1, D_MODEL), bat),
            pl.BlockSpec((None, 1, D_MODEL), bat),
            pl.BlockSpec((D_MODEL, PEER_HEADS * PEER_DK), lambda i: (0, 0)),
            pl.BlockSpec((nhp, N_KEYS, PEER_DK // 2), lambda i: (0, 0, 0)),
        ],
        out_specs=[pl.BlockSpec((tt, D_MODEL), tok), pl.BlockSpec((tt, SUBLANES, LANES), lambda i: (i, 0, 0)),
                   pl.BlockSpec((tt, PEER_SEL), tok), pl.BlockSpec((tt, PEER_SEL), tok)],
        out_shape=[jax.ShapeDtypeStruct((n, D_MODEL), F32), jax.ShapeDtypeStruct((n, SUBLANES, LANES), F32),
                   jax.ShapeDtypeStruct((n, PEER_SEL), I32), jax.ShapeDtypeStruct((n, PEER_SEL), F32)],
        scratch_shapes=[
            pltpu.VMEM((nhp, N_KEYS, tt), F32),
            pltpu.VMEM((nhp, PEER_TOPK, tt), F32),
            pltpu.VMEM((nhp, PEER_TOPK, tt), I32),
            pltpu.VMEM((PEER_HEADS, PEER_TOPK, tt), F32),
            pltpu.VMEM((PEER_HEADS, PEER_TOPK, tt), F32),
        ],
        compiler_params=_cparams(("arbitrary",)),
        name="peer_route",
    )(x2, sc, sh, wq, keys)


HALF_ROWS = SUBLANES // 2
HI_MASK = -65536
PAIR_TILES = PEER_SEL // 2
PAIR_ROWS = PAIR_TILES * SUBLANES


def _load_two_experts(tab_ref, ra, rb):
    wa = tab_ref[pl.ds(pl.multiple_of(ra, HALF_ROWS), HALF_ROWS), :]
    wb = tab_ref[pl.ds(pl.multiple_of(rb, HALF_ROWS), HALF_ROWS), :]
    w2 = jnp.concatenate([wa, wb], axis=0)
    return lax.bitcast_convert_type(w2 << 16, F32), lax.bitcast_convert_type(w2 & HI_MASK, F32)


def _fold_pairs(vs):
    row = lax.broadcasted_iota(I32, (SUBLANES, LANES), 0)
    shift = HALF_ROWS // 2
    while len(vs) > 1:
        low = (row & shift) == 0
        vs = [jnp.where(low, a + pltpu.roll(a, SUBLANES - shift, 0), b + pltpu.roll(b, shift, 0))
              for a, b in zip(vs[0::2], vs[1::2])]
        shift //= 2
    return vs[0]


def _fold_order():
    idx = [[2 * i if r < HALF_ROWS else 2 * i + 1 for r in range(SUBLANES)] for i in range(HALF_ROWS)]
    shift = HALF_ROWS // 2
    while len(idx) > 1:
        idx = [[a[r] if (r & shift) == 0 else b[r] for r in range(SUBLANES)]
               for a, b in zip(idx[0::2], idx[1::2])]
        shift //= 2
    return idx[0]


def _peer_u_kernel(e_sm, tab_ref, hv_ref, gate_ref, coef_ref, *, tt):
    row = lax.broadcasted_iota(I32, (SUBLANES, LANES), 0)
    low = row < HALF_ROWS
    eye = (lax.broadcasted_iota(I32, (PEER_SEL, LANES), 0) ==
           lax.broadcasted_iota(I32, (PEER_SEL, LANES), 1))
    order = _fold_order()

    def finish(t, part):
        col = jnp.sum(part, axis=-1, keepdims=True)
        a_row = jnp.sum(jnp.where(eye, col, 0.0), axis=0, keepdims=True)
        coef_ref[t] = gate_ref[pl.ds(t, 1), :] * jax.nn.gelu(a_row)

    def token(t, part_prev):
        finish(jnp.maximum(t - 1, 0), part_prev)
        hv = hv_ref[t]
        hsw = pltpu.roll(hv, HALF_ROWS, 0)
        h_lo = jnp.where(low, hv, hsw)
        h_hi = jnp.where(low, hsw, hv)
        folded = []
        for j in range(PEER_SEL // SUBLANES):
            prods = []
            for i in range(HALF_ROWS):
                ka = j * SUBLANES + order.index(2 * i)
                kb = j * SUBLANES + order.index(2 * i + 1)
                lo, hi = _load_two_experts(tab_ref, e_sm[t, ka], e_sm[t, kb])
                prods.append(lo * h_lo + hi * h_hi)
            folded.append(_fold_pairs(prods))
        return jnp.concatenate(folded, axis=0)

    last = lax.fori_loop(0, tt, token, jnp.zeros((PEER_SEL, LANES), F32))
    finish(tt - 1, last)


def _peer_v_kernel(e_sm, coef_ref, tab_ref, y_ref, cv_sc, *, tt, nacc):
    row = lax.broadcasted_iota(I32, (SUBLANES, LANES), 0)
    low = row < HALF_ROWS
    rr = lax.broadcasted_iota(I32, (PAIR_ROWS, LANES), 0)
    kk = lax.broadcasted_iota(I32, (PAIR_ROWS, LANES), 1)
    onehot = (kk == 2 * (rr >> 3) + ((rr >> 2) & 1)).astype(F32)
    ones = jnp.ones((LANES, LANES), BF16)

    def expand(t, slot):
        lhs = (onehot * coef_ref[t]).astype(BF16)
        cv_sc[slot] = jnp.dot(lhs, ones, preferred_element_type=F32)

    def process(t, slot):
        acc_lo = [jnp.zeros((SUBLANES, LANES), F32) for _ in range(nacc)]
        acc_hi = [jnp.zeros((SUBLANES, LANES), F32) for _ in range(nacc)]
        for j in range(PAIR_TILES):
            lo, hi = _load_two_experts(tab_ref, e_sm[t, 2 * j], e_sm[t, 2 * j + 1])
            cv = cv_sc[slot, j * SUBLANES:(j + 1) * SUBLANES, :]
            acc_lo[j % nacc] = acc_lo[j % nacc] + cv * lo
            acc_hi[j % nacc] = acc_hi[j % nacc] + cv * hi
        a_lo, a_hi = acc_lo[0], acc_hi[0]
        for i in range(1, nacc):
            a_lo = a_lo + acc_lo[i]
            a_hi = a_hi + acc_hi[i]
        a_lo = a_lo + pltpu.roll(a_lo, HALF_ROWS, 0)
        a_hi = a_hi + pltpu.roll(a_hi, HALF_ROWS, 0)
        y_ref[t] = jnp.where(low, a_lo, a_hi)

    expand(0, 0)

    def two_tokens(i, carry):
        t = 2 * i
        expand(t + 1, 1)
        process(t, 0)
        expand(jnp.minimum(t + 2, tt - 1), 0)
        process(t + 1, 1)
        return carry

    lax.fori_loop(0, tt // 2, two_tokens, 0)


def _resident_table_spec(tab):
    return pl.BlockSpec(tab.shape, lambda i: (0, 0), pipeline_mode=pl.Buffered(1))


def _peer_u_call(erow, tab, hv, gate, tt, n):
    return pl.pallas_call(
        functools.partial(_peer_u_kernel, tt=tt),
        grid=(n // tt,),
        in_specs=[
            pl.BlockSpec((tt, PEER_SEL), lambda i: (i, 0), memory_space=pltpu.SMEM),
            _resident_table_spec(tab),
            pl.BlockSpec((tt, SUBLANES, LANES), lambda i: (i, 0, 0)),
            pl.BlockSpec((tt, PEER_SEL), lambda i: (i, 0)),
        ],
        out_specs=pl.BlockSpec((tt, 1, PEER_SEL), lambda i: (i, 0, 0)),
        out_shape=jax.ShapeDtypeStruct((n, 1, PEER_SEL), F32),
        compiler_params=_cparams(("arbitrary",)),
        name="peer_u",
    )(erow, tab, hv, gate)


def _peer_v_call(erow, coef3, tab, tt, n):
    assert tt % 2 == 0
    return pl.pallas_call(
        functools.partial(_peer_v_kernel, tt=tt, nacc=4),
        grid=(n // tt,),
        in_specs=[
            pl.BlockSpec((tt, PEER_SEL), lambda i: (i, 0), memory_space=pltpu.SMEM),
            pl.BlockSpec((tt, 1, PEER_SEL), lambda i: (i, 0, 0)),
            _resident_table_spec(tab),
        ],
        out_specs=pl.BlockSpec((tt, SUBLANES, LANES), lambda i: (i, 0, 0)),
        out_shape=jax.ShapeDtypeStruct((n, SUBLANES, LANES), F32),
        scratch_shapes=[pltpu.VMEM((2, PAIR_ROWS, LANES), F32)],
        compiler_params=_cparams(("arbitrary",)),
        name="peer_v",
    )(erow, coef3, tab)


def _resln_kernel(x_ref, y_ref, g_ref, lng_ref, lnb_ref, o_ref, *, alpha):
    z = alpha * x_ref[...] + g_ref[...] * y_ref[...]
    o_ref[...] = _layer_norm(z, lng_ref[...], lnb_ref[...])


def _resln_call(x2, y2, g2, lng, lnb, seq, tm, alpha):
    n = x2.shape[0]
    tpb = seq // tm
    tok = lambda i: (i, 0)
    return pl.pallas_call(
        functools.partial(_resln_kernel, alpha=alpha),
        grid=(n // tm,),
        in_specs=[
            pl.BlockSpec((tm, D_MODEL), tok), pl.BlockSpec((tm, D_MODEL), tok),
            pl.BlockSpec((None, 1, D_MODEL), lambda i: (i // tpb, 0, 0)),
            pl.BlockSpec((1, D_MODEL), lambda i: (0, 0)), pl.BlockSpec((1, D_MODEL), lambda i: (0, 0)),
        ],
        out_specs=pl.BlockSpec((tm, D_MODEL), tok),
        out_shape=jax.ShapeDtypeStruct((n, D_MODEL), F32),
        compiler_params=_cparams(("arbitrary",)),
        name="res_ln",
    )(x2, y2, g2, lng, lnb)


def _rope_lane_tables(pos):
    inv = ROPE_THETA ** (-jnp.arange(0, ROT_DIM, 2, dtype=F32) / ROT_DIM)
    ang = pos.astype(F32)[:, None] * inv[None, :]
    cos, sin = jnp.cos(ang), jnp.sin(ang)
    lane = np.arange(LANES) % HEAD_DIM
    fidx = lane % ROT_HALF
    first = jnp.asarray(lane < ROT_HALF)
    second = jnp.asarray((lane >= ROT_HALF) & (lane < ROT_DIM))
    rot = jnp.asarray(lane < ROT_DIM)
    cl, sl = cos[:, fidx], sin[:, fidx]
    rc = jnp.where(rot, cl, 1.0)
    rs1 = jnp.where(second, sl, 0.0)
    rs2 = jnp.where(first, -sl, 0.0)
    return rc, rs1, rs2


def _pack_kernel(t_ref, o_ref):
    x = t_ref[...]
    half = x.shape[1] // 2
    lo = lax.bitcast_convert_type(x[:, :half].astype(BF16).astype(F32), I32)
    hi = lax.bitcast_convert_type(x[:, half:].astype(BF16).astype(F32), I32)
    o_ref[...] = lax.shift_right_logical(lo, jnp.full_like(lo, 16)) | hi


def _pack_table(tab, te=512):
    e, d = tab.shape
    words = pl.pallas_call(
        _pack_kernel, grid=(e // te,),
        in_specs=[pl.BlockSpec((te, d), lambda i: (i, 0))],
        out_specs=pl.BlockSpec((te, d // 2), lambda i: (i, 0)),
        out_shape=jax.ShapeDtypeStruct((e, d // 2), I32),
        compiler_params=_cparams(("arbitrary",)), name="pack_table",
    )(tab)
    return words.reshape(e * (d // 2) // LANES, LANES)


def _cmp_to_slc_wide(rows, n_slc):
    st = np.arange(rows) * CMP_STRIDE
    js = np.arange(n_slc) * SLC_LEN
    ov = np.minimum(st[:, None] + CMP_LEN, js[None, :] + SLC_LEN) - np.maximum(st[:, None], js[None, :])
    c2s = np.maximum(ov, 0).astype(np.float32) / CMP_STRIDE
    wide = np.zeros((N_KV, rows, LANES), np.float32)
    nblk = LANES // N_KV
    for g in range(N_KV):
        wide[g, :, g * nblk:g * nblk + n_slc] = c2s
    return jnp.asarray(wide)


def _gate_expanders():
    eb = np.zeros((3, LANES, ATT_WIDTH), np.float32)
    for hd in range(N_HEADS):
        for br in range(3):
            eb[br, hd * 3 + br, hd * HEAD_DIM:(hd + 1) * HEAD_DIM] = 1.0
    return jnp.asarray(eb)


class _Consts:
    def __init__(self, seq):
        self.rows = seq // CMP_STRIDE
        self.rope = _rope_lane_tables(jnp.arange(seq))
        cpos = jnp.arange(self.rows) * CMP_STRIDE + CMP_LEN - 1
        crope = _rope_lane_tables(cpos)
        ident = (jnp.ones_like(crope[0]), jnp.zeros_like(crope[0]), jnp.zeros_like(crope[0]))
        self.crope = tuple(jnp.stack([a, b]) for a, b in zip(crope, ident))
        self.c2s = _cmp_to_slc_wide(self.rows, seq // SLC_LEN)
        self.eb = _gate_expanders()


def _token_mixer_layer(x2, sc1, sh1, g1, w_in, cmp_pe, cmp_w1, cmp_w2, w_pool, pool_scale, w_lift, w_o,
                       lng, lnb, cst, bsz, seq, alpha):
    d = D_MODEL
    rows = cst.rows
    s1 = ATT_WIDTH + KV_WIDTH
    s2 = s1 + GATE_NSA
    s3 = s2 + POOL_WIDTH
    w_gate = jnp.pad(w_in[:, s1:s2], ((0, 0), (0, LANES - GATE_NSA)))
    w_all = jnp.concatenate([w_in[:, :s1], w_in[:, s2:s3], w_in[:, s3:], w_gate], axis=1).astype(BF16)
    q, kv, p_in, g_mrg, g_nsa = _inproj_call(x2, sc1, sh1, w_all, *cst.rope, seq, 256)

    eye_g = jnp.eye(N_KV, dtype=F32)
    zc = kv[:, :2 * LANES].reshape(bsz, rows, CMP_STRIDE, 2, LANES)
    zc = jnp.transpose(zc, (0, 3, 1, 2, 4)).reshape(bsz, 2, rows, CMP_STRIDE * LANES)
    w1x = jnp.einsum('klde,gh->klgdhe', cmp_w1.reshape(2, CMP_LEN, HEAD_DIM, HEAD_DIM), eye_g)
    w1x = w1x.reshape(2, CMP_LEN * LANES, LANES)
    half = CMP_STRIDE * LANES
    pex = jnp.broadcast_to(cmp_pe[:, :, None, :], (2, CMP_LEN, N_KV, HEAD_DIM)).reshape(2, 1, CMP_LEN * LANES)
    w2x = jnp.einsum('kef,gh->kgehf', cmp_w2, eye_g).reshape(2, LANES, LANES)
    kvc = _compress_call(zc, pex[:, :, :half], pex[:, :, half:], w1x[:, :half], w1x[:, half:], w2x,
                         *cst.crope)

    o_cmp, sel = _cmpsel_call(q, kvc, cst.c2s.astype(BF16), seq, 256)
    o_slc = _slc_call(q, kv, sel, seq, 256, 512)
    o_win = _win_call(q, kv, seq, 256)
    o_pool = _pool_call(p_in, w_pool.astype(BF16), pool_scale.reshape(1, -1), seq, 512)
    return _merge_call(o_cmp, o_slc, o_win, g_nsa, o_pool, g_mrg, x2, g1,
                       lng.reshape(1, d), lnb.reshape(1, d),
                       w_lift.astype(BF16), w_o.astype(BF16), cst.eb, seq, 256, alpha)


SC_CORES = 2
SC_SUBCORES = 16
SC_LANES = 16
SC_WORKERS = SC_CORES * SC_SUBCORES
SC_ROWS = 32
SC_TOKEN_SHARE = 5
SC_BLOCK = 256


def _sc_params():
    cp = pltpu.CompilerParams()
    if "needs_layout_passes" in pltpu.CompilerParams.__dataclass_fields__:
        cp = dataclasses.replace(cp, needs_layout_passes=False)
    return cp


def _sc_load_experts(erow_hbm, t, idx_v):
    pltpu.sync_copy(erow_hbm.at[t], idx_v)
    shift = jnp.full((SC_LANES,), (D_MODEL // 2 // LANES).bit_length() - 1, I32)
    for j in range(PEER_SEL // SC_LANES):
        sl = pl.ds(j * SC_LANES, SC_LANES)
        idx_v[sl] = lax.shift_right_logical(idx_v[sl], shift)


def _sc_gather_pipeline(tab_hbm, idx_v, rows_v, sems, nch, chunk_fn):
    def gather(c):
        return pltpu.async_copy(tab_hbm.at[idx_v.at[pl.ds(c * SC_ROWS, SC_ROWS)]],
                                rows_v.at[c % 2], sems.at[c % 2])

    pending = {0: gather(0)}
    for c in range(nch):
        if c + 1 < nch:
            pending[c + 1] = gather(c + 1)
        pending.pop(c).wait()
        chunk_fn(c, rows_v.at[c % 2])


def _sc_v_call(erow, coef, vtab, tok0):
    m = coef.shape[0]
    d = vtab.shape[1]
    assert m % SC_WORKERS == 0 and PEER_SEL % SC_ROWS == 0 and d % SC_BLOCK == 0
    tpw = m // SC_WORKERS
    nch = PEER_SEL // SC_ROWS
    nvec = SC_BLOCK // SC_LANES
    mesh = plsc.VectorSubcoreMesh(core_axis_name="c", subcore_axis_name="s")

    @functools.partial(
        pl.kernel, mesh=mesh, out_type=jax.ShapeDtypeStruct((m, d), F32),
        scratch_types=[pltpu.VMEM((PEER_SEL,), I32), pltpu.VMEM((PEER_SEL,), F32),
                       pltpu.VMEM((2, SC_ROWS, d), F32), pltpu.VMEM((d,), F32),
                       pltpu.SemaphoreType.DMA((2,))],
        compiler_params=_sc_params(), name="sc_peer_v")
    def run(tab_hbm, idx_hbm, coef_hbm, out_hbm, idx_v, coef_v, rows_v, acc_v, sems):
        wid = lax.axis_index("s") * SC_CORES + lax.axis_index("c")
        base = wid * tpw
        zero = jnp.zeros((SC_LANES,), F32)

        @pl.loop(0, tpw)
        def _(i):
            t = base + i
            _sc_load_experts(idx_hbm, tok0 + t, idx_v)
            pltpu.sync_copy(coef_hbm.at[t], coef_v)
            for j in range(d // SC_LANES):
                acc_v[pl.ds(j * SC_LANES, SC_LANES)] = zero

            def chunk(c, buf):
                @pl.loop(0, d // SC_BLOCK)
                def _(lb):
                    lane0 = pl.multiple_of(lb * SC_BLOCK, SC_BLOCK)
                    accs = tuple(acc_v[pl.ds(lane0 + q * SC_LANES, SC_LANES)] for q in range(nvec))

                    @plsc.parallel_loop(0, SC_ROWS, unroll=2, carry=accs)
                    def accs(r, acc):
                        ck = plsc.load_gather(coef_v, [jnp.full((SC_LANES,), c * SC_ROWS, I32) + r])
                        return tuple(acc[q] + ck * buf[r, pl.ds(lane0 + q * SC_LANES, SC_LANES)]
                                     for q in range(nvec))

                    for q in range(nvec):
                        acc_v[pl.ds(lane0 + q * SC_LANES, SC_LANES)] = accs[q]

            _sc_gather_pipeline(tab_hbm, idx_v, rows_v, sems, nch, chunk)
            pltpu.sync_copy(acc_v, out_hbm.at[t])

    return run(vtab, erow, coef)


def _sc_u_call(erow, hrows, utab, tok0, m):
    d = utab.shape[1]
    assert m % SC_WORKERS == 0 and PEER_SEL % SC_ROWS == 0 and d % SC_BLOCK == 0
    tpw = m // SC_WORKERS
    nch = PEER_SEL // SC_ROWS
    nvec = SC_BLOCK // SC_LANES
    mesh = plsc.VectorSubcoreMesh(core_axis_name="c", subcore_axis_name="s")

    @functools.partial(
        pl.kernel, mesh=mesh, out_type=jax.ShapeDtypeStruct((m, PEER_SEL), F32),
        scratch_types=[pltpu.VMEM((PEER_SEL,), I32), pltpu.VMEM((d,), F32),
                       pltpu.VMEM((2, SC_ROWS, d), F32), pltpu.VMEM((PEER_SEL * SC_LANES,), F32),
                       pltpu.VMEM((PEER_SEL,), F32), pltpu.SemaphoreType.DMA((2,))],
        compiler_params=_sc_params(), name="sc_peer_u")
    def run(tab_hbm, idx_hbm, h_hbm, out_hbm, idx_v, h_v, rows_v, part_v, a_v, sems):
        wid = lax.axis_index("s") * SC_CORES + lax.axis_index("c")
        base = wid * tpw
        zero = jnp.zeros((SC_LANES,), F32)
        lane_iota = lax.iota(I32, SC_LANES)

        @pl.loop(0, tpw)
        def _(i):
            t = base + i
            _sc_load_experts(idx_hbm, tok0 + t, idx_v)
            pltpu.sync_copy(h_hbm.at[tok0 + t], h_v)
            for j in range(PEER_SEL):
                part_v[pl.ds(j * SC_LANES, SC_LANES)] = zero

            def chunk(c, buf):
                @pl.loop(0, d // SC_BLOCK)
                def _(lb):
                    lane0 = pl.multiple_of(lb * SC_BLOCK, SC_BLOCK)
                    hs = [h_v[pl.ds(lane0 + q * SC_LANES, SC_LANES)] for q in range(nvec)]

                    @plsc.parallel_loop(0, SC_ROWS, unroll=2)
                    def _(r):
                        ps = [hs[q] * buf[r, pl.ds(lane0 + q * SC_LANES, SC_LANES)] for q in range(nvec)]
                        while len(ps) > 1:
                            ps = [x + y for x, y in zip(ps[0::2], ps[1::2])]
                        row = pl.multiple_of((c * SC_ROWS + r) * SC_LANES, SC_LANES)
                        plsc.addupdate(part_v.at[pl.ds(row, SC_LANES)], ps[0])

            _sc_gather_pipeline(tab_hbm, idx_v, rows_v, sems, nch, chunk)
            for g in range(PEER_SEL // SC_LANES):
                rowbase = (g * SC_LANES + lane_iota) * SC_LANES
                acc = plsc.load_gather(part_v, [rowbase])
                for l in range(1, SC_LANES):
                    acc = acc + plsc.load_gather(part_v, [rowbase + l])
                a_v[pl.ds(g * SC_LANES, SC_LANES)] = acc
            pltpu.sync_copy(a_v, out_hbm.at[t])

    return run(utab, erow, hrows)


def _coef_kernel(a_ref, g_ref, o_ref):
    o_ref[...] = g_ref[...] * jax.nn.gelu(a_ref[...])


def _coef_call(a, gate, tm, tok0):
    m = a.shape[0]
    assert tok0 % tm == 0
    spec = pl.BlockSpec((tm, PEER_SEL), lambda i: (i, 0))
    gspec = pl.BlockSpec((tm, PEER_SEL), lambda i: (i + tok0 // tm, 0))
    return pl.pallas_call(
        _coef_kernel, grid=(m // tm,), in_specs=[spec, gspec], out_specs=spec,
        out_shape=jax.ShapeDtypeStruct((m, PEER_SEL), F32),
        compiler_params=_cparams(("arbitrary",)), name="peer_coef",
    )(a, gate)


def _peer_layer(x2, sc2, sh2, g2, peer_wq, peer_keys, peer_u, peer_v, lng, lnb, seq, alpha,
                tt_route=256, tt_gather=64, tm=512, sc_tokens=0):
    n, d = x2.shape
    keys = peer_keys.reshape(2 * PEER_HEADS, N_KEYS, PEER_DK // 2)
    h2, hv, erow, gate = _route_call(x2, sc2, sh2, peer_wq.astype(BF16), keys.astype(BF16), seq, tt_route)
    n_tc = n - sc_tokens
    coef = _peer_u_call(erow, _pack_table(peer_u), hv, gate, tt_gather, n_tc)
    if sc_tokens:
        a_sc = _sc_u_call(erow, h2, peer_u, n_tc, sc_tokens)
        a_sc, coef = lax.optimization_barrier((a_sc, coef))
        coef_sc = _coef_call(a_sc, gate, tm, n_tc)
    y = _peer_v_call(erow, coef, _pack_table(peer_v), tt_gather, n_tc).reshape(n_tc, d)
    if sc_tokens:
        y = jnp.concatenate([y, _sc_v_call(erow, coef_sc, peer_v, n_tc)], axis=0)
    return _resln_call(x2, y, g2, lng.reshape(1, d), lnb.reshape(1, d), seq, tm, alpha)


def kernel(x, c, w_ada, b_ada, w_in, cmp_pe, cmp_w1, cmp_w2, w_pool, pool_scale, w_lift, w_o,
           ln_g, ln_b, peer_wq, peer_keys, peer_u, peer_v):
    bsz, seq, d = x.shape
    depth = w_ada.shape[0]
    n = bsz * seq
    assert d == D_MODEL and seq % 512 == 0 and SLC_TOPN <= seq // SLC_LEN <= LANES // N_KV
    alpha = (2 * depth) ** 0.25

    c_pad = jnp.zeros((SUBLANES, d), F32).at[:bsz].set(c)
    mods = _ada_call(c_pad, w_ada, b_ada)[:, :bsz]
    cst = _Consts(seq)
    x2 = x.reshape(n, d)
    for l in range(depth):
        sh1, sc1, g1, sh2, sc2, g2 = (mods[l][:, i * d:(i + 1) * d].reshape(bsz, 1, d) for i in range(6))
        x2 = _token_mixer_layer(x2, sc1, sh1, g1, w_in[l], cmp_pe[l], cmp_w1[l], cmp_w2[l], w_pool[l],
                                pool_scale[l], w_lift[l], w_o[l], ln_g[l, 0], ln_b[l, 0], cst, bsz, seq, alpha)
        x2 = _peer_layer(x2, sc2, sh2, g2, peer_wq[l], peer_keys[l], peer_u[l], peer_v[l],
                         ln_g[l, 1], ln_b[l, 1], seq, alpha, sc_tokens=SC_TOKEN_SHARE * n // 16)
    return x2.reshape(bsz, seq, d)
```

```python
import dataclasses
import functools

import jax
import jax.numpy as jnp
import numpy as np
from jax import lax
from jax.experimental import pallas as pl
from jax.experimental.pallas import tpu as pltpu
from jax.experimental.pallas import tpu_sc as plsc

F32 = jnp.float32
BF16 = jnp.bfloat16
I32 = jnp.int32
HI = lax.Precision.HIGHEST

D_MODEL = 1024
N_HEADS = 8
HEAD_DIM = 64
N_KV = 2
HPG = N_HEADS // N_KV
ROT_DIM = HEAD_DIM // 4
ROT_HALF = ROT_DIM // 2
ROPE_THETA = 500000.0
CMP_LEN = 32
CMP_STRIDE = 16
SLC_LEN = 64
SLC_TOPN = 16
WINDOW = 512
SCALE = HEAD_DIM ** -0.5
NEG = -1e30
FORCE_INIT = 1e6
FORCE_LOCAL = 2e6
POOL_GROUPS = 4
POOL_WINDOWS = (2, 4, 8, 16)
POOL_WIDTH = 512
POOL_GW = POOL_WIDTH // POOL_GROUPS
POOL_HALO = 16
ATT_WIDTH = N_HEADS * HEAD_DIM
KV_WIDTH = 3 * 2 * N_KV * HEAD_DIM
GATE_NSA = 3 * N_HEADS
MERGE_GATES = 2 * D_MODEL
PEER_HEADS = 8
N_KEYS = 128
PEER_TOPK = 16
PEER_DK = 128
PEER_SEL = PEER_HEADS * PEER_TOPK
LN_EPS = 1e-5

LANES = 128
SUBLANES = 8
VMEM_LIMIT = 56 * 1024 * 1024

NT_DIMS = (((1,), (1,)), ((), ()))


def _cparams(sem):
    return pltpu.CompilerParams(dimension_semantics=sem, vmem_limit_bytes=VMEM_LIMIT)


def _ada_kernel(c_ref, w_ref, b_ref, o_ref):
    c = c_ref[...]
    ca = c * jax.nn.sigmoid(c)
    o_ref[...] = jnp.dot(ca, w_ref[...], precision=HI, preferred_element_type=F32) + b_ref[...]


def _ada_call(c_pad, w_ada, b_ada):
    depth = w_ada.shape[0]
    nblk = w_ada.shape[2] // D_MODEL
    rows = c_pad.shape[0]
    return pl.pallas_call(
        _ada_kernel,
        grid=(depth, nblk),
        in_specs=[
            pl.BlockSpec((rows, D_MODEL), lambda l, j: (0, 0)),
            pl.BlockSpec((None, D_MODEL, D_MODEL), lambda l, j: (l, 0, j)),
            pl.BlockSpec((None, 1, D_MODEL), lambda l, j: (l, 0, j)),
        ],
        out_specs=pl.BlockSpec((None, rows, D_MODEL), lambda l, j: (l, 0, j)),
        out_shape=jax.ShapeDtypeStruct((depth, rows, nblk * D_MODEL), F32),
        compiler_params=_cparams(("arbitrary", "arbitrary")),
        name="ada_mod",
    )(c_pad, w_ada, b_ada.reshape(depth, 1, -1))


IN_COLS = ATT_WIDTH + KV_WIDTH + POOL_WIDTH + MERGE_GATES + LANES


def _rope_lanes(z, rc, rs1, rs2):
    return z * rc + pltpu.roll(z, ROT_HALF, 1) * rs1 + pltpu.roll(z, LANES - ROT_HALF, 1) * rs2


def _inproj_kernel(x_ref, sc_ref, sh_ref, w_ref, rc_ref, rs1_ref, rs2_ref,
                   q_ref, kv_ref, p_ref, mrg_ref, gn_ref):
    h = x_ref[...] * (1.0 + sc_ref[...]) + sh_ref[...]
    a = jnp.dot(h.astype(BF16), w_ref[...], preferred_element_type=F32)
    rc, rs1, rs2 = rc_ref[...], rs1_ref[...], rs2_ref[...]
    for j in range(ATT_WIDTH // LANES):
        q_ref[:, j * LANES:(j + 1) * LANES] = _rope_lanes(a[:, j * LANES:(j + 1) * LANES], rc, rs1, rs2)
    for br in range(3):
        c0 = ATT_WIDTH + br * 2 * LANES
        k = a[:, c0:c0 + LANES]
        if br > 0:
            k = _rope_lanes(k, rc, rs1, rs2)
        kv_ref[:, br * 2 * LANES:br * 2 * LANES + LANES] = k
        kv_ref[:, br * 2 * LANES + LANES:(br + 1) * 2 * LANES] = a[:, c0 + LANES:c0 + 2 * LANES]
    c1 = ATT_WIDTH + KV_WIDTH
    p_ref[...] = a[:, c1:c1 + POOL_WIDTH]
    mrg_ref[...] = a[:, c1 + POOL_WIDTH:c1 + POOL_WIDTH + MERGE_GATES]
    gn_ref[...] = a[:, c1 + POOL_WIDTH + MERGE_GATES:]


def _inproj_call(x2, sc, sh, w, rc, rs1, rs2, seq, tm):
    n = x2.shape[0]
    tpb = seq // tm
    tok = lambda i: (i, 0)
    bat = lambda i: (i // tpb, 0, 0)
    pos = lambda i: (i % tpb, 0)
    full = lambda i: (0, 0)
    return pl.pallas_call(
        _inproj_kernel,
        grid=(n // tm,),
        in_specs=[
            pl.BlockSpec((tm, D_MODEL), tok),
            pl.BlockSpec((None, 1, D_MODEL), bat),
            pl.BlockSpec((None, 1, D_MODEL), bat),
            pl.BlockSpec((D_MODEL, IN_COLS), full),
            pl.BlockSpec((tm, LANES), pos),
            pl.BlockSpec((tm, LANES), pos),
            pl.BlockSpec((tm, LANES), pos),
        ],
        out_specs=[
            pl.BlockSpec((tm, ATT_WIDTH), tok),
            pl.BlockSpec((tm, KV_WIDTH), tok),
            pl.BlockSpec((tm, POOL_WIDTH), tok),
            pl.BlockSpec((tm, MERGE_GATES), tok),
            pl.BlockSpec((tm, LANES), tok),
        ],
        out_shape=[
            jax.ShapeDtypeStruct((n, ATT_WIDTH), F32),
            jax.ShapeDtypeStruct((n, KV_WIDTH), F32),
            jax.ShapeDtypeStruct((n, POOL_WIDTH), F32),
            jax.ShapeDtypeStruct((n, MERGE_GATES), F32),
            jax.ShapeDtypeStruct((n, LANES), F32),
        ],
        compiler_params=_cparams(("arbitrary",)),
        name="in_proj",
    )(x2, sc, sh, w, rc, rs1, rs2)


def _compress_kernel(z_ref, pet_ref, peb_ref, w1t_ref, w1b_ref, w2_ref, rc_ref, rs1_ref, rs2_ref, o_ref):
    z = z_ref[...]
    rows = z.shape[0]
    top = jnp.dot(z + pet_ref[...], w1t_ref[...], precision=HI, preferred_element_type=F32)
    bot = jnp.dot(z + peb_ref[...], w1b_ref[...], precision=HI, preferred_element_type=F32)
    pre = top + pltpu.roll(bot, rows - 1, 0)
    y = jnp.dot(jax.nn.gelu(pre), w2_ref[...], precision=HI, preferred_element_type=F32)
    o_ref[...] = _rope_lanes(y, rc_ref[...], rs1_ref[...], rs2_ref[...])


def _compress_call(z, pet, peb, w1t, w1b, w2, rc, rs1, rs2):
    b, _, rows, width = z.shape
    kvsel = lambda i, j: (j, 0, 0)
    return pl.pallas_call(
        _compress_kernel,
        grid=(b, 2),
        in_specs=[
            pl.BlockSpec((None, None, rows, width), lambda i, j: (i, j, 0, 0)),
            pl.BlockSpec((None, 1, width), kvsel),
            pl.BlockSpec((None, 1, width), kvsel),
            pl.BlockSpec((None, width, LANES), kvsel),
            pl.BlockSpec((None, width, LANES), kvsel),
            pl.BlockSpec((None, LANES, LANES), kvsel),
            pl.BlockSpec((None, rows, LANES), kvsel),
            pl.BlockSpec((None, rows, LANES), kvsel),
            pl.BlockSpec((None, rows, LANES), kvsel),
        ],
        out_specs=pl.BlockSpec((None, None, rows, LANES), lambda i, j: (i, j, 0, 0)),
        out_shape=jax.ShapeDtypeStruct((b, 2, rows, LANES), F32),
        compiler_params=_cparams(("arbitrary", "arbitrary")),
        name="compress",
    )(z, pet, peb, w1t, w1b, w2, rc, rs1, rs2)


def _cmpsel_kernel(q_ref, kc_ref, vc_ref, c2s_ref, o_ref, sel_ref, *, tq):
    t0 = pl.program_id(1) * tq
    kc = kc_ref[...]
    vc = vc_ref[...]
    rows = kc.shape[0]
    trow = t0 + lax.broadcasted_iota(I32, (tq, rows), 0)
    cend = lax.broadcasted_iota(I32, (tq, rows), 1) * CMP_STRIDE + (CMP_LEN - 1)
    vis = cend <= trow
    anyv = (trow[:, :1] >= CMP_LEN - 1).astype(F32)
    imp = jnp.zeros((tq, LANES), F32)
    for g in range(N_KV):
        kg = kc[:, g * HEAD_DIM:(g + 1) * HEAD_DIM].astype(BF16)
        vg = vc[:, g * HEAD_DIM:(g + 1) * HEAD_DIM].astype(BF16)
        psum = jnp.zeros((tq, rows), F32)
        for h in range(HPG):
            hd = g * HPG + h
            qh = q_ref[:, hd * HEAD_DIM:(hd + 1) * HEAD_DIM].astype(BF16)
            s = lax.dot_general(qh, kg, NT_DIMS, preferred_element_type=F32) * SCALE
            s = jnp.where(vis, s, NEG)
            e = jnp.exp(s - jnp.max(s, axis=-1, keepdims=True))
            p = e / jnp.sum(e, axis=-1, keepdims=True) * anyv
            o_ref[:, hd * HEAD_DIM:(hd + 1) * HEAD_DIM] = jnp.dot(
                p.astype(BF16), vg, preferred_element_type=F32)
            psum = psum + p
        imp = imp + jnp.dot(psum.astype(BF16), c2s_ref[g], preferred_element_type=F32)
    lane = lax.broadcasted_iota(I32, (tq, LANES), 1)
    blk = lane & (SLC_LEN - 1)
    cur = lax.shift_right_logical(t0 + lax.broadcasted_iota(I32, (tq, LANES), 0), 6)
    score = jnp.where(blk <= cur, imp, NEG)
    score = jnp.where(blk == 0, FORCE_INIT, score)
    score = jnp.where(blk == cur, FORCE_LOCAL, score)
    sc_t = score.T
    nblk = LANES // N_KV
    jrow = lax.broadcasted_iota(I32, (nblk, tq), 0)
    sel_parts = []
    for g in range(N_KV):
        sc = sc_t[g * nblk:(g + 1) * nblk]
        cnt = jnp.zeros((nblk, tq), I32)
        for k in range(nblk):
            rk = sc[k:k + 1, :]
            ge = (rk >= sc).astype(I32)
            gt = (rk > sc).astype(I32)
            cnt = cnt + jnp.where(jrow > k, ge, gt)
        sel_parts.append((cnt < SLC_TOPN).astype(F32))
    sel_ref[...] = jnp.concatenate(sel_parts, axis=0).T


def _cmpsel_call(q, kvc, c2s, seq, tq):
    n = q.shape[0]
    b = n // seq
    nq = seq // tq
    rows = kvc.shape[2]
    tok = lambda i, j: (i * nq + j, 0)
    return pl.pallas_call(
        functools.partial(_cmpsel_kernel, tq=tq),
        grid=(b, nq),
        in_specs=[
            pl.BlockSpec((tq, ATT_WIDTH), tok),
            pl.BlockSpec((None, None, rows, LANES), lambda i, j: (i, 0, 0, 0)),
            pl.BlockSpec((None, None, rows, LANES), lambda i, j: (i, 1, 0, 0)),
            pl.BlockSpec((N_KV, rows, LANES), lambda i, j: (0, 0, 0)),
        ],
        out_specs=[pl.BlockSpec((tq, ATT_WIDTH), tok), pl.BlockSpec((tq, LANES), tok)],
        out_shape=[jax.ShapeDtypeStruct((n, ATT_WIDTH), F32), jax.ShapeDtypeStruct((n, LANES), F32)],
        compiler_params=_cparams(("arbitrary", "arbitrary")),
        name="cmp_select",
    )(q, kvc, kvc, c2s)


def _slc_kernel(q_ref, k_ref, v_ref, sel_ref, o_ref, qs_sc, m_sc, l_sc, acc_sc, *, tq, tk):
    qi = pl.program_id(1)
    kt = pl.program_id(2)
    nk = pl.num_programs(2)

    @pl.when(kt == 0)
    def _init():
        for hd in range(N_HEADS):
            g, h = divmod(hd, HPG)
            qs_sc[g, h * tq:(h + 1) * tq, :] = (q_ref[:, hd * HEAD_DIM:(hd + 1) * HEAD_DIM] * SCALE).astype(BF16)
        m_sc[...] = jnp.full(m_sc.shape, NEG, F32)
        l_sc[...] = jnp.zeros(l_sc.shape, F32)
        acc_sc[...] = jnp.zeros(acc_sc.shape, F32)

    @pl.when(kt * tk <= qi * tq + (tq - 1))
    def _step():
        t = qi * tq + lax.broadcasted_iota(I32, (tq, tk), 0)
        kp = kt * tk + lax.broadcasted_iota(I32, (tq, tk), 1)
        causal = kp <= t
        nblk = LANES // N_KV
        jb = lax.broadcasted_iota(I32, (nblk, tk), 0)
        kb = lax.shift_right_logical(kt * tk + lax.broadcasted_iota(I32, (nblk, tk), 1), 6)
        expand = (jb == kb).astype(BF16)
        for g in range(N_KV):
            selg = sel_ref[:, g * nblk:(g + 1) * nblk].astype(BF16)
            member = jnp.dot(selg, expand, preferred_element_type=F32)
            bias = jnp.where(jnp.logical_and(causal, member > 0.5), 0.0, NEG)
            bias = jnp.concatenate([bias] * HPG, axis=0)
            kg = k_ref[:, g * HEAD_DIM:(g + 1) * HEAD_DIM].astype(BF16)
            vg = v_ref[:, g * HEAD_DIM:(g + 1) * HEAD_DIM].astype(BF16)
            s = lax.dot_general(qs_sc[g], kg, NT_DIMS, preferred_element_type=F32) + bias
            chunks = [s[:, c * LANES:(c + 1) * LANES] for c in range(tk // LANES)]
            mc = chunks[0]
            for x in chunks[1:]:
                mc = jnp.maximum(mc, x)
            m_old = m_sc[g]
            m_new = jnp.maximum(m_old, jnp.max(mc, axis=-1, keepdims=True))
            alpha = jnp.exp(m_old - m_new)
            ps = [jnp.exp(x - m_new) for x in chunks]
            lsum = ps[0]
            for x in ps[1:]:
                lsum = lsum + x
            l_sc[g] = alpha * l_sc[g] + lsum
            p = jnp.concatenate(ps, axis=1).astype(BF16)
            acc_sc[g] = alpha[:, :HEAD_DIM] * acc_sc[g] + jnp.dot(p, vg, preferred_element_type=F32)
            m_sc[g] = m_new

    @pl.when(kt == nk - 1)
    def _fin():
        for hd in range(N_HEADS):
            g, h = divmod(hd, HPG)
            l = jnp.sum(l_sc[g, h * tq:(h + 1) * tq, :], axis=-1, keepdims=True)
            o_ref[:, hd * HEAD_DIM:(hd + 1) * HEAD_DIM] = acc_sc[g, h * tq:(h + 1) * tq, :] / l


def _slc_call(q, kv, sel, seq, tq, tk):
    n = q.shape[0]
    b = n // seq
    nq = seq // tq
    nk = seq // tk
    tok = lambda i, j, k: (i * nq + j, 0)

    def key_map(col):
        def f(i, j, k):
            last = (j * tq + tq - 1) // tk
            return (i * nk + jnp.minimum(k, last), col)
        return f

    return pl.pallas_call(
        functools.partial(_slc_kernel, tq=tq, tk=tk),
        grid=(b, nq, nk),
        in_specs=[
            pl.BlockSpec((tq, ATT_WIDTH), tok),
            pl.BlockSpec((tk, LANES), key_map(2)),
            pl.BlockSpec((tk, LANES), key_map(3)),
            pl.BlockSpec((tq, LANES), tok),
        ],
        out_specs=pl.BlockSpec((tq, ATT_WIDTH), tok),
        out_shape=jax.ShapeDtypeStruct((n, ATT_WIDTH), F32),
        scratch_shapes=[
            pltpu.VMEM((N_KV, HPG * tq, HEAD_DIM), BF16),
            pltpu.VMEM((N_KV, HPG * tq, LANES), F32),
            pltpu.VMEM((N_KV, HPG * tq, LANES), F32),
            pltpu.VMEM((N_KV, HPG * tq, HEAD_DIM), F32),
        ],
        compiler_params=_cparams(("arbitrary", "arbitrary", "arbitrary")),
        name="slc_attn",
    )(q, kv, kv, sel)


def _win_kernel(q_ref, *refs, tq, nkb):
    k_refs = refs[:nkb]
    v_refs = refs[nkb:2 * nkb]
    o_ref = refs[2 * nkb]
    qi = pl.program_id(1)
    t = qi * tq + lax.broadcasted_iota(I32, (tq, tq), 0)
    col = lax.broadcasted_iota(I32, (tq, tq), 1)
    biases = []
    for j in range(nkb):
        kp = (qi - (nkb - 1) + j) * tq + col
        diff = t - kp
        ok = jnp.logical_and(jnp.logical_and(diff >= 0, diff < WINDOW), kp >= 0)
        biases.append(jnp.concatenate([jnp.where(ok, 0.0, NEG)] * HPG, axis=0))
    for g in range(N_KV):
        qs = jnp.concatenate(
            [(q_ref[:, (g * HPG + h) * HEAD_DIM:(g * HPG + h + 1) * HEAD_DIM] * SCALE).astype(BF16)
             for h in range(HPG)], axis=0)
        chunks = []
        for j in range(nkb):
            kg = k_refs[j][:, g * HEAD_DIM:(g + 1) * HEAD_DIM].astype(BF16)
            s = lax.dot_general(qs, kg, NT_DIMS, preferred_element_type=F32) + biases[j]
            chunks += [s[:, c * LANES:(c + 1) * LANES] for c in range(tq // LANES)]
        mc = chunks[0]
        for x in chunks[1:]:
            mc = jnp.maximum(mc, x)
        m = jnp.broadcast_to(jnp.max(mc, axis=-1, keepdims=True), mc.shape)
        ps = [jnp.exp(x - m) for x in chunks]
        lsum = ps[0]
        for x in ps[1:]:
            lsum = lsum + x
        l = jnp.sum(lsum, axis=-1, keepdims=True)
        per = tq // LANES
        o = None
        for j in range(nkb):
            vg = v_refs[j][:, g * HEAD_DIM:(g + 1) * HEAD_DIM].astype(BF16)
            pj = jnp.concatenate(ps[j * per:(j + 1) * per], axis=1).astype(BF16)
            oj = jnp.dot(pj, vg, preferred_element_type=F32)
            o = oj if o is None else o + oj
        o = o / l
        for h in range(HPG):
            hd = g * HPG + h
            o_ref[:, hd * HEAD_DIM:(hd + 1) * HEAD_DIM] = o[h * tq:(h + 1) * tq]


def _win_call(q, kv, seq, tq):
    n = q.shape[0]
    b = n // seq
    nq = seq // tq
    nkb = WINDOW // tq + 1
    tok = lambda i, j: (i * nq + j, 0)

    def key_map(col, back):
        return lambda i, j: (i * nq + jnp.maximum(j - back, 0), col)

    k_specs = [pl.BlockSpec((tq, LANES), key_map(4, nkb - 1 - jj)) for jj in range(nkb)]
    v_specs = [pl.BlockSpec((tq, LANES), key_map(5, nkb - 1 - jj)) for jj in range(nkb)]
    return pl.pallas_call(
        functools.partial(_win_kernel, tq=tq, nkb=nkb),
        grid=(b, nq),
        in_specs=[pl.BlockSpec((tq, ATT_WIDTH), tok)] + k_specs + v_specs,
        out_specs=pl.BlockSpec((tq, ATT_WIDTH), tok),
        out_shape=jax.ShapeDtypeStruct((n, ATT_WIDTH), F32),
        compiler_params=_cparams(("arbitrary", "arbitrary")),
        name="win_attn",
    )(q, *([kv] * (2 * nkb)))


def _pool_kernel(p_ref, prev_ref, w_ref, sc_ref, o_ref, *, ts):
    i = pl.program_id(1)
    x = p_ref[...]
    prev = prev_ref[...] * (i > 0).astype(F32)
    xe = jnp.concatenate([prev, x], axis=0)
    t1 = (i * ts + 1 + lax.broadcasted_iota(I32, (ts, POOL_GW), 0)).astype(F32)
    for g, w in enumerate(POOL_WINDOWS):
        a = xe[:, g * POOL_GW:(g + 1) * POOL_GW]
        off = POOL_HALO
        span = 1
        while span < w:
            a = a[span:] + a[:-span]
            off -= span
            span *= 2
        sums = a[off:off + ts]
        cnt = jnp.minimum(t1, float(w))
        pooled = sums / cnt - x[:, g * POOL_GW:(g + 1) * POOL_GW]
        y = jnp.dot(pooled.astype(BF16), w_ref[g], preferred_element_type=F32)
        o_ref[:, g * POOL_GW:(g + 1) * POOL_GW] = y * sc_ref[:, g * POOL_GW:(g + 1) * POOL_GW]


def _pool_call(p_in, w_pool, pool_scale, seq, ts):
    n = p_in.shape[0]
    b = n // seq
    nt = seq // ts
    hpt = ts // POOL_HALO
    tok = lambda i, j: (i * nt + j, 0)
    return pl.pallas_call(
        functools.partial(_pool_kernel, ts=ts),
        grid=(b, nt),
        in_specs=[
            pl.BlockSpec((ts, POOL_WIDTH), tok),
            pl.BlockSpec((POOL_HALO, POOL_WIDTH), lambda i, j: (i * nt * hpt + jnp.maximum(j * hpt - 1, 0), 0)),
            pl.BlockSpec((POOL_GROUPS, POOL_GW, POOL_GW), lambda i, j: (0, 0, 0)),
            pl.BlockSpec((1, POOL_WIDTH), lambda i, j: (0, 0)),
        ],
        out_specs=pl.BlockSpec((ts, POOL_WIDTH), tok),
        out_shape=jax.ShapeDtypeStruct((n, POOL_WIDTH), F32),
        compiler_params=_cparams(("arbitrary", "arbitrary")),
        name="pool_mix",
    )(p_in, p_in, w_pool, pool_scale)


def _layer_norm(z, g, b):
    mu = jnp.mean(z, axis=-1, keepdims=True)
    zc = z - mu
    var = jnp.mean(zc * zc, axis=-1, keepdims=True)
    return zc * lax.rsqrt(var + LN_EPS) * g + b


def _merge_kernel(oc_ref, os_ref, ow_ref, gn_ref, op_ref, gm_ref, x_ref, g1_ref, lng_ref, lnb_ref,
                  wl_ref, wo_ref, eb_ref, o_ref, *, alpha):
    gate = jax.nn.sigmoid(gn_ref[...])
    branches = (oc_ref, os_ref, ow_ref)
    oatt = None
    for br in range(3):
        gx = jnp.dot(gate, eb_ref[br], precision=HI, preferred_element_type=F32)
        term = gx * branches[br][...]
        oatt = term if oatt is None else oatt + term
    la = jnp.dot(oatt.astype(BF16), wl_ref[0], preferred_element_type=F32)
    lb = jnp.dot(op_ref[...].astype(BF16), wl_ref[1], preferred_element_type=F32)
    gm = jax.nn.sigmoid(gm_ref[...])
    merged = gm[:, :D_MODEL] * la + gm[:, D_MODEL:] * lb
    y = jnp.dot(merged.astype(BF16), wo_ref[...], preferred_element_type=F32)
    z = alpha * x_ref[...] + g1_ref[...] * y
    o_ref[...] = _layer_norm(z, lng_ref[...], lnb_ref[...])


def _merge_call(oc, osl, ow, gn, op, gm, x2, g1, lng, lnb, wl, wo, eb, seq, tm, alpha):
    n = x2.shape[0]
    tpb = seq // tm
    tok = lambda i: (i, 0)
    bat = lambda i: (i // tpb, 0, 0)
    return pl.pallas_call(
        functools.partial(_merge_kernel, alpha=alpha),
        grid=(n // tm,),
        in_specs=[
            pl.BlockSpec((tm, ATT_WIDTH), tok), pl.BlockSpec((tm, ATT_WIDTH), tok),
            pl.BlockSpec((tm, ATT_WIDTH), tok), pl.BlockSpec((tm, LANES), tok),
            pl.BlockSpec((tm, POOL_WIDTH), tok), pl.BlockSpec((tm, MERGE_GATES), tok),
            pl.BlockSpec((tm, D_MODEL), tok),
            pl.BlockSpec((None, 1, D_MODEL), bat),
            pl.BlockSpec((1, D_MODEL), lambda i: (0, 0)), pl.BlockSpec((1, D_MODEL), lambda i: (0, 0)),
            pl.BlockSpec((2, ATT_WIDTH, D_MODEL), lambda i: (0, 0, 0)),
            pl.BlockSpec((D_MODEL, D_MODEL), lambda i: (0, 0)),
            pl.BlockSpec((3, LANES, ATT_WIDTH), lambda i: (0, 0, 0)),
        ],
        out_specs=pl.BlockSpec((tm, D_MODEL), tok),
        out_shape=jax.ShapeDtypeStruct((n, D_MODEL), F32),
        compiler_params=_cparams(("arbitrary",)),
        name="merge_out",
    )(oc, osl, ow, gn, op, gm, x2, g1, lng, lnb, wl, wo, eb)


def _extract_top(cur, ids, n):
    rows = cur.shape[0]
    rio = lax.broadcasted_iota(I32, cur.shape, 0)
    vals, outs = [], []
    for _ in range(n):
        m = jnp.max(cur, axis=0, keepdims=True)
        pos = jnp.min(jnp.where(cur == m, rio, rows), axis=0, keepdims=True)
        hit = rio == pos
        vals.append(m)
        outs.append(pos if ids is None else jnp.max(jnp.where(hit, ids, -1), axis=0, keepdims=True))
        cur = jnp.where(hit, -jnp.inf, cur)
    return jnp.concatenate(vals, axis=0), jnp.concatenate(outs, axis=0)


def _route_kernel(x_ref, sc_ref, sh_ref, wq_ref, keys_ref, h_ref, hv_ref, e_ref, g_ref,
                  st_sc, ts_sc, ti_sc, eo_sc, go_sc):
    h = x_ref[...] * (1.0 + sc_ref[...]) + sh_ref[...]
    h_ref[...] = h
    for r in range(SUBLANES):
        hv_ref[:, r, :] = h[:, r * LANES:(r + 1) * LANES]
    qp = jnp.dot(h.astype(BF16), wq_ref[...], preferred_element_type=F32).astype(BF16)
    half = PEER_DK // 2
    for hp in range(2 * PEER_HEADS):
        st_sc[hp] = lax.dot_general(keys_ref[hp], qp[:, hp * half:(hp + 1) * half], NT_DIMS,
                                    preferred_element_type=F32)

    def half_body(hp, carry):
        vals, ids = _extract_top(st_sc[hp], None, PEER_TOPK)
        ts_sc[hp] = vals
        ti_sc[hp] = ids
        return carry

    lax.fori_loop(0, 2 * PEER_HEADS, half_body, 0)

    def head_body(hh, carry):
        s1, s2 = ts_sc[2 * hh], ts_sc[2 * hh + 1]
        i1, i2 = ti_sc[2 * hh], ti_sc[2 * hh + 1]
        brow = lax.broadcasted_iota(I32, (SUBLANES, s1.shape[1]), 0)
        cands = [s1[0:1, :] + s2]
        cidxs = [i1[0:1, :] * N_KEYS + i2]
        for a in range(1, SUBLANES):
            ok = brow < PEER_TOPK // (a + 1)
            cands.append(jnp.where(ok, s1[a:a + 1, :] + s2[:SUBLANES], -jnp.inf))
            cidxs.append(i1[a:a + 1, :] * N_KEYS + i2[:SUBLANES])
        cands.append(s1[SUBLANES:] + s2[0:1, :])
        cidxs.append(i1[SUBLANES:] * N_KEYS + i2[0:1, :])
        sv, ei = _extract_top(jnp.concatenate(cands, axis=0), jnp.concatenate(cidxs, axis=0), PEER_TOPK)
        ex = jnp.exp(sv - sv[0:1, :])
        go_sc[hh] = ex / jnp.sum(ex, axis=0, keepdims=True)
        eo_sc[hh] = ei.astype(F32)
        return carry

    lax.fori_loop(0, PEER_HEADS, head_body, 0)
    e_all = jnp.concatenate([eo_sc[hh] for hh in range(PEER_HEADS)], axis=0)
    g_all = jnp.concatenate([go_sc[hh] for hh in range(PEER_HEADS)], axis=0)
    e_ref[...] = e_all.T.astype(I32) * (D_MODEL // 2 // LANES)
    g_ref[...] = g_all.T


def _route_call(x2, sc, sh, wq, keys, seq, tt):
    n = x2.shape[0]
    tpb = seq // tt
    tok = lambda i: (i, 0)
    bat = lambda i: (i // tpb, 0, 0)
    nhp = 2 * PEER_HEADS
    return pl.pallas_call(
        _route_kernel,
        grid=(n // tt,),
        in_specs=[
            pl.BlockSpec((tt, D_MODEL), tok),
            pl.BlockSpec((None, 1, D_MODEL), bat),
            pl.BlockSpec((None, 1, D_MODEL), bat),
            pl.BlockSpec((D_MODEL, PEER_HEADS * PEER_DK), lambda i: (0, 0)),
            pl.BlockSpec((nhp, N_KEYS, PEER_DK // 2), lambda i: (0, 0, 0)),
        ],
        out_specs=[pl.BlockSpec((tt, D_MODEL), tok), pl.BlockSpec((tt, SUBLANES, LANES), lambda i: (i, 0, 0)),
                   pl.BlockSpec((tt, PEER_SEL), tok), pl.BlockSpec((tt, PEER_SEL), tok)],
        out_shape=[jax.ShapeDtypeStruct((n, D_MODEL), F32), jax.ShapeDtypeStruct((n, SUBLANES, LANES), F32),
                   jax.ShapeDtypeStruct((n, PEER_SEL), I32), jax.ShapeDtypeStruct((n, PEER_SEL), F32)],
        scratch_shapes=[
            pltpu.VMEM((nhp, N_KEYS, tt), F32),
            pltpu.VMEM((nhp, PEER_TOPK, tt), F32),
            pltpu.VMEM((nhp, PEER_TOPK, tt), I32),
            pltpu.VMEM((PEER_HEADS, PEER_TOPK, tt), F32),
            pltpu.VMEM((PEER_HEADS, PEER_TOPK, tt), F32),
        ],
        compiler_params=_cparams(("arbitrary",)),
        name="peer_route",
    )(x2, sc, sh, wq, keys)


HALF_ROWS = SUBLANES // 2
HI_MASK = -65536
PAIR_TILES = PEER_SEL // 2
PAIR_ROWS = PAIR_TILES * SUBLANES


def _load_two_experts(tab_ref, ra, rb):
    wa = tab_ref[pl.ds(pl.multiple_of(ra, HALF_ROWS), HALF_ROWS), :]
    wb = tab_ref[pl.ds(pl.multiple_of(rb, HALF_ROWS), HALF_ROWS), :]
    w2 = jnp.concatenate([wa, wb], axis=0)
    return lax.bitcast_convert_type(w2 << 16, F32), lax.bitcast_convert_type(w2 & HI_MASK, F32)


def _fold_pairs(vs):
    row = lax.broadcasted_iota(I32, (SUBLANES, LANES), 0)
    shift = HALF_ROWS // 2
    while len(vs) > 1:
        low = (row & shift) == 0
        vs = [jnp.where(low, a + pltpu.roll(a, SUBLANES - shift, 0), b + pltpu.roll(b, shift, 0))
              for a, b in zip(vs[0::2], vs[1::2])]
        shift //= 2
    return vs[0]


def _fold_order():
    idx = [[2 * i if r < HALF_ROWS else 2 * i + 1 for r in range(SUBLANES)] for i in range(HALF_ROWS)]
    shift = HALF_ROWS // 2
    while len(idx) > 1:
        idx = [[a[r] if (r & shift) == 0 else b[r] for r in range(SUBLANES)]
               for a, b in zip(idx[0::2], idx[1::2])]
        shift //= 2
    return idx[0]


def _peer_u_kernel(e_sm, tab_ref, hv_ref, gate_ref, coef_ref, *, tt):
    row = lax.broadcasted_iota(I32, (SUBLANES, LANES), 0)
    low = row < HALF_ROWS
    eye = (lax.broadcasted_iota(I32, (PEER_SEL, LANES), 0) ==
           lax.broadcasted_iota(I32, (PEER_SEL, LANES), 1))
    order = _fold_order()

    def finish(t, part):
        col = jnp.sum(part, axis=-1, keepdims=True)
        a_row = jnp.sum(jnp.where(eye, col, 0.0), axis=0, keepdims=True)
        coef_ref[t] = gate_ref[pl.ds(t, 1), :] * jax.nn.gelu(a_row)

    def token(t, part_prev):
        finish(jnp.maximum(t - 1, 0), part_prev)
        hv = hv_ref[t]
        hsw = pltpu.roll(hv, HALF_ROWS, 0)
        h_lo = jnp.where(low, hv, hsw)
        h_hi = jnp.where(low, hsw, hv)
        folded = []
        for j in range(PEER_SEL // SUBLANES):
            prods = []
            for i in range(HALF_ROWS):
                ka = j * SUBLANES + order.index(2 * i)
                kb = j * SUBLANES + order.index(2 * i + 1)
                lo, hi = _load_two_experts(tab_ref, e_sm[t, ka], e_sm[t, kb])
                prods.append(lo * h_lo + hi * h_hi)
            folded.append(_fold_pairs(prods))
        return jnp.concatenate(folded, axis=0)

    last = lax.fori_loop(0, tt, token, jnp.zeros((PEER_SEL, LANES), F32))
    finish(tt - 1, last)


def _peer_v_kernel(e_sm, coef_ref, tab_ref, y_ref, cv_sc, *, tt, nacc):
    row = lax.broadcasted_iota(I32, (SUBLANES, LANES), 0)
    low = row < HALF_ROWS
    rr = lax.broadcasted_iota(I32, (PAIR_ROWS, LANES), 0)
    kk = lax.broadcasted_iota(I32, (PAIR_ROWS, LANES), 1)
    onehot = (kk == 2 * (rr >> 3) + ((rr >> 2) & 1)).astype(F32)
    ones = jnp.ones((LANES, LANES), BF16)

    def expand(t, slot):
        lhs = (onehot * coef_ref[t]).astype(BF16)
        cv_sc[slot] = jnp.dot(lhs, ones, preferred_element_type=F32)

    def process(t, slot):
        acc_lo = [jnp.zeros((SUBLANES, LANES), F32) for _ in range(nacc)]
        acc_hi = [jnp.zeros((SUBLANES, LANES), F32) for _ in range(nacc)]
        for j in range(PAIR_TILES):
            lo, hi = _load_two_experts(tab_ref, e_sm[t, 2 * j], e_sm[t, 2 * j + 1])
            cv = cv_sc[slot, j * SUBLANES:(j + 1) * SUBLANES, :]
            acc_lo[j % nacc] = acc_lo[j % nacc] + cv * lo
            acc_hi[j % nacc] = acc_hi[j % nacc] + cv * hi
        a_lo, a_hi = acc_lo[0], acc_hi[0]
        for i in range(1, nacc):
            a_lo = a_lo + acc_lo[i]
            a_hi = a_hi + acc_hi[i]
        a_lo = a_lo + pltpu.roll(a_lo, HALF_ROWS, 0)
        a_hi = a_hi + pltpu.roll(a_hi, HALF_ROWS, 0)
        y_ref[t] = jnp.where(low, a_lo, a_hi)

    expand(0, 0)

    def two_tokens(i, carry):
        t = 2 * i
        expand(t + 1, 1)
        process(t, 0)
        expand(jnp.minimum(t + 2, tt - 1), 0)
        process(t + 1, 1)
        return carry

    lax.fori_loop(0, tt // 2, two_tokens, 0)


def _resident_table_spec(tab):
    return pl.BlockSpec(tab.shape, lambda i: (0, 0), pipeline_mode=pl.Buffered(1))


def _peer_u_call(erow, tab, hv, gate, tt, n):
    return pl.pallas_call(
        functools.partial(_peer_u_kernel, tt=tt),
        grid=(n // tt,),
        in_specs=[
            pl.BlockSpec((tt, PEER_SEL), lambda i: (i, 0), memory_space=pltpu.SMEM),
            _resident_table_spec(tab),
            pl.BlockSpec((tt, SUBLANES, LANES), lambda i: (i, 0, 0)),
            pl.BlockSpec((tt, PEER_SEL), lambda i: (i, 0)),
        ],
        out_specs=pl.BlockSpec((tt, 1, PEER_SEL), lambda i: (i, 0, 0)),
        out_shape=jax.ShapeDtypeStruct((n, 1, PEER_SEL), F32),
        compiler_params=_cparams(("arbitrary",)),
        name="peer_u",
    )(erow, tab, hv, gate)


def _peer_v_call(erow, coef3, tab, tt, n):
    assert tt % 2 == 0
    return pl.pallas_call(
        functools.partial(_peer_v_kernel, tt=tt, nacc=4),
        grid=(n // tt,),
        in_specs=[
            pl.BlockSpec((tt, PEER_SEL), lambda i: (i, 0), memory_space=pltpu.SMEM),
            pl.BlockSpec((tt, 1, PEER_SEL), lambda i: (i, 0, 0)),
            _resident_table_spec(tab),
        ],
        out_specs=pl.BlockSpec((tt, SUBLANES, LANES), lambda i: (i, 0, 0)),
        out_shape=jax.ShapeDtypeStruct((n, SUBLANES, LANES), F32),
        scratch_shapes=[pltpu.VMEM((2, PAIR_ROWS, LANES), F32)],
        compiler_params=_cparams(("arbitrary",)),
        name="peer_v",
    )(erow, coef3, tab)


def _resln_kernel(x_ref, y_ref, g_ref, lng_ref, lnb_ref, o_ref, *, alpha):
    z = alpha * x_ref[...] + g_ref[...] * y_ref[...]
    o_ref[...] = _layer_norm(z, lng_ref[...], lnb_ref[...])


def _resln_call(x2, y2, g2, lng, lnb, seq, tm, alpha):
    n = x2.shape[0]
    tpb = seq // tm
    tok = lambda i: (i, 0)
    return pl.pallas_call(
        functools.partial(_resln_kernel, alpha=alpha),
        grid=(n // tm,),
        in_specs=[
            pl.BlockSpec((tm, D_MODEL), tok), pl.BlockSpec((tm, D_MODEL), tok),
            pl.BlockSpec((None, 1, D_MODEL), lambda i: (i // tpb, 0, 0)),
            pl.BlockSpec((1, D_MODEL), lambda i: (0, 0)), pl.BlockSpec((1, D_MODEL), lambda i: (0, 0)),
        ],
        out_specs=pl.BlockSpec((tm, D_MODEL), tok),
        out_shape=jax.ShapeDtypeStruct((n, D_MODEL), F32),
        compiler_params=_cparams(("arbitrary",)),
        name="res_ln",
    )(x2, y2, g2, lng, lnb)


def _rope_lane_tables(pos):
    inv = ROPE_THETA ** (-jnp.arange(0, ROT_DIM, 2, dtype=F32) / ROT_DIM)
    ang = pos.astype(F32)[:, None] * inv[None, :]
    cos, sin = jnp.cos(ang), jnp.sin(ang)
    lane = np.arange(LANES) % HEAD_DIM
    fidx = lane % ROT_HALF
    first = jnp.asarray(lane < ROT_HALF)
    second = jnp.asarray((lane >= ROT_HALF) & (lane < ROT_DIM))
    rot = jnp.asarray(lane < ROT_DIM)
    cl, sl = cos[:, fidx], sin[:, fidx]
    rc = jnp.where(rot, cl, 1.0)
    rs1 = jnp.where(second, sl, 0.0)
    rs2 = jnp.where(first, -sl, 0.0)
    return rc, rs1, rs2


def _pack_kernel(t_ref, o_ref):
    x = t_ref[...]
    half = x.shape[1] // 2
    lo = lax.bitcast_convert_type(x[:, :half].astype(BF16).astype(F32), I32)
    hi = lax.bitcast_convert_type(x[:, half:].astype(BF16).astype(F32), I32)
    o_ref[...] = lax.shift_right_logical(lo, jnp.full_like(lo, 16)) | hi


def _pack_table(tab, te=512):
    e, d = tab.shape
    words = pl.pallas_call(
        _pack_kernel, grid=(e // te,),
        in_specs=[pl.BlockSpec((te, d), lambda i: (i, 0))],
        out_specs=pl.BlockSpec((te, d // 2), lambda i: (i, 0)),
        out_shape=jax.ShapeDtypeStruct((e, d // 2), I32),
        compiler_params=_cparams(("arbitrary",)), name="pack_table",
    )(tab)
    return words.reshape(e * (d // 2) // LANES, LANES)


def _cmp_to_slc_wide(rows, n_slc):
    st = np.arange(rows) * CMP_STRIDE
    js = np.arange(n_slc) * SLC_LEN
    ov = np.minimum(st[:, None] + CMP_LEN, js[None, :] + SLC_LEN) - np.maximum(st[:, None], js[None, :])
    c2s = np.maximum(ov, 0).astype(np.float32) / CMP_STRIDE
    wide = np.zeros((N_KV, rows, LANES), np.float32)
    nblk = LANES // N_KV
    for g in range(N_KV):
        wide[g, :, g * nblk:g * nblk + n_slc] = c2s
    return jnp.asarray(wide)


def _gate_expanders():
    eb = np.zeros((3, LANES, ATT_WIDTH), np.float32)
    for hd in range(N_HEADS):
        for br in range(3):
            eb[br, hd * 3 + br, hd * HEAD_DIM:(hd + 1) * HEAD_DIM] = 1.0
    return jnp.asarray(eb)


class _Consts:
    def __init__(self, seq):
        self.rows = seq // CMP_STRIDE
        self.rope = _rope_lane_tables(jnp.arange(seq))
        cpos = jnp.arange(self.rows) * CMP_STRIDE + CMP_LEN - 1
        crope = _rope_lane_tables(cpos)
        ident = (jnp.ones_like(crope[0]), jnp.zeros_like(crope[0]), jnp.zeros_like(crope[0]))
        self.crope = tuple(jnp.stack([a, b]) for a, b in zip(crope, ident))
        self.c2s = _cmp_to_slc_wide(self.rows, seq // SLC_LEN)
        self.eb = _gate_expanders()


def _token_mixer_layer(x2, sc1, sh1, g1, w_in, cmp_pe, cmp_w1, cmp_w2, w_pool, pool_scale, w_lift, w_o,
                       lng, lnb, cst, bsz, seq, alpha):
    d = D_MODEL
    rows = cst.rows
    s1 = ATT_WIDTH + KV_WIDTH
    s2 = s1 + GATE_NSA
    s3 = s2 + POOL_WIDTH
    w_gate = jnp.pad(w_in[:, s1:s2], ((0, 0), (0, LANES - GATE_NSA)))
    w_all = jnp.concatenate([w_in[:, :s1], w_in[:, s2:s3], w_in[:, s3:], w_gate], axis=1).astype(BF16)
    q, kv, p_in, g_mrg, g_nsa = _inproj_call(x2, sc1, sh1, w_all, *cst.rope, seq, 256)

    eye_g = jnp.eye(N_KV, dtype=F32)
    zc = kv[:, :2 * LANES].reshape(bsz, rows, CMP_STRIDE, 2, LANES)
    zc = jnp.transpose(zc, (0, 3, 1, 2, 4)).reshape(bsz, 2, rows, CMP_STRIDE * LANES)
    w1x = jnp.einsum('klde,gh->klgdhe', cmp_w1.reshape(2, CMP_LEN, HEAD_DIM, HEAD_DIM), eye_g)
    w1x = w1x.reshape(2, CMP_LEN * LANES, LANES)
    half = CMP_STRIDE * LANES
    pex = jnp.broadcast_to(cmp_pe[:, :, None, :], (2, CMP_LEN, N_KV, HEAD_DIM)).reshape(2, 1, CMP_LEN * LANES)
    w2x = jnp.einsum('kef,gh->kgehf', cmp_w2, eye_g).reshape(2, LANES, LANES)
    kvc = _compress_call(zc, pex[:, :, :half], pex[:, :, half:], w1x[:, :half], w1x[:, half:], w2x,
                         *cst.crope)

    o_cmp, sel = _cmpsel_call(q, kvc, cst.c2s.astype(BF16), seq, 256)
    o_slc = _slc_call(q, kv, sel, seq, 256, 512)
    o_win = _win_call(q, kv, seq, 256)
    o_pool = _pool_call(p_in, w_pool.astype(BF16), pool_scale.reshape(1, -1), seq, 512)
    return _merge_call(o_cmp, o_slc, o_win, g_nsa, o_pool, g_mrg, x2, g1,
                       lng.reshape(1, d), lnb.reshape(1, d),
                       w_lift.astype(BF16), w_o.astype(BF16), cst.eb, seq, 256, alpha)


SC_CORES = 2
SC_SUBCORES = 16
SC_LANES = 16
SC_WORKERS = SC_CORES * SC_SUBCORES
SC_ROWS = 32
SC_TOKEN_SHARE = 5
SC_BLOCK = 256


def _sc_params():
    cp = pltpu.CompilerParams()
    if "needs_layout_passes" in pltpu.CompilerParams.__dataclass_fields__:
        cp = dataclasses.replace(cp, needs_layout_passes=False)
    return cp


def _sc_token_stream(tab_hbm, erow_hbm, side_hbm, tok0, side0, tpw, base, idx_bufs, side_bufs, rows_v, sems,
                     begin_fn, chunk_fn, end_fn):
    nch = PEER_SEL // SC_ROWS

    def idx_copy(t, s):
        return pltpu.make_async_copy(erow_hbm.at[tok0 + base + t], idx_bufs[s], sems.at[2 + s])

    def side_copy(t, s):
        return pltpu.make_async_copy(side_hbm.at[side0 + base + t], side_bufs[s], sems.at[4 + s])

    def row_copy(s, c):
        return pltpu.make_async_copy(tab_hbm.at[idx_bufs[s].at[pl.ds(c * SC_ROWS, SC_ROWS)]],
                                     rows_v.at[c % 2], sems.at[c % 2])

    def to_expert_ids(s):
        shift = jnp.full((SC_LANES,), (D_MODEL // 2 // LANES).bit_length() - 1, I32)
        for j in range(PEER_SEL // SC_LANES):
            sl = pl.ds(j * SC_LANES, SC_LANES)
            idx_bufs[s][sl] = lax.shift_right_logical(idx_bufs[s][sl], shift)

    idx_copy(0, 0).start()
    side_copy(0, 0).start()
    idx_copy(0, 0).wait()
    to_expert_ids(0)
    row_copy(0, 0).start()

    @pl.loop(0, tpw // 2)
    def _(i):
        for slot in (0, 1):
            other = 1 - slot
            t = 2 * i + slot
            tn = jnp.minimum(t + 1, tpw - 1)
            idx_copy(tn, other).start()
            side_copy(tn, other).start()
            side_copy(t, slot).wait()
            begin_fn(slot)
            for c in range(nch):
                if c + 1 < nch:
                    row_copy(slot, c + 1).start()
                else:
                    idx_copy(tn, other).wait()
                    to_expert_ids(other)
                    row_copy(other, 0).start()
                row_copy(slot, c).wait()
                chunk_fn(c, rows_v.at[c % 2], slot)
            end_fn(base + t, slot)

    row_copy(0, 0).wait()
    side_copy(0, 0).wait()


def _sc_v_call(erow, coef, vtab, tok0):
    m = coef.shape[0]
    d = vtab.shape[1]
    assert m % (2 * SC_WORKERS) == 0 and (PEER_SEL // SC_ROWS) % 2 == 0 and d % SC_BLOCK == 0
    tpw = m // SC_WORKERS
    nvec = SC_BLOCK // SC_LANES
    mesh = plsc.VectorSubcoreMesh(core_axis_name="c", subcore_axis_name="s")

    @functools.partial(
        pl.kernel, mesh=mesh, out_type=jax.ShapeDtypeStruct((m, d), F32),
        scratch_types=[pltpu.VMEM((PEER_SEL,), I32), pltpu.VMEM((PEER_SEL,), I32),
                       pltpu.VMEM((PEER_SEL,), F32), pltpu.VMEM((PEER_SEL,), F32),
                       pltpu.VMEM((2, SC_ROWS, d), F32), pltpu.VMEM((d,), F32),
                       pltpu.SemaphoreType.DMA((6,))],
        compiler_params=_sc_params(), name="sc_peer_v")
    def run(tab_hbm, erow_hbm, coef_hbm, out_hbm, idx_a, idx_b, coef_a, coef_b, rows_v, acc_v, sems):
        wid = lax.axis_index("s") * SC_CORES + lax.axis_index("c")
        zero = jnp.zeros((SC_LANES,), F32)
        coefs = (coef_a, coef_b)

        def begin(slot):
            for j in range(d // SC_LANES):
                acc_v[pl.ds(j * SC_LANES, SC_LANES)] = zero

        def chunk(c, buf, slot):
            @pl.loop(0, d // SC_BLOCK)
            def _(lb):
                lane0 = pl.multiple_of(lb * SC_BLOCK, SC_BLOCK)
                accs = tuple(acc_v[pl.ds(lane0 + q * SC_LANES, SC_LANES)] for q in range(nvec))

                @plsc.parallel_loop(0, SC_ROWS, unroll=2, carry=accs)
                def accs(r, acc):
                    ck = plsc.load_gather(coefs[slot], [jnp.full((SC_LANES,), c * SC_ROWS, I32) + r])
                    return tuple(acc[q] + ck * buf[r, pl.ds(lane0 + q * SC_LANES, SC_LANES)]
                                 for q in range(nvec))

                for q in range(nvec):
                    acc_v[pl.ds(lane0 + q * SC_LANES, SC_LANES)] = accs[q]

        def end(t, slot):
            pltpu.sync_copy(acc_v, out_hbm.at[t])

        _sc_token_stream(tab_hbm, erow_hbm, coef_hbm, tok0, 0, tpw, wid * tpw, (idx_a, idx_b), coefs,
                         rows_v, sems, begin, chunk, end)

    return run(vtab, erow, coef)


def _sc_u_call(erow, hrows, utab, tok0, m):
    d = utab.shape[1]
    assert m % (2 * SC_WORKERS) == 0 and (PEER_SEL // SC_ROWS) % 2 == 0 and d % SC_BLOCK == 0
    tpw = m // SC_WORKERS
    nvec = SC_BLOCK // SC_LANES
    mesh = plsc.VectorSubcoreMesh(core_axis_name="c", subcore_axis_name="s")

    @functools.partial(
        pl.kernel, mesh=mesh, out_type=jax.ShapeDtypeStruct((m, PEER_SEL), F32),
        scratch_types=[pltpu.VMEM((PEER_SEL,), I32), pltpu.VMEM((PEER_SEL,), I32),
                       pltpu.VMEM((d,), F32), pltpu.VMEM((d,), F32),
                       pltpu.VMEM((2, SC_ROWS, d), F32), pltpu.VMEM((PEER_SEL * SC_LANES,), F32),
                       pltpu.VMEM((PEER_SEL,), F32), pltpu.SemaphoreType.DMA((6,))],
        compiler_params=_sc_params(), name="sc_peer_u")
    def run(tab_hbm, erow_hbm, h_hbm, out_hbm, idx_a, idx_b, h_a, h_b, rows_v, part_v, a_v, sems):
        wid = lax.axis_index("s") * SC_CORES + lax.axis_index("c")
        zero = jnp.zeros((SC_LANES,), F32)
        lane_iota = lax.iota(I32, SC_LANES)
        hbufs = (h_a, h_b)

        def begin(slot):
            for j in range(PEER_SEL):
                part_v[pl.ds(j * SC_LANES, SC_LANES)] = zero

        def chunk(c, buf, slot):
            @pl.loop(0, d // SC_BLOCK)
            def _(lb):
                lane0 = pl.multiple_of(lb * SC_BLOCK, SC_BLOCK)
                hs = [hbufs[slot][pl.ds(lane0 + q * SC_LANES, SC_LANES)] for q in range(nvec)]

                @plsc.parallel_loop(0, SC_ROWS, unroll=2)
                def _(r):
                    ps = [hs[q] * buf[r, pl.ds(lane0 + q * SC_LANES, SC_LANES)] for q in range(nvec)]
                    while len(ps) > 1:
                        ps = [x + y for x, y in zip(ps[0::2], ps[1::2])]
                    row = pl.multiple_of((c * SC_ROWS + r) * SC_LANES, SC_LANES)
                    plsc.addupdate(part_v.at[pl.ds(row, SC_LANES)], ps[0])

        def end(t, slot):
            for g in range(PEER_SEL // SC_LANES):
                rowbase = (g * SC_LANES + lane_iota) * SC_LANES
                acc = plsc.load_gather(part_v, [rowbase])
                for l in range(1, SC_LANES):
                    acc = acc + plsc.load_gather(part_v, [rowbase + l])
                a_v[pl.ds(g * SC_LANES, SC_LANES)] = acc
            pltpu.sync_copy(a_v, out_hbm.at[t])

        _sc_token_stream(tab_hbm, erow_hbm, h_hbm, tok0, tok0, tpw, wid * tpw, (idx_a, idx_b), hbufs,
                         rows_v, sems, begin, chunk, end)

    return run(utab, erow, hrows)


def _coef_kernel(a_ref, g_ref, o_ref):
    o_ref[...] = g_ref[...] * jax.nn.gelu(a_ref[...])


def _coef_call(a, gate, tm, tok0):
    m = a.shape[0]
    assert tok0 % tm == 0
    spec = pl.BlockSpec((tm, PEER_SEL), lambda i: (i, 0))
    gspec = pl.BlockSpec((tm, PEER_SEL), lambda i: (i + tok0 // tm, 0))
    return pl.pallas_call(
        _coef_kernel, grid=(m // tm,), in_specs=[spec, gspec], out_specs=spec,
        out_shape=jax.ShapeDtypeStruct((m, PEER_SEL), F32),
        compiler_params=_cparams(("arbitrary",)), name="peer_coef",
    )(a, gate)


def _peer_layer(x2, sc2, sh2, g2, peer_wq, peer_keys, peer_u, peer_v, lng, lnb, seq, alpha,
                tt_route=256, tt_gather=64, tm=512, sc_tokens=0):
    n, d = x2.shape
    keys = peer_keys.reshape(2 * PEER_HEADS, N_KEYS, PEER_DK // 2)
    h2, hv, erow, gate = _route_call(x2, sc2, sh2, peer_wq.astype(BF16), keys.astype(BF16), seq, tt_route)
    n_tc = n - sc_tokens
    coef = _peer_u_call(erow, _pack_table(peer_u), hv, gate, tt_gather, n_tc)
    if sc_tokens:
        a_sc = _sc_u_call(erow, h2, peer_u, n_tc, sc_tokens)
        a_sc, coef = lax.optimization_barrier((a_sc, coef))
        coef_sc = _coef_call(a_sc, gate, tm, n_tc)
    y = _peer_v_call(erow, coef, _pack_table(peer_v), tt_gather, n_tc).reshape(n_tc, d)
    if sc_tokens:
        y = jnp.concatenate([y, _sc_v_call(erow, coef_sc, peer_v, n_tc)], axis=0)
    return _resln_call(x2, y, g2, lng.reshape(1, d), lnb.reshape(1, d), seq, tm, alpha)


def kernel(x, c, w_ada, b_ada, w_in, cmp_pe, cmp_w1, cmp_w2, w_pool, pool_scale, w_lift, w_o,
           ln_g, ln_b, peer_wq, peer_keys, peer_u, peer_v):
    bsz, seq, d = x.shape
    depth = w_ada.shape[0]
    n = bsz * seq
    assert d == D_MODEL and seq % 512 == 0 and SLC_TOPN <= seq // SLC_LEN <= LANES // N_KV
    alpha = (2 * depth) ** 0.25

    c_pad = jnp.zeros((SUBLANES, d), F32).at[:bsz].set(c)
    mods = _ada_call(c_pad, w_ada, b_ada)[:, :bsz]
    cst = _Consts(seq)
    x2 = x.reshape(n, d)
    for l in range(depth):
        sh1, sc1, g1, sh2, sc2, g2 = (mods[l][:, i * d:(i + 1) * d].reshape(bsz, 1, d) for i in range(6))
        x2 = _token_mixer_layer(x2, sc1, sh1, g1, w_in[l], cmp_pe[l], cmp_w1[l], cmp_w2[l], w_pool[l],
                                pool_scale[l], w_lift[l], w_o[l], ln_g[l, 0], ln_b[l, 0], cst, bsz, seq, alpha)
        x2 = _peer_layer(x2, sc2, sh2, g2, peer_wq[l], peer_keys[l], peer_u[l], peer_v[l],
                         ln_g[l, 1], ln_b[l, 1], seq, alpha, sc_tokens=SC_TOKEN_SHARE * n // 16)
    return x2.reshape(bsz, seq, d)
```

```python
import dataclasses
import functools

import jax
import jax.numpy as jnp
import numpy as np
from jax import lax
from jax.experimental import pallas as pl
from jax.experimental.pallas import tpu as pltpu
from jax.experimental.pallas import tpu_sc as plsc

F32 = jnp.float32
BF16 = jnp.bfloat16
I32 = jnp.int32
HI = lax.Precision.HIGHEST

D_MODEL = 1024
N_HEADS = 8
HEAD_DIM = 64
N_KV = 2
HPG = N_HEADS // N_KV
ROT_DIM = HEAD_DIM // 4
ROT_HALF = ROT_DIM // 2
ROPE_THETA = 500000.0
CMP_LEN = 32
CMP_STRIDE = 16
SLC_LEN = 64
SLC_TOPN = 16
WINDOW = 512
SCALE = HEAD_DIM ** -0.5
NEG = -1e30
FORCE_INIT = 1e6
FORCE_LOCAL = 2e6
POOL_GROUPS = 4
POOL_WINDOWS = (2, 4, 8, 16)
POOL_WIDTH = 512
POOL_GW = POOL_WIDTH // POOL_GROUPS
POOL_HALO = 16
ATT_WIDTH = N_HEADS * HEAD_DIM
KV_WIDTH = 3 * 2 * N_KV * HEAD_DIM
GATE_NSA = 3 * N_HEADS
MERGE_GATES = 2 * D_MODEL
PEER_HEADS = 8
N_KEYS = 128
PEER_TOPK = 16
PEER_DK = 128
PEER_SEL = PEER_HEADS * PEER_TOPK
LN_EPS = 1e-5

LANES = 128
SUBLANES = 8
VMEM_LIMIT = 56 * 1024 * 1024

NT_DIMS = (((1,), (1,)), ((), ()))


def _cparams(sem):
    return pltpu.CompilerParams(dimension_semantics=sem, vmem_limit_bytes=VMEM_LIMIT)


def _ada_kernel(c_ref, w_ref, b_ref, o_ref):
    c = c_ref[...]
    ca = c * jax.nn.sigmoid(c)
    o_ref[...] = jnp.dot(ca, w_ref[...], precision=HI, preferred_element_type=F32) + b_ref[...]


def _ada_call(c_pad, w_ada, b_ada):
    depth = w_ada.shape[0]
    nblk = w_ada.shape[2] // D_MODEL
    rows = c_pad.shape[0]
    return pl.pallas_call(
        _ada_kernel,
        grid=(depth, nblk),
        in_specs=[
            pl.BlockSpec((rows, D_MODEL), lambda l, j: (0, 0)),
            pl.BlockSpec((None, D_MODEL, D_MODEL), lambda l, j: (l, 0, j)),
            pl.BlockSpec((None, 1, D_MODEL), lambda l, j: (l, 0, j)),
        ],
        out_specs=pl.BlockSpec((None, rows, D_MODEL), lambda l, j: (l, 0, j)),
        out_shape=jax.ShapeDtypeStruct((depth, rows, nblk * D_MODEL), F32),
        compiler_params=_cparams(("arbitrary", "arbitrary")),
        name="ada_mod",
    )(c_pad, w_ada, b_ada.reshape(depth, 1, -1))


IN_COLS = ATT_WIDTH + KV_WIDTH + POOL_WIDTH + MERGE_GATES + LANES


def _rope_lanes(z, rc, rs1, rs2):
    return z * rc + pltpu.roll(z, ROT_HALF, 1) * rs1 + pltpu.roll(z, LANES - ROT_HALF, 1) * rs2


def _inproj_kernel(x_ref, sc_ref, sh_ref, w_ref, rc_ref, rs1_ref, rs2_ref,
                   q_ref, kv_ref, p_ref, mrg_ref, gn_ref):
    h = x_ref[...] * (1.0 + sc_ref[...]) + sh_ref[...]
    a = jnp.dot(h.astype(BF16), w_ref[...], preferred_element_type=F32)
    rc, rs1, rs2 = rc_ref[...], rs1_ref[...], rs2_ref[...]
    for j in range(ATT_WIDTH // LANES):
        q_ref[:, j * LANES:(j + 1) * LANES] = _rope_lanes(a[:, j * LANES:(j + 1) * LANES], rc, rs1, rs2)
    for br in range(3):
        c0 = ATT_WIDTH + br * 2 * LANES
        k = a[:, c0:c0 + LANES]
        if br > 0:
            k = _rope_lanes(k, rc, rs1, rs2)
        kv_ref[:, br * 2 * LANES:br * 2 * LANES + LANES] = k
        kv_ref[:, br * 2 * LANES + LANES:(br + 1) * 2 * LANES] = a[:, c0 + LANES:c0 + 2 * LANES]
    c1 = ATT_WIDTH + KV_WIDTH
    p_ref[...] = a[:, c1:c1 + POOL_WIDTH]
    mrg_ref[...] = a[:, c1 + POOL_WIDTH:c1 + POOL_WIDTH + MERGE_GATES]
    gn_ref[...] = a[:, c1 + POOL_WIDTH + MERGE_GATES:]


def _inproj_call(x2, sc, sh, w, rc, rs1, rs2, seq, tm):
    n = x2.shape[0]
    tpb = seq // tm
    tok = lambda i: (i, 0)
    bat = lambda i: (i // tpb, 0, 0)
    pos = lambda i: (i % tpb, 0)
    full = lambda i: (0, 0)
    return pl.pallas_call(
        _inproj_kernel,
        grid=(n // tm,),
        in_specs=[
            pl.BlockSpec((tm, D_MODEL), tok),
            pl.BlockSpec((None, 1, D_MODEL), bat),
            pl.BlockSpec((None, 1, D_MODEL), bat),
            pl.BlockSpec((D_MODEL, IN_COLS), full),
            pl.BlockSpec((tm, LANES), pos),
            pl.BlockSpec((tm, LANES), pos),
            pl.BlockSpec((tm, LANES), pos),
        ],
        out_specs=[
            pl.BlockSpec((tm, ATT_WIDTH), tok),
            pl.BlockSpec((tm, KV_WIDTH), tok),
            pl.BlockSpec((tm, POOL_WIDTH), tok),
            pl.BlockSpec((tm, MERGE_GATES), tok),
            pl.BlockSpec((tm, LANES), tok),
        ],
        out_shape=[
            jax.ShapeDtypeStruct((n, ATT_WIDTH), F32),
            jax.ShapeDtypeStruct((n, KV_WIDTH), F32),
            jax.ShapeDtypeStruct((n, POOL_WIDTH), F32),
            jax.ShapeDtypeStruct((n, MERGE_GATES), F32),
            jax.ShapeDtypeStruct((n, LANES), F32),
        ],
        compiler_params=_cparams(("arbitrary",)),
        name="in_proj",
    )(x2, sc, sh, w, rc, rs1, rs2)


def _compress_kernel(z_ref, pet_ref, peb_ref, w1t_ref, w1b_ref, w2_ref, rc_ref, rs1_ref, rs2_ref, o_ref):
    z = z_ref[...]
    rows = z.shape[0]
    top = jnp.dot(z + pet_ref[...], w1t_ref[...], precision=HI, preferred_element_type=F32)
    bot = jnp.dot(z + peb_ref[...], w1b_ref[...], precision=HI, preferred_element_type=F32)
    pre = top + pltpu.roll(bot, rows - 1, 0)
    y = jnp.dot(jax.nn.gelu(pre), w2_ref[...], precision=HI, preferred_element_type=F32)
    o_ref[...] = _rope_lanes(y, rc_ref[...], rs1_ref[...], rs2_ref[...])


def _compress_call(z, pet, peb, w1t, w1b, w2, rc, rs1, rs2):
    b, _, rows, width = z.shape
    kvsel = lambda i, j: (j, 0, 0)
    return pl.pallas_call(
        _compress_kernel,
        grid=(b, 2),
        in_specs=[
            pl.BlockSpec((None, None, rows, width), lambda i, j: (i, j, 0, 0)),
            pl.BlockSpec((None, 1, width), kvsel),
            pl.BlockSpec((None, 1, width), kvsel),
            pl.BlockSpec((None, width, LANES), kvsel),
            pl.BlockSpec((None, width, LANES), kvsel),
            pl.BlockSpec((None, LANES, LANES), kvsel),
            pl.BlockSpec((None, rows, LANES), kvsel),
            pl.BlockSpec((None, rows, LANES), kvsel),
            pl.BlockSpec((None, rows, LANES), kvsel),
        ],
        out_specs=pl.BlockSpec((None, None, rows, LANES), lambda i, j: (i, j, 0, 0)),
        out_shape=jax.ShapeDtypeStruct((b, 2, rows, LANES), F32),
        compiler_params=_cparams(("arbitrary", "arbitrary")),
        name="compress",
    )(z, pet, peb, w1t, w1b, w2, rc, rs1, rs2)


def _cmpsel_kernel(q_ref, kc_ref, vc_ref, c2s_ref, o_ref, sel_ref, *, tq):
    t0 = pl.program_id(1) * tq
    kc = kc_ref[...]
    vc = vc_ref[...]
    rows = kc.shape[0]
    trow = t0 + lax.broadcasted_iota(I32, (tq, rows), 0)
    cend = lax.broadcasted_iota(I32, (tq, rows), 1) * CMP_STRIDE + (CMP_LEN - 1)
    vis = cend <= trow
    anyv = (trow[:, :1] >= CMP_LEN - 1).astype(F32)
    imp = jnp.zeros((tq, LANES), F32)
    for g in range(N_KV):
        kg = kc[:, g * HEAD_DIM:(g + 1) * HEAD_DIM].astype(BF16)
        vg = vc[:, g * HEAD_DIM:(g + 1) * HEAD_DIM].astype(BF16)
        psum = jnp.zeros((tq, rows), F32)
        for h in range(HPG):
            hd = g * HPG + h
            qh = q_ref[:, hd * HEAD_DIM:(hd + 1) * HEAD_DIM].astype(BF16)
            s = lax.dot_general(qh, kg, NT_DIMS, preferred_element_type=F32) * SCALE
            s = jnp.where(vis, s, NEG)
            e = jnp.exp(s - jnp.max(s, axis=-1, keepdims=True))
            p = e / jnp.sum(e, axis=-1, keepdims=True) * anyv
            o_ref[:, hd * HEAD_DIM:(hd + 1) * HEAD_DIM] = jnp.dot(
                p.astype(BF16), vg, preferred_element_type=F32)
            psum = psum + p
        imp = imp + jnp.dot(psum.astype(BF16), c2s_ref[g], preferred_element_type=F32)
    lane = lax.broadcasted_iota(I32, (tq, LANES), 1)
    blk = lane & (SLC_LEN - 1)
    cur = lax.shift_right_logical(t0 + lax.broadcasted_iota(I32, (tq, LANES), 0), 6)
    score = jnp.where(blk <= cur, imp, NEG)
    score = jnp.where(blk == 0, FORCE_INIT, score)
    score = jnp.where(blk == cur, FORCE_LOCAL, score)
    sc_t = score.T
    nblk = LANES // N_KV
    jrow = lax.broadcasted_iota(I32, (nblk, tq), 0)
    sel_parts = []
    for g in range(N_KV):
        sc = sc_t[g * nblk:(g + 1) * nblk]
        cnt = jnp.zeros((nblk, tq), I32)
        for k in range(nblk):
            rk = sc[k:k + 1, :]
            ge = (rk >= sc).astype(I32)
            gt = (rk > sc).astype(I32)
            cnt = cnt + jnp.where(jrow > k, ge, gt)
        sel_parts.append((cnt < SLC_TOPN).astype(F32))
    sel_ref[...] = jnp.concatenate(sel_parts, axis=0).T


def _cmpsel_call(q, kvc, c2s, seq, tq):
    n = q.shape[0]
    b = n // seq
    nq = seq // tq
    rows = kvc.shape[2]
    tok = lambda i, j: (i * nq + j, 0)
    return pl.pallas_call(
        functools.partial(_cmpsel_kernel, tq=tq),
        grid=(b, nq),
        in_specs=[
            pl.BlockSpec((tq, ATT_WIDTH), tok),
            pl.BlockSpec((None, None, rows, LANES), lambda i, j: (i, 0, 0, 0)),
            pl.BlockSpec((None, None, rows, LANES), lambda i, j: (i, 1, 0, 0)),
            pl.BlockSpec((N_KV, rows, LANES), lambda i, j: (0, 0, 0)),
        ],
        out_specs=[pl.BlockSpec((tq, ATT_WIDTH), tok), pl.BlockSpec((tq, LANES), tok)],
        out_shape=[jax.ShapeDtypeStruct((n, ATT_WIDTH), F32), jax.ShapeDtypeStruct((n, LANES), F32)],
        compiler_params=_cparams(("arbitrary", "arbitrary")),
        name="cmp_select",
    )(q, kvc, kvc, c2s)


def _slc_kernel(q_ref, k_ref, v_ref, sel_ref, o_ref, qs_sc, m_sc, l_sc, acc_sc, *, tq, tk):
    qi = pl.program_id(1)
    kt = pl.program_id(2)
    nk = pl.num_programs(2)

    @pl.when(kt == 0)
    def _init():
        for hd in range(N_HEADS):
            g, h = divmod(hd, HPG)
            qs_sc[g, h * tq:(h + 1) * tq, :] = (q_ref[:, hd * HEAD_DIM:(hd + 1) * HEAD_DIM] * SCALE).astype(BF16)
        m_sc[...] = jnp.full(m_sc.shape, NEG, F32)
        l_sc[...] = jnp.zeros(l_sc.shape, F32)
        acc_sc[...] = jnp.zeros(acc_sc.shape, F32)

    @pl.when(kt * tk <= qi * tq + (tq - 1))
    def _step():
        t = qi * tq + lax.broadcasted_iota(I32, (tq, tk), 0)
        kp = kt * tk + lax.broadcasted_iota(I32, (tq, tk), 1)
        causal = kp <= t
        nblk = LANES // N_KV
        jb = lax.broadcasted_iota(I32, (nblk, tk), 0)
        kb = lax.shift_right_logical(kt * tk + lax.broadcasted_iota(I32, (nblk, tk), 1), 6)
        expand = (jb == kb).astype(BF16)
        for g in range(N_KV):
            selg = sel_ref[:, g * nblk:(g + 1) * nblk].astype(BF16)
            member = jnp.dot(selg, expand, preferred_element_type=F32)
            bias = jnp.where(jnp.logical_and(causal, member > 0.5), 0.0, NEG)
            bias = jnp.concatenate([bias] * HPG, axis=0)
            kg = k_ref[:, g * HEAD_DIM:(g + 1) * HEAD_DIM].astype(BF16)
            vg = v_ref[:, g * HEAD_DIM:(g + 1) * HEAD_DIM].astype(BF16)
            s = lax.dot_general(qs_sc[g], kg, NT_DIMS, preferred_element_type=F32) + bias
            chunks = [s[:, c * LANES:(c + 1) * LANES] for c in range(tk // LANES)]
            mc = chunks[0]
            for x in chunks[1:]:
                mc = jnp.maximum(mc, x)
            m_old = m_sc[g]
            m_new = jnp.maximum(m_old, jnp.max(mc, axis=-1, keepdims=True))
            alpha = jnp.exp(m_old - m_new)
            ps = [jnp.exp(x - m_new) for x in chunks]
            lsum = ps[0]
            for x in ps[1:]:
                lsum = lsum + x
            l_sc[g] = alpha * l_sc[g] + lsum
            p = jnp.concatenate(ps, axis=1).astype(BF16)
            acc_sc[g] = alpha[:, :HEAD_DIM] * acc_sc[g] + jnp.dot(p, vg, preferred_element_type=F32)
            m_sc[g] = m_new

    @pl.when(kt == nk - 1)
    def _fin():
        for hd in range(N_HEADS):
            g, h = divmod(hd, HPG)
            l = jnp.sum(l_sc[g, h * tq:(h + 1) * tq, :], axis=-1, keepdims=True)
            o_ref[:, hd * HEAD_DIM:(hd + 1) * HEAD_DIM] = acc_sc[g, h * tq:(h + 1) * tq, :] / l


def _slc_call(q, kv, sel, seq, tq, tk):
    n = q.shape[0]
    b = n // seq
    nq = seq // tq
    nk = seq // tk
    tok = lambda i, j, k: (i * nq + j, 0)

    def key_map(col):
        def f(i, j, k):
            last = (j * tq + tq - 1) // tk
            return (i * nk + jnp.minimum(k, last), col)
        return f

    return pl.pallas_call(
        functools.partial(_slc_kernel, tq=tq, tk=tk),
        grid=(b, nq, nk),
        in_specs=[
            pl.BlockSpec((tq, ATT_WIDTH), tok),
            pl.BlockSpec((tk, LANES), key_map(2)),
            pl.BlockSpec((tk, LANES), key_map(3)),
            pl.BlockSpec((tq, LANES), tok),
        ],
        out_specs=pl.BlockSpec((tq, ATT_WIDTH), tok),
        out_shape=jax.ShapeDtypeStruct((n, ATT_WIDTH), F32),
        scratch_shapes=[
            pltpu.VMEM((N_KV, HPG * tq, HEAD_DIM), BF16),
            pltpu.VMEM((N_KV, HPG * tq, LANES), F32),
            pltpu.VMEM((N_KV, HPG * tq, LANES), F32),
            pltpu.VMEM((N_KV, HPG * tq, HEAD_DIM), F32),
        ],
        compiler_params=_cparams(("arbitrary", "arbitrary", "arbitrary")),
        name="slc_attn",
    )(q, kv, kv, sel)


def _win_kernel(q_ref, *refs, tq, nkb):
    k_refs = refs[:nkb]
    v_refs = refs[nkb:2 * nkb]
    o_ref = refs[2 * nkb]
    qi = pl.program_id(1)
    t = qi * tq + lax.broadcasted_iota(I32, (tq, tq), 0)
    col = lax.broadcasted_iota(I32, (tq, tq), 1)
    biases = []
    for j in range(nkb):
        kp = (qi - (nkb - 1) + j) * tq + col
        diff = t - kp
        ok = jnp.logical_and(jnp.logical_and(diff >= 0, diff < WINDOW), kp >= 0)
        biases.append(jnp.concatenate([jnp.where(ok, 0.0, NEG)] * HPG, axis=0))
    for g in range(N_KV):
        qs = jnp.concatenate(
            [(q_ref[:, (g * HPG + h) * HEAD_DIM:(g * HPG + h + 1) * HEAD_DIM] * SCALE).astype(BF16)
             for h in range(HPG)], axis=0)
        chunks = []
        for j in range(nkb):
            kg = k_refs[j][:, g * HEAD_DIM:(g + 1) * HEAD_DIM].astype(BF16)
            s = lax.dot_general(qs, kg, NT_DIMS, preferred_element_type=F32) + biases[j]
            chunks += [s[:, c * LANES:(c + 1) * LANES] for c in range(tq // LANES)]
        mc = chunks[0]
        for x in chunks[1:]:
            mc = jnp.maximum(mc, x)
        m = jnp.broadcast_to(jnp.max(mc, axis=-1, keepdims=True), mc.shape)
        ps = [jnp.exp(x - m) for x in chunks]
        lsum = ps[0]
        for x in ps[1:]:
            lsum = lsum + x
        l = jnp.sum(lsum, axis=-1, keepdims=True)
        per = tq // LANES
        o = None
        for j in range(nkb):
            vg = v_refs[j][:, g * HEAD_DIM:(g + 1) * HEAD_DIM].astype(BF16)
            pj = jnp.concatenate(ps[j * per:(j + 1) * per], axis=1).astype(BF16)
            oj = jnp.dot(pj, vg, preferred_element_type=F32)
            o = oj if o is None else o + oj
        o = o / l
        for h in range(HPG):
            hd = g * HPG + h
            o_ref[:, hd * HEAD_DIM:(hd + 1) * HEAD_DIM] = o[h * tq:(h + 1) * tq]


def _win_call(q, kv, seq, tq):
    n = q.shape[0]
    b = n // seq
    nq = seq // tq
    nkb = WINDOW // tq + 1
    tok = lambda i, j: (i * nq + j, 0)

    def key_map(col, back):
        return lambda i, j: (i * nq + jnp.maximum(j - back, 0), col)

    k_specs = [pl.BlockSpec((tq, LANES), key_map(4, nkb - 1 - jj)) for jj in range(nkb)]
    v_specs = [pl.BlockSpec((tq, LANES), key_map(5, nkb - 1 - jj)) for jj in range(nkb)]
    return pl.pallas_call(
        functools.partial(_win_kernel, tq=tq, nkb=nkb),
        grid=(b, nq),
        in_specs=[pl.BlockSpec((tq, ATT_WIDTH), tok)] + k_specs + v_specs,
        out_specs=pl.BlockSpec((tq, ATT_WIDTH), tok),
        out_shape=jax.ShapeDtypeStruct((n, ATT_WIDTH), F32),
        compiler_params=_cparams(("arbitrary", "arbitrary")),
        name="win_attn",
    )(q, *([kv] * (2 * nkb)))


def _pool_kernel(p_ref, prev_ref, w_ref, sc_ref, o_ref, *, ts):
    i = pl.program_id(1)
    x = p_ref[...]
    prev = prev_ref[...] * (i > 0).astype(F32)
    xe = jnp.concatenate([prev, x], axis=0)
    t1 = (i * ts + 1 + lax.broadcasted_iota(I32, (ts, POOL_GW), 0)).astype(F32)
    for g, w in enumerate(POOL_WINDOWS):
        a = xe[:, g * POOL_GW:(g + 1) * POOL_GW]
        off = POOL_HALO
        span = 1
        while span < w:
            a = a[span:] + a[:-span]
            off -= span
            span *= 2
        sums = a[off:off + ts]
        cnt = jnp.minimum(t1, float(w))
        pooled = sums / cnt - x[:, g * POOL_GW:(g + 1) * POOL_GW]
        y = jnp.dot(pooled.astype(BF16), w_ref[g], preferred_element_type=F32)
        o_ref[:, g * POOL_GW:(g + 1) * POOL_GW] = y * sc_ref[:, g * POOL_GW:(g + 1) * POOL_GW]


def _pool_call(p_in, w_pool, pool_scale, seq, ts):
    n = p_in.shape[0]
    b = n // seq
    nt = seq // ts
    hpt = ts // POOL_HALO
    tok = lambda i, j: (i * nt + j, 0)
    return pl.pallas_call(
        functools.partial(_pool_kernel, ts=ts),
        grid=(b, nt),
        in_specs=[
            pl.BlockSpec((ts, POOL_WIDTH), tok),
            pl.BlockSpec((POOL_HALO, POOL_WIDTH), lambda i, j: (i * nt * hpt + jnp.maximum(j * hpt - 1, 0), 0)),
            pl.BlockSpec((POOL_GROUPS, POOL_GW, POOL_GW), lambda i, j: (0, 0, 0)),
            pl.BlockSpec((1, POOL_WIDTH), lambda i, j: (0, 0)),
        ],
        out_specs=pl.BlockSpec((ts, POOL_WIDTH), tok),
        out_shape=jax.ShapeDtypeStruct((n, POOL_WIDTH), F32),
        compiler_params=_cparams(("arbitrary", "arbitrary")),
        name="pool_mix",
    )(p_in, p_in, w_pool, pool_scale)


def _layer_norm(z, g, b):
    mu = jnp.mean(z, axis=-1, keepdims=True)
    zc = z - mu
    var = jnp.mean(zc * zc, axis=-1, keepdims=True)
    return zc * lax.rsqrt(var + LN_EPS) * g + b


def _merge_kernel(oc_ref, os_ref, ow_ref, gn_ref, op_ref, gm_ref, x_ref, g1_ref, lng_ref, lnb_ref,
                  wl_ref, wo_ref, eb_ref, o_ref, *, alpha):
    gate = jax.nn.sigmoid(gn_ref[...])
    branches = (oc_ref, os_ref, ow_ref)
    oatt = None
    for br in range(3):
        gx = jnp.dot(gate, eb_ref[br], precision=HI, preferred_element_type=F32)
        term = gx * branches[br][...]
        oatt = term if oatt is None else oatt + term
    la = jnp.dot(oatt.astype(BF16), wl_ref[0], preferred_element_type=F32)
    lb = jnp.dot(op_ref[...].astype(BF16), wl_ref[1], preferred_element_type=F32)
    gm = jax.nn.sigmoid(gm_ref[...])
    merged = gm[:, :D_MODEL] * la + gm[:, D_MODEL:] * lb
    y = jnp.dot(merged.astype(BF16), wo_ref[...], preferred_element_type=F32)
    z = alpha * x_ref[...] + g1_ref[...] * y
    o_ref[...] = _layer_norm(z, lng_ref[...], lnb_ref[...])


def _merge_call(oc, osl, ow, gn, op, gm, x2, g1, lng, lnb, wl, wo, eb, seq, tm, alpha):
    n = x2.shape[0]
    tpb = seq // tm
    tok = lambda i: (i, 0)
    bat = lambda i: (i // tpb, 0, 0)
    return pl.pallas_call(
        functools.partial(_merge_kernel, alpha=alpha),
        grid=(n // tm,),
        in_specs=[
            pl.BlockSpec((tm, ATT_WIDTH), tok), pl.BlockSpec((tm, ATT_WIDTH), tok),
            pl.BlockSpec((tm, ATT_WIDTH), tok), pl.BlockSpec((tm, LANES), tok),
            pl.BlockSpec((tm, POOL_WIDTH), tok), pl.BlockSpec((tm, MERGE_GATES), tok),
            pl.BlockSpec((tm, D_MODEL), tok),
            pl.BlockSpec((None, 1, D_MODEL), bat),
            pl.BlockSpec((1, D_MODEL), lambda i: (0, 0)), pl.BlockSpec((1, D_MODEL), lambda i: (0, 0)),
            pl.BlockSpec((2, ATT_WIDTH, D_MODEL), lambda i: (0, 0, 0)),
            pl.BlockSpec((D_MODEL, D_MODEL), lambda i: (0, 0)),
            pl.BlockSpec((3, LANES, ATT_WIDTH), lambda i: (0, 0, 0)),
        ],
        out_specs=pl.BlockSpec((tm, D_MODEL), tok),
        out_shape=jax.ShapeDtypeStruct((n, D_MODEL), F32),
        compiler_params=_cparams(("arbitrary",)),
        name="merge_out",
    )(oc, osl, ow, gn, op, gm, x2, g1, lng, lnb, wl, wo, eb)


def _extract_top(cur, ids, n):
    rows = cur.shape[0]
    rio = lax.broadcasted_iota(I32, cur.shape, 0)
    vals, outs = [], []
    for _ in range(n):
        m = jnp.max(cur, axis=0, keepdims=True)
        pos = jnp.min(jnp.where(cur == m, rio, rows), axis=0, keepdims=True)
        hit = rio == pos
        vals.append(m)
        outs.append(pos if ids is None else jnp.max(jnp.where(hit, ids, -1), axis=0, keepdims=True))
        cur = jnp.where(hit, -jnp.inf, cur)
    return jnp.concatenate(vals, axis=0), jnp.concatenate(outs, axis=0)


def _route_kernel(x_ref, sc_ref, sh_ref, wq_ref, keys_ref, h_ref, hv_ref, e_ref, g_ref,
                  st_sc, ts_sc, ti_sc, eo_sc, go_sc):
    h = x_ref[...] * (1.0 + sc_ref[...]) + sh_ref[...]
    h_ref[...] = h
    for r in range(SUBLANES):
        hv_ref[:, r, :] = h[:, r * LANES:(r + 1) * LANES]
    qp = jnp.dot(h.astype(BF16), wq_ref[...], preferred_element_type=F32).astype(BF16)
    half = PEER_DK // 2
    for hp in range(2 * PEER_HEADS):
        st_sc[hp] = lax.dot_general(keys_ref[hp], qp[:, hp * half:(hp + 1) * half], NT_DIMS,
                                    preferred_element_type=F32)

    def half_body(hp, carry):
        vals, ids = _extract_top(st_sc[hp], None, PEER_TOPK)
        ts_sc[hp] = vals
        ti_sc[hp] = ids
        return carry

    lax.fori_loop(0, 2 * PEER_HEADS, half_body, 0)

    def head_body(hh, carry):
        s1, s2 = ts_sc[2 * hh], ts_sc[2 * hh + 1]
        i1, i2 = ti_sc[2 * hh], ti_sc[2 * hh + 1]
        brow = lax.broadcasted_iota(I32, (SUBLANES, s1.shape[1]), 0)
        cands = [s1[0:1, :] + s2]
        cidxs = [i1[0:1, :] * N_KEYS + i2]
        for a in range(1, SUBLANES):
            ok = brow < PEER_TOPK // (a + 1)
            cands.append(jnp.where(ok, s1[a:a + 1, :] + s2[:SUBLANES], -jnp.inf))
            cidxs.append(i1[a:a + 1, :] * N_KEYS + i2[:SUBLANES])
        cands.append(s1[SUBLANES:] + s2[0:1, :])
        cidxs.append(i1[SUBLANES:] * N_KEYS + i2[0:1, :])
        sv, ei = _extract_top(jnp.concatenate(cands, axis=0), jnp.concatenate(cidxs, axis=0), PEER_TOPK)
        ex = jnp.exp(sv - sv[0:1, :])
        go_sc[hh] = ex / jnp.sum(ex, axis=0, keepdims=True)
        eo_sc[hh] = ei.astype(F32)
        return carry

    lax.fori_loop(0, PEER_HEADS, head_body, 0)
    e_all = jnp.concatenate([eo_sc[hh] for hh in range(PEER_HEADS)], axis=0)
    g_all = jnp.concatenate([go_sc[hh] for hh in range(PEER_HEADS)], axis=0)
    e_ref[...] = e_all.T.astype(I32) * (D_MODEL // 2 // LANES)
    g_ref[...] = g_all.T


def _route_call(x2, sc, sh, wq, keys, seq, tt):
    n = x2.shape[0]
    tpb = seq // tt
    tok = lambda i: (i, 0)
    bat = lambda i: (i // tpb, 0, 0)
    nhp = 2 * PEER_HEADS
    return pl.pallas_call(
        _route_kernel,
        grid=(n // tt,),
        in_specs=[
            pl.BlockSpec((tt, D_MODEL), tok),
            pl.BlockSpec((None, 1, D_MODEL), bat),
            pl.BlockSpec((None, 1, D_MODEL), bat),
            pl.BlockSpec((D_MODEL, PEER_HEADS * PEER_DK), lambda i: (0, 0)),
            pl.BlockSpec((nhp, N_KEYS, PEER_DK // 2), lambda i: (0, 0, 0)),
        ],
        out_specs=[pl.BlockSpec((tt, D_MODEL), tok), pl.BlockSpec((tt, SUBLANES, LANES), lambda i: (i, 0, 0)),
                   pl.BlockSpec((tt, PEER_SEL), tok), pl.BlockSpec((tt, PEER_SEL), tok)],
        out_shape=[jax.ShapeDtypeStruct((n, D_MODEL), F32), jax.ShapeDtypeStruct((n, SUBLANES, LANES), F32),
                   jax.ShapeDtypeStruct((n, PEER_SEL), I32), jax.ShapeDtypeStruct((n, PEER_SEL), F32)],
        scratch_shapes=[
            pltpu.VMEM((nhp, N_KEYS, tt), F32),
            pltpu.VMEM((nhp, PEER_TOPK, tt), F32),
            pltpu.VMEM((nhp, PEER_TOPK, tt), I32),
            pltpu.VMEM((PEER_HEADS, PEER_TOPK, tt), F32),
            pltpu.VMEM((PEER_HEADS, PEER_TOPK, tt), F32),
        ],
        compiler_params=_cparams(("arbitrary",)),
        name="peer_route",
    )(x2, sc, sh, wq, keys)


HALF_ROWS = SUBLANES // 2
HI_MASK = -65536
PAIR_TILES = PEER_SEL // 2
PAIR_ROWS = PAIR_TILES * SUBLANES


def _load_two_experts(tab_ref, ra, rb):
    wa = tab_ref[pl.ds(pl.multiple_of(ra, HALF_ROWS), HALF_ROWS), :]
    wb = tab_ref[pl.ds(pl.multiple_of(rb, HALF_ROWS), HALF_ROWS), :]
    w2 = jnp.concatenate([wa, wb], axis=0)
    return lax.bitcast_convert_type(w2 << 16, F32), lax.bitcast_convert_type(w2 & HI_MASK, F32)


def _fold_pairs(vs):
    row = lax.broadcasted_iota(I32, (SUBLANES, LANES), 0)
    shift = HALF_ROWS // 2
    while len(vs) > 1:
        low = (row & shift) == 0
        vs = [jnp.where(low, a + pltpu.roll(a, SUBLANES - shift, 0), b + pltpu.roll(b, shift, 0))
              for a, b in zip(vs[0::2], vs[1::2])]
        shift //= 2
    return vs[0]


def _fold_order():
    idx = [[2 * i if r < HALF_ROWS else 2 * i + 1 for r in range(SUBLANES)] for i in range(HALF_ROWS)]
    shift = HALF_ROWS // 2
    while len(idx) > 1:
        idx = [[a[r] if (r & shift) == 0 else b[r] for r in range(SUBLANES)]
               for a, b in zip(idx[0::2], idx[1::2])]
        shift //= 2
    return idx[0]


def _peer_u_kernel(e_sm, tab_ref, hv_ref, gate_ref, coef_ref, *, tt):
    row = lax.broadcasted_iota(I32, (SUBLANES, LANES), 0)
    low = row < HALF_ROWS
    eye = (lax.broadcasted_iota(I32, (PEER_SEL, LANES), 0) ==
           lax.broadcasted_iota(I32, (PEER_SEL, LANES), 1))
    order = _fold_order()

    def finish(t, part):
        col = jnp.sum(part, axis=-1, keepdims=True)
        a_row = jnp.sum(jnp.where(eye, col, 0.0), axis=0, keepdims=True)
        coef_ref[t] = gate_ref[pl.ds(t, 1), :] * jax.nn.gelu(a_row)

    def token(t, part_prev):
        finish(jnp.maximum(t - 1, 0), part_prev)
        hv = hv_ref[t]
        hsw = pltpu.roll(hv, HALF_ROWS, 0)
        h_lo = jnp.where(low, hv, hsw)
        h_hi = jnp.where(low, hsw, hv)
        folded = []
        for j in range(PEER_SEL // SUBLANES):
            prods = []
            for i in range(HALF_ROWS):
                ka = j * SUBLANES + order.index(2 * i)
                kb = j * SUBLANES + order.index(2 * i + 1)
                lo, hi = _load_two_experts(tab_ref, e_sm[t, ka], e_sm[t, kb])
                prods.append(lo * h_lo + hi * h_hi)
            folded.append(_fold_pairs(prods))
        return jnp.concatenate(folded, axis=0)

    last = lax.fori_loop(0, tt, token, jnp.zeros((PEER_SEL, LANES), F32))
    finish(tt - 1, last)


def _peer_v_kernel(e_sm, coef_ref, tab_ref, y_ref, cv_sc, *, tt, nacc):
    row = lax.broadcasted_iota(I32, (SUBLANES, LANES), 0)
    low = row < HALF_ROWS
    rr = lax.broadcasted_iota(I32, (PAIR_ROWS, LANES), 0)
    kk = lax.broadcasted_iota(I32, (PAIR_ROWS, LANES), 1)
    onehot = (kk == 2 * (rr >> 3) + ((rr >> 2) & 1)).astype(F32)
    ones = jnp.ones((LANES, LANES), BF16)

    def expand(t, slot):
        lhs = (onehot * coef_ref[t]).astype(BF16)
        cv_sc[slot] = jnp.dot(lhs, ones, preferred_element_type=F32)

    def process(t, slot):
        acc_lo = [jnp.zeros((SUBLANES, LANES), F32) for _ in range(nacc)]
        acc_hi = [jnp.zeros((SUBLANES, LANES), F32) for _ in range(nacc)]
        for j in range(PAIR_TILES):
            lo, hi = _load_two_experts(tab_ref, e_sm[t, 2 * j], e_sm[t, 2 * j + 1])
            cv = cv_sc[slot, j * SUBLANES:(j + 1) * SUBLANES, :]
            acc_lo[j % nacc] = acc_lo[j % nacc] + cv * lo
            acc_hi[j % nacc] = acc_hi[j % nacc] + cv * hi
        a_lo, a_hi = acc_lo[0], acc_hi[0]
        for i in range(1, nacc):
            a_lo = a_lo + acc_lo[i]
            a_hi = a_hi + acc_hi[i]
        a_lo = a_lo + pltpu.roll(a_lo, HALF_ROWS, 0)
        a_hi = a_hi + pltpu.roll(a_hi, HALF_ROWS, 0)
        y_ref[t] = jnp.where(low, a_lo, a_hi)

    expand(0, 0)

    def two_tokens(i, carry):
        t = 2 * i
        expand(t + 1, 1)
        process(t, 0)
        expand(jnp.minimum(t + 2, tt - 1), 0)
        process(t + 1, 1)
        return carry

    lax.fori_loop(0, tt // 2, two_tokens, 0)


def _resident_table_spec(tab):
    return pl.BlockSpec(tab.shape, lambda i: (0, 0), pipeline_mode=pl.Buffered(1))


def _peer_u_call(erow, tab, hv, gate, tt, n):
    return pl.pallas_call(
        functools.partial(_peer_u_kernel, tt=tt),
        grid=(n // tt,),
        in_specs=[
            pl.BlockSpec((tt, PEER_SEL), lambda i: (i, 0), memory_space=pltpu.SMEM),
            _resident_table_spec(tab),
            pl.BlockSpec((tt, SUBLANES, LANES), lambda i: (i, 0, 0)),
            pl.BlockSpec((tt, PEER_SEL), lambda i: (i, 0)),
        ],
        out_specs=pl.BlockSpec((tt, 1, PEER_SEL), lambda i: (i, 0, 0)),
        out_shape=jax.ShapeDtypeStruct((n, 1, PEER_SEL), F32),
        compiler_params=_cparams(("arbitrary",)),
        name="peer_u",
    )(erow, tab, hv, gate)


def _peer_v_call(erow, coef3, tab, tt, n):
    assert tt % 2 == 0
    return pl.pallas_call(
        functools.partial(_peer_v_kernel, tt=tt, nacc=4),
        grid=(n // tt,),
        in_specs=[
            pl.BlockSpec((tt, PEER_SEL), lambda i: (i, 0), memory_space=pltpu.SMEM),
            pl.BlockSpec((tt, 1, PEER_SEL), lambda i: (i, 0, 0)),
            _resident_table_spec(tab),
        ],
        out_specs=pl.BlockSpec((tt, SUBLANES, LANES), lambda i: (i, 0, 0)),
        out_shape=jax.ShapeDtypeStruct((n, SUBLANES, LANES), F32),
        scratch_shapes=[pltpu.VMEM((2, PAIR_ROWS, LANES), F32)],
        compiler_params=_cparams(("arbitrary",)),
        name="peer_v",
    )(erow, coef3, tab)


def _resln_kernel(x_ref, y_ref, g_ref, lng_ref, lnb_ref, o_ref, *, alpha):
    z = alpha * x_ref[...] + g_ref[...] * y_ref[...]
    o_ref[...] = _layer_norm(z, lng_ref[...], lnb_ref[...])


def _resln_call(x2, y2, g2, lng, lnb, seq, tm, alpha):
    n = x2.shape[0]
    tpb = seq // tm
    tok = lambda i: (i, 0)
    return pl.pallas_call(
        functools.partial(_resln_kernel, alpha=alpha),
        grid=(n // tm,),
        in_specs=[
            pl.BlockSpec((tm, D_MODEL), tok), pl.BlockSpec((tm, D_MODEL), tok),
            pl.BlockSpec((None, 1, D_MODEL), lambda i: (i // tpb, 0, 0)),
            pl.BlockSpec((1, D_MODEL), lambda i: (0, 0)), pl.BlockSpec((1, D_MODEL), lambda i: (0, 0)),
        ],
        out_specs=pl.BlockSpec((tm, D_MODEL), tok),
        out_shape=jax.ShapeDtypeStruct((n, D_MODEL), F32),
        compiler_params=_cparams(("arbitrary",)),
        name="res_ln",
    )(x2, y2, g2, lng, lnb)


def _rope_lane_tables(pos):
    inv = ROPE_THETA ** (-jnp.arange(0, ROT_DIM, 2, dtype=F32) / ROT_DIM)
    ang = pos.astype(F32)[:, None] * inv[None, :]
    cos, sin = jnp.cos(ang), jnp.sin(ang)
    lane = np.arange(LANES) % HEAD_DIM
    fidx = lane % ROT_HALF
    first = jnp.asarray(lane < ROT_HALF)
    second = jnp.asarray((lane >= ROT_HALF) & (lane < ROT_DIM))
    rot = jnp.asarray(lane < ROT_DIM)
    cl, sl = cos[:, fidx], sin[:, fidx]
    rc = jnp.where(rot, cl, 1.0)
    rs1 = jnp.where(second, sl, 0.0)
    rs2 = jnp.where(first, -sl, 0.0)
    return rc, rs1, rs2


def _pack_kernel(t_ref, o_ref):
    x = t_ref[...]
    half = x.shape[1] // 2
    lo = lax.bitcast_convert_type(x[:, :half].astype(BF16).astype(F32), I32)
    hi = lax.bitcast_convert_type(x[:, half:].astype(BF16).astype(F32), I32)
    o_ref[...] = lax.shift_right_logical(lo, jnp.full_like(lo, 16)) | hi


def _pack_table(tab, te=512):
    e, d = tab.shape
    words = pl.pallas_call(
        _pack_kernel, grid=(e // te,),
        in_specs=[pl.BlockSpec((te, d), lambda i: (i, 0))],
        out_specs=pl.BlockSpec((te, d // 2), lambda i: (i, 0)),
        out_shape=jax.ShapeDtypeStruct((e, d // 2), I32),
        compiler_params=_cparams(("arbitrary",)), name="pack_table",
    )(tab)
    return words.reshape(e * (d // 2) // LANES, LANES)


def _cmp_to_slc_wide(rows, n_slc):
    st = np.arange(rows) * CMP_STRIDE
    js = np.arange(n_slc) * SLC_LEN
    ov = np.minimum(st[:, None] + CMP_LEN, js[None, :] + SLC_LEN) - np.maximum(st[:, None], js[None, :])
    c2s = np.maximum(ov, 0).astype(np.float32) / CMP_STRIDE
    wide = np.zeros((N_KV, rows, LANES), np.float32)
    nblk = LANES // N_KV
    for g in range(N_KV):
        wide[g, :, g * nblk:g * nblk + n_slc] = c2s
    return jnp.asarray(wide)


def _gate_expanders():
    eb = np.zeros((3, LANES, ATT_WIDTH), np.float32)
    for hd in range(N_HEADS):
        for br in range(3):
            eb[br, hd * 3 + br, hd * HEAD_DIM:(hd + 1) * HEAD_DIM] = 1.0
    return jnp.asarray(eb)


class _Consts:
    def __init__(self, seq):
        self.rows = seq // CMP_STRIDE
        self.rope = _rope_lane_tables(jnp.arange(seq))
        cpos = jnp.arange(self.rows) * CMP_STRIDE + CMP_LEN - 1
        crope = _rope_lane_tables(cpos)
        ident = (jnp.ones_like(crope[0]), jnp.zeros_like(crope[0]), jnp.zeros_like(crope[0]))
        self.crope = tuple(jnp.stack([a, b]) for a, b in zip(crope, ident))
        self.c2s = _cmp_to_slc_wide(self.rows, seq // SLC_LEN)
        self.eb = _gate_expanders()


def _token_mixer_layer(x2, sc1, sh1, g1, w_in, cmp_pe, cmp_w1, cmp_w2, w_pool, pool_scale, w_lift, w_o,
                       lng, lnb, cst, bsz, seq, alpha):
    d = D_MODEL
    rows = cst.rows
    s1 = ATT_WIDTH + KV_WIDTH
    s2 = s1 + GATE_NSA
    s3 = s2 + POOL_WIDTH
    w_gate = jnp.pad(w_in[:, s1:s2], ((0, 0), (0, LANES - GATE_NSA)))
    w_all = jnp.concatenate([w_in[:, :s1], w_in[:, s2:s3], w_in[:, s3:], w_gate], axis=1).astype(BF16)
    q, kv, p_in, g_mrg, g_nsa = _inproj_call(x2, sc1, sh1, w_all, *cst.rope, seq, 256)

    eye_g = jnp.eye(N_KV, dtype=F32)
    zc = kv[:, :2 * LANES].reshape(bsz, rows, CMP_STRIDE, 2, LANES)
    zc = jnp.transpose(zc, (0, 3, 1, 2, 4)).reshape(bsz, 2, rows, CMP_STRIDE * LANES)
    w1x = jnp.einsum('klde,gh->klgdhe', cmp_w1.reshape(2, CMP_LEN, HEAD_DIM, HEAD_DIM), eye_g)
    w1x = w1x.reshape(2, CMP_LEN * LANES, LANES)
    half = CMP_STRIDE * LANES
    pex = jnp.broadcast_to(cmp_pe[:, :, None, :], (2, CMP_LEN, N_KV, HEAD_DIM)).reshape(2, 1, CMP_LEN * LANES)
    w2x = jnp.einsum('kef,gh->kgehf', cmp_w2, eye_g).reshape(2, LANES, LANES)
    kvc = _compress_call(zc, pex[:, :, :half], pex[:, :, half:], w1x[:, :half], w1x[:, half:], w2x,
                         *cst.crope)

    o_cmp, sel = _cmpsel_call(q, kvc, cst.c2s.astype(BF16), seq, 256)
    o_slc = _slc_call(q, kv, sel, seq, 256, 512)
    o_win = _win_call(q, kv, seq, 256)
    o_pool = _pool_call(p_in, w_pool.astype(BF16), pool_scale.reshape(1, -1), seq, 512)
    return _merge_call(o_cmp, o_slc, o_win, g_nsa, o_pool, g_mrg, x2, g1,
                       lng.reshape(1, d), lnb.reshape(1, d),
                       w_lift.astype(BF16), w_o.astype(BF16), cst.eb, seq, 256, alpha)


SC_CORES = 2
SC_SUBCORES = 16
SC_LANES = 16
SC_WORKERS = SC_CORES * SC_SUBCORES
SC_ROWS = 32
SC_TOKEN_SHARE = 8
SC_BLOCK = 256


def _sc_params():
    cp = pltpu.CompilerParams()
    if "needs_layout_passes" in pltpu.CompilerParams.__dataclass_fields__:
        cp = dataclasses.replace(cp, needs_layout_passes=False)
    return cp


def _sc_token_stream(tabs, erow_hbm, side_hbm, tok0, tpw, base, idx_bufs, side_bufs, rows_v, sems,
                     begin_fn, chunk_fn, end_fn):
    per_tab = PEER_SEL // SC_ROWS
    nch = per_tab * len(tabs)
    assert nch % 2 == 0

    def idx_copy(t, s):
        return pltpu.make_async_copy(erow_hbm.at[tok0 + base + t], idx_bufs[s], sems.at[2 + s])

    def side_copy(t, s):
        return pltpu.make_async_copy(side_hbm.at[tok0 + base + t], side_bufs[s], sems.at[4 + s])

    def row_copy(s, c):
        rows = idx_bufs[s].at[pl.ds((c % per_tab) * SC_ROWS, SC_ROWS)]
        return pltpu.make_async_copy(tabs[c // per_tab].at[rows], rows_v.at[c % 2], sems.at[c % 2])

    def to_expert_ids(s):
        shift = jnp.full((SC_LANES,), (D_MODEL // 2 // LANES).bit_length() - 1, I32)
        for j in range(PEER_SEL // SC_LANES):
            sl = pl.ds(j * SC_LANES, SC_LANES)
            idx_bufs[s][sl] = lax.shift_right_logical(idx_bufs[s][sl], shift)

    idx_copy(0, 0).start()
    side_copy(0, 0).start()
    idx_copy(0, 0).wait()
    to_expert_ids(0)
    row_copy(0, 0).start()

    @pl.loop(0, tpw // 2)
    def _(i):
        for slot in (0, 1):
            other = 1 - slot
            t = 2 * i + slot
            tn = jnp.minimum(t + 1, tpw - 1)
            idx_copy(tn, other).start()
            side_copy(tn, other).start()
            side_copy(t, slot).wait()
            begin_fn(tok0 + base + t, slot)
            for c in range(nch):
                if c + 1 < nch:
                    row_copy(slot, c + 1).start()
                else:
                    idx_copy(tn, other).wait()
                    to_expert_ids(other)
                    row_copy(other, 0).start()
                row_copy(slot, c).wait()
                chunk_fn(c, rows_v.at[c % 2], slot)
            end_fn(base + t, slot)

    row_copy(0, 0).wait()
    side_copy(0, 0).wait()


GELU_C0 = 0.7978845608028654
GELU_C1 = 0.044715


def _sc_peer_call(erow, hrows, gate, utab, vtab, tok0, m):
    d = utab.shape[1]
    assert m % (2 * SC_WORKERS) == 0 and d % SC_BLOCK == 0
    tpw = m // SC_WORKERS
    per_tab = PEER_SEL // SC_ROWS
    nvec = SC_BLOCK // SC_LANES
    mesh = plsc.VectorSubcoreMesh(core_axis_name="c", subcore_axis_name="s")

    @functools.partial(
        pl.kernel, mesh=mesh, out_type=jax.ShapeDtypeStruct((m, d), F32),
        scratch_types=[pltpu.VMEM((PEER_SEL,), I32), pltpu.VMEM((PEER_SEL,), I32),
                       pltpu.VMEM((d,), F32), pltpu.VMEM((d,), F32),
                       pltpu.VMEM((2, SC_ROWS, d), F32), pltpu.VMEM((PEER_SEL * SC_LANES,), F32),
                       pltpu.VMEM((PEER_SEL,), F32), pltpu.VMEM((PEER_SEL,), F32), pltpu.VMEM((d,), F32),
                       pltpu.SemaphoreType.DMA((6,))],
        compiler_params=_sc_params(), name="sc_peer")
    def run(utab_hbm, vtab_hbm, erow_hbm, h_hbm, gate_hbm, out_hbm,
            idx_a, idx_b, h_a, h_b, rows_v, part_v, coef_v, gate_v, acc_v, sems):
        wid = lax.axis_index("s") * SC_CORES + lax.axis_index("c")
        zero = jnp.zeros((SC_LANES,), F32)
        lane_iota = lax.iota(I32, SC_LANES)
        hbufs = (h_a, h_b)

        def begin(tg, slot):
            pltpu.sync_copy(gate_hbm.at[tg], gate_v)
            for j in range(PEER_SEL):
                part_v[pl.ds(j * SC_LANES, SC_LANES)] = zero
            for j in range(d // SC_LANES):
                acc_v[pl.ds(j * SC_LANES, SC_LANES)] = zero

        def coefficients():
            for g in range(PEER_SEL // SC_LANES):
                rowbase = (g * SC_LANES + lane_iota) * SC_LANES
                a = plsc.load_gather(part_v, [rowbase])
                for l in range(1, SC_LANES):
                    a = a + plsc.load_gather(part_v, [rowbase + l])
                z = GELU_C0 * (a + GELU_C1 * a * a * a)
                th = 1.0 - 2.0 / (jnp.exp(2.0 * z) + 1.0)
                sl = pl.ds(g * SC_LANES, SC_LANES)
                coef_v[sl] = gate_v[sl] * (0.5 * a * (1.0 + th))

        def chunk(c, buf, slot):
            if c < per_tab:
                @pl.loop(0, d // SC_BLOCK)
                def _(lb):
                    lane0 = pl.multiple_of(lb * SC_BLOCK, SC_BLOCK)
                    hs = [hbufs[slot][pl.ds(lane0 + q * SC_LANES, SC_LANES)] for q in range(nvec)]

                    @plsc.parallel_loop(0, SC_ROWS, unroll=2)
                    def _(r):
                        ps = [hs[q] * buf[r, pl.ds(lane0 + q * SC_LANES, SC_LANES)] for q in range(nvec)]
                        while len(ps) > 1:
                            ps = [x + y for x, y in zip(ps[0::2], ps[1::2])]
                        row = pl.multiple_of((c * SC_ROWS + r) * SC_LANES, SC_LANES)
                        plsc.addupdate(part_v.at[pl.ds(row, SC_LANES)], ps[0])

                if c == per_tab - 1:
                    coefficients()
            else:
                cc = c - per_tab

                @pl.loop(0, d // SC_BLOCK)
                def _(lb):
                    lane0 = pl.multiple_of(lb * SC_BLOCK, SC_BLOCK)
                    accs = tuple(acc_v[pl.ds(lane0 + q * SC_LANES, SC_LANES)] for q in range(nvec))

                    @plsc.parallel_loop(0, SC_ROWS, unroll=2, carry=accs)
                    def accs(r, acc):
                        ck = plsc.load_gather(coef_v, [jnp.full((SC_LANES,), cc * SC_ROWS, I32) + r])
                        return tuple(acc[q] + ck * buf[r, pl.ds(lane0 + q * SC_LANES, SC_LANES)]
                                     for q in range(nvec))

                    for q in range(nvec):
                        acc_v[pl.ds(lane0 + q * SC_LANES, SC_LANES)] = accs[q]

        def end(t, slot):
            pltpu.sync_copy(acc_v, out_hbm.at[t])

        _sc_token_stream((utab_hbm, vtab_hbm), erow_hbm, h_hbm, tok0, tpw, wid * tpw, (idx_a, idx_b), hbufs,
                         rows_v, sems, begin, chunk, end)

    return run(utab, vtab, erow, hrows, gate)


def _peer_layer(x2, sc2, sh2, g2, peer_wq, peer_keys, utab, vtab, upack, vpack, lng, lnb, seq, alpha,
                tt_route=256, tt_gather=64, tm=512, sc_tokens=0):
    n, d = x2.shape
    keys = peer_keys.reshape(2 * PEER_HEADS, N_KEYS, PEER_DK // 2)
    h2, hv, erow, gate = _route_call(x2, sc2, sh2, peer_wq.astype(BF16), keys.astype(BF16), seq, tt_route)
    n_tc = n - sc_tokens
    if sc_tokens:
        y_sc = _sc_peer_call(erow, h2, gate, utab, vtab, n_tc, sc_tokens)
    coef = _peer_u_call(erow, upack, hv, gate, tt_gather, n_tc)
    y = _peer_v_call(erow, coef, vpack, tt_gather, n_tc).reshape(n_tc, d)
    if sc_tokens:
        y = jnp.concatenate([y, y_sc], axis=0)
    return _resln_call(x2, y, g2, lng.reshape(1, d), lnb.reshape(1, d), seq, tm, alpha)


def kernel(x, c, w_ada, b_ada, w_in, cmp_pe, cmp_w1, cmp_w2, w_pool, pool_scale, w_lift, w_o,
           ln_g, ln_b, peer_wq, peer_keys, peer_u, peer_v):
    bsz, seq, d = x.shape
    depth = w_ada.shape[0]
    assert d == D_MODEL and seq % 512 == 0 and SLC_TOPN <= seq // SLC_LEN <= LANES // N_KV
    alpha = (2 * depth) ** 0.25

    c_pad = jnp.zeros((SUBLANES, d), F32).at[:bsz].set(c)
    mods = _ada_call(c_pad, w_ada, b_ada)[:, :bsz]
    cst = _Consts(seq)
    nchain = 2 if bsz % 2 == 0 else 1
    bpc = bsz // nchain
    xs = [x[i * bpc:(i + 1) * bpc].reshape(bpc * seq, d) for i in range(nchain)]
    sc_tokens = SC_TOKEN_SHARE * bpc * seq // 16
    for l in range(depth):
        upack, vpack = _pack_table(peer_u[l]), _pack_table(peer_v[l])
        for i in range(nchain):
            sh1, sc1, g1, sh2, sc2, g2 = (mods[l][i * bpc:(i + 1) * bpc, j * d:(j + 1) * d].reshape(bpc, 1, d)
                                          for j in range(6))
            xi = _token_mixer_layer(xs[i], sc1, sh1, g1, w_in[l], cmp_pe[l], cmp_w1[l], cmp_w2[l], w_pool[l],
                                    pool_scale[l], w_lift[l], w_o[l], ln_g[l, 0], ln_b[l, 0], cst, bpc, seq, alpha)
            xs[i] = _peer_layer(xi, sc2, sh2, g2, peer_wq[l], peer_keys[l], peer_u[l], peer_v[l], upack, vpack,
                                ln_g[l, 1], ln_b[l, 1], seq, alpha, sc_tokens=sc_tokens)
    return jnp.concatenate(xs, axis=0).reshape(bsz, seq, d)
```

```python
import dataclasses
import functools

import jax
import jax.numpy as jnp
import numpy as np
from jax import lax
from jax.experimental import pallas as pl
from jax.experimental.pallas import tpu as pltpu
from jax.experimental.pallas import tpu_sc as plsc

F32 = jnp.float32
BF16 = jnp.bfloat16
I32 = jnp.int32
HI = lax.Precision.HIGHEST

D_MODEL = 1024
N_HEADS = 8
HEAD_DIM = 64
N_KV = 2
HPG = N_HEADS // N_KV
ROT_DIM = HEAD_DIM // 4
ROT_HALF = ROT_DIM // 2
ROPE_THETA = 500000.0
CMP_LEN = 32
CMP_STRIDE = 16
SLC_LEN = 64
SLC_TOPN = 16
WINDOW = 512
SCALE = HEAD_DIM ** -0.5
NEG = -1e30
FORCE_INIT = 1e6
FORCE_LOCAL = 2e6
POOL_GROUPS = 4
POOL_WINDOWS = (2, 4, 8, 16)
POOL_WIDTH = 512
POOL_GW = POOL_WIDTH // POOL_GROUPS
POOL_HALO = 16
ATT_WIDTH = N_HEADS * HEAD_DIM
KV_WIDTH = 3 * 2 * N_KV * HEAD_DIM
GATE_NSA = 3 * N_HEADS
MERGE_GATES = 2 * D_MODEL
PEER_HEADS = 8
N_KEYS = 128
PEER_TOPK = 16
PEER_DK = 128
PEER_SEL = PEER_HEADS * PEER_TOPK
LN_EPS = 1e-5

LANES = 128
SUBLANES = 8
VMEM_LIMIT = 56 * 1024 * 1024

NT_DIMS = (((1,), (1,)), ((), ()))


def _cparams(sem):
    return pltpu.CompilerParams(dimension_semantics=sem, vmem_limit_bytes=VMEM_LIMIT)


def _ada_kernel(c_ref, w_ref, b_ref, o_ref):
    c = c_ref[...]
    ca = c * jax.nn.sigmoid(c)
    o_ref[...] = jnp.dot(ca, w_ref[...], precision=HI, preferred_element_type=F32) + b_ref[...]


def _ada_call(c_pad, w_ada, b_ada):
    depth = w_ada.shape[0]
    nblk = w_ada.shape[2] // D_MODEL
    rows = c_pad.shape[0]
    return pl.pallas_call(
        _ada_kernel,
        grid=(depth, nblk),
        in_specs=[
            pl.BlockSpec((rows, D_MODEL), lambda l, j: (0, 0)),
            pl.BlockSpec((None, D_MODEL, D_MODEL), lambda l, j: (l, 0, j)),
            pl.BlockSpec((None, 1, D_MODEL), lambda l, j: (l, 0, j)),
        ],
        out_specs=pl.BlockSpec((None, rows, D_MODEL), lambda l, j: (l, 0, j)),
        out_shape=jax.ShapeDtypeStruct((depth, rows, nblk * D_MODEL), F32),
        compiler_params=_cparams(("arbitrary", "arbitrary")),
        name="ada_mod",
    )(c_pad, w_ada, b_ada.reshape(depth, 1, -1))


IN_COLS = ATT_WIDTH + KV_WIDTH + POOL_WIDTH + MERGE_GATES + LANES


def _rope_lanes(z, rc, rs1, rs2):
    return z * rc + pltpu.roll(z, ROT_HALF, 1) * rs1 + pltpu.roll(z, LANES - ROT_HALF, 1) * rs2


def _inproj_kernel(x_ref, sc_ref, sh_ref, w_ref, rc_ref, rs1_ref, rs2_ref,
                   q_ref, kv_ref, p_ref, mrg_ref, gn_ref):
    h = x_ref[...] * (1.0 + sc_ref[...]) + sh_ref[...]
    a = jnp.dot(h.astype(BF16), w_ref[...], preferred_element_type=F32)
    rc, rs1, rs2 = rc_ref[...], rs1_ref[...], rs2_ref[...]
    for j in range(ATT_WIDTH // LANES):
        q_ref[:, j * LANES:(j + 1) * LANES] = _rope_lanes(a[:, j * LANES:(j + 1) * LANES], rc, rs1, rs2)
    for br in range(3):
        c0 = ATT_WIDTH + br * 2 * LANES
        k = a[:, c0:c0 + LANES]
        if br > 0:
            k = _rope_lanes(k, rc, rs1, rs2)
        kv_ref[:, br * 2 * LANES:br * 2 * LANES + LANES] = k
        kv_ref[:, br * 2 * LANES + LANES:(br + 1) * 2 * LANES] = a[:, c0 + LANES:c0 + 2 * LANES]
    c1 = ATT_WIDTH + KV_WIDTH
    p_ref[...] = a[:, c1:c1 + POOL_WIDTH]
    mrg_ref[...] = a[:, c1 + POOL_WIDTH:c1 + POOL_WIDTH + MERGE_GATES]
    gn_ref[...] = a[:, c1 + POOL_WIDTH + MERGE_GATES:]


def _inproj_call(x2, sc, sh, w, rc, rs1, rs2, seq, tm):
    n = x2.shape[0]
    tpb = seq // tm
    tok = lambda i: (i, 0)
    bat = lambda i: (i // tpb, 0, 0)
    pos = lambda i: (i % tpb, 0)
    full = lambda i: (0, 0)
    return pl.pallas_call(
        _inproj_kernel,
        grid=(n // tm,),
        in_specs=[
            pl.BlockSpec((tm, D_MODEL), tok),
            pl.BlockSpec((None, 1, D_MODEL), bat),
            pl.BlockSpec((None, 1, D_MODEL), bat),
            pl.BlockSpec((D_MODEL, IN_COLS), full),
            pl.BlockSpec((tm, LANES), pos),
            pl.BlockSpec((tm, LANES), pos),
            pl.BlockSpec((tm, LANES), pos),
        ],
        out_specs=[
            pl.BlockSpec((tm, ATT_WIDTH), tok),
            pl.BlockSpec((tm, KV_WIDTH), tok),
            pl.BlockSpec((tm, POOL_WIDTH), tok),
            pl.BlockSpec((tm, MERGE_GATES), tok),
            pl.BlockSpec((tm, LANES), tok),
        ],
        out_shape=[
            jax.ShapeDtypeStruct((n, ATT_WIDTH), F32),
            jax.ShapeDtypeStruct((n, KV_WIDTH), F32),
            jax.ShapeDtypeStruct((n, POOL_WIDTH), F32),
            jax.ShapeDtypeStruct((n, MERGE_GATES), F32),
            jax.ShapeDtypeStruct((n, LANES), F32),
        ],
        compiler_params=_cparams(("arbitrary",)),
        name="in_proj",
    )(x2, sc, sh, w, rc, rs1, rs2)


def _compress_kernel(z_ref, pet_ref, peb_ref, w1t_ref, w1b_ref, w2_ref, rc_ref, rs1_ref, rs2_ref, o_ref):
    z = z_ref[...]
    rows = z.shape[0]
    top = jnp.dot(z + pet_ref[...], w1t_ref[...], precision=HI, preferred_element_type=F32)
    bot = jnp.dot(z + peb_ref[...], w1b_ref[...], precision=HI, preferred_element_type=F32)
    pre = top + pltpu.roll(bot, rows - 1, 0)
    y = jnp.dot(jax.nn.gelu(pre), w2_ref[...], precision=HI, preferred_element_type=F32)
    o_ref[...] = _rope_lanes(y, rc_ref[...], rs1_ref[...], rs2_ref[...])


def _compress_call(z, pet, peb, w1t, w1b, w2, rc, rs1, rs2):
    b, _, rows, width = z.shape
    kvsel = lambda i, j: (j, 0, 0)
    return pl.pallas_call(
        _compress_kernel,
        grid=(b, 2),
        in_specs=[
            pl.BlockSpec((None, None, rows, width), lambda i, j: (i, j, 0, 0)),
            pl.BlockSpec((None, 1, width), kvsel),
            pl.BlockSpec((None, 1, width), kvsel),
            pl.BlockSpec((None, width, LANES), kvsel),
            pl.BlockSpec((None, width, LANES), kvsel),
            pl.BlockSpec((None, LANES, LANES), kvsel),
            pl.BlockSpec((None, rows, LANES), kvsel),
            pl.BlockSpec((None, rows, LANES), kvsel),
            pl.BlockSpec((None, rows, LANES), kvsel),
        ],
        out_specs=pl.BlockSpec((None, None, rows, LANES), lambda i, j: (i, j, 0, 0)),
        out_shape=jax.ShapeDtypeStruct((b, 2, rows, LANES), F32),
        compiler_params=_cparams(("arbitrary", "arbitrary")),
        name="compress",
    )(z, pet, peb, w1t, w1b, w2, rc, rs1, rs2)


def _cmpsel_kernel(q_ref, kc_ref, vc_ref, c2s_ref, o_ref, sel_ref, *, tq):
    t0 = pl.program_id(1) * tq
    kc = kc_ref[...]
    vc = vc_ref[...]
    rows = kc.shape[0]
    trow = t0 + lax.broadcasted_iota(I32, (tq, rows), 0)
    cend = lax.broadcasted_iota(I32, (tq, rows), 1) * CMP_STRIDE + (CMP_LEN - 1)
    vis = cend <= trow
    anyv = (trow[:, :1] >= CMP_LEN - 1).astype(F32)
    imp = jnp.zeros((tq, LANES), F32)
    for g in range(N_KV):
        kg = kc[:, g * HEAD_DIM:(g + 1) * HEAD_DIM].astype(BF16)
        vg = vc[:, g * HEAD_DIM:(g + 1) * HEAD_DIM].astype(BF16)
        psum = jnp.zeros((tq, rows), F32)
        for h in range(HPG):
            hd = g * HPG + h
            qh = q_ref[:, hd * HEAD_DIM:(hd + 1) * HEAD_DIM].astype(BF16)
            s = lax.dot_general(qh, kg, NT_DIMS, preferred_element_type=F32) * SCALE
            s = jnp.where(vis, s, NEG)
            e = jnp.exp(s - jnp.max(s, axis=-1, keepdims=True))
            p = e / jnp.sum(e, axis=-1, keepdims=True) * anyv
            o_ref[:, hd * HEAD_DIM:(hd + 1) * HEAD_DIM] = jnp.dot(
                p.astype(BF16), vg, preferred_element_type=F32)
            psum = psum + p
        imp = imp + jnp.dot(psum.astype(BF16), c2s_ref[g], preferred_element_type=F32)
    lane = lax.broadcasted_iota(I32, (tq, LANES), 1)
    blk = lane & (SLC_LEN - 1)
    cur = lax.shift_right_logical(t0 + lax.broadcasted_iota(I32, (tq, LANES), 0), 6)
    score = jnp.where(blk <= cur, imp, NEG)
    score = jnp.where(blk == 0, FORCE_INIT, score)
    score = jnp.where(blk == cur, FORCE_LOCAL, score)
    sc_t = score.T
    nblk = LANES // N_KV
    jrow = lax.broadcasted_iota(I32, (nblk, tq), 0)
    sel_parts = []
    for g in range(N_KV):
        sc = sc_t[g * nblk:(g + 1) * nblk]
        cnt = jnp.zeros((nblk, tq), I32)
        for k in range(nblk):
            rk = sc[k:k + 1, :]
            ge = (rk >= sc).astype(I32)
            gt = (rk > sc).astype(I32)
            cnt = cnt + jnp.where(jrow > k, ge, gt)
        sel_parts.append((cnt < SLC_TOPN).astype(F32))
    sel_ref[...] = jnp.concatenate(sel_parts, axis=0).T


def _cmpsel_call(q, kvc, c2s, seq, tq):
    n = q.shape[0]
    b = n // seq
    nq = seq // tq
    rows = kvc.shape[2]
    tok = lambda i, j: (i * nq + j, 0)
    return pl.pallas_call(
        functools.partial(_cmpsel_kernel, tq=tq),
        grid=(b, nq),
        in_specs=[
            pl.BlockSpec((tq, ATT_WIDTH), tok),
            pl.BlockSpec((None, None, rows, LANES), lambda i, j: (i, 0, 0, 0)),
            pl.BlockSpec((None, None, rows, LANES), lambda i, j: (i, 1, 0, 0)),
            pl.BlockSpec((N_KV, rows, LANES), lambda i, j: (0, 0, 0)),
        ],
        out_specs=[pl.BlockSpec((tq, ATT_WIDTH), tok), pl.BlockSpec((tq, LANES), tok)],
        out_shape=[jax.ShapeDtypeStruct((n, ATT_WIDTH), F32), jax.ShapeDtypeStruct((n, LANES), F32)],
        compiler_params=_cparams(("arbitrary", "arbitrary")),
        name="cmp_select",
    )(q, kvc, kvc, c2s)


def _slc_kernel(q_ref, k_ref, v_ref, sel_ref, o_ref, qs_sc, m_sc, l_sc, acc_sc, *, tq, tk):
    qi = pl.program_id(1)
    kt = pl.program_id(2)
    nk = pl.num_programs(2)

    @pl.when(kt == 0)
    def _init():
        for hd in range(N_HEADS):
            g, h = divmod(hd, HPG)
            qs_sc[g, h * tq:(h + 1) * tq, :] = (q_ref[:, hd * HEAD_DIM:(hd + 1) * HEAD_DIM] * SCALE).astype(BF16)
        m_sc[...] = jnp.full(m_sc.shape, NEG, F32)
        l_sc[...] = jnp.zeros(l_sc.shape, F32)
        acc_sc[...] = jnp.zeros(acc_sc.shape, F32)

    @pl.when(kt * tk <= qi * tq + (tq - 1))
    def _step():
        t = qi * tq + lax.broadcasted_iota(I32, (tq, tk), 0)
        kp = kt * tk + lax.broadcasted_iota(I32, (tq, tk), 1)
        causal = kp <= t
        nblk = LANES // N_KV
        jb = lax.broadcasted_iota(I32, (nblk, tk), 0)
        kb = lax.shift_right_logical(kt * tk + lax.broadcasted_iota(I32, (nblk, tk), 1), 6)
        expand = (jb == kb).astype(BF16)
        for g in range(N_KV):
            selg = sel_ref[:, g * nblk:(g + 1) * nblk].astype(BF16)
            member = jnp.dot(selg, expand, preferred_element_type=F32)
            bias = jnp.where(jnp.logical_and(causal, member > 0.5), 0.0, NEG)
            bias = jnp.concatenate([bias] * HPG, axis=0)
            kg = k_ref[:, g * HEAD_DIM:(g + 1) * HEAD_DIM].astype(BF16)
            vg = v_ref[:, g * HEAD_DIM:(g + 1) * HEAD_DIM].astype(BF16)
            s = lax.dot_general(qs_sc[g], kg, NT_DIMS, preferred_element_type=F32) + bias
            chunks = [s[:, c * LANES:(c + 1) * LANES] for c in range(tk // LANES)]
            mc = chunks[0]
            for x in chunks[1:]:
                mc = jnp.maximum(mc, x)
            m_old = m_sc[g]
            m_new = jnp.maximum(m_old, jnp.max(mc, axis=-1, keepdims=True))
            alpha = jnp.exp(m_old - m_new)
            ps = [jnp.exp(x - m_new) for x in chunks]
            lsum = ps[0]
            for x in ps[1:]:
                lsum = lsum + x
            l_sc[g] = alpha * l_sc[g] + lsum
            p = jnp.concatenate(ps, axis=1).astype(BF16)
            acc_sc[g] = alpha[:, :HEAD_DIM] * acc_sc[g] + jnp.dot(p, vg, preferred_element_type=F32)
            m_sc[g] = m_new

    @pl.when(kt == nk - 1)
    def _fin():
        for hd in range(N_HEADS):
            g, h = divmod(hd, HPG)
            l = jnp.sum(l_sc[g, h * tq:(h + 1) * tq, :], axis=-1, keepdims=True)
            o_ref[:, hd * HEAD_DIM:(hd + 1) * HEAD_DIM] = acc_sc[g, h * tq:(h + 1) * tq, :] / l


def _slc_call(q, kv, sel, seq, tq, tk):
    n = q.shape[0]
    b = n // seq
    nq = seq // tq
    nk = seq // tk
    tok = lambda i, j, k: (i * nq + j, 0)

    def key_map(col):
        def f(i, j, k):
            last = (j * tq + tq - 1) // tk
            return (i * nk + jnp.minimum(k, last), col)
        return f

    return pl.pallas_call(
        functools.partial(_slc_kernel, tq=tq, tk=tk),
        grid=(b, nq, nk),
        in_specs=[
            pl.BlockSpec((tq, ATT_WIDTH), tok),
            pl.BlockSpec((tk, LANES), key_map(2)),
            pl.BlockSpec((tk, LANES), key_map(3)),
            pl.BlockSpec((tq, LANES), tok),
        ],
        out_specs=pl.BlockSpec((tq, ATT_WIDTH), tok),
        out_shape=jax.ShapeDtypeStruct((n, ATT_WIDTH), F32),
        scratch_shapes=[
            pltpu.VMEM((N_KV, HPG * tq, HEAD_DIM), BF16),
            pltpu.VMEM((N_KV, HPG * tq, LANES), F32),
            pltpu.VMEM((N_KV, HPG * tq, LANES), F32),
            pltpu.VMEM((N_KV, HPG * tq, HEAD_DIM), F32),
        ],
        compiler_params=_cparams(("arbitrary", "arbitrary", "arbitrary")),
        name="slc_attn",
    )(q, kv, kv, sel)


def _win_kernel(q_ref, *refs, tq, nkb):
    k_refs = refs[:nkb]
    v_refs = refs[nkb:2 * nkb]
    o_ref = refs[2 * nkb]
    qi = pl.program_id(1)
    t = qi * tq + lax.broadcasted_iota(I32, (tq, tq), 0)
    col = lax.broadcasted_iota(I32, (tq, tq), 1)
    biases = []
    for j in range(nkb):
        kp = (qi - (nkb - 1) + j) * tq + col
        diff = t - kp
        ok = jnp.logical_and(jnp.logical_and(diff >= 0, diff < WINDOW), kp >= 0)
        biases.append(jnp.concatenate([jnp.where(ok, 0.0, NEG)] * HPG, axis=0))
    for g in range(N_KV):
        qs = jnp.concatenate(
            [(q_ref[:, (g * HPG + h) * HEAD_DIM:(g * HPG + h + 1) * HEAD_DIM] * SCALE).astype(BF16)
             for h in range(HPG)], axis=0)
        chunks = []
        for j in range(nkb):
            kg = k_refs[j][:, g * HEAD_DIM:(g + 1) * HEAD_DIM].astype(BF16)
            s = lax.dot_general(qs, kg, NT_DIMS, preferred_element_type=F32) + biases[j]
            chunks += [s[:, c * LANES:(c + 1) * LANES] for c in range(tq // LANES)]
        mc = chunks[0]
        for x in chunks[1:]:
            mc = jnp.maximum(mc, x)
        m = jnp.broadcast_to(jnp.max(mc, axis=-1, keepdims=True), mc.shape)
        ps = [jnp.exp(x - m) for x in chunks]
        lsum = ps[0]
        for x in ps[1:]:
            lsum = lsum + x
        l = jnp.sum(lsum, axis=-1, keepdims=True)
        per = tq // LANES
        o = None
        for j in range(nkb):
            vg = v_refs[j][:, g * HEAD_DIM:(g + 1) * HEAD_DIM].astype(BF16)
            pj = jnp.concatenate(ps[j * per:(j + 1) * per], axis=1).astype(BF16)
            oj = jnp.dot(pj, vg, preferred_element_type=F32)
            o = oj if o is None else o + oj
        o = o / l
        for h in range(HPG):
            hd = g * HPG + h
            o_ref[:, hd * HEAD_DIM:(hd + 1) * HEAD_DIM] = o[h * tq:(h + 1) * tq]


def _win_call(q, kv, seq, tq):
    n = q.shape[0]
    b = n // seq
    nq = seq // tq
    nkb = WINDOW // tq + 1
    tok = lambda i, j: (i * nq + j, 0)

    def key_map(col, back):
        return lambda i, j: (i * nq + jnp.maximum(j - back, 0), col)

    k_specs = [pl.BlockSpec((tq, LANES), key_map(4, nkb - 1 - jj)) for jj in range(nkb)]
    v_specs = [pl.BlockSpec((tq, LANES), key_map(5, nkb - 1 - jj)) for jj in range(nkb)]
    return pl.pallas_call(
        functools.partial(_win_kernel, tq=tq, nkb=nkb),
        grid=(b, nq),
        in_specs=[pl.BlockSpec((tq, ATT_WIDTH), tok)] + k_specs + v_specs,
        out_specs=pl.BlockSpec((tq, ATT_WIDTH), tok),
        out_shape=jax.ShapeDtypeStruct((n, ATT_WIDTH), F32),
        compiler_params=_cparams(("arbitrary", "arbitrary")),
        name="win_attn",
    )(q, *([kv] * (2 * nkb)))


def _pool_kernel(p_ref, prev_ref, w_ref, sc_ref, o_ref, *, ts):
    i = pl.program_id(1)
    x = p_ref[...]
    prev = prev_ref[...] * (i > 0).astype(F32)
    xe = jnp.concatenate([prev, x], axis=0)
    t1 = (i * ts + 1 + lax.broadcasted_iota(I32, (ts, POOL_GW), 0)).astype(F32)
    for g, w in enumerate(POOL_WINDOWS):
        a = xe[:, g * POOL_GW:(g + 1) * POOL_GW]
        off = POOL_HALO
        span = 1
        while span < w:
            a = a[span:] + a[:-span]
            off -= span
            span *= 2
        sums = a[off:off + ts]
        cnt = jnp.minimum(t1, float(w))
        pooled = sums / cnt - x[:, g * POOL_GW:(g + 1) * POOL_GW]
        y = jnp.dot(pooled.astype(BF16), w_ref[g], preferred_element_type=F32)
        o_ref[:, g * POOL_GW:(g + 1) * POOL_GW] = y * sc_ref[:, g * POOL_GW:(g + 1) * POOL_GW]


def _pool_call(p_in, w_pool, pool_scale, seq, ts):
    n = p_in.shape[0]
    b = n // seq
    nt = seq // ts
    hpt = ts // POOL_HALO
    tok = lambda i, j: (i * nt + j, 0)
    return pl.pallas_call(
        functools.partial(_pool_kernel, ts=ts),
        grid=(b, nt),
        in_specs=[
            pl.BlockSpec((ts, POOL_WIDTH), tok),
            pl.BlockSpec((POOL_HALO, POOL_WIDTH), lambda i, j: (i * nt * hpt + jnp.maximum(j * hpt - 1, 0), 0)),
            pl.BlockSpec((POOL_GROUPS, POOL_GW, POOL_GW), lambda i, j: (0, 0, 0)),
            pl.BlockSpec((1, POOL_WIDTH), lambda i, j: (0, 0)),
        ],
        out_specs=pl.BlockSpec((ts, POOL_WIDTH), tok),
        out_shape=jax.ShapeDtypeStruct((n, POOL_WIDTH), F32),
        compiler_params=_cparams(("arbitrary", "arbitrary")),
        name="pool_mix",
    )(p_in, p_in, w_pool, pool_scale)


def _layer_norm(z, g, b):
    mu = jnp.mean(z, axis=-1, keepdims=True)
    zc = z - mu
    var = jnp.mean(zc * zc, axis=-1, keepdims=True)
    return zc * lax.rsqrt(var + LN_EPS) * g + b


def _merge_kernel(oc_ref, os_ref, ow_ref, gn_ref, op_ref, gm_ref, x_ref, g1_ref, lng_ref, lnb_ref,
                  wl_ref, wo_ref, eb_ref, o_ref, *, alpha):
    gate = jax.nn.sigmoid(gn_ref[...])
    branches = (oc_ref, os_ref, ow_ref)
    oatt = None
    for br in range(3):
        gx = jnp.dot(gate, eb_ref[br], precision=HI, preferred_element_type=F32)
        term = gx * branches[br][...]
        oatt = term if oatt is None else oatt + term
    la = jnp.dot(oatt.astype(BF16), wl_ref[0], preferred_element_type=F32)
    lb = jnp.dot(op_ref[...].astype(BF16), wl_ref[1], preferred_element_type=F32)
    gm = jax.nn.sigmoid(gm_ref[...])
    merged = gm[:, :D_MODEL] * la + gm[:, D_MODEL:] * lb
    y = jnp.dot(merged.astype(BF16), wo_ref[...], preferred_element_type=F32)
    z = alpha * x_ref[...] + g1_ref[...] * y
    o_ref[...] = _layer_norm(z, lng_ref[...], lnb_ref[...])


def _merge_call(oc, osl, ow, gn, op, gm, x2, g1, lng, lnb, wl, wo, eb, seq, tm, alpha):
    n = x2.shape[0]
    tpb = seq // tm
    tok = lambda i: (i, 0)
    bat = lambda i: (i // tpb, 0, 0)
    return pl.pallas_call(
        functools.partial(_merge_kernel, alpha=alpha),
        grid=(n // tm,),
        in_specs=[
            pl.BlockSpec((tm, ATT_WIDTH), tok), pl.BlockSpec((tm, ATT_WIDTH), tok),
            pl.BlockSpec((tm, ATT_WIDTH), tok), pl.BlockSpec((tm, LANES), tok),
            pl.BlockSpec((tm, POOL_WIDTH), tok), pl.BlockSpec((tm, MERGE_GATES), tok),
            pl.BlockSpec((tm, D_MODEL), tok),
            pl.BlockSpec((None, 1, D_MODEL), bat),
            pl.BlockSpec((1, D_MODEL), lambda i: (0, 0)), pl.BlockSpec((1, D_MODEL), lambda i: (0, 0)),
            pl.BlockSpec((2, ATT_WIDTH, D_MODEL), lambda i: (0, 0, 0)),
            pl.BlockSpec((D_MODEL, D_MODEL), lambda i: (0, 0)),
            pl.BlockSpec((3, LANES, ATT_WIDTH), lambda i: (0, 0, 0)),
        ],
        out_specs=pl.BlockSpec((tm, D_MODEL), tok),
        out_shape=jax.ShapeDtypeStruct((n, D_MODEL), F32),
        compiler_params=_cparams(("arbitrary",)),
        name="merge_out",
    )(oc, osl, ow, gn, op, gm, x2, g1, lng, lnb, wl, wo, eb)


def _extract_top(cur, ids, n):
    rows = cur.shape[0]
    rio = lax.broadcasted_iota(I32, cur.shape, 0)
    vals, outs = [], []
    for _ in range(n):
        m = jnp.max(cur, axis=0, keepdims=True)
        pos = jnp.min(jnp.where(cur == m, rio, rows), axis=0, keepdims=True)
        hit = rio == pos
        vals.append(m)
        outs.append(pos if ids is None else jnp.max(jnp.where(hit, ids, -1), axis=0, keepdims=True))
        cur = jnp.where(hit, -jnp.inf, cur)
    return jnp.concatenate(vals, axis=0), jnp.concatenate(outs, axis=0)


def _route_kernel(x_ref, sc_ref, sh_ref, wq_ref, keys_ref, h_ref, hv_ref, e_ref, g_ref,
                  st_sc, ts_sc, ti_sc, eo_sc, go_sc):
    h = x_ref[...] * (1.0 + sc_ref[...]) + sh_ref[...]
    h_ref[...] = h
    for r in range(SUBLANES):
        hv_ref[:, r, :] = h[:, r * LANES:(r + 1) * LANES]
    qp = jnp.dot(h.astype(BF16), wq_ref[...], preferred_element_type=F32).astype(BF16)
    half = PEER_DK // 2
    for hp in range(2 * PEER_HEADS):
        st_sc[hp] = lax.dot_general(keys_ref[hp], qp[:, hp * half:(hp + 1) * half], NT_DIMS,
                                    preferred_element_type=F32)

    def half_body(hp, carry):
        vals, ids = _extract_top(st_sc[hp], None, PEER_TOPK)
        ts_sc[hp] = vals
        ti_sc[hp] = ids
        return carry

    lax.fori_loop(0, 2 * PEER_HEADS, half_body, 0)

    def head_body(hh, carry):
        s1, s2 = ts_sc[2 * hh], ts_sc[2 * hh + 1]
        i1, i2 = ti_sc[2 * hh], ti_sc[2 * hh + 1]
        brow = lax.broadcasted_iota(I32, (SUBLANES, s1.shape[1]), 0)
        cands = [s1[0:1, :] + s2]
        cidxs = [i1[0:1, :] * N_KEYS + i2]
        for a in range(1, SUBLANES):
            ok = brow < PEER_TOPK // (a + 1)
            cands.append(jnp.where(ok, s1[a:a + 1, :] + s2[:SUBLANES], -jnp.inf))
            cidxs.append(i1[a:a + 1, :] * N_KEYS + i2[:SUBLANES])
        cands.append(s1[SUBLANES:] + s2[0:1, :])
        cidxs.append(i1[SUBLANES:] * N_KEYS + i2[0:1, :])
        sv, ei = _extract_top(jnp.concatenate(cands, axis=0), jnp.concatenate(cidxs, axis=0), PEER_TOPK)
        ex = jnp.exp(sv - sv[0:1, :])
        go_sc[hh] = ex / jnp.sum(ex, axis=0, keepdims=True)
        eo_sc[hh] = ei.astype(F32)
        return carry

    lax.fori_loop(0, PEER_HEADS, head_body, 0)
    e_all = jnp.concatenate([eo_sc[hh] for hh in range(PEER_HEADS)], axis=0)
    g_all = jnp.concatenate([go_sc[hh] for hh in range(PEER_HEADS)], axis=0)
    e_ref[...] = e_all.T.astype(I32) * (D_MODEL // 2 // LANES)
    g_ref[...] = g_all.T


def _route_call(x2, sc, sh, wq, keys, seq, tt):
    n = x2.shape[0]
    tpb = seq // tt
    tok = lambda i: (i, 0)
    bat = lambda i: (i // tpb, 0, 0)
    nhp = 2 * PEER_HEADS
    return pl.pallas_call(
        _route_kernel,
        grid=(n // tt,),
        in_specs=[
            pl.BlockSpec((tt, D_MODEL), tok),
            pl.BlockSpec((None, 1, D_MODEL), bat),
            pl.BlockSpec((None, 1, D_MODEL), bat),
            pl.BlockSpec((D_MODEL, PEER_HEADS * PEER_DK), lambda i: (0, 0)),
            pl.BlockSpec((nhp, N_KEYS, PEER_DK // 2), lambda i: (0, 0, 0)),
        ],
        out_specs=[pl.BlockSpec((tt, D_MODEL), tok), pl.BlockSpec((tt, SUBLANES, LANES), lambda i: (i, 0, 0)),
                   pl.BlockSpec((tt, PEER_SEL), tok), pl.BlockSpec((tt, PEER_SEL), tok)],
        out_shape=[jax.ShapeDtypeStruct((n, D_MODEL), F32), jax.ShapeDtypeStruct((n, SUBLANES, LANES), F32),
                   jax.ShapeDtypeStruct((n, PEER_SEL), I32), jax.ShapeDtypeStruct((n, PEER_SEL), F32)],
        scratch_shapes=[
            pltpu.VMEM((nhp, N_KEYS, tt), F32),
            pltpu.VMEM((nhp, PEER_TOPK, tt), F32),
            pltpu.VMEM((nhp, PEER_TOPK, tt), I32),
            pltpu.VMEM((PEER_HEADS, PEER_TOPK, tt), F32),
            pltpu.VMEM((PEER_HEADS, PEER_TOPK, tt), F32),
        ],
        compiler_params=_cparams(("arbitrary",)),
        name="peer_route",
    )(x2, sc, sh, wq, keys)


HALF_ROWS = SUBLANES // 2
HI_MASK = -65536
PAIR_TILES = PEER_SEL // 2
PAIR_ROWS = PAIR_TILES * SUBLANES


def _load_two_experts(tab_ref, ra, rb):
    wa = tab_ref[pl.ds(pl.multiple_of(ra, HALF_ROWS), HALF_ROWS), :]
    wb = tab_ref[pl.ds(pl.multiple_of(rb, HALF_ROWS), HALF_ROWS), :]
    w2 = jnp.concatenate([wa, wb], axis=0)
    return lax.bitcast_convert_type(w2 << 16, F32), lax.bitcast_convert_type(w2 & HI_MASK, F32)


def _fold_pairs(vs):
    row = lax.broadcasted_iota(I32, (SUBLANES, LANES), 0)
    shift = HALF_ROWS // 2
    while len(vs) > 1:
        low = (row & shift) == 0
        vs = [jnp.where(low, a + pltpu.roll(a, SUBLANES - shift, 0), b + pltpu.roll(b, shift, 0))
              for a, b in zip(vs[0::2], vs[1::2])]
        shift //= 2
    return vs[0]


def _fold_order():
    idx = [[2 * i if r < HALF_ROWS else 2 * i + 1 for r in range(SUBLANES)] for i in range(HALF_ROWS)]
    shift = HALF_ROWS // 2
    while len(idx) > 1:
        idx = [[a[r] if (r & shift) == 0 else b[r] for r in range(SUBLANES)]
               for a, b in zip(idx[0::2], idx[1::2])]
        shift //= 2
    return idx[0]


def _peer_u_kernel(e_sm, tab_ref, hv_ref, gate_ref, coef_ref, *, tt):
    row = lax.broadcasted_iota(I32, (SUBLANES, LANES), 0)
    low = row < HALF_ROWS
    eye = (lax.broadcasted_iota(I32, (PEER_SEL, LANES), 0) ==
           lax.broadcasted_iota(I32, (PEER_SEL, LANES), 1))
    order = _fold_order()

    def finish(t, part):
        col = jnp.sum(part, axis=-1, keepdims=True)
        a_row = jnp.sum(jnp.where(eye, col, 0.0), axis=0, keepdims=True)
        coef_ref[t] = gate_ref[pl.ds(t, 1), :] * jax.nn.gelu(a_row)

    def token(t, part_prev):
        finish(jnp.maximum(t - 1, 0), part_prev)
        hv = hv_ref[t]
        hsw = pltpu.roll(hv, HALF_ROWS, 0)
        h_lo = jnp.where(low, hv, hsw)
        h_hi = jnp.where(low, hsw, hv)
        folded = []
        for j in range(PEER_SEL // SUBLANES):
            prods = []
            for i in range(HALF_ROWS):
                ka = j * SUBLANES + order.index(2 * i)
                kb = j * SUBLANES + order.index(2 * i + 1)
                lo, hi = _load_two_experts(tab_ref, e_sm[t, ka], e_sm[t, kb])
                prods.append(lo * h_lo + hi * h_hi)
            folded.append(_fold_pairs(prods))
        return jnp.concatenate(folded, axis=0)

    last = lax.fori_loop(0, tt, token, jnp.zeros((PEER_SEL, LANES), F32))
    finish(tt - 1, last)


def _peer_v_kernel(e_sm, coef_ref, tab_ref, y_ref, cv_sc, *, tt, nacc):
    row = lax.broadcasted_iota(I32, (SUBLANES, LANES), 0)
    low = row < HALF_ROWS
    rr = lax.broadcasted_iota(I32, (PAIR_ROWS, LANES), 0)
    kk = lax.broadcasted_iota(I32, (PAIR_ROWS, LANES), 1)
    onehot = (kk == 2 * (rr >> 3) + ((rr >> 2) & 1)).astype(F32)
    ones = jnp.ones((LANES, LANES), BF16)

    def expand(t, slot):
        lhs = (onehot * coef_ref[t]).astype(BF16)
        cv_sc[slot] = jnp.dot(lhs, ones, preferred_element_type=F32)

    def process(t, slot):
        acc_lo = [jnp.zeros((SUBLANES, LANES), F32) for _ in range(nacc)]
        acc_hi = [jnp.zeros((SUBLANES, LANES), F32) for _ in range(nacc)]
        for j in range(PAIR_TILES):
            lo, hi = _load_two_experts(tab_ref, e_sm[t, 2 * j], e_sm[t, 2 * j + 1])
            cv = cv_sc[slot, j * SUBLANES:(j + 1) * SUBLANES, :]
            acc_lo[j % nacc] = acc_lo[j % nacc] + cv * lo
            acc_hi[j % nacc] = acc_hi[j % nacc] + cv * hi
        a_lo, a_hi = acc_lo[0], acc_hi[0]
        for i in range(1, nacc):
            a_lo = a_lo + acc_lo[i]
            a_hi = a_hi + acc_hi[i]
        a_lo = a_lo + pltpu.roll(a_lo, HALF_ROWS, 0)
        a_hi = a_hi + pltpu.roll(a_hi, HALF_ROWS, 0)
        y_ref[t] = jnp.where(low, a_lo, a_hi)

    expand(0, 0)

    def two_tokens(i, carry):
        t = 2 * i
        expand(t + 1, 1)
        process(t, 0)
        expand(jnp.minimum(t + 2, tt - 1), 0)
        process(t + 1, 1)
        return carry

    lax.fori_loop(0, tt // 2, two_tokens, 0)


def _resident_table_spec(tab):
    return pl.BlockSpec(tab.shape, lambda i: (0, 0), pipeline_mode=pl.Buffered(1))


def _peer_u_call(erow, tab, hv, gate, tt, n):
    return pl.pallas_call(
        functools.partial(_peer_u_kernel, tt=tt),
        grid=(n // tt,),
        in_specs=[
            pl.BlockSpec((tt, PEER_SEL), lambda i: (i, 0), memory_space=pltpu.SMEM),
            _resident_table_spec(tab),
            pl.BlockSpec((tt, SUBLANES, LANES), lambda i: (i, 0, 0)),
            pl.BlockSpec((tt, PEER_SEL), lambda i: (i, 0)),
        ],
        out_specs=pl.BlockSpec((tt, 1, PEER_SEL), lambda i: (i, 0, 0)),
        out_shape=jax.ShapeDtypeStruct((n, 1, PEER_SEL), F32),
        compiler_params=_cparams(("arbitrary",)),
        name="peer_u",
    )(erow, tab, hv, gate)


def _peer_v_call(erow, coef3, tab, tt, n):
    assert tt % 2 == 0
    return pl.pallas_call(
        functools.partial(_peer_v_kernel, tt=tt, nacc=4),
        grid=(n // tt,),
        in_specs=[
            pl.BlockSpec((tt, PEER_SEL), lambda i: (i, 0), memory_space=pltpu.SMEM),
            pl.BlockSpec((tt, 1, PEER_SEL), lambda i: (i, 0, 0)),
            _resident_table_spec(tab),
        ],
        out_specs=pl.BlockSpec((tt, SUBLANES, LANES), lambda i: (i, 0, 0)),
        out_shape=jax.ShapeDtypeStruct((n, SUBLANES, LANES), F32),
        scratch_shapes=[pltpu.VMEM((2, PAIR_ROWS, LANES), F32)],
        compiler_params=_cparams(("arbitrary",)),
        name="peer_v",
    )(erow, coef3, tab)


def _resln_kernel(x_ref, y_ref, g_ref, lng_ref, lnb_ref, o_ref, *, alpha):
    z = alpha * x_ref[...] + g_ref[...] * y_ref[...]
    o_ref[...] = _layer_norm(z, lng_ref[...], lnb_ref[...])


def _resln_call(x2, y2, g2, lng, lnb, seq, tm, alpha):
    n = x2.shape[0]
    tpb = seq // tm
    tok = lambda i: (i, 0)
    return pl.pallas_call(
        functools.partial(_resln_kernel, alpha=alpha),
        grid=(n // tm,),
        in_specs=[
            pl.BlockSpec((tm, D_MODEL), tok), pl.BlockSpec((tm, D_MODEL), tok),
            pl.BlockSpec((None, 1, D_MODEL), lambda i: (i // tpb, 0, 0)),
            pl.BlockSpec((1, D_MODEL), lambda i: (0, 0)), pl.BlockSpec((1, D_MODEL), lambda i: (0, 0)),
        ],
        out_specs=pl.BlockSpec((tm, D_MODEL), tok),
        out_shape=jax.ShapeDtypeStruct((n, D_MODEL), F32),
        compiler_params=_cparams(("arbitrary",)),
        name="res_ln",
    )(x2, y2, g2, lng, lnb)


def _rope_lane_tables(pos):
    inv = ROPE_THETA ** (-jnp.arange(0, ROT_DIM, 2, dtype=F32) / ROT_DIM)
    ang = pos.astype(F32)[:, None] * inv[None, :]
    cos, sin = jnp.cos(ang), jnp.sin(ang)
    lane = np.arange(LANES) % HEAD_DIM
    fidx = lane % ROT_HALF
    first = jnp.asarray(lane < ROT_HALF)
    second = jnp.asarray((lane >= ROT_HALF) & (lane < ROT_DIM))
    rot = jnp.asarray(lane < ROT_DIM)
    cl, sl = cos[:, fidx], sin[:, fidx]
    rc = jnp.where(rot, cl, 1.0)
    rs1 = jnp.where(second, sl, 0.0)
    rs2 = jnp.where(first, -sl, 0.0)
    return rc, rs1, rs2


def _pack_kernel(t_ref, o_ref):
    x = t_ref[...]
    half = x.shape[1] // 2
    lo = lax.bitcast_convert_type(x[:, :half].astype(BF16).astype(F32), I32)
    hi = lax.bitcast_convert_type(x[:, half:].astype(BF16).astype(F32), I32)
    o_ref[...] = lax.shift_right_logical(lo, jnp.full_like(lo, 16)) | hi


def _pack_table(tab, te=512):
    e, d = tab.shape
    words = pl.pallas_call(
        _pack_kernel, grid=(e // te,),
        in_specs=[pl.BlockSpec((te, d), lambda i: (i, 0))],
        out_specs=pl.BlockSpec((te, d // 2), lambda i: (i, 0)),
        out_shape=jax.ShapeDtypeStruct((e, d // 2), I32),
        compiler_params=_cparams(("arbitrary",)), name="pack_table",
    )(tab)
    return words.reshape(e * (d // 2) // LANES, LANES)


def _cmp_to_slc_wide(rows, n_slc):
    st = np.arange(rows) * CMP_STRIDE
    js = np.arange(n_slc) * SLC_LEN
    ov = np.minimum(st[:, None] + CMP_LEN, js[None, :] + SLC_LEN) - np.maximum(st[:, None], js[None, :])
    c2s = np.maximum(ov, 0).astype(np.float32) / CMP_STRIDE
    wide = np.zeros((N_KV, rows, LANES), np.float32)
    nblk = LANES // N_KV
    for g in range(N_KV):
        wide[g, :, g * nblk:g * nblk + n_slc] = c2s
    return jnp.asarray(wide)


def _gate_expanders():
    eb = np.zeros((3, LANES, ATT_WIDTH), np.float32)
    for hd in range(N_HEADS):
        for br in range(3):
            eb[br, hd * 3 + br, hd * HEAD_DIM:(hd + 1) * HEAD_DIM] = 1.0
    return jnp.asarray(eb)


class _Consts:
    def __init__(self, seq):
        self.rows = seq // CMP_STRIDE
        self.rope = _rope_lane_tables(jnp.arange(seq))
        cpos = jnp.arange(self.rows) * CMP_STRIDE + CMP_LEN - 1
        crope = _rope_lane_tables(cpos)
        ident = (jnp.ones_like(crope[0]), jnp.zeros_like(crope[0]), jnp.zeros_like(crope[0]))
        self.crope = tuple(jnp.stack([a, b]) for a, b in zip(crope, ident))
        self.c2s = _cmp_to_slc_wide(self.rows, seq // SLC_LEN)
        self.eb = _gate_expanders()


def _token_mixer_layer(x2, sc1, sh1, g1, w_in, cmp_pe, cmp_w1, cmp_w2, w_pool, pool_scale, w_lift, w_o,
                       lng, lnb, cst, bsz, seq, alpha):
    d = D_MODEL
    rows = cst.rows
    s1 = ATT_WIDTH + KV_WIDTH
    s2 = s1 + GATE_NSA
    s3 = s2 + POOL_WIDTH
    w_gate = jnp.pad(w_in[:, s1:s2], ((0, 0), (0, LANES - GATE_NSA)))
    w_all = jnp.concatenate([w_in[:, :s1], w_in[:, s2:s3], w_in[:, s3:], w_gate], axis=1).astype(BF16)
    q, kv, p_in, g_mrg, g_nsa = _inproj_call(x2, sc1, sh1, w_all, *cst.rope, seq, 256)

    eye_g = jnp.eye(N_KV, dtype=F32)
    zc = kv[:, :2 * LANES].reshape(bsz, rows, CMP_STRIDE, 2, LANES)
    zc = jnp.transpose(zc, (0, 3, 1, 2, 4)).reshape(bsz, 2, rows, CMP_STRIDE * LANES)
    w1x = jnp.einsum('klde,gh->klgdhe', cmp_w1.reshape(2, CMP_LEN, HEAD_DIM, HEAD_DIM), eye_g)
    w1x = w1x.reshape(2, CMP_LEN * LANES, LANES)
    half = CMP_STRIDE * LANES
    pex = jnp.broadcast_to(cmp_pe[:, :, None, :], (2, CMP_LEN, N_KV, HEAD_DIM)).reshape(2, 1, CMP_LEN * LANES)
    w2x = jnp.einsum('kef,gh->kgehf', cmp_w2, eye_g).reshape(2, LANES, LANES)
    kvc = _compress_call(zc, pex[:, :, :half], pex[:, :, half:], w1x[:, :half], w1x[:, half:], w2x,
                         *cst.crope)

    o_cmp, sel = _cmpsel_call(q, kvc, cst.c2s.astype(BF16), seq, 256)
    o_slc = _slc_call(q, kv, sel, seq, 256, 512)
    o_win = _win_call(q, kv, seq, 256)
    o_pool = _pool_call(p_in, w_pool.astype(BF16), pool_scale.reshape(1, -1), seq, 512)
    return _merge_call(o_cmp, o_slc, o_win, g_nsa, o_pool, g_mrg, x2, g1,
                       lng.reshape(1, d), lnb.reshape(1, d),
                       w_lift.astype(BF16), w_o.astype(BF16), cst.eb, seq, 256, alpha)


SC_CORES = 2
SC_SUBCORES = 16
SC_LANES = 16
SC_WORKERS = SC_CORES * SC_SUBCORES
SC_ROWS = 32
SC_TOKEN_SHARE = 10
SC_BLOCK = 256


def _sc_params():
    cp = pltpu.CompilerParams()
    if "needs_layout_passes" in pltpu.CompilerParams.__dataclass_fields__:
        cp = dataclasses.replace(cp, needs_layout_passes=False)
    return cp


def _sc_token_stream(tabs, erow_hbm, side_hbm, tok0, tpw, base, idx_bufs, side_bufs, rows_v, sems,
                     begin_fn, chunk_fn, end_fn):
    per_tab = PEER_SEL // SC_ROWS
    nch = per_tab * len(tabs)
    assert nch % 2 == 0

    def idx_copy(t, s):
        return pltpu.make_async_copy(erow_hbm.at[tok0 + base + t], idx_bufs[s], sems.at[2 + s])

    def side_copy(t, s):
        return pltpu.make_async_copy(side_hbm.at[tok0 + base + t], side_bufs[s], sems.at[4 + s])

    def row_copy(s, c):
        rows = idx_bufs[s].at[pl.ds((c % per_tab) * SC_ROWS, SC_ROWS)]
        return pltpu.make_async_copy(tabs[c // per_tab].at[rows], rows_v.at[c % 2], sems.at[c % 2])

    def to_expert_ids(s):
        shift = jnp.full((SC_LANES,), (D_MODEL // 2 // LANES).bit_length() - 1, I32)
        for j in range(PEER_SEL // SC_LANES):
            sl = pl.ds(j * SC_LANES, SC_LANES)
            idx_bufs[s][sl] = lax.shift_right_logical(idx_bufs[s][sl], shift)

    idx_copy(0, 0).start()
    side_copy(0, 0).start()
    idx_copy(0, 0).wait()
    to_expert_ids(0)
    row_copy(0, 0).start()

    @pl.loop(0, tpw // 2)
    def _(i):
        for slot in (0, 1):
            other = 1 - slot
            t = 2 * i + slot
            tn = jnp.minimum(t + 1, tpw - 1)
            idx_copy(tn, other).start()
            side_copy(tn, other).start()
            side_copy(t, slot).wait()
            begin_fn(tok0 + base + t, slot)
            for c in range(nch):
                if c + 1 < nch:
                    row_copy(slot, c + 1).start()
                else:
                    idx_copy(tn, other).wait()
                    to_expert_ids(other)
                    row_copy(other, 0).start()
                row_copy(slot, c).wait()
                chunk_fn(c, rows_v.at[c % 2], slot)
            end_fn(base + t, slot)

    row_copy(0, 0).wait()
    side_copy(0, 0).wait()


GELU_C0 = 0.7978845608028654
GELU_C1 = 0.044715


def _sc_peer_call(erow, hrows, gate, utab, vtab, tok0, m):
    d = utab.shape[1]
    assert m % (2 * SC_WORKERS) == 0 and d % SC_BLOCK == 0
    tpw = m // SC_WORKERS
    per_tab = PEER_SEL // SC_ROWS
    nvec = SC_BLOCK // SC_LANES
    mesh = plsc.VectorSubcoreMesh(core_axis_name="c", subcore_axis_name="s")

    @functools.partial(
        pl.kernel, mesh=mesh, out_type=jax.ShapeDtypeStruct((m, d), F32),
        scratch_types=[pltpu.VMEM((PEER_SEL,), I32), pltpu.VMEM((PEER_SEL,), I32),
                       pltpu.VMEM((d,), F32), pltpu.VMEM((d,), F32),
                       pltpu.VMEM((2, SC_ROWS, d), F32), pltpu.VMEM((PEER_SEL * SC_LANES,), F32),
                       pltpu.VMEM((PEER_SEL,), F32), pltpu.VMEM((PEER_SEL,), F32), pltpu.VMEM((d,), F32),
                       pltpu.SemaphoreType.DMA((6,))],
        compiler_params=_sc_params(), name="sc_peer")
    def run(utab_hbm, vtab_hbm, erow_hbm, h_hbm, gate_hbm, out_hbm,
            idx_a, idx_b, h_a, h_b, rows_v, part_v, coef_v, gate_v, acc_v, sems):
        wid = lax.axis_index("s") * SC_CORES + lax.axis_index("c")
        zero = jnp.zeros((SC_LANES,), F32)
        lane_iota = lax.iota(I32, SC_LANES)
        hbufs = (h_a, h_b)

        def begin(tg, slot):
            pltpu.sync_copy(gate_hbm.at[tg], gate_v)
            for j in range(PEER_SEL):
                part_v[pl.ds(j * SC_LANES, SC_LANES)] = zero
            for j in range(d // SC_LANES):
                acc_v[pl.ds(j * SC_LANES, SC_LANES)] = zero

        def coefficients():
            for g in range(PEER_SEL // SC_LANES):
                rowbase = (g * SC_LANES + lane_iota) * SC_LANES
                a = plsc.load_gather(part_v, [rowbase])
                for l in range(1, SC_LANES):
                    a = a + plsc.load_gather(part_v, [rowbase + l])
                z = GELU_C0 * (a + GELU_C1 * a * a * a)
                th = 1.0 - 2.0 / (jnp.exp(2.0 * z) + 1.0)
                sl = pl.ds(g * SC_LANES, SC_LANES)
                coef_v[sl] = gate_v[sl] * (0.5 * a * (1.0 + th))

        def chunk(c, buf, slot):
            if c < per_tab:
                @pl.loop(0, d // SC_BLOCK)
                def _(lb):
                    lane0 = pl.multiple_of(lb * SC_BLOCK, SC_BLOCK)
                    hs = [hbufs[slot][pl.ds(lane0 + q * SC_LANES, SC_LANES)] for q in range(nvec)]

                    @plsc.parallel_loop(0, SC_ROWS, unroll=2)
                    def _(r):
                        ps = [hs[q] * buf[r, pl.ds(lane0 + q * SC_LANES, SC_LANES)] for q in range(nvec)]
                        while len(ps) > 1:
                            ps = [x + y for x, y in zip(ps[0::2], ps[1::2])]
                        row = pl.multiple_of((c * SC_ROWS + r) * SC_LANES, SC_LANES)
                        plsc.addupdate(part_v.at[pl.ds(row, SC_LANES)], ps[0])

                if c == per_tab - 1:
                    coefficients()
            else:
                cc = c - per_tab

                @pl.loop(0, d // SC_BLOCK)
                def _(lb):
                    lane0 = pl.multiple_of(lb * SC_BLOCK, SC_BLOCK)
                    accs = tuple(acc_v[pl.ds(lane0 + q * SC_LANES, SC_LANES)] for q in range(nvec))

                    @plsc.parallel_loop(0, SC_ROWS, unroll=2, carry=accs)
                    def accs(r, acc):
                        ck = plsc.load_gather(coef_v, [jnp.full((SC_LANES,), cc * SC_ROWS, I32) + r])
                        return tuple(acc[q] + ck * buf[r, pl.ds(lane0 + q * SC_LANES, SC_LANES)]
                                     for q in range(nvec))

                    for q in range(nvec):
                        acc_v[pl.ds(lane0 + q * SC_LANES, SC_LANES)] = accs[q]

        def end(t, slot):
            pltpu.sync_copy(acc_v, out_hbm.at[t])

        _sc_token_stream((utab_hbm, vtab_hbm), erow_hbm, h_hbm, tok0, tpw, wid * tpw, (idx_a, idx_b), hbufs,
                         rows_v, sems, begin, chunk, end)

    return run(utab, vtab, erow, hrows, gate)


def _peer_layer(x2, sc2, sh2, g2, peer_wq, peer_keys, utab, vtab, upack, vpack, lng, lnb, seq, alpha,
                tt_route=256, tt_gather=64, tm=512, sc_tokens=0):
    n, d = x2.shape
    keys = peer_keys.reshape(2 * PEER_HEADS, N_KEYS, PEER_DK // 2)
    h2, hv, erow, gate = _route_call(x2, sc2, sh2, peer_wq.astype(BF16), keys.astype(BF16), seq, tt_route)
    n_tc = n - sc_tokens
    if sc_tokens:
        y_sc = _sc_peer_call(erow, h2, gate, utab, vtab, n_tc, sc_tokens)
    coef = _peer_u_call(erow, upack, hv, gate, tt_gather, n_tc)
    y = _peer_v_call(erow, coef, vpack, tt_gather, n_tc).reshape(n_tc, d)
    if sc_tokens:
        y = jnp.concatenate([y, y_sc], axis=0)
    return _resln_call(x2, y, g2, lng.reshape(1, d), lnb.reshape(1, d), seq, tm, alpha)


def kernel(x, c, w_ada, b_ada, w_in, cmp_pe, cmp_w1, cmp_w2, w_pool, pool_scale, w_lift, w_o,
           ln_g, ln_b, peer_wq, peer_keys, peer_u, peer_v):
    bsz, seq, d = x.shape
    depth = w_ada.shape[0]
    assert d == D_MODEL and seq % 512 == 0 and SLC_TOPN <= seq // SLC_LEN <= LANES // N_KV
    alpha = (2 * depth) ** 0.25

    c_pad = jnp.zeros((SUBLANES, d), F32).at[:bsz].set(c)
    mods = _ada_call(c_pad, w_ada, b_ada)[:, :bsz]
    cst = _Consts(seq)
    nchain = 2 if bsz % 2 == 0 else 1
    bpc = bsz // nchain
    xs = [x[i * bpc:(i + 1) * bpc].reshape(bpc * seq, d) for i in range(nchain)]
    sc_tokens = SC_TOKEN_SHARE * bpc * seq // 16
    for l in range(depth):
        upack, vpack = _pack_table(peer_u[l]), _pack_table(peer_v[l])
        for i in range(nchain):
            sh1, sc1, g1, sh2, sc2, g2 = (mods[l][i * bpc:(i + 1) * bpc, j * d:(j + 1) * d].reshape(bpc, 1, d)
                                          for j in range(6))
            xi = _token_mixer_layer(xs[i], sc1, sh1, g1, w_in[l], cmp_pe[l], cmp_w1[l], cmp_w2[l], w_pool[l],
                                    pool_scale[l], w_lift[l], w_o[l], ln_g[l, 0], ln_b[l, 0], cst, bpc, seq, alpha)
            xs[i] = _peer_layer(xi, sc2, sh2, g2, peer_wq[l], peer_keys[l], peer_u[l], peer_v[l], upack, vpack,
                                ln_g[l, 1], ln_b[l, 1], seq, alpha, sc_tokens=sc_tokens)
    return jnp.concatenate(xs, axis=0).reshape(bsz, seq, d)
```

```python
import dataclasses
import functools

import jax
import jax.numpy as jnp
import numpy as np
from jax import lax
from jax.experimental import pallas as pl
from jax.experimental.pallas import tpu as pltpu
from jax.experimental.pallas import tpu_sc as plsc

F32 = jnp.float32
BF16 = jnp.bfloat16
I32 = jnp.int32
HI = lax.Precision.HIGHEST

D_MODEL = 1024
N_HEADS = 8
HEAD_DIM = 64
N_KV = 2
HPG = N_HEADS // N_KV
ROT_DIM = HEAD_DIM // 4
ROT_HALF = ROT_DIM // 2
ROPE_THETA = 500000.0
CMP_LEN = 32
CMP_STRIDE = 16
SLC_LEN = 64
SLC_TOPN = 16
WINDOW = 512
SCALE = HEAD_DIM ** -0.5
NEG = -1e30
FORCE_INIT = 1e6
FORCE_LOCAL = 2e6
POOL_GROUPS = 4
POOL_WINDOWS = (2, 4, 8, 16)
POOL_WIDTH = 512
POOL_GW = POOL_WIDTH // POOL_GROUPS
POOL_HALO = 16
ATT_WIDTH = N_HEADS * HEAD_DIM
KV_WIDTH = 3 * 2 * N_KV * HEAD_DIM
GATE_NSA = 3 * N_HEADS
MERGE_GATES = 2 * D_MODEL
PEER_HEADS = 8
N_KEYS = 128
PEER_TOPK = 16
PEER_DK = 128
PEER_SEL = PEER_HEADS * PEER_TOPK
LN_EPS = 1e-5

LANES = 128
SUBLANES = 8
VMEM_LIMIT = 56 * 1024 * 1024

NT_DIMS = (((1,), (1,)), ((), ()))


def _cparams(sem):
    return pltpu.CompilerParams(dimension_semantics=sem, vmem_limit_bytes=VMEM_LIMIT)


def _ada_kernel(c_ref, w_ref, b_ref, o_ref):
    c = c_ref[...]
    ca = c * jax.nn.sigmoid(c)
    o_ref[...] = jnp.dot(ca, w_ref[...], precision=HI, preferred_element_type=F32) + b_ref[...]


def _ada_call(c_pad, w_ada, b_ada):
    depth = w_ada.shape[0]
    nblk = w_ada.shape[2] // D_MODEL
    rows = c_pad.shape[0]
    return pl.pallas_call(
        _ada_kernel,
        grid=(depth, nblk),
        in_specs=[
            pl.BlockSpec((rows, D_MODEL), lambda l, j: (0, 0)),
            pl.BlockSpec((None, D_MODEL, D_MODEL), lambda l, j: (l, 0, j)),
            pl.BlockSpec((None, 1, D_MODEL), lambda l, j: (l, 0, j)),
        ],
        out_specs=pl.BlockSpec((None, rows, D_MODEL), lambda l, j: (l, 0, j)),
        out_shape=jax.ShapeDtypeStruct((depth, rows, nblk * D_MODEL), F32),
        compiler_params=_cparams(("arbitrary", "arbitrary")),
        name="ada_mod",
    )(c_pad, w_ada, b_ada.reshape(depth, 1, -1))


IN_COLS = ATT_WIDTH + KV_WIDTH + POOL_WIDTH + MERGE_GATES + LANES


def _rope_lanes(z, rc, rs1, rs2):
    return z * rc + pltpu.roll(z, ROT_HALF, 1) * rs1 + pltpu.roll(z, LANES - ROT_HALF, 1) * rs2


def _inproj_kernel(x_ref, sc_ref, sh_ref, w_ref, rc_ref, rs1_ref, rs2_ref,
                   q_ref, kv_ref, p_ref, mrg_ref, gn_ref):
    h = x_ref[...] * (1.0 + sc_ref[...]) + sh_ref[...]
    a = jnp.dot(h.astype(BF16), w_ref[...], preferred_element_type=F32)
    rc, rs1, rs2 = rc_ref[...], rs1_ref[...], rs2_ref[...]
    for j in range(ATT_WIDTH // LANES):
        q_ref[:, j * LANES:(j + 1) * LANES] = _rope_lanes(a[:, j * LANES:(j + 1) * LANES], rc, rs1, rs2)
    for br in range(3):
        c0 = ATT_WIDTH + br * 2 * LANES
        k = a[:, c0:c0 + LANES]
        if br > 0:
            k = _rope_lanes(k, rc, rs1, rs2)
        kv_ref[:, br * 2 * LANES:br * 2 * LANES + LANES] = k
        kv_ref[:, br * 2 * LANES + LANES:(br + 1) * 2 * LANES] = a[:, c0 + LANES:c0 + 2 * LANES]
    c1 = ATT_WIDTH + KV_WIDTH
    p_ref[...] = a[:, c1:c1 + POOL_WIDTH]
    mrg_ref[...] = a[:, c1 + POOL_WIDTH:c1 + POOL_WIDTH + MERGE_GATES]
    gn_ref[...] = a[:, c1 + POOL_WIDTH + MERGE_GATES:]


def _inproj_call(x2, sc, sh, w, rc, rs1, rs2, seq, tm):
    n = x2.shape[0]
    tpb = seq // tm
    tok = lambda i: (i, 0)
    bat = lambda i: (i // tpb, 0, 0)
    pos = lambda i: (i % tpb, 0)
    full = lambda i: (0, 0)
    return pl.pallas_call(
        _inproj_kernel,
        grid=(n // tm,),
        in_specs=[
            pl.BlockSpec((tm, D_MODEL), tok),
            pl.BlockSpec((None, 1, D_MODEL), bat),
            pl.BlockSpec((None, 1, D_MODEL), bat),
            pl.BlockSpec((D_MODEL, IN_COLS), full),
            pl.BlockSpec((tm, LANES), pos),
            pl.BlockSpec((tm, LANES), pos),
            pl.BlockSpec((tm, LANES), pos),
        ],
        out_specs=[
            pl.BlockSpec((tm, ATT_WIDTH), tok),
            pl.BlockSpec((tm, KV_WIDTH), tok),
            pl.BlockSpec((tm, POOL_WIDTH), tok),
            pl.BlockSpec((tm, MERGE_GATES), tok),
            pl.BlockSpec((tm, LANES), tok),
        ],
        out_shape=[
            jax.ShapeDtypeStruct((n, ATT_WIDTH), F32),
            jax.ShapeDtypeStruct((n, KV_WIDTH), F32),
            jax.ShapeDtypeStruct((n, POOL_WIDTH), F32),
            jax.ShapeDtypeStruct((n, MERGE_GATES), F32),
            jax.ShapeDtypeStruct((n, LANES), F32),
        ],
        compiler_params=_cparams(("arbitrary",)),
        name="in_proj",
    )(x2, sc, sh, w, rc, rs1, rs2)


def _compress_kernel(z_ref, pet_ref, peb_ref, w1t_ref, w1b_ref, w2_ref, rc_ref, rs1_ref, rs2_ref, o_ref):
    z = z_ref[...]
    rows = z.shape[0]
    top = jnp.dot(z + pet_ref[...], w1t_ref[...], precision=HI, preferred_element_type=F32)
    bot = jnp.dot(z + peb_ref[...], w1b_ref[...], precision=HI, preferred_element_type=F32)
    pre = top + pltpu.roll(bot, rows - 1, 0)
    y = jnp.dot(jax.nn.gelu(pre), w2_ref[...], precision=HI, preferred_element_type=F32)
    o_ref[...] = _rope_lanes(y, rc_ref[...], rs1_ref[...], rs2_ref[...])


def _compress_call(z, pet, peb, w1t, w1b, w2, rc, rs1, rs2):
    b, _, rows, width = z.shape
    kvsel = lambda i, j: (j, 0, 0)
    return pl.pallas_call(
        _compress_kernel,
        grid=(b, 2),
        in_specs=[
            pl.BlockSpec((None, None, rows, width), lambda i, j: (i, j, 0, 0)),
            pl.BlockSpec((None, 1, width), kvsel),
            pl.BlockSpec((None, 1, width), kvsel),
            pl.BlockSpec((None, width, LANES), kvsel),
            pl.BlockSpec((None, width, LANES), kvsel),
            pl.BlockSpec((None, LANES, LANES), kvsel),
            pl.BlockSpec((None, rows, LANES), kvsel),
            pl.BlockSpec((None, rows, LANES), kvsel),
            pl.BlockSpec((None, rows, LANES), kvsel),
        ],
        out_specs=pl.BlockSpec((None, None, rows, LANES), lambda i, j: (i, j, 0, 0)),
        out_shape=jax.ShapeDtypeStruct((b, 2, rows, LANES), F32),
        compiler_params=_cparams(("arbitrary", "arbitrary")),
        name="compress",
    )(z, pet, peb, w1t, w1b, w2, rc, rs1, rs2)


def _cmpsel_kernel(q_ref, kc_ref, vc_ref, c2s_ref, o_ref, sel_ref, *, tq):
    t0 = pl.program_id(1) * tq
    kc = kc_ref[...]
    vc = vc_ref[...]
    rows = kc.shape[0]
    trow = t0 + lax.broadcasted_iota(I32, (tq, rows), 0)
    cend = lax.broadcasted_iota(I32, (tq, rows), 1) * CMP_STRIDE + (CMP_LEN - 1)
    vis = cend <= trow
    anyv = (trow[:, :1] >= CMP_LEN - 1).astype(F32)
    imp = jnp.zeros((tq, LANES), F32)
    for g in range(N_KV):
        kg = kc[:, g * HEAD_DIM:(g + 1) * HEAD_DIM].astype(BF16)
        vg = vc[:, g * HEAD_DIM:(g + 1) * HEAD_DIM].astype(BF16)
        psum = jnp.zeros((tq, rows), F32)
        for h in range(HPG):
            hd = g * HPG + h
            qh = q_ref[:, hd * HEAD_DIM:(hd + 1) * HEAD_DIM].astype(BF16)
            s = lax.dot_general(qh, kg, NT_DIMS, preferred_element_type=F32) * SCALE
            s = jnp.where(vis, s, NEG)
            e = jnp.exp(s - jnp.max(s, axis=-1, keepdims=True))
            p = e / jnp.sum(e, axis=-1, keepdims=True) * anyv
            o_ref[:, hd * HEAD_DIM:(hd + 1) * HEAD_DIM] = jnp.dot(
                p.astype(BF16), vg, preferred_element_type=F32)
            psum = psum + p
        imp = imp + jnp.dot(psum.astype(BF16), c2s_ref[g], preferred_element_type=F32)
    lane = lax.broadcasted_iota(I32, (tq, LANES), 1)
    blk = lane & (SLC_LEN - 1)
    cur = lax.shift_right_logical(t0 + lax.broadcasted_iota(I32, (tq, LANES), 0), 6)
    score = jnp.where(blk <= cur, imp, NEG)
    score = jnp.where(blk == 0, FORCE_INIT, score)
    score = jnp.where(blk == cur, FORCE_LOCAL, score)
    sc_t = score.T
    nblk = LANES // N_KV
    jrow = lax.broadcasted_iota(I32, (nblk, tq), 0)
    sel_parts = []
    for g in range(N_KV):
        sc = sc_t[g * nblk:(g + 1) * nblk]
        cnt = jnp.zeros((nblk, tq), I32)
        for k in range(nblk):
            rk = sc[k:k + 1, :]
            ge = (rk >= sc).astype(I32)
            gt = (rk > sc).astype(I32)
            cnt = cnt + jnp.where(jrow > k, ge, gt)
        sel_parts.append((cnt < SLC_TOPN).astype(F32))
    sel_ref[...] = jnp.concatenate(sel_parts, axis=0).T


def _cmpsel_call(q, kvc, c2s, seq, tq):
    n = q.shape[0]
    b = n // seq
    nq = seq // tq
    rows = kvc.shape[2]
    tok = lambda i, j: (i * nq + j, 0)
    return pl.pallas_call(
        functools.partial(_cmpsel_kernel, tq=tq),
        grid=(b, nq),
        in_specs=[
            pl.BlockSpec((tq, ATT_WIDTH), tok),
            pl.BlockSpec((None, None, rows, LANES), lambda i, j: (i, 0, 0, 0)),
            pl.BlockSpec((None, None, rows, LANES), lambda i, j: (i, 1, 0, 0)),
            pl.BlockSpec((N_KV, rows, LANES), lambda i, j: (0, 0, 0)),
        ],
        out_specs=[pl.BlockSpec((tq, ATT_WIDTH), tok), pl.BlockSpec((tq, LANES), tok)],
        out_shape=[jax.ShapeDtypeStruct((n, ATT_WIDTH), F32), jax.ShapeDtypeStruct((n, LANES), F32)],
        compiler_params=_cparams(("arbitrary", "arbitrary")),
        name="cmp_select",
    )(q, kvc, kvc, c2s)


def _slc_kernel(q_ref, k_ref, v_ref, sel_ref, o_ref, qs_sc, m_sc, l_sc, acc_sc, *, tq, tk):
    qi = pl.program_id(1)
    kt = pl.program_id(2)
    nk = pl.num_programs(2)

    @pl.when(kt == 0)
    def _init():
        for hd in range(N_HEADS):
            g, h = divmod(hd, HPG)
            qs_sc[g, h * tq:(h + 1) * tq, :] = (q_ref[:, hd * HEAD_DIM:(hd + 1) * HEAD_DIM] * SCALE).astype(BF16)
        m_sc[...] = jnp.full(m_sc.shape, NEG, F32)
        l_sc[...] = jnp.zeros(l_sc.shape, F32)
        acc_sc[...] = jnp.zeros(acc_sc.shape, F32)

    @pl.when(kt * tk <= qi * tq + (tq - 1))
    def _step():
        t = qi * tq + lax.broadcasted_iota(I32, (tq, tk), 0)
        kp = kt * tk + lax.broadcasted_iota(I32, (tq, tk), 1)
        causal = kp <= t
        nblk = LANES // N_KV
        jb = lax.broadcasted_iota(I32, (nblk, tk), 0)
        kb = lax.shift_right_logical(kt * tk + lax.broadcasted_iota(I32, (nblk, tk), 1), 6)
        expand = (jb == kb).astype(BF16)
        for g in range(N_KV):
            selg = sel_ref[:, g * nblk:(g + 1) * nblk].astype(BF16)
            member = jnp.dot(selg, expand, preferred_element_type=F32)
            bias = jnp.where(jnp.logical_and(causal, member > 0.5), 0.0, NEG)
            bias = jnp.concatenate([bias] * HPG, axis=0)
            kg = k_ref[:, g * HEAD_DIM:(g + 1) * HEAD_DIM].astype(BF16)
            vg = v_ref[:, g * HEAD_DIM:(g + 1) * HEAD_DIM].astype(BF16)
            s = lax.dot_general(qs_sc[g], kg, NT_DIMS, preferred_element_type=F32) + bias
            chunks = [s[:, c * LANES:(c + 1) * LANES] for c in range(tk // LANES)]
            mc = chunks[0]
            for x in chunks[1:]:
                mc = jnp.maximum(mc, x)
            m_old = m_sc[g]
            m_new = jnp.maximum(m_old, jnp.max(mc, axis=-1, keepdims=True))
            alpha = jnp.exp(m_old - m_new)
            ps = [jnp.exp(x - m_new) for x in chunks]
            lsum = ps[0]
            for x in ps[1:]:
                lsum = lsum + x
            l_sc[g] = alpha * l_sc[g] + lsum
            p = jnp.concatenate(ps, axis=1).astype(BF16)
            acc_sc[g] = alpha[:, :HEAD_DIM] * acc_sc[g] + jnp.dot(p, vg, preferred_element_type=F32)
            m_sc[g] = m_new

    @pl.when(kt == nk - 1)
    def _fin():
        for hd in range(N_HEADS):
            g, h = divmod(hd, HPG)
            l = jnp.sum(l_sc[g, h * tq:(h + 1) * tq, :], axis=-1, keepdims=True)
            o_ref[:, hd * HEAD_DIM:(hd + 1) * HEAD_DIM] = acc_sc[g, h * tq:(h + 1) * tq, :] / l


def _slc_call(q, kv, sel, seq, tq, tk):
    n = q.shape[0]
    b = n // seq
    nq = seq // tq
    nk = seq // tk
    tok = lambda i, j, k: (i * nq + j, 0)

    def key_map(col):
        def f(i, j, k):
            last = (j * tq + tq - 1) // tk
            return (i * nk + jnp.minimum(k, last), col)
        return f

    return pl.pallas_call(
        functools.partial(_slc_kernel, tq=tq, tk=tk),
        grid=(b, nq, nk),
        in_specs=[
            pl.BlockSpec((tq, ATT_WIDTH), tok),
            pl.BlockSpec((tk, LANES), key_map(2)),
            pl.BlockSpec((tk, LANES), key_map(3)),
            pl.BlockSpec((tq, LANES), tok),
        ],
        out_specs=pl.BlockSpec((tq, ATT_WIDTH), tok),
        out_shape=jax.ShapeDtypeStruct((n, ATT_WIDTH), F32),
        scratch_shapes=[
            pltpu.VMEM((N_KV, HPG * tq, HEAD_DIM), BF16),
            pltpu.VMEM((N_KV, HPG * tq, LANES), F32),
            pltpu.VMEM((N_KV, HPG * tq, LANES), F32),
            pltpu.VMEM((N_KV, HPG * tq, HEAD_DIM), F32),
        ],
        compiler_params=_cparams(("arbitrary", "arbitrary", "arbitrary")),
        name="slc_attn",
    )(q, kv, kv, sel)


def _win_kernel(q_ref, *refs, tq, nkb):
    k_refs = refs[:nkb]
    v_refs = refs[nkb:2 * nkb]
    o_ref = refs[2 * nkb]
    qi = pl.program_id(1)
    t = qi * tq + lax.broadcasted_iota(I32, (tq, tq), 0)
    col = lax.broadcasted_iota(I32, (tq, tq), 1)
    biases = []
    for j in range(nkb):
        kp = (qi - (nkb - 1) + j) * tq + col
        diff = t - kp
        ok = jnp.logical_and(jnp.logical_and(diff >= 0, diff < WINDOW), kp >= 0)
        biases.append(jnp.concatenate([jnp.where(ok, 0.0, NEG)] * HPG, axis=0))
    for g in range(N_KV):
        qs = jnp.concatenate(
            [(q_ref[:, (g * HPG + h) * HEAD_DIM:(g * HPG + h + 1) * HEAD_DIM] * SCALE).astype(BF16)
             for h in range(HPG)], axis=0)
        chunks = []
        for j in range(nkb):
            kg = k_refs[j][:, g * HEAD_DIM:(g + 1) * HEAD_DIM].astype(BF16)
            s = lax.dot_general(qs, kg, NT_DIMS, preferred_element_type=F32) + biases[j]
            chunks += [s[:, c * LANES:(c + 1) * LANES] for c in range(tq // LANES)]
        mc = chunks[0]
        for x in chunks[1:]:
            mc = jnp.maximum(mc, x)
        m = jnp.broadcast_to(jnp.max(mc, axis=-1, keepdims=True), mc.shape)
        ps = [jnp.exp(x - m) for x in chunks]
        lsum = ps[0]
        for x in ps[1:]:
            lsum = lsum + x
        l = jnp.sum(lsum, axis=-1, keepdims=True)
        per = tq // LANES
        o = None
        for j in range(nkb):
            vg = v_refs[j][:, g * HEAD_DIM:(g + 1) * HEAD_DIM].astype(BF16)
            pj = jnp.concatenate(ps[j * per:(j + 1) * per], axis=1).astype(BF16)
            oj = jnp.dot(pj, vg, preferred_element_type=F32)
            o = oj if o is None else o + oj
        o = o / l
        for h in range(HPG):
            hd = g * HPG + h
            o_ref[:, hd * HEAD_DIM:(hd + 1) * HEAD_DIM] = o[h * tq:(h + 1) * tq]


def _win_call(q, kv, seq, tq):
    n = q.shape[0]
    b = n // seq
    nq = seq // tq
    nkb = WINDOW // tq + 1
    tok = lambda i, j: (i * nq + j, 0)

    def key_map(col, back):
        return lambda i, j: (i * nq + jnp.maximum(j - back, 0), col)

    k_specs = [pl.BlockSpec((tq, LANES), key_map(4, nkb - 1 - jj)) for jj in range(nkb)]
    v_specs = [pl.BlockSpec((tq, LANES), key_map(5, nkb - 1 - jj)) for jj in range(nkb)]
    return pl.pallas_call(
        functools.partial(_win_kernel, tq=tq, nkb=nkb),
        grid=(b, nq),
        in_specs=[pl.BlockSpec((tq, ATT_WIDTH), tok)] + k_specs + v_specs,
        out_specs=pl.BlockSpec((tq, ATT_WIDTH), tok),
        out_shape=jax.ShapeDtypeStruct((n, ATT_WIDTH), F32),
        compiler_params=_cparams(("arbitrary", "arbitrary")),
        name="win_attn",
    )(q, *([kv] * (2 * nkb)))


def _pool_kernel(p_ref, prev_ref, w_ref, sc_ref, o_ref, *, ts):
    i = pl.program_id(1)
    x = p_ref[...]
    prev = prev_ref[...] * (i > 0).astype(F32)
    xe = jnp.concatenate([prev, x], axis=0)
    t1 = (i * ts + 1 + lax.broadcasted_iota(I32, (ts, POOL_GW), 0)).astype(F32)
    for g, w in enumerate(POOL_WINDOWS):
        a = xe[:, g * POOL_GW:(g + 1) * POOL_GW]
        off = POOL_HALO
        span = 1
        while span < w:
            a = a[span:] + a[:-span]
            off -= span
            span *= 2
        sums = a[off:off + ts]
        cnt = jnp.minimum(t1, float(w))
        pooled = sums / cnt - x[:, g * POOL_GW:(g + 1) * POOL_GW]
        y = jnp.dot(pooled.astype(BF16), w_ref[g], preferred_element_type=F32)
        o_ref[:, g * POOL_GW:(g + 1) * POOL_GW] = y * sc_ref[:, g * POOL_GW:(g + 1) * POOL_GW]


def _pool_call(p_in, w_pool, pool_scale, seq, ts):
    n = p_in.shape[0]
    b = n // seq
    nt = seq // ts
    hpt = ts // POOL_HALO
    tok = lambda i, j: (i * nt + j, 0)
    return pl.pallas_call(
        functools.partial(_pool_kernel, ts=ts),
        grid=(b, nt),
        in_specs=[
            pl.BlockSpec((ts, POOL_WIDTH), tok),
            pl.BlockSpec((POOL_HALO, POOL_WIDTH), lambda i, j: (i * nt * hpt + jnp.maximum(j * hpt - 1, 0), 0)),
            pl.BlockSpec((POOL_GROUPS, POOL_GW, POOL_GW), lambda i, j: (0, 0, 0)),
            pl.BlockSpec((1, POOL_WIDTH), lambda i, j: (0, 0)),
        ],
        out_specs=pl.BlockSpec((ts, POOL_WIDTH), tok),
        out_shape=jax.ShapeDtypeStruct((n, POOL_WIDTH), F32),
        compiler_params=_cparams(("arbitrary", "arbitrary")),
        name="pool_mix",
    )(p_in, p_in, w_pool, pool_scale)


def _layer_norm(z, g, b):
    mu = jnp.mean(z, axis=-1, keepdims=True)
    zc = z - mu
    var = jnp.mean(zc * zc, axis=-1, keepdims=True)
    return zc * lax.rsqrt(var + LN_EPS) * g + b


def _merge_kernel(oc_ref, os_ref, ow_ref, gn_ref, op_ref, gm_ref, x_ref, g1_ref, lng_ref, lnb_ref,
                  wl_ref, wo_ref, eb_ref, o_ref, *, alpha):
    gate = jax.nn.sigmoid(gn_ref[...])
    branches = (oc_ref, os_ref, ow_ref)
    oatt = None
    for br in range(3):
        gx = jnp.dot(gate, eb_ref[br], precision=HI, preferred_element_type=F32)
        term = gx * branches[br][...]
        oatt = term if oatt is None else oatt + term
    la = jnp.dot(oatt.astype(BF16), wl_ref[0], preferred_element_type=F32)
    lb = jnp.dot(op_ref[...].astype(BF16), wl_ref[1], preferred_element_type=F32)
    gm = jax.nn.sigmoid(gm_ref[...])
    merged = gm[:, :D_MODEL] * la + gm[:, D_MODEL:] * lb
    y = jnp.dot(merged.astype(BF16), wo_ref[...], preferred_element_type=F32)
    z = alpha * x_ref[...] + g1_ref[...] * y
    o_ref[...] = _layer_norm(z, lng_ref[...], lnb_ref[...])


def _merge_call(oc, osl, ow, gn, op, gm, x2, g1, lng, lnb, wl, wo, eb, seq, tm, alpha):
    n = x2.shape[0]
    tpb = seq // tm
    tok = lambda i: (i, 0)
    bat = lambda i: (i // tpb, 0, 0)
    return pl.pallas_call(
        functools.partial(_merge_kernel, alpha=alpha),
        grid=(n // tm,),
        in_specs=[
            pl.BlockSpec((tm, ATT_WIDTH), tok), pl.BlockSpec((tm, ATT_WIDTH), tok),
            pl.BlockSpec((tm, ATT_WIDTH), tok), pl.BlockSpec((tm, LANES), tok),
            pl.BlockSpec((tm, POOL_WIDTH), tok), pl.BlockSpec((tm, MERGE_GATES), tok),
            pl.BlockSpec((tm, D_MODEL), tok),
            pl.BlockSpec((None, 1, D_MODEL), bat),
            pl.BlockSpec((1, D_MODEL), lambda i: (0, 0)), pl.BlockSpec((1, D_MODEL), lambda i: (0, 0)),
            pl.BlockSpec((2, ATT_WIDTH, D_MODEL), lambda i: (0, 0, 0)),
            pl.BlockSpec((D_MODEL, D_MODEL), lambda i: (0, 0)),
            pl.BlockSpec((3, LANES, ATT_WIDTH), lambda i: (0, 0, 0)),
        ],
        out_specs=pl.BlockSpec((tm, D_MODEL), tok),
        out_shape=jax.ShapeDtypeStruct((n, D_MODEL), F32),
        compiler_params=_cparams(("arbitrary",)),
        name="merge_out",
    )(oc, osl, ow, gn, op, gm, x2, g1, lng, lnb, wl, wo, eb)


def _extract_top(cur, ids, n):
    rows = cur.shape[0]
    rio = lax.broadcasted_iota(I32, cur.shape, 0)
    vals, outs = [], []
    for _ in range(n):
        m = jnp.max(cur, axis=0, keepdims=True)
        pos = jnp.min(jnp.where(cur == m, rio, rows), axis=0, keepdims=True)
        hit = rio == pos
        vals.append(m)
        outs.append(pos if ids is None else jnp.max(jnp.where(hit, ids, -1), axis=0, keepdims=True))
        cur = jnp.where(hit, -jnp.inf, cur)
    return jnp.concatenate(vals, axis=0), jnp.concatenate(outs, axis=0)


def _route_kernel(x_ref, sc_ref, sh_ref, wq_ref, keys_ref, h_ref, hv_ref, e_ref, g_ref,
                  st_sc, ts_sc, ti_sc, eo_sc, go_sc):
    h = x_ref[...] * (1.0 + sc_ref[...]) + sh_ref[...]
    h_ref[...] = h
    for r in range(SUBLANES):
        hv_ref[:, r, :] = h[:, r * LANES:(r + 1) * LANES]
    qp = jnp.dot(h.astype(BF16), wq_ref[...], preferred_element_type=F32).astype(BF16)
    half = PEER_DK // 2
    for hp in range(2 * PEER_HEADS):
        st_sc[hp] = lax.dot_general(keys_ref[hp], qp[:, hp * half:(hp + 1) * half], NT_DIMS,
                                    preferred_element_type=F32)

    def half_body(hp, carry):
        vals, ids = _extract_top(st_sc[hp], None, PEER_TOPK)
        ts_sc[hp] = vals
        ti_sc[hp] = ids
        return carry

    lax.fori_loop(0, 2 * PEER_HEADS, half_body, 0)

    def head_body(hh, carry):
        s1, s2 = ts_sc[2 * hh], ts_sc[2 * hh + 1]
        i1, i2 = ti_sc[2 * hh], ti_sc[2 * hh + 1]
        brow = lax.broadcasted_iota(I32, (SUBLANES, s1.shape[1]), 0)
        cands = [s1[0:1, :] + s2]
        cidxs = [i1[0:1, :] * N_KEYS + i2]
        for a in range(1, SUBLANES):
            ok = brow < PEER_TOPK // (a + 1)
            cands.append(jnp.where(ok, s1[a:a + 1, :] + s2[:SUBLANES], -jnp.inf))
            cidxs.append(i1[a:a + 1, :] * N_KEYS + i2[:SUBLANES])
        cands.append(s1[SUBLANES:] + s2[0:1, :])
        cidxs.append(i1[SUBLANES:] * N_KEYS + i2[0:1, :])
        sv, ei = _extract_top(jnp.concatenate(cands, axis=0), jnp.concatenate(cidxs, axis=0), PEER_TOPK)
        ex = jnp.exp(sv - sv[0:1, :])
        go_sc[hh] = ex / jnp.sum(ex, axis=0, keepdims=True)
        eo_sc[hh] = ei.astype(F32)
        return carry

    lax.fori_loop(0, PEER_HEADS, head_body, 0)
    e_all = jnp.concatenate([eo_sc[hh] for hh in range(PEER_HEADS)], axis=0)
    g_all = jnp.concatenate([go_sc[hh] for hh in range(PEER_HEADS)], axis=0)
    e_ref[...] = e_all.T.astype(I32) * (D_MODEL // 2 // LANES)
    g_ref[...] = g_all.T


def _route_call(x2, sc, sh, wq, keys, seq, tt):
    n = x2.shape[0]
    tpb = seq // tt
    tok = lambda i: (i, 0)
    bat = lambda i: (i // tpb, 0, 0)
    nhp = 2 * PEER_HEADS
    return pl.pallas_call(
        _route_kernel,
        grid=(n // tt,),
        in_specs=[
            pl.BlockSpec((tt, D_MODEL), tok),
            pl.BlockSpec((None, 1, D_MODEL), bat),
            pl.BlockSpec((None, 1, D_MODEL), bat),
            pl.BlockSpec((D_MODEL, PEER_HEADS * PEER_DK), lambda i: (0, 0)),
            pl.BlockSpec((nhp, N_KEYS, PEER_DK // 2), lambda i: (0, 0, 0)),
        ],
        out_specs=[pl.BlockSpec((tt, D_MODEL), tok), pl.BlockSpec((tt, SUBLANES, LANES), lambda i: (i, 0, 0)),
                   pl.BlockSpec((tt, PEER_SEL), tok), pl.BlockSpec((tt, PEER_SEL), tok)],
        out_shape=[jax.ShapeDtypeStruct((n, D_MODEL), F32), jax.ShapeDtypeStruct((n, SUBLANES, LANES), F32),
                   jax.ShapeDtypeStruct((n, PEER_SEL), I32), jax.ShapeDtypeStruct((n, PEER_SEL), F32)],
        scratch_shapes=[
            pltpu.VMEM((nhp, N_KEYS, tt), F32),
            pltpu.VMEM((nhp, PEER_TOPK, tt), F32),
            pltpu.VMEM((nhp, PEER_TOPK, tt), I32),
            pltpu.VMEM((PEER_HEADS, PEER_TOPK, tt), F32),
            pltpu.VMEM((PEER_HEADS, PEER_TOPK, tt), F32),
        ],
        compiler_params=_cparams(("arbitrary",)),
        name="peer_route",
    )(x2, sc, sh, wq, keys)


HALF_ROWS = SUBLANES // 2
HI_MASK = -65536
PAIR_TILES = PEER_SEL // 2
PAIR_ROWS = PAIR_TILES * SUBLANES


def _load_two_experts(tab_ref, ra, rb):
    wa = tab_ref[pl.ds(pl.multiple_of(ra, HALF_ROWS), HALF_ROWS), :]
    wb = tab_ref[pl.ds(pl.multiple_of(rb, HALF_ROWS), HALF_ROWS), :]
    w2 = jnp.concatenate([wa, wb], axis=0)
    return lax.bitcast_convert_type(w2 << 16, F32), lax.bitcast_convert_type(w2 & HI_MASK, F32)


def _fold_pairs(vs):
    row = lax.broadcasted_iota(I32, (SUBLANES, LANES), 0)
    shift = HALF_ROWS // 2
    while len(vs) > 1:
        low = (row & shift) == 0
        vs = [jnp.where(low, a + pltpu.roll(a, SUBLANES - shift, 0), b + pltpu.roll(b, shift, 0))
              for a, b in zip(vs[0::2], vs[1::2])]
        shift //= 2
    return vs[0]


def _fold_order():
    idx = [[2 * i if r < HALF_ROWS else 2 * i + 1 for r in range(SUBLANES)] for i in range(HALF_ROWS)]
    shift = HALF_ROWS // 2
    while len(idx) > 1:
        idx = [[a[r] if (r & shift) == 0 else b[r] for r in range(SUBLANES)]
               for a, b in zip(idx[0::2], idx[1::2])]
        shift //= 2
    return idx[0]


def _peer_u_kernel(e_sm, tab_ref, hv_ref, gate_ref, coef_ref, *, tt):
    row = lax.broadcasted_iota(I32, (SUBLANES, LANES), 0)
    low = row < HALF_ROWS
    eye = (lax.broadcasted_iota(I32, (PEER_SEL, LANES), 0) ==
           lax.broadcasted_iota(I32, (PEER_SEL, LANES), 1))
    order = _fold_order()

    def finish(t, part):
        col = jnp.sum(part, axis=-1, keepdims=True)
        a_row = jnp.sum(jnp.where(eye, col, 0.0), axis=0, keepdims=True)
        coef_ref[t] = gate_ref[pl.ds(t, 1), :] * jax.nn.gelu(a_row)

    def token(t, part_prev):
        finish(jnp.maximum(t - 1, 0), part_prev)
        hv = hv_ref[t]
        hsw = pltpu.roll(hv, HALF_ROWS, 0)
        h_lo = jnp.where(low, hv, hsw)
        h_hi = jnp.where(low, hsw, hv)
        folded = []
        for j in range(PEER_SEL // SUBLANES):
            prods = []
            for i in range(HALF_ROWS):
                ka = j * SUBLANES + order.index(2 * i)
                kb = j * SUBLANES + order.index(2 * i + 1)
                lo, hi = _load_two_experts(tab_ref, e_sm[t, ka], e_sm[t, kb])
                prods.append(lo * h_lo + hi * h_hi)
            folded.append(_fold_pairs(prods))
        return jnp.concatenate(folded, axis=0)

    last = lax.fori_loop(0, tt, token, jnp.zeros((PEER_SEL, LANES), F32))
    finish(tt - 1, last)


def _peer_v_kernel(e_sm, coef_ref, tab_ref, y_ref, cv_sc, *, tt, nacc):
    row = lax.broadcasted_iota(I32, (SUBLANES, LANES), 0)
    low = row < HALF_ROWS
    rr = lax.broadcasted_iota(I32, (PAIR_ROWS, LANES), 0)
    kk = lax.broadcasted_iota(I32, (PAIR_ROWS, LANES), 1)
    onehot = (kk == 2 * (rr >> 3) + ((rr >> 2) & 1)).astype(F32)
    ones = jnp.ones((LANES, LANES), BF16)

    def expand(t, slot):
        lhs = (onehot * coef_ref[t]).astype(BF16)
        cv_sc[slot] = jnp.dot(lhs, ones, preferred_element_type=F32)

    def process(t, slot):
        acc_lo = [jnp.zeros((SUBLANES, LANES), F32) for _ in range(nacc)]
        acc_hi = [jnp.zeros((SUBLANES, LANES), F32) for _ in range(nacc)]
        for j in range(PAIR_TILES):
            grp = e_sm.at[t, pl.ds(2 * j // SUBLANES * SUBLANES, SUBLANES)]
            lo, hi = _load_two_experts(tab_ref, grp[2 * j % SUBLANES], grp[(2 * j + 1) % SUBLANES])
            cv = cv_sc[slot, j * SUBLANES:(j + 1) * SUBLANES, :]
            acc_lo[j % nacc] = acc_lo[j % nacc] + cv * lo
            acc_hi[j % nacc] = acc_hi[j % nacc] + cv * hi
        a_lo, a_hi = acc_lo[0], acc_hi[0]
        for i in range(1, nacc):
            a_lo = a_lo + acc_lo[i]
            a_hi = a_hi + acc_hi[i]
        a_lo = a_lo + pltpu.roll(a_lo, HALF_ROWS, 0)
        a_hi = a_hi + pltpu.roll(a_hi, HALF_ROWS, 0)
        y_ref[t] = jnp.where(low, a_lo, a_hi)

    expand(0, 0)

    def two_tokens(i, carry):
        t = 2 * i
        expand(t + 1, 1)
        process(t, 0)
        expand(jnp.minimum(t + 2, tt - 1), 0)
        process(t + 1, 1)
        return carry

    lax.fori_loop(0, tt // 2, two_tokens, 0)


def _resident_table_spec(tab):
    return pl.BlockSpec(tab.shape, lambda i: (0, 0), pipeline_mode=pl.Buffered(1))


def _peer_u_call(erow, tab, hv, gate, tt, n):
    return pl.pallas_call(
        functools.partial(_peer_u_kernel, tt=tt),
        grid=(n // tt,),
        in_specs=[
            pl.BlockSpec((tt, PEER_SEL), lambda i: (i, 0), memory_space=pltpu.SMEM),
            _resident_table_spec(tab),
            pl.BlockSpec((tt, SUBLANES, LANES), lambda i: (i, 0, 0)),
            pl.BlockSpec((tt, PEER_SEL), lambda i: (i, 0)),
        ],
        out_specs=pl.BlockSpec((tt, 1, PEER_SEL), lambda i: (i, 0, 0)),
        out_shape=jax.ShapeDtypeStruct((n, 1, PEER_SEL), F32),
        compiler_params=_cparams(("arbitrary",)),
        name="peer_u",
    )(erow, tab, hv, gate)


def _peer_v_call(erow, coef3, tab, tt, n):
    assert tt % 2 == 0
    return pl.pallas_call(
        functools.partial(_peer_v_kernel, tt=tt, nacc=4),
        grid=(n // tt,),
        in_specs=[
            pl.BlockSpec((tt, PEER_SEL), lambda i: (i, 0), memory_space=pltpu.SMEM),
            pl.BlockSpec((tt, 1, PEER_SEL), lambda i: (i, 0, 0)),
            _resident_table_spec(tab),
        ],
        out_specs=pl.BlockSpec((tt, SUBLANES, LANES), lambda i: (i, 0, 0)),
        out_shape=jax.ShapeDtypeStruct((n, SUBLANES, LANES), F32),
        scratch_shapes=[pltpu.VMEM((2, PAIR_ROWS, LANES), F32)],
        compiler_params=_cparams(("arbitrary",)),
        name="peer_v",
    )(erow, coef3, tab)


def _resln_kernel(x_ref, ytc_ref, ysc_ref, g_ref, lng_ref, lnb_ref, o_ref, *, alpha, tc_tiles):
    y_tc = jnp.concatenate([ytc_ref[:, r, :] for r in range(SUBLANES)], axis=1)
    y = jnp.where(pl.program_id(0) < tc_tiles, y_tc, ysc_ref[...])
    z = alpha * x_ref[...] + g_ref[...] * y
    o_ref[...] = _layer_norm(z, lng_ref[...], lnb_ref[...])


def _resln_call(x2, y_tc, y_sc, g2, lng, lnb, seq, tm, alpha):
    n = x2.shape[0]
    n_tc = y_tc.shape[0]
    assert n_tc % tm == 0 and n_tc > 0
    tc_tiles = n_tc // tm
    if y_sc is None:
        y_sc = jnp.zeros((tm, D_MODEL), F32)
    tpb = seq // tm
    tok = lambda i: (i, 0)
    return pl.pallas_call(
        functools.partial(_resln_kernel, alpha=alpha, tc_tiles=tc_tiles),
        grid=(n // tm,),
        in_specs=[
            pl.BlockSpec((tm, D_MODEL), tok),
            pl.BlockSpec((tm, SUBLANES, LANES), lambda i: (jnp.minimum(i, tc_tiles - 1), 0, 0)),
            pl.BlockSpec((tm, D_MODEL), lambda i: (jnp.maximum(i - tc_tiles, 0), 0)),
            pl.BlockSpec((None, 1, D_MODEL), lambda i: (i // tpb, 0, 0)),
            pl.BlockSpec((1, D_MODEL), lambda i: (0, 0)), pl.BlockSpec((1, D_MODEL), lambda i: (0, 0)),
        ],
        out_specs=pl.BlockSpec((tm, D_MODEL), tok),
        out_shape=jax.ShapeDtypeStruct((n, D_MODEL), F32),
        compiler_params=_cparams(("arbitrary",)),
        name="res_ln",
    )(x2, y_tc, y_sc, g2, lng, lnb)


def _rope_lane_tables(pos):
    inv = ROPE_THETA ** (-jnp.arange(0, ROT_DIM, 2, dtype=F32) / ROT_DIM)
    ang = pos.astype(F32)[:, None] * inv[None, :]
    cos, sin = jnp.cos(ang), jnp.sin(ang)
    lane = np.arange(LANES) % HEAD_DIM
    fidx = lane % ROT_HALF
    first = jnp.asarray(lane < ROT_HALF)
    second = jnp.asarray((lane >= ROT_HALF) & (lane < ROT_DIM))
    rot = jnp.asarray(lane < ROT_DIM)
    cl, sl = cos[:, fidx], sin[:, fidx]
    rc = jnp.where(rot, cl, 1.0)
    rs1 = jnp.where(second, sl, 0.0)
    rs2 = jnp.where(first, -sl, 0.0)
    return rc, rs1, rs2


def _pack_kernel(t_ref, o_ref):
    x = t_ref[...]
    half = x.shape[1] // 2
    lo = lax.bitcast_convert_type(x[:, :half].astype(BF16).astype(F32), I32)
    hi = lax.bitcast_convert_type(x[:, half:].astype(BF16).astype(F32), I32)
    w = lax.shift_right_logical(lo, jnp.full_like(lo, 16)) | hi
    rows = x.shape[0]
    per = half // LANES
    for r in range(per):
        o_ref[pl.ds(r, rows, stride=per), :] = w[:, r * LANES:(r + 1) * LANES]


def _pack_table(tab, te=512):
    e, d = tab.shape
    per = d // 2 // LANES
    return pl.pallas_call(
        _pack_kernel, grid=(e // te,),
        in_specs=[pl.BlockSpec((te, d), lambda i: (i, 0))],
        out_specs=pl.BlockSpec((te * per, LANES), lambda i: (i, 0)),
        out_shape=jax.ShapeDtypeStruct((e * per, LANES), I32),
        compiler_params=_cparams(("arbitrary",)), name="pack_table",
    )(tab)


def _cmp_to_slc_wide(rows, n_slc):
    st = np.arange(rows) * CMP_STRIDE
    js = np.arange(n_slc) * SLC_LEN
    ov = np.minimum(st[:, None] + CMP_LEN, js[None, :] + SLC_LEN) - np.maximum(st[:, None], js[None, :])
    c2s = np.maximum(ov, 0).astype(np.float32) / CMP_STRIDE
    wide = np.zeros((N_KV, rows, LANES), np.float32)
    nblk = LANES // N_KV
    for g in range(N_KV):
        wide[g, :, g * nblk:g * nblk + n_slc] = c2s
    return jnp.asarray(wide)


def _gate_expanders():
    eb = np.zeros((3, LANES, ATT_WIDTH), np.float32)
    for hd in range(N_HEADS):
        for br in range(3):
            eb[br, hd * 3 + br, hd * HEAD_DIM:(hd + 1) * HEAD_DIM] = 1.0
    return jnp.asarray(eb)


class _Consts:
    def __init__(self, seq):
        self.rows = seq // CMP_STRIDE
        self.rope = _rope_lane_tables(jnp.arange(seq))
        cpos = jnp.arange(self.rows) * CMP_STRIDE + CMP_LEN - 1
        crope = _rope_lane_tables(cpos)
        ident = (jnp.ones_like(crope[0]), jnp.zeros_like(crope[0]), jnp.zeros_like(crope[0]))
        self.crope = tuple(jnp.stack([a, b]) for a, b in zip(crope, ident))
        self.c2s = _cmp_to_slc_wide(self.rows, seq // SLC_LEN)
        self.eb = _gate_expanders()


def _token_mixer_layer(x2, sc1, sh1, g1, w_in, cmp_pe, cmp_w1, cmp_w2, w_pool, pool_scale, w_lift, w_o,
                       lng, lnb, cst, bsz, seq, alpha):
    d = D_MODEL
    rows = cst.rows
    s1 = ATT_WIDTH + KV_WIDTH
    s2 = s1 + GATE_NSA
    s3 = s2 + POOL_WIDTH
    w_gate = jnp.pad(w_in[:, s1:s2], ((0, 0), (0, LANES - GATE_NSA)))
    w_all = jnp.concatenate([w_in[:, :s1], w_in[:, s2:s3], w_in[:, s3:], w_gate], axis=1).astype(BF16)
    q, kv, p_in, g_mrg, g_nsa = _inproj_call(x2, sc1, sh1, w_all, *cst.rope, seq, 256)

    eye_g = jnp.eye(N_KV, dtype=F32)
    zc = kv[:, :2 * LANES].reshape(bsz, rows, CMP_STRIDE, 2, LANES)
    zc = jnp.transpose(zc, (0, 3, 1, 2, 4)).reshape(bsz, 2, rows, CMP_STRIDE * LANES)
    w1x = jnp.einsum('klde,gh->klgdhe', cmp_w1.reshape(2, CMP_LEN, HEAD_DIM, HEAD_DIM), eye_g)
    w1x = w1x.reshape(2, CMP_LEN * LANES, LANES)
    half = CMP_STRIDE * LANES
    pex = jnp.broadcast_to(cmp_pe[:, :, None, :], (2, CMP_LEN, N_KV, HEAD_DIM)).reshape(2, 1, CMP_LEN * LANES)
    w2x = jnp.einsum('kef,gh->kgehf', cmp_w2, eye_g).reshape(2, LANES, LANES)
    kvc = _compress_call(zc, pex[:, :, :half], pex[:, :, half:], w1x[:, :half], w1x[:, half:], w2x,
                         *cst.crope)

    o_cmp, sel = _cmpsel_call(q, kvc, cst.c2s.astype(BF16), seq, 256)
    o_slc = _slc_call(q, kv, sel, seq, 256, 512)
    o_win = _win_call(q, kv, seq, 256)
    o_pool = _pool_call(p_in, w_pool.astype(BF16), pool_scale.reshape(1, -1), seq, 512)
    return _merge_call(o_cmp, o_slc, o_win, g_nsa, o_pool, g_mrg, x2, g1,
                       lng.reshape(1, d), lnb.reshape(1, d),
                       w_lift.astype(BF16), w_o.astype(BF16), cst.eb, seq, 256, alpha)


SC_CORES = 2
SC_SUBCORES = 16
SC_LANES = 16
SC_WORKERS = SC_CORES * SC_SUBCORES
SC_ROWS = 32
SC_TOKEN_SHARE = 10
SC_BLOCK = 256


def _sc_params():
    cp = pltpu.CompilerParams()
    if "needs_layout_passes" in pltpu.CompilerParams.__dataclass_fields__:
        cp = dataclasses.replace(cp, needs_layout_passes=False)
    return cp


def _sc_token_stream(tabs, erow_hbm, side_hbm, tok0, tpw, base, idx_bufs, side_bufs, rows_v, sems,
                     begin_fn, chunk_fn, end_fn):
    per_tab = PEER_SEL // SC_ROWS
    nch = per_tab * len(tabs)
    assert nch % 2 == 0

    def idx_copy(t, s):
        return pltpu.make_async_copy(erow_hbm.at[tok0 + base + t], idx_bufs[s], sems.at[2 + s])

    def side_copy(t, s):
        return pltpu.make_async_copy(side_hbm.at[tok0 + base + t], side_bufs[s], sems.at[4 + s])

    def row_copy(s, c):
        rows = idx_bufs[s].at[pl.ds((c % per_tab) * SC_ROWS, SC_ROWS)]
        return pltpu.make_async_copy(tabs[c // per_tab].at[rows], rows_v.at[c % 2], sems.at[c % 2])

    def to_expert_ids(s):
        shift = jnp.full((SC_LANES,), (D_MODEL // 2 // LANES).bit_length() - 1, I32)
        for j in range(PEER_SEL // SC_LANES):
            sl = pl.ds(j * SC_LANES, SC_LANES)
            idx_bufs[s][sl] = lax.shift_right_logical(idx_bufs[s][sl], shift)

    idx_copy(0, 0).start()
    side_copy(0, 0).start()
    idx_copy(0, 0).wait()
    to_expert_ids(0)
    row_copy(0, 0).start()

    @pl.loop(0, tpw // 2)
    def _(i):
        for slot in (0, 1):
            other = 1 - slot
            t = 2 * i + slot
            tn = jnp.minimum(t + 1, tpw - 1)
            idx_copy(tn, other).start()
            side_copy(tn, other).start()
            side_copy(t, slot).wait()
            begin_fn(tok0 + base + t, slot)
            for c in range(nch):
                if c + 1 < nch:
                    row_copy(slot, c + 1).start()
                else:
                    idx_copy(tn, other).wait()
                    to_expert_ids(other)
                    row_copy(other, 0).start()
                row_copy(slot, c).wait()
                chunk_fn(c, rows_v.at[c % 2], slot)
            end_fn(base + t, slot)

    row_copy(0, 0).wait()
    side_copy(0, 0).wait()


GELU_C0 = 0.7978845608028654
GELU_C1 = 0.044715


def _sc_peer_call(erow, hrows, gate, utab, vtab, tok0, m):
    d = utab.shape[1]
    assert m % (2 * SC_WORKERS) == 0 and d % SC_BLOCK == 0
    tpw = m // SC_WORKERS
    per_tab = PEER_SEL // SC_ROWS
    nvec = SC_BLOCK // SC_LANES
    mesh = plsc.VectorSubcoreMesh(core_axis_name="c", subcore_axis_name="s")

    @functools.partial(
        pl.kernel, mesh=mesh, out_type=jax.ShapeDtypeStruct((m, d), F32),
        scratch_types=[pltpu.VMEM((PEER_SEL,), I32), pltpu.VMEM((PEER_SEL,), I32),
                       pltpu.VMEM((d,), F32), pltpu.VMEM((d,), F32),
                       pltpu.VMEM((2, SC_ROWS, d), F32), pltpu.VMEM((PEER_SEL * SC_LANES,), F32),
                       pltpu.VMEM((PEER_SEL,), F32), pltpu.VMEM((PEER_SEL,), F32), pltpu.VMEM((d,), F32),
                       pltpu.SemaphoreType.DMA((6,))],
        compiler_params=_sc_params(), name="sc_peer")
    def run(utab_hbm, vtab_hbm, erow_hbm, h_hbm, gate_hbm, out_hbm,
            idx_a, idx_b, h_a, h_b, rows_v, part_v, coef_v, gate_v, acc_v, sems):
        wid = lax.axis_index("s") * SC_CORES + lax.axis_index("c")
        zero = jnp.zeros((SC_LANES,), F32)
        lane_iota = lax.iota(I32, SC_LANES)
        hbufs = (h_a, h_b)

        def begin(tg, slot):
            pltpu.sync_copy(gate_hbm.at[tg], gate_v)
            for j in range(PEER_SEL):
                part_v[pl.ds(j * SC_LANES, SC_LANES)] = zero
            for j in range(d // SC_LANES):
                acc_v[pl.ds(j * SC_LANES, SC_LANES)] = zero

        def coefficients():
            for g in range(PEER_SEL // SC_LANES):
                rowbase = (g * SC_LANES + lane_iota) * SC_LANES
                a = plsc.load_gather(part_v, [rowbase])
                for l in range(1, SC_LANES):
                    a = a + plsc.load_gather(part_v, [rowbase + l])
                z = GELU_C0 * (a + GELU_C1 * a * a * a)
                th = 1.0 - 2.0 / (jnp.exp(2.0 * z) + 1.0)
                sl = pl.ds(g * SC_LANES, SC_LANES)
                coef_v[sl] = gate_v[sl] * (0.5 * a * (1.0 + th))

        def chunk(c, buf, slot):
            if c < per_tab:
                @pl.loop(0, d // SC_BLOCK)
                def _(lb):
                    lane0 = pl.multiple_of(lb * SC_BLOCK, SC_BLOCK)
                    hs = [hbufs[slot][pl.ds(lane0 + q * SC_LANES, SC_LANES)] for q in range(nvec)]

                    @plsc.parallel_loop(0, SC_ROWS, unroll=2)
                    def _(r):
                        ps = [hs[q] * buf[r, pl.ds(lane0 + q * SC_LANES, SC_LANES)] for q in range(nvec)]
                        while len(ps) > 1:
                            ps = [x + y for x, y in zip(ps[0::2], ps[1::2])]
                        row = pl.multiple_of((c * SC_ROWS + r) * SC_LANES, SC_LANES)
                        plsc.addupdate(part_v.at[pl.ds(row, SC_LANES)], ps[0])

                if c == per_tab - 1:
                    coefficients()
            else:
                cc = c - per_tab

                @pl.loop(0, d // SC_BLOCK)
                def _(lb):
                    lane0 = pl.multiple_of(lb * SC_BLOCK, SC_BLOCK)
                    accs = tuple(acc_v[pl.ds(lane0 + q * SC_LANES, SC_LANES)] for q in range(nvec))

                    @plsc.parallel_loop(0, SC_ROWS, unroll=2, carry=accs)
                    def accs(r, acc):
                        ck = plsc.load_gather(coef_v, [jnp.full((SC_LANES,), cc * SC_ROWS, I32) + r])
                        return tuple(acc[q] + ck * buf[r, pl.ds(lane0 + q * SC_LANES, SC_LANES)]
                                     for q in range(nvec))

                    for q in range(nvec):
                        acc_v[pl.ds(lane0 + q * SC_LANES, SC_LANES)] = accs[q]

        def end(t, slot):
            pltpu.sync_copy(acc_v, out_hbm.at[t])

        _sc_token_stream((utab_hbm, vtab_hbm), erow_hbm, h_hbm, tok0, tpw, wid * tpw, (idx_a, idx_b), hbufs,
                         rows_v, sems, begin, chunk, end)

    return run(utab, vtab, erow, hrows, gate)


def _peer_layer(x2, sc2, sh2, g2, peer_wq, peer_keys, utab, vtab, upack, vpack, lng, lnb, seq, alpha,
                tt_route=512, tt_gather=64, tm=512, sc_tokens=0):
    n, d = x2.shape
    keys = peer_keys.reshape(2 * PEER_HEADS, N_KEYS, PEER_DK // 2)
    h2, hv, erow, gate = _route_call(x2, sc2, sh2, peer_wq.astype(BF16), keys.astype(BF16), seq, tt_route)
    n_tc = n - sc_tokens
    y_sc = _sc_peer_call(erow, h2, gate, utab, vtab, n_tc, sc_tokens) if sc_tokens else None
    coef = _peer_u_call(erow, upack, hv, gate, tt_gather, n_tc)
    y_tc = _peer_v_call(erow, coef, vpack, tt_gather, n_tc)
    return _resln_call(x2, y_tc, y_sc, g2, lng.reshape(1, d), lnb.reshape(1, d), seq, tm, alpha)


def kernel(x, c, w_ada, b_ada, w_in, cmp_pe, cmp_w1, cmp_w2, w_pool, pool_scale, w_lift, w_o,
           ln_g, ln_b, peer_wq, peer_keys, peer_u, peer_v):
    bsz, seq, d = x.shape
    depth = w_ada.shape[0]
    assert d == D_MODEL and seq % 512 == 0 and SLC_TOPN <= seq // SLC_LEN <= LANES // N_KV
    alpha = (2 * depth) ** 0.25

    c_pad = jnp.zeros((SUBLANES, d), F32).at[:bsz].set(c)
    mods = _ada_call(c_pad, w_ada, b_ada)[:, :bsz]
    cst = _Consts(seq)
    nchain = 2 if bsz % 2 == 0 else 1
    bpc = bsz // nchain
    xs = [x[i * bpc:(i + 1) * bpc].reshape(bpc * seq, d) for i in range(nchain)]
    sc_tokens = SC_TOKEN_SHARE * bpc * seq // 16
    for l in range(depth):
        upack, vpack = _pack_table(peer_u[l]), _pack_table(peer_v[l])
        for i in range(nchain):
            sh1, sc1, g1, sh2, sc2, g2 = (mods[l][i * bpc:(i + 1) * bpc, j * d:(j + 1) * d].reshape(bpc, 1, d)
                                          for j in range(6))
            xi = _token_mixer_layer(xs[i], sc1, sh1, g1, w_in[l], cmp_pe[l], cmp_w1[l], cmp_w2[l], w_pool[l],
                                    pool_scale[l], w_lift[l], w_o[l], ln_g[l, 0], ln_b[l, 0], cst, bpc, seq, alpha)
            xs[i] = _peer_layer(xi, sc2, sh2, g2, peer_wq[l], peer_keys[l], peer_u[l], peer_v[l], upack, vpack,
                                ln_g[l, 1], ln_b[l, 1], seq, alpha, sc_tokens=sc_tokens)
    return jnp.concatenate(xs, axis=0).reshape(bsz, seq, d)
```

```python
import dataclasses
import functools

import jax
import jax.numpy as jnp
import numpy as np
from jax import lax
from jax.experimental import pallas as pl
from jax.experimental.pallas import tpu as pltpu
from jax.experimental.pallas import tpu_sc as plsc

F32 = jnp.float32
BF16 = jnp.bfloat16
I32 = jnp.int32
HI = lax.Precision.HIGHEST

D_MODEL = 1024
N_HEADS = 8
HEAD_DIM = 64
N_KV = 2
HPG = N_HEADS // N_KV
ROT_DIM = HEAD_DIM // 4
ROT_HALF = ROT_DIM // 2
ROPE_THETA = 500000.0
CMP_LEN = 32
CMP_STRIDE = 16
SLC_LEN = 64
SLC_TOPN = 16
WINDOW = 512
SCALE = HEAD_DIM ** -0.5
NEG = -1e30
FORCE_INIT = 1e6
FORCE_LOCAL = 2e6
POOL_GROUPS = 4
POOL_WINDOWS = (2, 4, 8, 16)
POOL_WIDTH = 512
POOL_GW = POOL_WIDTH // POOL_GROUPS
POOL_HALO = 16
ATT_WIDTH = N_HEADS * HEAD_DIM
KV_WIDTH = 3 * 2 * N_KV * HEAD_DIM
GATE_NSA = 3 * N_HEADS
MERGE_GATES = 2 * D_MODEL
PEER_HEADS = 8
N_KEYS = 128
PEER_TOPK = 16
PEER_DK = 128
PEER_SEL = PEER_HEADS * PEER_TOPK
LN_EPS = 1e-5

LANES = 128
SUBLANES = 8
VMEM_LIMIT = 56 * 1024 * 1024

NT_DIMS = (((1,), (1,)), ((), ()))


def _cparams(sem):
    return pltpu.CompilerParams(dimension_semantics=sem, vmem_limit_bytes=VMEM_LIMIT)


def _ada_kernel(c_ref, w_ref, b_ref, o_ref):
    c = c_ref[...]
    ca = c * jax.nn.sigmoid(c)
    o_ref[...] = jnp.dot(ca, w_ref[...], precision=HI, preferred_element_type=F32) + b_ref[...]


def _ada_call(c_pad, w_ada, b_ada):
    depth = w_ada.shape[0]
    nblk = w_ada.shape[2] // D_MODEL
    rows = c_pad.shape[0]
    return pl.pallas_call(
        _ada_kernel,
        grid=(depth, nblk),
        in_specs=[
            pl.BlockSpec((rows, D_MODEL), lambda l, j: (0, 0)),
            pl.BlockSpec((None, D_MODEL, D_MODEL), lambda l, j: (l, 0, j)),
            pl.BlockSpec((None, 1, D_MODEL), lambda l, j: (l, 0, j)),
        ],
        out_specs=pl.BlockSpec((None, rows, D_MODEL), lambda l, j: (l, 0, j)),
        out_shape=jax.ShapeDtypeStruct((depth, rows, nblk * D_MODEL), F32),
        compiler_params=_cparams(("arbitrary", "arbitrary")),
        name="ada_mod",
    )(c_pad, w_ada, b_ada.reshape(depth, 1, -1))


IN_COLS = ATT_WIDTH + KV_WIDTH + POOL_WIDTH + MERGE_GATES + LANES


def _rope_lanes(z, rc, rs1, rs2):
    return z * rc + pltpu.roll(z, ROT_HALF, 1) * rs1 + pltpu.roll(z, LANES - ROT_HALF, 1) * rs2


def _inproj_kernel(x_ref, sc_ref, sh_ref, w_ref, rc_ref, rs1_ref, rs2_ref,
                   q_ref, kv_ref, p_ref, mrg_ref, gn_ref):
    h = x_ref[...] * (1.0 + sc_ref[...]) + sh_ref[...]
    a = jnp.dot(h.astype(BF16), w_ref[...], preferred_element_type=F32)
    rc, rs1, rs2 = rc_ref[...], rs1_ref[...], rs2_ref[...]
    for j in range(ATT_WIDTH // LANES):
        q_ref[:, j * LANES:(j + 1) * LANES] = _rope_lanes(a[:, j * LANES:(j + 1) * LANES], rc, rs1, rs2)
    for br in range(3):
        c0 = ATT_WIDTH + br * 2 * LANES
        k = a[:, c0:c0 + LANES]
        if br > 0:
            k = _rope_lanes(k, rc, rs1, rs2)
        kv_ref[:, br * 2 * LANES:br * 2 * LANES + LANES] = k
        kv_ref[:, br * 2 * LANES + LANES:(br + 1) * 2 * LANES] = a[:, c0 + LANES:c0 + 2 * LANES]
    c1 = ATT_WIDTH + KV_WIDTH
    p_ref[...] = a[:, c1:c1 + POOL_WIDTH]
    mrg_ref[...] = a[:, c1 + POOL_WIDTH:c1 + POOL_WIDTH + MERGE_GATES]
    gn_ref[...] = a[:, c1 + POOL_WIDTH + MERGE_GATES:]


def _inproj_call(x2, sc, sh, w, rc, rs1, rs2, seq, tm):
    n = x2.shape[0]
    tpb = seq // tm
    tok = lambda i: (i, 0)
    bat = lambda i: (i // tpb, 0, 0)
    pos = lambda i: (i % tpb, 0)
    full = lambda i: (0, 0)
    return pl.pallas_call(
        _inproj_kernel,
        grid=(n // tm,),
        in_specs=[
            pl.BlockSpec((tm, D_MODEL), tok),
            pl.BlockSpec((None, 1, D_MODEL), bat),
            pl.BlockSpec((None, 1, D_MODEL), bat),
            pl.BlockSpec((D_MODEL, IN_COLS), full),
            pl.BlockSpec((tm, LANES), pos),
            pl.BlockSpec((tm, LANES), pos),
            pl.BlockSpec((tm, LANES), pos),
        ],
        out_specs=[
            pl.BlockSpec((tm, ATT_WIDTH), tok),
            pl.BlockSpec((tm, KV_WIDTH), tok),
            pl.BlockSpec((tm, POOL_WIDTH), tok),
            pl.BlockSpec((tm, MERGE_GATES), tok),
            pl.BlockSpec((tm, LANES), tok),
        ],
        out_shape=[
            jax.ShapeDtypeStruct((n, ATT_WIDTH), F32),
            jax.ShapeDtypeStruct((n, KV_WIDTH), F32),
            jax.ShapeDtypeStruct((n, POOL_WIDTH), F32),
            jax.ShapeDtypeStruct((n, MERGE_GATES), F32),
            jax.ShapeDtypeStruct((n, LANES), F32),
        ],
        compiler_params=_cparams(("arbitrary",)),
        name="in_proj",
    )(x2, sc, sh, w, rc, rs1, rs2)


def _compress_kernel(z_ref, pet_ref, peb_ref, w1t_ref, w1b_ref, w2_ref, rc_ref, rs1_ref, rs2_ref, o_ref):
    z = z_ref[...]
    rows = z.shape[0]
    top = jnp.dot(z + pet_ref[...], w1t_ref[...], precision=HI, preferred_element_type=F32)
    bot = jnp.dot(z + peb_ref[...], w1b_ref[...], precision=HI, preferred_element_type=F32)
    pre = top + pltpu.roll(bot, rows - 1, 0)
    y = jnp.dot(jax.nn.gelu(pre), w2_ref[...], precision=HI, preferred_element_type=F32)
    o_ref[...] = _rope_lanes(y, rc_ref[...], rs1_ref[...], rs2_ref[...])


def _compress_call(z, pet, peb, w1t, w1b, w2, rc, rs1, rs2):
    b, _, rows, width = z.shape
    kvsel = lambda i, j: (j, 0, 0)
    return pl.pallas_call(
        _compress_kernel,
        grid=(b, 2),
        in_specs=[
            pl.BlockSpec((None, None, rows, width), lambda i, j: (i, j, 0, 0)),
            pl.BlockSpec((None, 1, width), kvsel),
            pl.BlockSpec((None, 1, width), kvsel),
            pl.BlockSpec((None, width, LANES), kvsel),
            pl.BlockSpec((None, width, LANES), kvsel),
            pl.BlockSpec((None, LANES, LANES), kvsel),
            pl.BlockSpec((None, rows, LANES), kvsel),
            pl.BlockSpec((None, rows, LANES), kvsel),
            pl.BlockSpec((None, rows, LANES), kvsel),
        ],
        out_specs=pl.BlockSpec((None, None, rows, LANES), lambda i, j: (i, j, 0, 0)),
        out_shape=jax.ShapeDtypeStruct((b, 2, rows, LANES), F32),
        compiler_params=_cparams(("arbitrary", "arbitrary")),
        name="compress",
    )(z, pet, peb, w1t, w1b, w2, rc, rs1, rs2)


def _cmpsel_kernel(q_ref, kc_ref, vc_ref, c2s_ref, o_ref, sel_ref, *, tq):
    t0 = pl.program_id(1) * tq
    kc = kc_ref[...]
    vc = vc_ref[...]
    rows = kc.shape[0]
    trow = t0 + lax.broadcasted_iota(I32, (tq, rows), 0)
    cend = lax.broadcasted_iota(I32, (tq, rows), 1) * CMP_STRIDE + (CMP_LEN - 1)
    vis = cend <= trow
    anyv = (trow[:, :1] >= CMP_LEN - 1).astype(F32)
    imp = jnp.zeros((tq, LANES), F32)
    for g in range(N_KV):
        kg = kc[:, g * HEAD_DIM:(g + 1) * HEAD_DIM].astype(BF16)
        vg = vc[:, g * HEAD_DIM:(g + 1) * HEAD_DIM].astype(BF16)
        psum = jnp.zeros((tq, rows), F32)
        for h in range(HPG):
            hd = g * HPG + h
            qh = q_ref[:, hd * HEAD_DIM:(hd + 1) * HEAD_DIM].astype(BF16)
            s = lax.dot_general(qh, kg, NT_DIMS, preferred_element_type=F32) * SCALE
            s = jnp.where(vis, s, NEG)
            e = jnp.exp(s - jnp.max(s, axis=-1, keepdims=True))
            p = e / jnp.sum(e, axis=-1, keepdims=True) * anyv
            o_ref[:, hd * HEAD_DIM:(hd + 1) * HEAD_DIM] = jnp.dot(
                p.astype(BF16), vg, preferred_element_type=F32)
            psum = psum + p
        imp = imp + jnp.dot(psum.astype(BF16), c2s_ref[g], preferred_element_type=F32)
    lane = lax.broadcasted_iota(I32, (tq, LANES), 1)
    blk = lane & (SLC_LEN - 1)
    cur = lax.shift_right_logical(t0 + lax.broadcasted_iota(I32, (tq, LANES), 0), 6)
    score = jnp.where(blk <= cur, imp, NEG)
    score = jnp.where(blk == 0, FORCE_INIT, score)
    score = jnp.where(blk == cur, FORCE_LOCAL, score)
    sc_t = score.T
    nblk = LANES // N_KV
    jrow = lax.broadcasted_iota(I32, (nblk, tq), 0)
    sel_parts = []
    for g in range(N_KV):
        sc = sc_t[g * nblk:(g + 1) * nblk]
        cnt = jnp.zeros((nblk, tq), I32)
        for k in range(nblk):
            rk = sc[k:k + 1, :]
            ge = (rk >= sc).astype(I32)
            gt = (rk > sc).astype(I32)
            cnt = cnt + jnp.where(jrow > k, ge, gt)
        sel_parts.append((cnt < SLC_TOPN).astype(F32))
    sel_ref[...] = jnp.concatenate(sel_parts, axis=0).T


def _cmpsel_call(q, kvc, c2s, seq, tq):
    n = q.shape[0]
    b = n // seq
    nq = seq // tq
    rows = kvc.shape[2]
    tok = lambda i, j: (i * nq + j, 0)
    return pl.pallas_call(
        functools.partial(_cmpsel_kernel, tq=tq),
        grid=(b, nq),
        in_specs=[
            pl.BlockSpec((tq, ATT_WIDTH), tok),
            pl.BlockSpec((None, None, rows, LANES), lambda i, j: (i, 0, 0, 0)),
            pl.BlockSpec((None, None, rows, LANES), lambda i, j: (i, 1, 0, 0)),
            pl.BlockSpec((N_KV, rows, LANES), lambda i, j: (0, 0, 0)),
        ],
        out_specs=[pl.BlockSpec((tq, ATT_WIDTH), tok), pl.BlockSpec((tq, LANES), tok)],
        out_shape=[jax.ShapeDtypeStruct((n, ATT_WIDTH), F32), jax.ShapeDtypeStruct((n, LANES), F32)],
        compiler_params=_cparams(("arbitrary", "arbitrary")),
        name="cmp_select",
    )(q, kvc, kvc, c2s)


def _slc_kernel(q_ref, k_ref, v_ref, sel_ref, o_ref, qs_sc, m_sc, l_sc, acc_sc, *, tq, tk):
    qi = pl.program_id(1)
    kt = pl.program_id(2)
    nk = pl.num_programs(2)

    @pl.when(kt == 0)
    def _init():
        for hd in range(N_HEADS):
            g, h = divmod(hd, HPG)
            qs_sc[g, h * tq:(h + 1) * tq, :] = (q_ref[:, hd * HEAD_DIM:(hd + 1) * HEAD_DIM] * SCALE).astype(BF16)
        m_sc[...] = jnp.full(m_sc.shape, NEG, F32)
        l_sc[...] = jnp.zeros(l_sc.shape, F32)
        acc_sc[...] = jnp.zeros(acc_sc.shape, F32)

    @pl.when(kt * tk <= qi * tq + (tq - 1))
    def _step():
        t = qi * tq + lax.broadcasted_iota(I32, (tq, tk), 0)
        kp = kt * tk + lax.broadcasted_iota(I32, (tq, tk), 1)
        causal = kp <= t
        nblk = LANES // N_KV
        jb = lax.broadcasted_iota(I32, (nblk, tk), 0)
        kb = lax.shift_right_logical(kt * tk + lax.broadcasted_iota(I32, (nblk, tk), 1), 6)
        expand = (jb == kb).astype(BF16)
        for g in range(N_KV):
            selg = sel_ref[:, g * nblk:(g + 1) * nblk].astype(BF16)
            member = jnp.dot(selg, expand, preferred_element_type=F32)
            bias = jnp.where(jnp.logical_and(causal, member > 0.5), 0.0, NEG)
            bias = jnp.concatenate([bias] * HPG, axis=0)
            kg = k_ref[:, g * HEAD_DIM:(g + 1) * HEAD_DIM].astype(BF16)
            vg = v_ref[:, g * HEAD_DIM:(g + 1) * HEAD_DIM].astype(BF16)
            s = lax.dot_general(qs_sc[g], kg, NT_DIMS, preferred_element_type=F32) + bias
            chunks = [s[:, c * LANES:(c + 1) * LANES] for c in range(tk // LANES)]
            mc = chunks[0]
            for x in chunks[1:]:
                mc = jnp.maximum(mc, x)
            m_old = m_sc[g]
            m_new = jnp.maximum(m_old, jnp.max(mc, axis=-1, keepdims=True))
            alpha = jnp.exp(m_old - m_new)
            ps = [jnp.exp(x - m_new) for x in chunks]
            lsum = ps[0]
            for x in ps[1:]:
                lsum = lsum + x
            l_sc[g] = alpha * l_sc[g] + lsum
            p = jnp.concatenate(ps, axis=1).astype(BF16)
            acc_sc[g] = alpha[:, :HEAD_DIM] * acc_sc[g] + jnp.dot(p, vg, preferred_element_type=F32)
            m_sc[g] = m_new

    @pl.when(kt == nk - 1)
    def _fin():
        for hd in range(N_HEADS):
            g, h = divmod(hd, HPG)
            l = jnp.sum(l_sc[g, h * tq:(h + 1) * tq, :], axis=-1, keepdims=True)
            o_ref[:, hd * HEAD_DIM:(hd + 1) * HEAD_DIM] = acc_sc[g, h * tq:(h + 1) * tq, :] / l


def _slc_call(q, kv, sel, seq, tq, tk):
    n = q.shape[0]
    b = n // seq
    nq = seq // tq
    nk = seq // tk
    tok = lambda i, j, k: (i * nq + j, 0)

    def key_map(col):
        def f(i, j, k):
            last = (j * tq + tq - 1) // tk
            return (i * nk + jnp.minimum(k, last), col)
        return f

    return pl.pallas_call(
        functools.partial(_slc_kernel, tq=tq, tk=tk),
        grid=(b, nq, nk),
        in_specs=[
            pl.BlockSpec((tq, ATT_WIDTH), tok),
            pl.BlockSpec((tk, LANES), key_map(2)),
            pl.BlockSpec((tk, LANES), key_map(3)),
            pl.BlockSpec((tq, LANES), tok),
        ],
        out_specs=pl.BlockSpec((tq, ATT_WIDTH), tok),
        out_shape=jax.ShapeDtypeStruct((n, ATT_WIDTH), F32),
        scratch_shapes=[
            pltpu.VMEM((N_KV, HPG * tq, HEAD_DIM), BF16),
            pltpu.VMEM((N_KV, HPG * tq, LANES), F32),
            pltpu.VMEM((N_KV, HPG * tq, LANES), F32),
            pltpu.VMEM((N_KV, HPG * tq, HEAD_DIM), F32),
        ],
        compiler_params=_cparams(("arbitrary", "arbitrary", "arbitrary")),
        name="slc_attn",
    )(q, kv, kv, sel)


def _win_kernel(q_ref, *refs, tq, nkb):
    k_refs = refs[:nkb]
    v_refs = refs[nkb:2 * nkb]
    o_ref = refs[2 * nkb]
    qi = pl.program_id(1)
    t = qi * tq + lax.broadcasted_iota(I32, (tq, tq), 0)
    col = lax.broadcasted_iota(I32, (tq, tq), 1)
    biases = []
    for j in range(nkb):
        kp = (qi - (nkb - 1) + j) * tq + col
        diff = t - kp
        ok = jnp.logical_and(jnp.logical_and(diff >= 0, diff < WINDOW), kp >= 0)
        biases.append(jnp.concatenate([jnp.where(ok, 0.0, NEG)] * HPG, axis=0))
    for g in range(N_KV):
        qs = jnp.concatenate(
            [(q_ref[:, (g * HPG + h) * HEAD_DIM:(g * HPG + h + 1) * HEAD_DIM] * SCALE).astype(BF16)
             for h in range(HPG)], axis=0)
        chunks = []
        for j in range(nkb):
            kg = k_refs[j][:, g * HEAD_DIM:(g + 1) * HEAD_DIM].astype(BF16)
            s = lax.dot_general(qs, kg, NT_DIMS, preferred_element_type=F32) + biases[j]
            chunks += [s[:, c * LANES:(c + 1) * LANES] for c in range(tq // LANES)]
        mc = chunks[0]
        for x in chunks[1:]:
            mc = jnp.maximum(mc, x)
        m = jnp.broadcast_to(jnp.max(mc, axis=-1, keepdims=True), mc.shape)
        ps = [jnp.exp(x - m) for x in chunks]
        lsum = ps[0]
        for x in ps[1:]:
            lsum = lsum + x
        l = jnp.sum(lsum, axis=-1, keepdims=True)
        per = tq // LANES
        o = None
        for j in range(nkb):
            vg = v_refs[j][:, g * HEAD_DIM:(g + 1) * HEAD_DIM].astype(BF16)
            pj = jnp.concatenate(ps[j * per:(j + 1) * per], axis=1).astype(BF16)
            oj = jnp.dot(pj, vg, preferred_element_type=F32)
            o = oj if o is None else o + oj
        o = o / l
        for h in range(HPG):
            hd = g * HPG + h
            o_ref[:, hd * HEAD_DIM:(hd + 1) * HEAD_DIM] = o[h * tq:(h + 1) * tq]


def _win_call(q, kv, seq, tq):
    n = q.shape[0]
    b = n // seq
    nq = seq // tq
    nkb = WINDOW // tq + 1
    tok = lambda i, j: (i * nq + j, 0)

    def key_map(col, back):
        return lambda i, j: (i * nq + jnp.maximum(j - back, 0), col)

    k_specs = [pl.BlockSpec((tq, LANES), key_map(4, nkb - 1 - jj)) for jj in range(nkb)]
    v_specs = [pl.BlockSpec((tq, LANES), key_map(5, nkb - 1 - jj)) for jj in range(nkb)]
    return pl.pallas_call(
        functools.partial(_win_kernel, tq=tq, nkb=nkb),
        grid=(b, nq),
        in_specs=[pl.BlockSpec((tq, ATT_WIDTH), tok)] + k_specs + v_specs,
        out_specs=pl.BlockSpec((tq, ATT_WIDTH), tok),
        out_shape=jax.ShapeDtypeStruct((n, ATT_WIDTH), F32),
        compiler_params=_cparams(("arbitrary", "arbitrary")),
        name="win_attn",
    )(q, *([kv] * (2 * nkb)))


def _pool_kernel(p_ref, prev_ref, w_ref, sc_ref, o_ref, *, ts):
    i = pl.program_id(1)
    x = p_ref[...]
    prev = prev_ref[...] * (i > 0).astype(F32)
    xe = jnp.concatenate([prev, x], axis=0)
    t1 = (i * ts + 1 + lax.broadcasted_iota(I32, (ts, POOL_GW), 0)).astype(F32)
    for g, w in enumerate(POOL_WINDOWS):
        a = xe[:, g * POOL_GW:(g + 1) * POOL_GW]
        off = POOL_HALO
        span = 1
        while span < w:
            a = a[span:] + a[:-span]
            off -= span
            span *= 2
        sums = a[off:off + ts]
        cnt = jnp.minimum(t1, float(w))
        pooled = sums / cnt - x[:, g * POOL_GW:(g + 1) * POOL_GW]
        y = jnp.dot(pooled.astype(BF16), w_ref[g], preferred_element_type=F32)
        o_ref[:, g * POOL_GW:(g + 1) * POOL_GW] = y * sc_ref[:, g * POOL_GW:(g + 1) * POOL_GW]


def _pool_call(p_in, w_pool, pool_scale, seq, ts):
    n = p_in.shape[0]
    b = n // seq
    nt = seq // ts
    hpt = ts // POOL_HALO
    tok = lambda i, j: (i * nt + j, 0)
    return pl.pallas_call(
        functools.partial(_pool_kernel, ts=ts),
        grid=(b, nt),
        in_specs=[
            pl.BlockSpec((ts, POOL_WIDTH), tok),
            pl.BlockSpec((POOL_HALO, POOL_WIDTH), lambda i, j: (i * nt * hpt + jnp.maximum(j * hpt - 1, 0), 0)),
            pl.BlockSpec((POOL_GROUPS, POOL_GW, POOL_GW), lambda i, j: (0, 0, 0)),
            pl.BlockSpec((1, POOL_WIDTH), lambda i, j: (0, 0)),
        ],
        out_specs=pl.BlockSpec((ts, POOL_WIDTH), tok),
        out_shape=jax.ShapeDtypeStruct((n, POOL_WIDTH), F32),
        compiler_params=_cparams(("arbitrary", "arbitrary")),
        name="pool_mix",
    )(p_in, p_in, w_pool, pool_scale)


def _layer_norm(z, g, b):
    mu = jnp.mean(z, axis=-1, keepdims=True)
    zc = z - mu
    var = jnp.mean(zc * zc, axis=-1, keepdims=True)
    return zc * lax.rsqrt(var + LN_EPS) * g + b


def _merge_kernel(oc_ref, os_ref, ow_ref, gn_ref, op_ref, gm_ref, x_ref, g1_ref, lng_ref, lnb_ref,
                  wl_ref, wo_ref, eb_ref, o_ref, *, alpha):
    gate = jax.nn.sigmoid(gn_ref[...])
    branches = (oc_ref, os_ref, ow_ref)
    oatt = None
    for br in range(3):
        gx = jnp.dot(gate, eb_ref[br], precision=HI, preferred_element_type=F32)
        term = gx * branches[br][...]
        oatt = term if oatt is None else oatt + term
    la = jnp.dot(oatt.astype(BF16), wl_ref[0], preferred_element_type=F32)
    lb = jnp.dot(op_ref[...].astype(BF16), wl_ref[1], preferred_element_type=F32)
    gm = jax.nn.sigmoid(gm_ref[...])
    merged = gm[:, :D_MODEL] * la + gm[:, D_MODEL:] * lb
    y = jnp.dot(merged.astype(BF16), wo_ref[...], preferred_element_type=F32)
    z = alpha * x_ref[...] + g1_ref[...] * y
    o_ref[...] = _layer_norm(z, lng_ref[...], lnb_ref[...])


def _merge_call(oc, osl, ow, gn, op, gm, x2, g1, lng, lnb, wl, wo, eb, seq, tm, alpha):
    n = x2.shape[0]
    tpb = seq // tm
    tok = lambda i: (i, 0)
    bat = lambda i: (i // tpb, 0, 0)
    return pl.pallas_call(
        functools.partial(_merge_kernel, alpha=alpha),
        grid=(n // tm,),
        in_specs=[
            pl.BlockSpec((tm, ATT_WIDTH), tok), pl.BlockSpec((tm, ATT_WIDTH), tok),
            pl.BlockSpec((tm, ATT_WIDTH), tok), pl.BlockSpec((tm, LANES), tok),
            pl.BlockSpec((tm, POOL_WIDTH), tok), pl.BlockSpec((tm, MERGE_GATES), tok),
            pl.BlockSpec((tm, D_MODEL), tok),
            pl.BlockSpec((None, 1, D_MODEL), bat),
            pl.BlockSpec((1, D_MODEL), lambda i: (0, 0)), pl.BlockSpec((1, D_MODEL), lambda i: (0, 0)),
            pl.BlockSpec((2, ATT_WIDTH, D_MODEL), lambda i: (0, 0, 0)),
            pl.BlockSpec((D_MODEL, D_MODEL), lambda i: (0, 0)),
            pl.BlockSpec((3, LANES, ATT_WIDTH), lambda i: (0, 0, 0)),
        ],
        out_specs=pl.BlockSpec((tm, D_MODEL), tok),
        out_shape=jax.ShapeDtypeStruct((n, D_MODEL), F32),
        compiler_params=_cparams(("arbitrary",)),
        name="merge_out",
    )(oc, osl, ow, gn, op, gm, x2, g1, lng, lnb, wl, wo, eb)


def _extract_top(cur, ids, n):
    rows = cur.shape[0]
    rio = lax.broadcasted_iota(I32, cur.shape, 0)
    vals, outs = [], []
    for _ in range(n):
        m = jnp.max(cur, axis=0, keepdims=True)
        pos = jnp.min(jnp.where(cur == m, rio, rows), axis=0, keepdims=True)
        hit = rio == pos
        vals.append(m)
        outs.append(pos if ids is None else jnp.max(jnp.where(hit, ids, -1), axis=0, keepdims=True))
        cur = jnp.where(hit, -jnp.inf, cur)
    return jnp.concatenate(vals, axis=0), jnp.concatenate(outs, axis=0)


def _route_kernel(x_ref, sc_ref, sh_ref, wq_ref, keys_ref, h_ref, hv_ref, e_ref, g_ref,
                  st_sc, ts_sc, ti_sc, eo_sc, go_sc):
    h = x_ref[...] * (1.0 + sc_ref[...]) + sh_ref[...]
    h_ref[...] = h
    for r in range(SUBLANES):
        hv_ref[:, r, :] = h[:, r * LANES:(r + 1) * LANES]
    qp = jnp.dot(h.astype(BF16), wq_ref[...], preferred_element_type=F32).astype(BF16)
    half = PEER_DK // 2
    for hp in range(2 * PEER_HEADS):
        st_sc[hp] = lax.dot_general(keys_ref[hp], qp[:, hp * half:(hp + 1) * half], NT_DIMS,
                                    preferred_element_type=F32)

    def half_body(hp, carry):
        vals, ids = _extract_top(st_sc[hp], None, PEER_TOPK)
        ts_sc[hp] = vals
        ti_sc[hp] = ids
        return carry

    lax.fori_loop(0, 2 * PEER_HEADS, half_body, 0)

    def head_body(hh, carry):
        s1, s2 = ts_sc[2 * hh], ts_sc[2 * hh + 1]
        i1, i2 = ti_sc[2 * hh], ti_sc[2 * hh + 1]
        brow = lax.broadcasted_iota(I32, (SUBLANES, s1.shape[1]), 0)
        cands = [s1[0:1, :] + s2]
        cidxs = [i1[0:1, :] * N_KEYS + i2]
        for a in range(1, SUBLANES):
            ok = brow < PEER_TOPK // (a + 1)
            cands.append(jnp.where(ok, s1[a:a + 1, :] + s2[:SUBLANES], -jnp.inf))
            cidxs.append(i1[a:a + 1, :] * N_KEYS + i2[:SUBLANES])
        cands.append(s1[SUBLANES:] + s2[0:1, :])
        cidxs.append(i1[SUBLANES:] * N_KEYS + i2[0:1, :])
        sv, ei = _extract_top(jnp.concatenate(cands, axis=0), jnp.concatenate(cidxs, axis=0), PEER_TOPK)
        ex = jnp.exp(sv - sv[0:1, :])
        go_sc[hh] = ex / jnp.sum(ex, axis=0, keepdims=True)
        eo_sc[hh] = ei.astype(F32)
        return carry

    lax.fori_loop(0, PEER_HEADS, head_body, 0)
    e_all = jnp.concatenate([eo_sc[hh] for hh in range(PEER_HEADS)], axis=0)
    g_all = jnp.concatenate([go_sc[hh] for hh in range(PEER_HEADS)], axis=0)
    e_ref[...] = e_all.T.astype(I32) * (D_MODEL // 2 // LANES)
    g_ref[...] = g_all.T


def _route_call(x2, sc, sh, wq, keys, seq, tt):
    n = x2.shape[0]
    tpb = seq // tt
    tok = lambda i: (i, 0)
    bat = lambda i: (i // tpb, 0, 0)
    nhp = 2 * PEER_HEADS
    return pl.pallas_call(
        _route_kernel,
        grid=(n // tt,),
        in_specs=[
            pl.BlockSpec((tt, D_MODEL), tok),
            pl.BlockSpec((None, 1, D_MODEL), bat),
            pl.BlockSpec((None, 1, D_MODEL), bat),
            pl.BlockSpec((D_MODEL, PEER_HEADS * PEER_DK), lambda i: (0, 0)),
            pl.BlockSpec((nhp, N_KEYS, PEER_DK // 2), lambda i: (0, 0, 0)),
        ],
        out_specs=[pl.BlockSpec((tt, D_MODEL), tok), pl.BlockSpec((tt, SUBLANES, LANES), lambda i: (i, 0, 0)),
                   pl.BlockSpec((tt, PEER_SEL), tok), pl.BlockSpec((tt, PEER_SEL), tok)],
        out_shape=[jax.ShapeDtypeStruct((n, D_MODEL), F32), jax.ShapeDtypeStruct((n, SUBLANES, LANES), F32),
                   jax.ShapeDtypeStruct((n, PEER_SEL), I32), jax.ShapeDtypeStruct((n, PEER_SEL), F32)],
        scratch_shapes=[
            pltpu.VMEM((nhp, N_KEYS, tt), F32),
            pltpu.VMEM((nhp, PEER_TOPK, tt), F32),
            pltpu.VMEM((nhp, PEER_TOPK, tt), I32),
            pltpu.VMEM((PEER_HEADS, PEER_TOPK, tt), F32),
            pltpu.VMEM((PEER_HEADS, PEER_TOPK, tt), F32),
        ],
        compiler_params=_cparams(("arbitrary",)),
        name="peer_route",
    )(x2, sc, sh, wq, keys)


HALF_ROWS = SUBLANES // 2
HI_MASK = -65536
PAIR_TILES = PEER_SEL // 2
PAIR_ROWS = PAIR_TILES * SUBLANES


def _load_two_experts(tab_ref, ra, rb):
    wa = tab_ref[pl.ds(pl.multiple_of(ra, HALF_ROWS), HALF_ROWS), :]
    wb = tab_ref[pl.ds(pl.multiple_of(rb, HALF_ROWS), HALF_ROWS), :]
    w2 = jnp.concatenate([wa, wb], axis=0)
    return lax.bitcast_convert_type(w2 << 16, F32), lax.bitcast_convert_type(w2 & HI_MASK, F32)


def _fold_pairs(vs):
    row = lax.broadcasted_iota(I32, (SUBLANES, LANES), 0)
    shift = HALF_ROWS // 2
    while len(vs) > 1:
        low = (row & shift) == 0
        vs = [jnp.where(low, a + pltpu.roll(a, SUBLANES - shift, 0), b + pltpu.roll(b, shift, 0))
              for a, b in zip(vs[0::2], vs[1::2])]
        shift //= 2
    return vs[0]


def _fold_order():
    idx = [[2 * i if r < HALF_ROWS else 2 * i + 1 for r in range(SUBLANES)] for i in range(HALF_ROWS)]
    shift = HALF_ROWS // 2
    while len(idx) > 1:
        idx = [[a[r] if (r & shift) == 0 else b[r] for r in range(SUBLANES)]
               for a, b in zip(idx[0::2], idx[1::2])]
        shift //= 2
    return idx[0]


def _peer_u_kernel(e_sm, tab_ref, hv_ref, gate_ref, coef_ref, *, tt):
    row = lax.broadcasted_iota(I32, (SUBLANES, LANES), 0)
    low = row < HALF_ROWS
    eye = (lax.broadcasted_iota(I32, (PEER_SEL, LANES), 0) ==
           lax.broadcasted_iota(I32, (PEER_SEL, LANES), 1))
    order = _fold_order()

    def finish(t, part):
        col = jnp.sum(part, axis=-1, keepdims=True)
        a_row = jnp.sum(jnp.where(eye, col, 0.0), axis=0, keepdims=True)
        coef_ref[t] = gate_ref[pl.ds(t, 1), :] * jax.nn.gelu(a_row)

    def token(t, part_prev):
        finish(jnp.maximum(t - 1, 0), part_prev)
        hv = hv_ref[t]
        hsw = pltpu.roll(hv, HALF_ROWS, 0)
        h_lo = jnp.where(low, hv, hsw)
        h_hi = jnp.where(low, hsw, hv)
        folded = []
        for j in range(PEER_SEL // SUBLANES):
            prods = []
            for i in range(HALF_ROWS):
                ka = j * SUBLANES + order.index(2 * i)
                kb = j * SUBLANES + order.index(2 * i + 1)
                lo, hi = _load_two_experts(tab_ref, e_sm[t, ka], e_sm[t, kb])
                prods.append(lo * h_lo + hi * h_hi)
            folded.append(_fold_pairs(prods))
        return jnp.concatenate(folded, axis=0)

    last = lax.fori_loop(0, tt, token, jnp.zeros((PEER_SEL, LANES), F32))
    finish(tt - 1, last)


def _peer_v_kernel(e_sm, coef_ref, tab_ref, y_ref, cv_sc, *, tt, nacc):
    row = lax.broadcasted_iota(I32, (SUBLANES, LANES), 0)
    low = row < HALF_ROWS
    rr = lax.broadcasted_iota(I32, (PAIR_ROWS, LANES), 0)
    kk = lax.broadcasted_iota(I32, (PAIR_ROWS, LANES), 1)
    onehot = (kk == 2 * (rr >> 3) + ((rr >> 2) & 1)).astype(F32)
    ones = jnp.ones((LANES, LANES), BF16)

    def expand(t, slot):
        lhs = (onehot * coef_ref[t]).astype(BF16)
        cv_sc[slot] = jnp.dot(lhs, ones, preferred_element_type=F32)

    def process(t, slot):
        acc_lo = [jnp.zeros((SUBLANES, LANES), F32) for _ in range(nacc)]
        acc_hi = [jnp.zeros((SUBLANES, LANES), F32) for _ in range(nacc)]
        for j in range(PAIR_TILES):
            grp = e_sm.at[t, pl.ds(2 * j // SUBLANES * SUBLANES, SUBLANES)]
            lo, hi = _load_two_experts(tab_ref, grp[2 * j % SUBLANES], grp[(2 * j + 1) % SUBLANES])
            cv = cv_sc[slot, j * SUBLANES:(j + 1) * SUBLANES, :]
            acc_lo[j % nacc] = acc_lo[j % nacc] + cv * lo
            acc_hi[j % nacc] = acc_hi[j % nacc] + cv * hi
        a_lo, a_hi = acc_lo[0], acc_hi[0]
        for i in range(1, nacc):
            a_lo = a_lo + acc_lo[i]
            a_hi = a_hi + acc_hi[i]
        a_lo = a_lo + pltpu.roll(a_lo, HALF_ROWS, 0)
        a_hi = a_hi + pltpu.roll(a_hi, HALF_ROWS, 0)
        y_ref[t] = jnp.where(low, a_lo, a_hi)

    expand(0, 0)

    def two_tokens(i, carry):
        t = 2 * i
        expand(t + 1, 1)
        process(t, 0)
        expand(jnp.minimum(t + 2, tt - 1), 0)
        process(t + 1, 1)
        return carry

    lax.fori_loop(0, tt // 2, two_tokens, 0)


def _resident_table_spec(tab):
    return pl.BlockSpec(tab.shape, lambda i: (0, 0), pipeline_mode=pl.Buffered(1))


def _peer_u_call(erow, tab, hv, gate, tt, n):
    return pl.pallas_call(
        functools.partial(_peer_u_kernel, tt=tt),
        grid=(n // tt,),
        in_specs=[
            pl.BlockSpec((tt, PEER_SEL), lambda i: (i, 0), memory_space=pltpu.SMEM),
            _resident_table_spec(tab),
            pl.BlockSpec((tt, SUBLANES, LANES), lambda i: (i, 0, 0)),
            pl.BlockSpec((tt, PEER_SEL), lambda i: (i, 0)),
        ],
        out_specs=pl.BlockSpec((tt, 1, PEER_SEL), lambda i: (i, 0, 0)),
        out_shape=jax.ShapeDtypeStruct((n, 1, PEER_SEL), F32),
        compiler_params=_cparams(("arbitrary",)),
        name="peer_u",
    )(erow, tab, hv, gate)


def _peer_v_call(erow, coef3, tab, tt, n):
    assert tt % 2 == 0
    return pl.pallas_call(
        functools.partial(_peer_v_kernel, tt=tt, nacc=4),
        grid=(n // tt,),
        in_specs=[
            pl.BlockSpec((tt, PEER_SEL), lambda i: (i, 0), memory_space=pltpu.SMEM),
            pl.BlockSpec((tt, 1, PEER_SEL), lambda i: (i, 0, 0)),
            _resident_table_spec(tab),
        ],
        out_specs=pl.BlockSpec((tt, SUBLANES, LANES), lambda i: (i, 0, 0)),
        out_shape=jax.ShapeDtypeStruct((n, SUBLANES, LANES), F32),
        scratch_shapes=[pltpu.VMEM((2, PAIR_ROWS, LANES), F32)],
        compiler_params=_cparams(("arbitrary",)),
        name="peer_v",
    )(erow, coef3, tab)


def _resln_kernel(x_ref, ytc_ref, ysc_ref, g_ref, lng_ref, lnb_ref, o_ref, *, alpha, tc_tiles):
    y_tc = jnp.concatenate([ytc_ref[:, r, :] for r in range(SUBLANES)], axis=1)
    y = jnp.where(pl.program_id(0) < tc_tiles, y_tc, ysc_ref[...])
    z = alpha * x_ref[...] + g_ref[...] * y
    o_ref[...] = _layer_norm(z, lng_ref[...], lnb_ref[...])


def _resln_call(x2, y_tc, y_sc, g2, lng, lnb, seq, tm, alpha):
    n = x2.shape[0]
    n_tc = y_tc.shape[0]
    assert n_tc % tm == 0 and n_tc > 0
    tc_tiles = n_tc // tm
    if y_sc is None:
        y_sc = jnp.zeros((tm, D_MODEL), F32)
    tpb = seq // tm
    tok = lambda i: (i, 0)
    return pl.pallas_call(
        functools.partial(_resln_kernel, alpha=alpha, tc_tiles=tc_tiles),
        grid=(n // tm,),
        in_specs=[
            pl.BlockSpec((tm, D_MODEL), tok),
            pl.BlockSpec((tm, SUBLANES, LANES), lambda i: (jnp.minimum(i, tc_tiles - 1), 0, 0)),
            pl.BlockSpec((tm, D_MODEL), lambda i: (jnp.maximum(i - tc_tiles, 0), 0)),
            pl.BlockSpec((None, 1, D_MODEL), lambda i: (i // tpb, 0, 0)),
            pl.BlockSpec((1, D_MODEL), lambda i: (0, 0)), pl.BlockSpec((1, D_MODEL), lambda i: (0, 0)),
        ],
        out_specs=pl.BlockSpec((tm, D_MODEL), tok),
        out_shape=jax.ShapeDtypeStruct((n, D_MODEL), F32),
        compiler_params=_cparams(("arbitrary",)),
        name="res_ln",
    )(x2, y_tc, y_sc, g2, lng, lnb)


def _rope_lane_tables(pos):
    inv = ROPE_THETA ** (-jnp.arange(0, ROT_DIM, 2, dtype=F32) / ROT_DIM)
    ang = pos.astype(F32)[:, None] * inv[None, :]
    cos, sin = jnp.cos(ang), jnp.sin(ang)
    lane = np.arange(LANES) % HEAD_DIM
    fidx = lane % ROT_HALF
    first = jnp.asarray(lane < ROT_HALF)
    second = jnp.asarray((lane >= ROT_HALF) & (lane < ROT_DIM))
    rot = jnp.asarray(lane < ROT_DIM)
    cl, sl = cos[:, fidx], sin[:, fidx]
    rc = jnp.where(rot, cl, 1.0)
    rs1 = jnp.where(second, sl, 0.0)
    rs2 = jnp.where(first, -sl, 0.0)
    return rc, rs1, rs2


def _pack_kernel(t_ref, o_ref):
    x = t_ref[...]
    half = x.shape[1] // 2
    lo = lax.bitcast_convert_type(x[:, :half].astype(BF16).astype(F32), I32)
    hi = lax.bitcast_convert_type(x[:, half:].astype(BF16).astype(F32), I32)
    w = lax.shift_right_logical(lo, jnp.full_like(lo, 16)) | hi
    rows = x.shape[0]
    per = half // LANES
    for r in range(per):
        o_ref[pl.ds(r, rows, stride=per), :] = w[:, r * LANES:(r + 1) * LANES]


def _pack_table(tab, te=512):
    e, d = tab.shape
    per = d // 2 // LANES
    return pl.pallas_call(
        _pack_kernel, grid=(e // te,),
        in_specs=[pl.BlockSpec((te, d), lambda i: (i, 0))],
        out_specs=pl.BlockSpec((te * per, LANES), lambda i: (i, 0)),
        out_shape=jax.ShapeDtypeStruct((e * per, LANES), I32),
        compiler_params=_cparams(("arbitrary",)), name="pack_table",
    )(tab)


def _cmp_to_slc_wide(rows, n_slc):
    st = np.arange(rows) * CMP_STRIDE
    js = np.arange(n_slc) * SLC_LEN
    ov = np.minimum(st[:, None] + CMP_LEN, js[None, :] + SLC_LEN) - np.maximum(st[:, None], js[None, :])
    c2s = np.maximum(ov, 0).astype(np.float32) / CMP_STRIDE
    wide = np.zeros((N_KV, rows, LANES), np.float32)
    nblk = LANES // N_KV
    for g in range(N_KV):
        wide[g, :, g * nblk:g * nblk + n_slc] = c2s
    return jnp.asarray(wide)


def _gate_expanders():
    eb = np.zeros((3, LANES, ATT_WIDTH), np.float32)
    for hd in range(N_HEADS):
        for br in range(3):
            eb[br, hd * 3 + br, hd * HEAD_DIM:(hd + 1) * HEAD_DIM] = 1.0
    return jnp.asarray(eb)


class _Consts:
    def __init__(self, seq):
        self.rows = seq // CMP_STRIDE
        self.rope = _rope_lane_tables(jnp.arange(seq))
        cpos = jnp.arange(self.rows) * CMP_STRIDE + CMP_LEN - 1
        crope = _rope_lane_tables(cpos)
        ident = (jnp.ones_like(crope[0]), jnp.zeros_like(crope[0]), jnp.zeros_like(crope[0]))
        self.crope = tuple(jnp.stack([a, b]) for a, b in zip(crope, ident))
        self.c2s = _cmp_to_slc_wide(self.rows, seq // SLC_LEN)
        self.eb = _gate_expanders()


def _token_mixer_layer(x2, sc1, sh1, g1, w_in, cmp_pe, cmp_w1, cmp_w2, w_pool, pool_scale, w_lift, w_o,
                       lng, lnb, cst, bsz, seq, alpha):
    d = D_MODEL
    rows = cst.rows
    s1 = ATT_WIDTH + KV_WIDTH
    s2 = s1 + GATE_NSA
    s3 = s2 + POOL_WIDTH
    w_gate = jnp.pad(w_in[:, s1:s2], ((0, 0), (0, LANES - GATE_NSA)))
    w_all = jnp.concatenate([w_in[:, :s1], w_in[:, s2:s3], w_in[:, s3:], w_gate], axis=1).astype(BF16)
    q, kv, p_in, g_mrg, g_nsa = _inproj_call(x2, sc1, sh1, w_all, *cst.rope, seq, 256)

    eye_g = jnp.eye(N_KV, dtype=F32)
    zc = kv[:, :2 * LANES].reshape(bsz, rows, CMP_STRIDE, 2, LANES)
    zc = jnp.transpose(zc, (0, 3, 1, 2, 4)).reshape(bsz, 2, rows, CMP_STRIDE * LANES)
    w1x = jnp.einsum('klde,gh->klgdhe', cmp_w1.reshape(2, CMP_LEN, HEAD_DIM, HEAD_DIM), eye_g)
    w1x = w1x.reshape(2, CMP_LEN * LANES, LANES)
    half = CMP_STRIDE * LANES
    pex = jnp.broadcast_to(cmp_pe[:, :, None, :], (2, CMP_LEN, N_KV, HEAD_DIM)).reshape(2, 1, CMP_LEN * LANES)
    w2x = jnp.einsum('kef,gh->kgehf', cmp_w2, eye_g).reshape(2, LANES, LANES)
    kvc = _compress_call(zc, pex[:, :, :half], pex[:, :, half:], w1x[:, :half], w1x[:, half:], w2x,
                         *cst.crope)

    o_cmp, sel = _cmpsel_call(q, kvc, cst.c2s.astype(BF16), seq, 256)
    o_slc = _slc_call(q, kv, sel, seq, 256, 512)
    o_win = _win_call(q, kv, seq, 256)
    o_pool = _pool_call(p_in, w_pool.astype(BF16), pool_scale.reshape(1, -1), seq, 512)
    return _merge_call(o_cmp, o_slc, o_win, g_nsa, o_pool, g_mrg, x2, g1,
                       lng.reshape(1, d), lnb.reshape(1, d),
                       w_lift.astype(BF16), w_o.astype(BF16), cst.eb, seq, 256, alpha)


SC_CORES = 2
SC_SUBCORES = 16
SC_LANES = 16
SC_WORKERS = SC_CORES * SC_SUBCORES
SC_ROWS = 32
SC_TOKEN_SHARE = 10
SC_BLOCK = 256


def _sc_params():
    cp = pltpu.CompilerParams()
    if "needs_layout_passes" in pltpu.CompilerParams.__dataclass_fields__:
        cp = dataclasses.replace(cp, needs_layout_passes=False)
    return cp


def _sc_token_stream(tabs, erow_hbm, sides, out_hbm, out_bufs, tok0, tpw, base, idx_bufs, rows_v, sems,
                     begin_fn, chunk_fn):
    per_tab = PEER_SEL // SC_ROWS
    nch = per_tab * len(tabs)
    assert nch % 2 == 0

    def idx_copy(t, s):
        return pltpu.make_async_copy(erow_hbm.at[tok0 + base + t], idx_bufs[s], sems.at[2 + s])

    def out_copy(t, s):
        return pltpu.make_async_copy(out_bufs[s], out_hbm.at[base + t], sems.at[4 + s])

    def side_copy(k, t, s):
        hbm, bufs = sides[k]
        return pltpu.make_async_copy(hbm.at[tok0 + base + t], bufs[s], sems.at[6 + 2 * k + s])

    def row_copy(s, c):
        rows = idx_bufs[s].at[pl.ds((c % per_tab) * SC_ROWS, SC_ROWS)]
        return pltpu.make_async_copy(tabs[c // per_tab].at[rows], rows_v.at[c % 2], sems.at[c % 2])

    def to_expert_ids(s):
        shift = jnp.full((SC_LANES,), (D_MODEL // 2 // LANES).bit_length() - 1, I32)
        for j in range(PEER_SEL // SC_LANES):
            sl = pl.ds(j * SC_LANES, SC_LANES)
            idx_bufs[s][sl] = lax.shift_right_logical(idx_bufs[s][sl], shift)

    idx_copy(0, 0).start()
    for k in range(len(sides)):
        side_copy(k, 0, 0).start()
    idx_copy(0, 0).wait()
    to_expert_ids(0)
    row_copy(0, 0).start()

    @pl.loop(0, tpw // 2)
    def _(i):
        for slot in (0, 1):
            other = 1 - slot
            t = 2 * i + slot
            tn = jnp.minimum(t + 1, tpw - 1)
            idx_copy(tn, other).start()
            for k in range(len(sides)):
                side_copy(k, tn, other).start()
            for k in range(len(sides)):
                side_copy(k, t, slot).wait()

            @pl.when(i > 0)
            def _():
                out_copy(t, slot).wait()

            begin_fn(slot)
            for c in range(nch):
                if c + 1 < nch:
                    row_copy(slot, c + 1).start()
                else:
                    idx_copy(tn, other).wait()
                    to_expert_ids(other)
                    row_copy(other, 0).start()
                row_copy(slot, c).wait()
                chunk_fn(c, rows_v.at[c % 2], slot)
            out_copy(t, slot).start()

    row_copy(0, 0).wait()
    for k in range(len(sides)):
        side_copy(k, 0, 0).wait()
    out_copy(0, 0).wait()
    out_copy(0, 1).wait()


GELU_C0 = 0.7978845608028654
GELU_C1 = 0.044715


def _sc_peer_call(erow, hrows, gate, utab, vtab, tok0, m):
    d = utab.shape[1]
    assert m % (2 * SC_WORKERS) == 0 and d % SC_BLOCK == 0
    tpw = m // SC_WORKERS
    per_tab = PEER_SEL // SC_ROWS
    nvec = SC_BLOCK // SC_LANES
    mesh = plsc.VectorSubcoreMesh(core_axis_name="c", subcore_axis_name="s")
    vec = lambda nelem, dt: pltpu.VMEM((nelem,), dt)

    @functools.partial(
        pl.kernel, mesh=mesh, out_type=jax.ShapeDtypeStruct((m, d), F32),
        scratch_types=[vec(PEER_SEL, I32), vec(PEER_SEL, I32), vec(d, F32), vec(d, F32),
                       vec(PEER_SEL, F32), vec(PEER_SEL, F32), vec(d, F32), vec(d, F32),
                       pltpu.VMEM((2, SC_ROWS, d), F32), vec(PEER_SEL * SC_LANES, F32), vec(PEER_SEL, F32),
                       pltpu.SemaphoreType.DMA((10,))],
        compiler_params=_sc_params(), name="sc_peer")
    def run(utab_hbm, vtab_hbm, erow_hbm, h_hbm, gate_hbm, out_hbm,
            idx_a, idx_b, h_a, h_b, gate_a, gate_b, acc_a, acc_b, rows_v, part_v, coef_v, sems):
        wid = lax.axis_index("s") * SC_CORES + lax.axis_index("c")
        zero = jnp.zeros((SC_LANES,), F32)
        lane_iota = lax.iota(I32, SC_LANES)
        hbufs, gbufs, accs_v = (h_a, h_b), (gate_a, gate_b), (acc_a, acc_b)

        def begin(slot):
            for j in range(PEER_SEL):
                part_v[pl.ds(j * SC_LANES, SC_LANES)] = zero
            for j in range(d // SC_LANES):
                accs_v[slot][pl.ds(j * SC_LANES, SC_LANES)] = zero

        def coefficients(slot):
            for g in range(PEER_SEL // SC_LANES):
                rowbase = (g * SC_LANES + lane_iota) * SC_LANES
                a = plsc.load_gather(part_v, [rowbase])
                for l in range(1, SC_LANES):
                    a = a + plsc.load_gather(part_v, [rowbase + l])
                z = GELU_C0 * (a + GELU_C1 * a * a * a)
                th = 1.0 - 2.0 / (jnp.exp(2.0 * z) + 1.0)
                sl = pl.ds(g * SC_LANES, SC_LANES)
                coef_v[sl] = gbufs[slot][sl] * (0.5 * a * (1.0 + th))

        def chunk(c, buf, slot):
            if c < per_tab:
                @pl.loop(0, d // SC_BLOCK)
                def _(lb):
                    lane0 = pl.multiple_of(lb * SC_BLOCK, SC_BLOCK)
                    hs = [hbufs[slot][pl.ds(lane0 + q * SC_LANES, SC_LANES)] for q in range(nvec)]

                    @plsc.parallel_loop(0, SC_ROWS, unroll=2)
                    def _(r):
                        ps = [hs[q] * buf[r, pl.ds(lane0 + q * SC_LANES, SC_LANES)] for q in range(nvec)]
                        while len(ps) > 1:
                            ps = [x + y for x, y in zip(ps[0::2], ps[1::2])]
                        row = pl.multiple_of((c * SC_ROWS + r) * SC_LANES, SC_LANES)
                        plsc.addupdate(part_v.at[pl.ds(row, SC_LANES)], ps[0])

                if c == per_tab - 1:
                    coefficients(slot)
            else:
                cc = c - per_tab
                acc_v = accs_v[slot]

                @pl.loop(0, d // SC_BLOCK)
                def _(lb):
                    lane0 = pl.multiple_of(lb * SC_BLOCK, SC_BLOCK)
                    accs = tuple(acc_v[pl.ds(lane0 + q * SC_LANES, SC_LANES)] for q in range(nvec))

                    @plsc.parallel_loop(0, SC_ROWS, unroll=2, carry=accs)
                    def accs(r, acc):
                        ck = plsc.load_gather(coef_v, [jnp.full((SC_LANES,), cc * SC_ROWS, I32) + r])
                        return tuple(acc[q] + ck * buf[r, pl.ds(lane0 + q * SC_LANES, SC_LANES)]
                                     for q in range(nvec))

                    for q in range(nvec):
                        acc_v[pl.ds(lane0 + q * SC_LANES, SC_LANES)] = accs[q]

        _sc_token_stream((utab_hbm, vtab_hbm), erow_hbm, ((h_hbm, hbufs), (gate_hbm, gbufs)), out_hbm, accs_v,
                         tok0, tpw, wid * tpw, (idx_a, idx_b), rows_v, sems, begin, chunk)

    return run(utab, vtab, erow, hrows, gate)


def _peer_layer(x2, sc2, sh2, g2, peer_wq, peer_keys, utab, vtab, upack, vpack, lng, lnb, seq, alpha,
                tt_route=512, tt_gather=64, tm=512, sc_tokens=0):
    n, d = x2.shape
    keys = peer_keys.reshape(2 * PEER_HEADS, N_KEYS, PEER_DK // 2)
    h2, hv, erow, gate = _route_call(x2, sc2, sh2, peer_wq.astype(BF16), keys.astype(BF16), seq, tt_route)
    n_tc = n - sc_tokens
    y_sc = _sc_peer_call(erow, h2, gate, utab, vtab, n_tc, sc_tokens) if sc_tokens else None
    coef = _peer_u_call(erow, upack, hv, gate, tt_gather, n_tc)
    y_tc = _peer_v_call(erow, coef, vpack, tt_gather, n_tc)
    return _resln_call(x2, y_tc, y_sc, g2, lng.reshape(1, d), lnb.reshape(1, d), seq, tm, alpha)


def kernel(x, c, w_ada, b_ada, w_in, cmp_pe, cmp_w1, cmp_w2, w_pool, pool_scale, w_lift, w_o,
           ln_g, ln_b, peer_wq, peer_keys, peer_u, peer_v):
    bsz, seq, d = x.shape
    depth = w_ada.shape[0]
    assert d == D_MODEL and seq % 512 == 0 and SLC_TOPN <= seq // SLC_LEN <= LANES // N_KV
    alpha = (2 * depth) ** 0.25

    c_pad = jnp.zeros((SUBLANES, d), F32).at[:bsz].set(c)
    mods = _ada_call(c_pad, w_ada, b_ada)[:, :bsz]
    cst = _Consts(seq)
    nchain = 2 if bsz % 2 == 0 else 1
    bpc = bsz // nchain
    xs = [x[i * bpc:(i + 1) * bpc].reshape(bpc * seq, d) for i in range(nchain)]
    sc_tokens = SC_TOKEN_SHARE * bpc * seq // 16
    for l in range(depth):
        upack, vpack = _pack_table(peer_u[l]), _pack_table(peer_v[l])
        for i in range(nchain):
            sh1, sc1, g1, sh2, sc2, g2 = (mods[l][i * bpc:(i + 1) * bpc, j * d:(j + 1) * d].reshape(bpc, 1, d)
                                          for j in range(6))
            xi = _token_mixer_layer(xs[i], sc1, sh1, g1, w_in[l], cmp_pe[l], cmp_w1[l], cmp_w2[l], w_pool[l],
                                    pool_scale[l], w_lift[l], w_o[l], ln_g[l, 0], ln_b[l, 0], cst, bpc, seq, alpha)
            xs[i] = _peer_layer(xi, sc2, sh2, g2, peer_wq[l], peer_keys[l], peer_u[l], peer_v[l], upack, vpack,
                                ln_g[l, 1], ln_b[l, 1], seq, alpha, sc_tokens=sc_tokens)
    return jnp.concatenate(xs, axis=0).reshape(bsz, seq, d)
```

```python
import dataclasses
import functools

import jax
import jax.numpy as jnp
import numpy as np
from jax import lax
from jax.experimental import pallas as pl
from jax.experimental.pallas import tpu as pltpu
from jax.experimental.pallas import tpu_sc as plsc

F32 = jnp.float32
BF16 = jnp.bfloat16
I32 = jnp.int32
HI = lax.Precision.HIGHEST

D_MODEL = 1024
N_HEADS = 8
HEAD_DIM = 64
N_KV = 2
HPG = N_HEADS // N_KV
ROT_DIM = HEAD_DIM // 4
ROT_HALF = ROT_DIM // 2
ROPE_THETA = 500000.0
CMP_LEN = 32
CMP_STRIDE = 16
SLC_LEN = 64
SLC_TOPN = 16
WINDOW = 512
SCALE = HEAD_DIM ** -0.5
NEG = -1e30
FORCE_INIT = 1e6
FORCE_LOCAL = 2e6
POOL_GROUPS = 4
POOL_WINDOWS = (2, 4, 8, 16)
POOL_WIDTH = 512
POOL_GW = POOL_WIDTH // POOL_GROUPS
POOL_HALO = 16
ATT_WIDTH = N_HEADS * HEAD_DIM
KV_WIDTH = 3 * 2 * N_KV * HEAD_DIM
GATE_NSA = 3 * N_HEADS
MERGE_GATES = 2 * D_MODEL
PEER_HEADS = 8
N_KEYS = 128
PEER_TOPK = 16
PEER_DK = 128
PEER_SEL = PEER_HEADS * PEER_TOPK
LN_EPS = 1e-5

LANES = 128
SUBLANES = 8
VMEM_LIMIT = 56 * 1024 * 1024

NT_DIMS = (((1,), (1,)), ((), ()))


def _cparams(sem):
    return pltpu.CompilerParams(dimension_semantics=sem, vmem_limit_bytes=VMEM_LIMIT)


def _ada_kernel(c_ref, w_ref, b_ref, o_ref):
    c = c_ref[...]
    ca = c * jax.nn.sigmoid(c)
    o_ref[...] = jnp.dot(ca, w_ref[...], precision=HI, preferred_element_type=F32) + b_ref[...]


def _ada_call(c_pad, w_ada, b_ada):
    depth = w_ada.shape[0]
    nblk = w_ada.shape[2] // D_MODEL
    rows = c_pad.shape[0]
    return pl.pallas_call(
        _ada_kernel,
        grid=(depth, nblk),
        in_specs=[
            pl.BlockSpec((rows, D_MODEL), lambda l, j: (0, 0)),
            pl.BlockSpec((None, D_MODEL, D_MODEL), lambda l, j: (l, 0, j)),
            pl.BlockSpec((None, 1, D_MODEL), lambda l, j: (l, 0, j)),
        ],
        out_specs=pl.BlockSpec((None, rows, D_MODEL), lambda l, j: (l, 0, j)),
        out_shape=jax.ShapeDtypeStruct((depth, rows, nblk * D_MODEL), F32),
        compiler_params=_cparams(("arbitrary", "arbitrary")),
        name="ada_mod",
    )(c_pad, w_ada, b_ada.reshape(depth, 1, -1))


IN_COLS = ATT_WIDTH + KV_WIDTH + POOL_WIDTH + MERGE_GATES + LANES


def _rope_lanes(z, rc, rs1, rs2):
    return z * rc + pltpu.roll(z, ROT_HALF, 1) * rs1 + pltpu.roll(z, LANES - ROT_HALF, 1) * rs2


def _inproj_kernel(x_ref, sc_ref, sh_ref, w_ref, rc_ref, rs1_ref, rs2_ref,
                   q_ref, kv_ref, p_ref, mrg_ref, gn_ref):
    h = x_ref[...] * (1.0 + sc_ref[...]) + sh_ref[...]
    a = jnp.dot(h.astype(BF16), w_ref[...], preferred_element_type=F32)
    rc, rs1, rs2 = rc_ref[...], rs1_ref[...], rs2_ref[...]
    for j in range(ATT_WIDTH // LANES):
        q_ref[:, j * LANES:(j + 1) * LANES] = _rope_lanes(a[:, j * LANES:(j + 1) * LANES], rc, rs1, rs2)
    for br in range(3):
        c0 = ATT_WIDTH + br * 2 * LANES
        k = a[:, c0:c0 + LANES]
        if br > 0:
            k = _rope_lanes(k, rc, rs1, rs2)
        kv_ref[:, br * 2 * LANES:br * 2 * LANES + LANES] = k
        kv_ref[:, br * 2 * LANES + LANES:(br + 1) * 2 * LANES] = a[:, c0 + LANES:c0 + 2 * LANES]
    c1 = ATT_WIDTH + KV_WIDTH
    p_ref[...] = a[:, c1:c1 + POOL_WIDTH]
    mrg_ref[...] = a[:, c1 + POOL_WIDTH:c1 + POOL_WIDTH + MERGE_GATES]
    gn_ref[...] = a[:, c1 + POOL_WIDTH + MERGE_GATES:]


def _inproj_call(x2, sc, sh, w, rc, rs1, rs2, seq, tm):
    n = x2.shape[0]
    tpb = seq // tm
    tok = lambda i: (i, 0)
    bat = lambda i: (i // tpb, 0, 0)
    pos = lambda i: (i % tpb, 0)
    full = lambda i: (0, 0)
    return pl.pallas_call(
        _inproj_kernel,
        grid=(n // tm,),
        in_specs=[
            pl.BlockSpec((tm, D_MODEL), tok),
            pl.BlockSpec((None, 1, D_MODEL), bat),
            pl.BlockSpec((None, 1, D_MODEL), bat),
            pl.BlockSpec((D_MODEL, IN_COLS), full),
            pl.BlockSpec((tm, LANES), pos),
            pl.BlockSpec((tm, LANES), pos),
            pl.BlockSpec((tm, LANES), pos),
        ],
        out_specs=[
            pl.BlockSpec((tm, ATT_WIDTH), tok),
            pl.BlockSpec((tm, KV_WIDTH), tok),
            pl.BlockSpec((tm, POOL_WIDTH), tok),
            pl.BlockSpec((tm, MERGE_GATES), tok),
            pl.BlockSpec((tm, LANES), tok),
        ],
        out_shape=[
            jax.ShapeDtypeStruct((n, ATT_WIDTH), F32),
            jax.ShapeDtypeStruct((n, KV_WIDTH), F32),
            jax.ShapeDtypeStruct((n, POOL_WIDTH), F32),
            jax.ShapeDtypeStruct((n, MERGE_GATES), F32),
            jax.ShapeDtypeStruct((n, LANES), F32),
        ],
        compiler_params=_cparams(("arbitrary",)),
        name="in_proj",
    )(x2, sc, sh, w, rc, rs1, rs2)


def _compress_kernel(z_ref, pet_ref, peb_ref, w1t_ref, w1b_ref, w2_ref, rc_ref, rs1_ref, rs2_ref, o_ref):
    z = z_ref[...]
    rows = z.shape[0]
    top = jnp.dot(z + pet_ref[...], w1t_ref[...], precision=HI, preferred_element_type=F32)
    bot = jnp.dot(z + peb_ref[...], w1b_ref[...], precision=HI, preferred_element_type=F32)
    pre = top + pltpu.roll(bot, rows - 1, 0)
    y = jnp.dot(jax.nn.gelu(pre), w2_ref[...], precision=HI, preferred_element_type=F32)
    o_ref[...] = _rope_lanes(y, rc_ref[...], rs1_ref[...], rs2_ref[...])


def _compress_call(z, pet, peb, w1t, w1b, w2, rc, rs1, rs2):
    b, _, rows, width = z.shape
    kvsel = lambda i, j: (j, 0, 0)
    return pl.pallas_call(
        _compress_kernel,
        grid=(b, 2),
        in_specs=[
            pl.BlockSpec((None, None, rows, width), lambda i, j: (i, j, 0, 0)),
            pl.BlockSpec((None, 1, width), kvsel),
            pl.BlockSpec((None, 1, width), kvsel),
            pl.BlockSpec((None, width, LANES), kvsel),
            pl.BlockSpec((None, width, LANES), kvsel),
            pl.BlockSpec((None, LANES, LANES), kvsel),
            pl.BlockSpec((None, rows, LANES), kvsel),
            pl.BlockSpec((None, rows, LANES), kvsel),
            pl.BlockSpec((None, rows, LANES), kvsel),
        ],
        out_specs=pl.BlockSpec((None, None, rows, LANES), lambda i, j: (i, j, 0, 0)),
        out_shape=jax.ShapeDtypeStruct((b, 2, rows, LANES), F32),
        compiler_params=_cparams(("arbitrary", "arbitrary")),
        name="compress",
    )(z, pet, peb, w1t, w1b, w2, rc, rs1, rs2)


def _cmpsel_kernel(q_ref, kc_ref, vc_ref, c2s_ref, o_ref, sel_ref, *, tq):
    t0 = pl.program_id(1) * tq
    kc = kc_ref[...]
    vc = vc_ref[...]
    rows = kc.shape[0]
    trow = t0 + lax.broadcasted_iota(I32, (tq, rows), 0)
    cend = lax.broadcasted_iota(I32, (tq, rows), 1) * CMP_STRIDE + (CMP_LEN - 1)
    vis = cend <= trow
    anyv = (trow[:, :1] >= CMP_LEN - 1).astype(F32)
    imp = jnp.zeros((tq, LANES), F32)
    for g in range(N_KV):
        kg = kc[:, g * HEAD_DIM:(g + 1) * HEAD_DIM].astype(BF16)
        vg = vc[:, g * HEAD_DIM:(g + 1) * HEAD_DIM].astype(BF16)
        psum = jnp.zeros((tq, rows), F32)
        for h in range(HPG):
            hd = g * HPG + h
            qh = q_ref[:, hd * HEAD_DIM:(hd + 1) * HEAD_DIM].astype(BF16)
            s = lax.dot_general(qh, kg, NT_DIMS, preferred_element_type=F32) * SCALE
            s = jnp.where(vis, s, NEG)
            e = jnp.exp(s - jnp.max(s, axis=-1, keepdims=True))
            p = e / jnp.sum(e, axis=-1, keepdims=True) * anyv
            o_ref[:, hd * HEAD_DIM:(hd + 1) * HEAD_DIM] = jnp.dot(
                p.astype(BF16), vg, preferred_element_type=F32)
            psum = psum + p
        imp = imp + jnp.dot(psum.astype(BF16), c2s_ref[g], preferred_element_type=F32)
    lane = lax.broadcasted_iota(I32, (tq, LANES), 1)
    blk = lane & (SLC_LEN - 1)
    cur = lax.shift_right_logical(t0 + lax.broadcasted_iota(I32, (tq, LANES), 0), 6)
    score = jnp.where(blk <= cur, imp, NEG)
    score = jnp.where(blk == 0, FORCE_INIT, score)
    score = jnp.where(blk == cur, FORCE_LOCAL, score)
    sc_t = score.T
    nblk = LANES // N_KV
    jrow = lax.broadcasted_iota(I32, (nblk, tq), 0)
    sel_parts = []
    for g in range(N_KV):
        sc = sc_t[g * nblk:(g + 1) * nblk]
        cnt = jnp.zeros((nblk, tq), I32)
        for k in range(nblk):
            rk = sc[k:k + 1, :]
            ge = (rk >= sc).astype(I32)
            gt = (rk > sc).astype(I32)
            cnt = cnt + jnp.where(jrow > k, ge, gt)
        sel_parts.append((cnt < SLC_TOPN).astype(F32))
    sel_ref[...] = jnp.concatenate(sel_parts, axis=0).T


def _cmpsel_call(q, kvc, c2s, seq, tq):
    n = q.shape[0]
    b = n // seq
    nq = seq // tq
    rows = kvc.shape[2]
    tok = lambda i, j: (i * nq + j, 0)
    return pl.pallas_call(
        functools.partial(_cmpsel_kernel, tq=tq),
        grid=(b, nq),
        in_specs=[
            pl.BlockSpec((tq, ATT_WIDTH), tok),
            pl.BlockSpec((None, None, rows, LANES), lambda i, j: (i, 0, 0, 0)),
            pl.BlockSpec((None, None, rows, LANES), lambda i, j: (i, 1, 0, 0)),
            pl.BlockSpec((N_KV, rows, LANES), lambda i, j: (0, 0, 0)),
        ],
        out_specs=[pl.BlockSpec((tq, ATT_WIDTH), tok), pl.BlockSpec((tq, LANES), tok)],
        out_shape=[jax.ShapeDtypeStruct((n, ATT_WIDTH), F32), jax.ShapeDtypeStruct((n, LANES), F32)],
        compiler_params=_cparams(("arbitrary", "arbitrary")),
        name="cmp_select",
    )(q, kvc, kvc, c2s)


def _slc_kernel(q_ref, k_ref, v_ref, sel_ref, o_ref, qs_sc, m_sc, l_sc, acc_sc, *, tq, tk):
    qi = pl.program_id(1)
    kt = pl.program_id(2)
    nk = pl.num_programs(2)

    @pl.when(kt == 0)
    def _init():
        for hd in range(N_HEADS):
            g, h = divmod(hd, HPG)
            qs_sc[g, h * tq:(h + 1) * tq, :] = (q_ref[:, hd * HEAD_DIM:(hd + 1) * HEAD_DIM] * SCALE).astype(BF16)
        m_sc[...] = jnp.full(m_sc.shape, NEG, F32)
        l_sc[...] = jnp.zeros(l_sc.shape, F32)
        acc_sc[...] = jnp.zeros(acc_sc.shape, F32)

    @pl.when(kt * tk <= qi * tq + (tq - 1))
    def _step():
        t = qi * tq + lax.broadcasted_iota(I32, (tq, tk), 0)
        kp = kt * tk + lax.broadcasted_iota(I32, (tq, tk), 1)
        causal = kp <= t
        nblk = LANES // N_KV
        jb = lax.broadcasted_iota(I32, (nblk, tk), 0)
        kb = lax.shift_right_logical(kt * tk + lax.broadcasted_iota(I32, (nblk, tk), 1), 6)
        expand = (jb == kb).astype(BF16)
        for g in range(N_KV):
            selg = sel_ref[:, g * nblk:(g + 1) * nblk].astype(BF16)
            member = jnp.dot(selg, expand, preferred_element_type=F32)
            bias = jnp.where(jnp.logical_and(causal, member > 0.5), 0.0, NEG)
            bias = jnp.concatenate([bias] * HPG, axis=0)
            kg = k_ref[:, g * HEAD_DIM:(g + 1) * HEAD_DIM].astype(BF16)
            vg = v_ref[:, g * HEAD_DIM:(g + 1) * HEAD_DIM].astype(BF16)
            s = lax.dot_general(qs_sc[g], kg, NT_DIMS, preferred_element_type=F32) + bias
            chunks = [s[:, c * LANES:(c + 1) * LANES] for c in range(tk // LANES)]
            mc = chunks[0]
            for x in chunks[1:]:
                mc = jnp.maximum(mc, x)
            m_old = m_sc[g]
            m_new = jnp.maximum(m_old, jnp.max(mc, axis=-1, keepdims=True))
            alpha = jnp.exp(m_old - m_new)
            ps = [jnp.exp(x - m_new) for x in chunks]
            lsum = ps[0]
            for x in ps[1:]:
                lsum = lsum + x
            l_sc[g] = alpha * l_sc[g] + lsum
            p = jnp.concatenate(ps, axis=1).astype(BF16)
            acc_sc[g] = alpha[:, :HEAD_DIM] * acc_sc[g] + jnp.dot(p, vg, preferred_element_type=F32)
            m_sc[g] = m_new

    @pl.when(kt == nk - 1)
    def _fin():
        for hd in range(N_HEADS):
            g, h = divmod(hd, HPG)
            l = jnp.sum(l_sc[g, h * tq:(h + 1) * tq, :], axis=-1, keepdims=True)
            o_ref[:, hd * HEAD_DIM:(hd + 1) * HEAD_DIM] = acc_sc[g, h * tq:(h + 1) * tq, :] / l


def _slc_call(q, kv, sel, seq, tq, tk):
    n = q.shape[0]
    b = n // seq
    nq = seq // tq
    nk = seq // tk
    tok = lambda i, j, k: (i * nq + j, 0)

    def key_map(col):
        def f(i, j, k):
            last = (j * tq + tq - 1) // tk
            return (i * nk + jnp.minimum(k, last), col)
        return f

    return pl.pallas_call(
        functools.partial(_slc_kernel, tq=tq, tk=tk),
        grid=(b, nq, nk),
        in_specs=[
            pl.BlockSpec((tq, ATT_WIDTH), tok),
            pl.BlockSpec((tk, LANES), key_map(2)),
            pl.BlockSpec((tk, LANES), key_map(3)),
            pl.BlockSpec((tq, LANES), tok),
        ],
        out_specs=pl.BlockSpec((tq, ATT_WIDTH), tok),
        out_shape=jax.ShapeDtypeStruct((n, ATT_WIDTH), F32),
        scratch_shapes=[
            pltpu.VMEM((N_KV, HPG * tq, HEAD_DIM), BF16),
            pltpu.VMEM((N_KV, HPG * tq, LANES), F32),
            pltpu.VMEM((N_KV, HPG * tq, LANES), F32),
            pltpu.VMEM((N_KV, HPG * tq, HEAD_DIM), F32),
        ],
        compiler_params=_cparams(("arbitrary", "arbitrary", "arbitrary")),
        name="slc_attn",
    )(q, kv, kv, sel)


def _win_kernel(q_ref, *refs, tq, nkb):
    k_refs = refs[:nkb]
    v_refs = refs[nkb:2 * nkb]
    o_ref = refs[2 * nkb]
    qi = pl.program_id(1)
    t = qi * tq + lax.broadcasted_iota(I32, (tq, tq), 0)
    col = lax.broadcasted_iota(I32, (tq, tq), 1)
    biases = []
    for j in range(nkb):
        kp = (qi - (nkb - 1) + j) * tq + col
        diff = t - kp
        ok = jnp.logical_and(jnp.logical_and(diff >= 0, diff < WINDOW), kp >= 0)
        biases.append(jnp.concatenate([jnp.where(ok, 0.0, NEG)] * HPG, axis=0))
    for g in range(N_KV):
        qs = jnp.concatenate(
            [(q_ref[:, (g * HPG + h) * HEAD_DIM:(g * HPG + h + 1) * HEAD_DIM] * SCALE).astype(BF16)
             for h in range(HPG)], axis=0)
        chunks = []
        for j in range(nkb):
            kg = k_refs[j][:, g * HEAD_DIM:(g + 1) * HEAD_DIM].astype(BF16)
            s = lax.dot_general(qs, kg, NT_DIMS, preferred_element_type=F32) + biases[j]
            chunks += [s[:, c * LANES:(c + 1) * LANES] for c in range(tq // LANES)]
        mc = chunks[0]
        for x in chunks[1:]:
            mc = jnp.maximum(mc, x)
        m = jnp.broadcast_to(jnp.max(mc, axis=-1, keepdims=True), mc.shape)
        ps = [jnp.exp(x - m) for x in chunks]
        lsum = ps[0]
        for x in ps[1:]:
            lsum = lsum + x
        l = jnp.sum(lsum, axis=-1, keepdims=True)
        per = tq // LANES
        o = None
        for j in range(nkb):
            vg = v_refs[j][:, g * HEAD_DIM:(g + 1) * HEAD_DIM].astype(BF16)
            pj = jnp.concatenate(ps[j * per:(j + 1) * per], axis=1).astype(BF16)
            oj = jnp.dot(pj, vg, preferred_element_type=F32)
            o = oj if o is None else o + oj
        o = o / l
        for h in range(HPG):
            hd = g * HPG + h
            o_ref[:, hd * HEAD_DIM:(hd + 1) * HEAD_DIM] = o[h * tq:(h + 1) * tq]


def _win_call(q, kv, seq, tq):
    n = q.shape[0]
    b = n // seq
    nq = seq // tq
    nkb = WINDOW // tq + 1
    tok = lambda i, j: (i * nq + j, 0)

    def key_map(col, back):
        return lambda i, j: (i * nq + jnp.maximum(j - back, 0), col)

    k_specs = [pl.BlockSpec((tq, LANES), key_map(4, nkb - 1 - jj)) for jj in range(nkb)]
    v_specs = [pl.BlockSpec((tq, LANES), key_map(5, nkb - 1 - jj)) for jj in range(nkb)]
    return pl.pallas_call(
        functools.partial(_win_kernel, tq=tq, nkb=nkb),
        grid=(b, nq),
        in_specs=[pl.BlockSpec((tq, ATT_WIDTH), tok)] + k_specs + v_specs,
        out_specs=pl.BlockSpec((tq, ATT_WIDTH), tok),
        out_shape=jax.ShapeDtypeStruct((n, ATT_WIDTH), F32),
        compiler_params=_cparams(("arbitrary", "arbitrary")),
        name="win_attn",
    )(q, *([kv] * (2 * nkb)))


def _pool_kernel(p_ref, prev_ref, w_ref, sc_ref, o_ref, *, ts):
    i = pl.program_id(1)
    x = p_ref[...]
    prev = prev_ref[...] * (i > 0).astype(F32)
    xe = jnp.concatenate([prev, x], axis=0)
    t1 = (i * ts + 1 + lax.broadcasted_iota(I32, (ts, POOL_GW), 0)).astype(F32)
    for g, w in enumerate(POOL_WINDOWS):
        a = xe[:, g * POOL_GW:(g + 1) * POOL_GW]
        off = POOL_HALO
        span = 1
        while span < w:
            a = a[span:] + a[:-span]
            off -= span
            span *= 2
        sums = a[off:off + ts]
        cnt = jnp.minimum(t1, float(w))
        pooled = sums / cnt - x[:, g * POOL_GW:(g + 1) * POOL_GW]
        y = jnp.dot(pooled.astype(BF16), w_ref[g], preferred_element_type=F32)
        o_ref[:, g * POOL_GW:(g + 1) * POOL_GW] = y * sc_ref[:, g * POOL_GW:(g + 1) * POOL_GW]


def _pool_call(p_in, w_pool, pool_scale, seq, ts):
    n = p_in.shape[0]
    b = n // seq
    nt = seq // ts
    hpt = ts // POOL_HALO
    tok = lambda i, j: (i * nt + j, 0)
    return pl.pallas_call(
        functools.partial(_pool_kernel, ts=ts),
        grid=(b, nt),
        in_specs=[
            pl.BlockSpec((ts, POOL_WIDTH), tok),
            pl.BlockSpec((POOL_HALO, POOL_WIDTH), lambda i, j: (i * nt * hpt + jnp.maximum(j * hpt - 1, 0), 0)),
            pl.BlockSpec((POOL_GROUPS, POOL_GW, POOL_GW), lambda i, j: (0, 0, 0)),
            pl.BlockSpec((1, POOL_WIDTH), lambda i, j: (0, 0)),
        ],
        out_specs=pl.BlockSpec((ts, POOL_WIDTH), tok),
        out_shape=jax.ShapeDtypeStruct((n, POOL_WIDTH), F32),
        compiler_params=_cparams(("arbitrary", "arbitrary")),
        name="pool_mix",
    )(p_in, p_in, w_pool, pool_scale)


def _layer_norm(z, g, b):
    mu = jnp.mean(z, axis=-1, keepdims=True)
    zc = z - mu
    var = jnp.mean(zc * zc, axis=-1, keepdims=True)
    return zc * lax.rsqrt(var + LN_EPS) * g + b


def _merge_kernel(oc_ref, os_ref, ow_ref, gn_ref, op_ref, gm_ref, x_ref, g1_ref, lng_ref, lnb_ref,
                  wl_ref, wo_ref, eb_ref, o_ref, *, alpha):
    gate = jax.nn.sigmoid(gn_ref[...])
    gate_hi = gate.astype(BF16)
    gate_lo = (gate - gate_hi.astype(F32)).astype(BF16)
    branches = (oc_ref, os_ref, ow_ref)
    oatt = None
    for br in range(3):
        gx = (jnp.dot(gate_hi, eb_ref[br], preferred_element_type=F32)
              + jnp.dot(gate_lo, eb_ref[br], preferred_element_type=F32))
        term = gx * branches[br][...]
        oatt = term if oatt is None else oatt + term
    la = jnp.dot(oatt.astype(BF16), wl_ref[0], preferred_element_type=F32)
    lb = jnp.dot(op_ref[...].astype(BF16), wl_ref[1], preferred_element_type=F32)
    gm = jax.nn.sigmoid(gm_ref[...])
    merged = gm[:, :D_MODEL] * la + gm[:, D_MODEL:] * lb
    y = jnp.dot(merged.astype(BF16), wo_ref[...], preferred_element_type=F32)
    z = alpha * x_ref[...] + g1_ref[...] * y
    o_ref[...] = _layer_norm(z, lng_ref[...], lnb_ref[...])


def _merge_call(oc, osl, ow, gn, op, gm, x2, g1, lng, lnb, wl, wo, eb, seq, tm, alpha):
    n = x2.shape[0]
    tpb = seq // tm
    tok = lambda i: (i, 0)
    bat = lambda i: (i // tpb, 0, 0)
    return pl.pallas_call(
        functools.partial(_merge_kernel, alpha=alpha),
        grid=(n // tm,),
        in_specs=[
            pl.BlockSpec((tm, ATT_WIDTH), tok), pl.BlockSpec((tm, ATT_WIDTH), tok),
            pl.BlockSpec((tm, ATT_WIDTH), tok), pl.BlockSpec((tm, LANES), tok),
            pl.BlockSpec((tm, POOL_WIDTH), tok), pl.BlockSpec((tm, MERGE_GATES), tok),
            pl.BlockSpec((tm, D_MODEL), tok),
            pl.BlockSpec((None, 1, D_MODEL), bat),
            pl.BlockSpec((1, D_MODEL), lambda i: (0, 0)), pl.BlockSpec((1, D_MODEL), lambda i: (0, 0)),
            pl.BlockSpec((2, ATT_WIDTH, D_MODEL), lambda i: (0, 0, 0)),
            pl.BlockSpec((D_MODEL, D_MODEL), lambda i: (0, 0)),
            pl.BlockSpec((3, LANES, ATT_WIDTH), lambda i: (0, 0, 0)),
        ],
        out_specs=pl.BlockSpec((tm, D_MODEL), tok),
        out_shape=jax.ShapeDtypeStruct((n, D_MODEL), F32),
        compiler_params=_cparams(("arbitrary",)),
        name="merge_out",
    )(oc, osl, ow, gn, op, gm, x2, g1, lng, lnb, wl, wo, eb)


def _extract_top(cur, ids, n):
    rows = cur.shape[0]
    rio = lax.broadcasted_iota(I32, cur.shape, 0)
    vals, outs = [], []
    for _ in range(n):
        m = jnp.max(cur, axis=0, keepdims=True)
        pos = jnp.min(jnp.where(cur == m, rio, rows), axis=0, keepdims=True)
        hit = rio == pos
        vals.append(m)
        outs.append(pos if ids is None else jnp.max(jnp.where(hit, ids, -1), axis=0, keepdims=True))
        cur = jnp.where(hit, -jnp.inf, cur)
    return jnp.concatenate(vals, axis=0), jnp.concatenate(outs, axis=0)


def _route_kernel(x_ref, sc_ref, sh_ref, wq_ref, keys_ref, h_ref, hv_ref, e_ref, g_ref,
                  st_sc, ts_sc, ti_sc, eo_sc, go_sc):
    h = x_ref[...] * (1.0 + sc_ref[...]) + sh_ref[...]
    h_ref[...] = h
    for r in range(SUBLANES):
        hv_ref[:, r, :] = h[:, r * LANES:(r + 1) * LANES]
    qp = jnp.dot(h.astype(BF16), wq_ref[...], preferred_element_type=F32).astype(BF16)
    half = PEER_DK // 2
    for hp in range(2 * PEER_HEADS):
        st_sc[hp] = lax.dot_general(keys_ref[hp], qp[:, hp * half:(hp + 1) * half], NT_DIMS,
                                    preferred_element_type=F32)

    def half_body(hp, carry):
        vals, ids = _extract_top(st_sc[hp], None, PEER_TOPK)
        ts_sc[hp] = vals
        ti_sc[hp] = ids
        return carry

    lax.fori_loop(0, 2 * PEER_HEADS, half_body, 0)

    def head_body(hh, carry):
        s1, s2 = ts_sc[2 * hh], ts_sc[2 * hh + 1]
        i1, i2 = ti_sc[2 * hh], ti_sc[2 * hh + 1]
        brow = lax.broadcasted_iota(I32, (SUBLANES, s1.shape[1]), 0)
        cands = [s1[0:1, :] + s2]
        cidxs = [i1[0:1, :] * N_KEYS + i2]
        for a in range(1, SUBLANES):
            ok = brow < PEER_TOPK // (a + 1)
            cands.append(jnp.where(ok, s1[a:a + 1, :] + s2[:SUBLANES], -jnp.inf))
            cidxs.append(i1[a:a + 1, :] * N_KEYS + i2[:SUBLANES])
        cands.append(s1[SUBLANES:] + s2[0:1, :])
        cidxs.append(i1[SUBLANES:] * N_KEYS + i2[0:1, :])
        sv, ei = _extract_top(jnp.concatenate(cands, axis=0), jnp.concatenate(cidxs, axis=0), PEER_TOPK)
        ex = jnp.exp(sv - sv[0:1, :])
        go_sc[hh] = ex / jnp.sum(ex, axis=0, keepdims=True)
        eo_sc[hh] = ei.astype(F32)
        return carry

    lax.fori_loop(0, PEER_HEADS, head_body, 0)
    e_all = jnp.concatenate([eo_sc[hh] for hh in range(PEER_HEADS)], axis=0)
    g_all = jnp.concatenate([go_sc[hh] for hh in range(PEER_HEADS)], axis=0)
    e_ref[...] = e_all.T.astype(I32) * (D_MODEL // 2 // LANES)
    g_ref[...] = g_all.T


def _route_call(x2, sc, sh, wq, keys, seq, tt):
    n = x2.shape[0]
    tpb = seq // tt
    tok = lambda i: (i, 0)
    bat = lambda i: (i // tpb, 0, 0)
    nhp = 2 * PEER_HEADS
    return pl.pallas_call(
        _route_kernel,
        grid=(n // tt,),
        in_specs=[
            pl.BlockSpec((tt, D_MODEL), tok),
            pl.BlockSpec((None, 1, D_MODEL), bat),
            pl.BlockSpec((None, 1, D_MODEL), bat),
            pl.BlockSpec((D_MODEL, PEER_HEADS * PEER_DK), lambda i: (0, 0)),
            pl.BlockSpec((nhp, N_KEYS, PEER_DK // 2), lambda i: (0, 0, 0)),
        ],
        out_specs=[pl.BlockSpec((tt, D_MODEL), tok), pl.BlockSpec((tt, SUBLANES, LANES), lambda i: (i, 0, 0)),
                   pl.BlockSpec((tt, PEER_SEL), tok), pl.BlockSpec((tt, PEER_SEL), tok)],
        out_shape=[jax.ShapeDtypeStruct((n, D_MODEL), F32), jax.ShapeDtypeStruct((n, SUBLANES, LANES), F32),
                   jax.ShapeDtypeStruct((n, PEER_SEL), I32), jax.ShapeDtypeStruct((n, PEER_SEL), F32)],
        scratch_shapes=[
            pltpu.VMEM((nhp, N_KEYS, tt), F32),
            pltpu.VMEM((nhp, PEER_TOPK, tt), F32),
            pltpu.VMEM((nhp, PEER_TOPK, tt), I32),
            pltpu.VMEM((PEER_HEADS, PEER_TOPK, tt), F32),
            pltpu.VMEM((PEER_HEADS, PEER_TOPK, tt), F32),
        ],
        compiler_params=_cparams(("arbitrary",)),
        name="peer_route",
    )(x2, sc, sh, wq, keys)


HALF_ROWS = SUBLANES // 2
HI_MASK = -65536
PAIR_TILES = PEER_SEL // 2
PAIR_ROWS = PAIR_TILES * SUBLANES


def _load_two_experts(tab_ref, ra, rb):
    wa = tab_ref[pl.ds(pl.multiple_of(ra, HALF_ROWS), HALF_ROWS), :]
    wb = tab_ref[pl.ds(pl.multiple_of(rb, HALF_ROWS), HALF_ROWS), :]
    w2 = jnp.concatenate([wa, wb], axis=0)
    return lax.bitcast_convert_type(w2 << 16, F32), lax.bitcast_convert_type(w2 & HI_MASK, F32)


def _fold_pairs(vs):
    row = lax.broadcasted_iota(I32, (SUBLANES, LANES), 0)
    shift = HALF_ROWS // 2
    while len(vs) > 1:
        low = (row & shift) == 0
        vs = [jnp.where(low, a + pltpu.roll(a, SUBLANES - shift, 0), b + pltpu.roll(b, shift, 0))
              for a, b in zip(vs[0::2], vs[1::2])]
        shift //= 2
    return vs[0]


def _fold_order():
    idx = [[2 * i if r < HALF_ROWS else 2 * i + 1 for r in range(SUBLANES)] for i in range(HALF_ROWS)]
    shift = HALF_ROWS // 2
    while len(idx) > 1:
        idx = [[a[r] if (r & shift) == 0 else b[r] for r in range(SUBLANES)]
               for a, b in zip(idx[0::2], idx[1::2])]
        shift //= 2
    return idx[0]


def _peer_u_kernel(e_sm, tab_ref, hv_ref, gate_ref, coef_ref, *, tt):
    row = lax.broadcasted_iota(I32, (SUBLANES, LANES), 0)
    low = row < HALF_ROWS
    eye = (lax.broadcasted_iota(I32, (PEER_SEL, LANES), 0) ==
           lax.broadcasted_iota(I32, (PEER_SEL, LANES), 1))
    order = _fold_order()

    def finish(t, part):
        col = jnp.sum(part, axis=-1, keepdims=True)
        a_row = jnp.sum(jnp.where(eye, col, 0.0), axis=0, keepdims=True)
        coef_ref[t] = gate_ref[pl.ds(t, 1), :] * jax.nn.gelu(a_row)

    def token(t, part_prev):
        finish(jnp.maximum(t - 1, 0), part_prev)
        hv = hv_ref[t]
        hsw = pltpu.roll(hv, HALF_ROWS, 0)
        h_lo = jnp.where(low, hv, hsw)
        h_hi = jnp.where(low, hsw, hv)
        folded = []
        for j in range(PEER_SEL // SUBLANES):
            prods = []
            for i in range(HALF_ROWS):
                ka = j * SUBLANES + order.index(2 * i)
                kb = j * SUBLANES + order.index(2 * i + 1)
                lo, hi = _load_two_experts(tab_ref, e_sm[t, ka], e_sm[t, kb])
                prods.append(lo * h_lo + hi * h_hi)
            folded.append(_fold_pairs(prods))
        return jnp.concatenate(folded, axis=0)

    last = lax.fori_loop(0, tt, token, jnp.zeros((PEER_SEL, LANES), F32))
    finish(tt - 1, last)


def _peer_v_kernel(e_sm, coef_ref, tab_ref, y_ref, cv_sc, *, tt, nacc):
    row = lax.broadcasted_iota(I32, (SUBLANES, LANES), 0)
    low = row < HALF_ROWS
    rr = lax.broadcasted_iota(I32, (PAIR_ROWS, LANES), 0)
    kk = lax.broadcasted_iota(I32, (PAIR_ROWS, LANES), 1)
    onehot = (kk == 2 * (rr >> 3) + ((rr >> 2) & 1)).astype(F32)
    ones = jnp.ones((LANES, LANES), BF16)

    def expand(t, slot):
        lhs = (onehot * coef_ref[t]).astype(BF16)
        cv_sc[slot] = jnp.dot(lhs, ones, preferred_element_type=F32)

    def process(t, slot):
        acc_lo = [jnp.zeros((SUBLANES, LANES), F32) for _ in range(nacc)]
        acc_hi = [jnp.zeros((SUBLANES, LANES), F32) for _ in range(nacc)]
        for j in range(PAIR_TILES):
            grp = e_sm.at[t, pl.ds(2 * j // SUBLANES * SUBLANES, SUBLANES)]
            lo, hi = _load_two_experts(tab_ref, grp[2 * j % SUBLANES], grp[(2 * j + 1) % SUBLANES])
            cv = cv_sc[slot, j * SUBLANES:(j + 1) * SUBLANES, :]
            acc_lo[j % nacc] = acc_lo[j % nacc] + cv * lo
            acc_hi[j % nacc] = acc_hi[j % nacc] + cv * hi
        a_lo, a_hi = acc_lo[0], acc_hi[0]
        for i in range(1, nacc):
            a_lo = a_lo + acc_lo[i]
            a_hi = a_hi + acc_hi[i]
        a_lo = a_lo + pltpu.roll(a_lo, HALF_ROWS, 0)
        a_hi = a_hi + pltpu.roll(a_hi, HALF_ROWS, 0)
        y_ref[t] = jnp.where(low, a_lo, a_hi)

    expand(0, 0)

    def two_tokens(i, carry):
        t = 2 * i
        expand(t + 1, 1)
        process(t, 0)
        expand(jnp.minimum(t + 2, tt - 1), 0)
        process(t + 1, 1)
        return carry

    lax.fori_loop(0, tt // 2, two_tokens, 0)


def _resident_table_spec(tab):
    return pl.BlockSpec(tab.shape, lambda i: (0, 0), pipeline_mode=pl.Buffered(1))


def _peer_u_call(erow, tab, hv, gate, tt, n):
    return pl.pallas_call(
        functools.partial(_peer_u_kernel, tt=tt),
        grid=(n // tt,),
        in_specs=[
            pl.BlockSpec((tt, PEER_SEL), lambda i: (i, 0), memory_space=pltpu.SMEM),
            _resident_table_spec(tab),
            pl.BlockSpec((tt, SUBLANES, LANES), lambda i: (i, 0, 0)),
            pl.BlockSpec((tt, PEER_SEL), lambda i: (i, 0)),
        ],
        out_specs=pl.BlockSpec((tt, 1, PEER_SEL), lambda i: (i, 0, 0)),
        out_shape=jax.ShapeDtypeStruct((n, 1, PEER_SEL), F32),
        compiler_params=_cparams(("arbitrary",)),
        name="peer_u",
    )(erow, tab, hv, gate)


def _peer_v_call(erow, coef3, tab, tt, n):
    assert tt % 2 == 0
    return pl.pallas_call(
        functools.partial(_peer_v_kernel, tt=tt, nacc=4),
        grid=(n // tt,),
        in_specs=[
            pl.BlockSpec((tt, PEER_SEL), lambda i: (i, 0), memory_space=pltpu.SMEM),
            pl.BlockSpec((tt, 1, PEER_SEL), lambda i: (i, 0, 0)),
            _resident_table_spec(tab),
        ],
        out_specs=pl.BlockSpec((tt, SUBLANES, LANES), lambda i: (i, 0, 0)),
        out_shape=jax.ShapeDtypeStruct((n, SUBLANES, LANES), F32),
        scratch_shapes=[pltpu.VMEM((2, PAIR_ROWS, LANES), F32)],
        compiler_params=_cparams(("arbitrary",)),
        name="peer_v",
    )(erow, coef3, tab)


def _resln_kernel(x_ref, ytc_ref, ysc_ref, g_ref, lng_ref, lnb_ref, o_ref, *, alpha, tc_tiles):
    y_tc = jnp.concatenate([ytc_ref[:, r, :] for r in range(SUBLANES)], axis=1)
    y = jnp.where(pl.program_id(0) < tc_tiles, y_tc, ysc_ref[...])
    z = alpha * x_ref[...] + g_ref[...] * y
    o_ref[...] = _layer_norm(z, lng_ref[...], lnb_ref[...])


def _resln_call(x2, y_tc, y_sc, g2, lng, lnb, seq, tm, alpha):
    n = x2.shape[0]
    n_tc = y_tc.shape[0]
    assert n_tc % tm == 0 and n_tc > 0
    tc_tiles = n_tc // tm
    if y_sc is None:
        y_sc = jnp.zeros((tm, D_MODEL), F32)
    tpb = seq // tm
    tok = lambda i: (i, 0)
    return pl.pallas_call(
        functools.partial(_resln_kernel, alpha=alpha, tc_tiles=tc_tiles),
        grid=(n // tm,),
        in_specs=[
            pl.BlockSpec((tm, D_MODEL), tok),
            pl.BlockSpec((tm, SUBLANES, LANES), lambda i: (jnp.minimum(i, tc_tiles - 1), 0, 0)),
            pl.BlockSpec((tm, D_MODEL), lambda i: (jnp.maximum(i - tc_tiles, 0), 0)),
            pl.BlockSpec((None, 1, D_MODEL), lambda i: (i // tpb, 0, 0)),
            pl.BlockSpec((1, D_MODEL), lambda i: (0, 0)), pl.BlockSpec((1, D_MODEL), lambda i: (0, 0)),
        ],
        out_specs=pl.BlockSpec((tm, D_MODEL), tok),
        out_shape=jax.ShapeDtypeStruct((n, D_MODEL), F32),
        compiler_params=_cparams(("arbitrary",)),
        name="res_ln",
    )(x2, y_tc, y_sc, g2, lng, lnb)


def _rope_lane_tables(pos):
    inv = ROPE_THETA ** (-jnp.arange(0, ROT_DIM, 2, dtype=F32) / ROT_DIM)
    ang = pos.astype(F32)[:, None] * inv[None, :]
    cos, sin = jnp.cos(ang), jnp.sin(ang)
    lane = np.arange(LANES) % HEAD_DIM
    fidx = lane % ROT_HALF
    first = jnp.asarray(lane < ROT_HALF)
    second = jnp.asarray((lane >= ROT_HALF) & (lane < ROT_DIM))
    rot = jnp.asarray(lane < ROT_DIM)
    cl, sl = cos[:, fidx], sin[:, fidx]
    rc = jnp.where(rot, cl, 1.0)
    rs1 = jnp.where(second, sl, 0.0)
    rs2 = jnp.where(first, -sl, 0.0)
    return rc, rs1, rs2


def _pack_kernel(t_ref, o_ref):
    x = t_ref[...]
    half = x.shape[1] // 2
    lo = lax.bitcast_convert_type(x[:, :half].astype(BF16).astype(F32), I32)
    hi = lax.bitcast_convert_type(x[:, half:].astype(BF16).astype(F32), I32)
    w = lax.shift_right_logical(lo, jnp.full_like(lo, 16)) | hi
    rows = x.shape[0]
    per = half // LANES
    for r in range(per):
        o_ref[pl.ds(r, rows, stride=per), :] = w[:, r * LANES:(r + 1) * LANES]


def _pack_table(tab, te=512):
    e, d = tab.shape
    per = d // 2 // LANES
    return pl.pallas_call(
        _pack_kernel, grid=(e // te,),
        in_specs=[pl.BlockSpec((te, d), lambda i: (i, 0))],
        out_specs=pl.BlockSpec((te * per, LANES), lambda i: (i, 0)),
        out_shape=jax.ShapeDtypeStruct((e * per, LANES), I32),
        compiler_params=_cparams(("arbitrary",)), name="pack_table",
    )(tab)


def _cmp_to_slc_wide(rows, n_slc):
    st = np.arange(rows) * CMP_STRIDE
    js = np.arange(n_slc) * SLC_LEN
    ov = np.minimum(st[:, None] + CMP_LEN, js[None, :] + SLC_LEN) - np.maximum(st[:, None], js[None, :])
    c2s = np.maximum(ov, 0).astype(np.float32) / CMP_STRIDE
    wide = np.zeros((N_KV, rows, LANES), np.float32)
    nblk = LANES // N_KV
    for g in range(N_KV):
        wide[g, :, g * nblk:g * nblk + n_slc] = c2s
    return jnp.asarray(wide)


def _gate_expanders():
    eb = np.zeros((3, LANES, ATT_WIDTH), np.float32)
    for hd in range(N_HEADS):
        for br in range(3):
            eb[br, hd * 3 + br, hd * HEAD_DIM:(hd + 1) * HEAD_DIM] = 1.0
    return jnp.asarray(eb)


class _Consts:
    def __init__(self, seq):
        self.rows = seq // CMP_STRIDE
        self.rope = _rope_lane_tables(jnp.arange(seq))
        cpos = jnp.arange(self.rows) * CMP_STRIDE + CMP_LEN - 1
        crope = _rope_lane_tables(cpos)
        ident = (jnp.ones_like(crope[0]), jnp.zeros_like(crope[0]), jnp.zeros_like(crope[0]))
        self.crope = tuple(jnp.stack([a, b]) for a, b in zip(crope, ident))
        self.c2s = _cmp_to_slc_wide(self.rows, seq // SLC_LEN)
        self.eb = _gate_expanders()


def _token_mixer_layer(x2, sc1, sh1, g1, w_in, cmp_pe, cmp_w1, cmp_w2, w_pool, pool_scale, w_lift, w_o,
                       lng, lnb, cst, bsz, seq, alpha):
    d = D_MODEL
    rows = cst.rows
    s1 = ATT_WIDTH + KV_WIDTH
    s2 = s1 + GATE_NSA
    s3 = s2 + POOL_WIDTH
    w_gate = jnp.pad(w_in[:, s1:s2], ((0, 0), (0, LANES - GATE_NSA)))
    w_all = jnp.concatenate([w_in[:, :s1], w_in[:, s2:s3], w_in[:, s3:], w_gate], axis=1).astype(BF16)
    q, kv, p_in, g_mrg, g_nsa = _inproj_call(x2, sc1, sh1, w_all, *cst.rope, seq, 256)

    eye_g = jnp.eye(N_KV, dtype=F32)
    zc = kv[:, :2 * LANES].reshape(bsz, rows, CMP_STRIDE, 2, LANES)
    zc = jnp.transpose(zc, (0, 3, 1, 2, 4)).reshape(bsz, 2, rows, CMP_STRIDE * LANES)
    w1x = jnp.einsum('klde,gh->klgdhe', cmp_w1.reshape(2, CMP_LEN, HEAD_DIM, HEAD_DIM), eye_g)
    w1x = w1x.reshape(2, CMP_LEN * LANES, LANES)
    half = CMP_STRIDE * LANES
    pex = jnp.broadcast_to(cmp_pe[:, :, None, :], (2, CMP_LEN, N_KV, HEAD_DIM)).reshape(2, 1, CMP_LEN * LANES)
    w2x = jnp.einsum('kef,gh->kgehf', cmp_w2, eye_g).reshape(2, LANES, LANES)
    kvc = _compress_call(zc, pex[:, :, :half], pex[:, :, half:], w1x[:, :half], w1x[:, half:], w2x,
                         *cst.crope)

    o_cmp, sel = _cmpsel_call(q, kvc, cst.c2s.astype(BF16), seq, 256)
    o_slc = _slc_call(q, kv, sel, seq, 256, 512)
    o_win = _win_call(q, kv, seq, 256)
    o_pool = _pool_call(p_in, w_pool.astype(BF16), pool_scale.reshape(1, -1), seq, 512)
    return _merge_call(o_cmp, o_slc, o_win, g_nsa, o_pool, g_mrg, x2, g1,
                       lng.reshape(1, d), lnb.reshape(1, d),
                       w_lift.astype(BF16), w_o.astype(BF16), cst.eb.astype(BF16), seq, 256, alpha)


SC_CORES = 2
SC_SUBCORES = 16
SC_LANES = 16
SC_WORKERS = SC_CORES * SC_SUBCORES
SC_ROWS = 32
SC_TOKEN_SHARE = 21
SC_BLOCK = 256


def _sc_params():
    cp = pltpu.CompilerParams()
    if "needs_layout_passes" in pltpu.CompilerParams.__dataclass_fields__:
        cp = dataclasses.replace(cp, needs_layout_passes=False)
    return cp


def _sc_token_stream(tabs, erow_hbm, sides, out_hbm, out_bufs, tok0, tpw, base, idx_bufs, rows_v, sems,
                     begin_fn, chunk_fn):
    per_tab = PEER_SEL // SC_ROWS
    nch = per_tab * len(tabs)
    assert nch % 2 == 0

    def idx_copy(t, s):
        return pltpu.make_async_copy(erow_hbm.at[tok0 + base + t], idx_bufs[s], sems.at[2 + s])

    def out_copy(t, s):
        return pltpu.make_async_copy(out_bufs[s], out_hbm.at[base + t], sems.at[4 + s])

    def side_copy(k, t, s):
        hbm, bufs = sides[k]
        return pltpu.make_async_copy(hbm.at[tok0 + base + t], bufs[s], sems.at[6 + 2 * k + s])

    def row_copy(s, c):
        rows = idx_bufs[s].at[pl.ds((c % per_tab) * SC_ROWS, SC_ROWS)]
        return pltpu.make_async_copy(tabs[c // per_tab].at[rows], rows_v.at[c % 2], sems.at[c % 2])

    def to_expert_ids(s):
        shift = jnp.full((SC_LANES,), (D_MODEL // 2 // LANES).bit_length() - 1, I32)
        for j in range(PEER_SEL // SC_LANES):
            sl = pl.ds(j * SC_LANES, SC_LANES)
            idx_bufs[s][sl] = lax.shift_right_logical(idx_bufs[s][sl], shift)

    idx_copy(0, 0).start()
    for k in range(len(sides)):
        side_copy(k, 0, 0).start()
    idx_copy(0, 0).wait()
    to_expert_ids(0)
    row_copy(0, 0).start()

    @pl.loop(0, tpw // 2)
    def _(i):
        for slot in (0, 1):
            other = 1 - slot
            t = 2 * i + slot
            tn = jnp.minimum(t + 1, tpw - 1)
            idx_copy(tn, other).start()
            for k in range(len(sides)):
                side_copy(k, tn, other).start()
            for k in range(len(sides)):
                side_copy(k, t, slot).wait()

            @pl.when(i > 0)
            def _():
                out_copy(t, slot).wait()

            begin_fn(slot)
            for c in range(nch):
                if c + 1 < nch:
                    row_copy(slot, c + 1).start()
                else:
                    idx_copy(tn, other).wait()
                    to_expert_ids(other)
                    row_copy(other, 0).start()
                row_copy(slot, c).wait()
                chunk_fn(c, rows_v.at[c % 2], slot)
            out_copy(t, slot).start()

    row_copy(0, 0).wait()
    for k in range(len(sides)):
        side_copy(k, 0, 0).wait()
    out_copy(0, 0).wait()
    out_copy(0, 1).wait()


GELU_C0 = 0.7978845608028654
GELU_C1 = 0.044715


def _sc_peer_call(erow, hrows, gate, utab, vtab, tok0, m):
    d = utab.shape[1]
    assert m % (2 * SC_WORKERS) == 0 and d % SC_BLOCK == 0
    tpw = m // SC_WORKERS
    per_tab = PEER_SEL // SC_ROWS
    nvec = SC_BLOCK // SC_LANES
    mesh = plsc.VectorSubcoreMesh(core_axis_name="c", subcore_axis_name="s")
    vec = lambda nelem, dt: pltpu.VMEM((nelem,), dt)

    @functools.partial(
        pl.kernel, mesh=mesh, out_type=jax.ShapeDtypeStruct((m, d), F32),
        scratch_types=[vec(PEER_SEL, I32), vec(PEER_SEL, I32), vec(d, F32), vec(d, F32),
                       vec(PEER_SEL, F32), vec(PEER_SEL, F32), vec(d, F32), vec(d, F32),
                       pltpu.VMEM((2, SC_ROWS, d), F32), vec(PEER_SEL * SC_LANES, F32), vec(PEER_SEL, F32),
                       pltpu.SemaphoreType.DMA((10,))],
        compiler_params=_sc_params(), name="sc_peer")
    def run(utab_hbm, vtab_hbm, erow_hbm, h_hbm, gate_hbm, out_hbm,
            idx_a, idx_b, h_a, h_b, gate_a, gate_b, acc_a, acc_b, rows_v, part_v, coef_v, sems):
        wid = lax.axis_index("s") * SC_CORES + lax.axis_index("c")
        zero = jnp.zeros((SC_LANES,), F32)
        lane_iota = lax.iota(I32, SC_LANES)
        hbufs, gbufs, accs_v = (h_a, h_b), (gate_a, gate_b), (acc_a, acc_b)

        def begin(slot):
            for j in range(PEER_SEL):
                part_v[pl.ds(j * SC_LANES, SC_LANES)] = zero
            for j in range(d // SC_LANES):
                accs_v[slot][pl.ds(j * SC_LANES, SC_LANES)] = zero

        def coefficients(slot):
            for g in range(PEER_SEL // SC_LANES):
                rowbase = (g * SC_LANES + lane_iota) * SC_LANES
                a = plsc.load_gather(part_v, [rowbase])
                for l in range(1, SC_LANES):
                    a = a + plsc.load_gather(part_v, [rowbase + l])
                z = GELU_C0 * (a + GELU_C1 * a * a * a)
                th = 1.0 - 2.0 / (jnp.exp(2.0 * z) + 1.0)
                sl = pl.ds(g * SC_LANES, SC_LANES)
                coef_v[sl] = gbufs[slot][sl] * (0.5 * a * (1.0 + th))

        def chunk(c, buf, slot):
            if c < per_tab:
                @pl.loop(0, d // SC_BLOCK)
                def _(lb):
                    lane0 = pl.multiple_of(lb * SC_BLOCK, SC_BLOCK)
                    hs = [hbufs[slot][pl.ds(lane0 + q * SC_LANES, SC_LANES)] for q in range(nvec)]

                    @plsc.parallel_loop(0, SC_ROWS, unroll=2)
                    def _(r):
                        ps = [hs[q] * buf[r, pl.ds(lane0 + q * SC_LANES, SC_LANES)] for q in range(nvec)]
                        while len(ps) > 1:
                            ps = [x + y for x, y in zip(ps[0::2], ps[1::2])]
                        row = pl.multiple_of((c * SC_ROWS + r) * SC_LANES, SC_LANES)
                        plsc.addupdate(part_v.at[pl.ds(row, SC_LANES)], ps[0])

                if c == per_tab - 1:
                    coefficients(slot)
            else:
                cc = c - per_tab
                acc_v = accs_v[slot]

                @pl.loop(0, d // SC_BLOCK)
                def _(lb):
                    lane0 = pl.multiple_of(lb * SC_BLOCK, SC_BLOCK)
                    accs = tuple(acc_v[pl.ds(lane0 + q * SC_LANES, SC_LANES)] for q in range(nvec))

                    @plsc.parallel_loop(0, SC_ROWS, unroll=2, carry=accs)
                    def accs(r, acc):
                        ck = plsc.load_gather(coef_v, [jnp.full((SC_LANES,), cc * SC_ROWS, I32) + r])
                        return tuple(acc[q] + ck * buf[r, pl.ds(lane0 + q * SC_LANES, SC_LANES)]
                                     for q in range(nvec))

                    for q in range(nvec):
                        acc_v[pl.ds(lane0 + q * SC_LANES, SC_LANES)] = accs[q]

        _sc_token_stream((utab_hbm, vtab_hbm), erow_hbm, ((h_hbm, hbufs), (gate_hbm, gbufs)), out_hbm, accs_v,
                         tok0, tpw, wid * tpw, (idx_a, idx_b), rows_v, sems, begin, chunk)

    return run(utab, vtab, erow, hrows, gate)


def _peer_layer(x2, sc2, sh2, g2, peer_wq, peer_keys, utab, vtab, upack, vpack, lng, lnb, seq, alpha,
                tt_route=512, tt_gather=64, tm=128, sc_tokens=0):
    n, d = x2.shape
    keys = peer_keys.reshape(2 * PEER_HEADS, N_KEYS, PEER_DK // 2)
    h2, hv, erow, gate = _route_call(x2, sc2, sh2, peer_wq.astype(BF16), keys.astype(BF16), seq, tt_route)
    n_tc = n - sc_tokens
    y_sc = _sc_peer_call(erow, h2, gate, utab, vtab, n_tc, sc_tokens) if sc_tokens else None
    coef = _peer_u_call(erow, upack, hv, gate, tt_gather, n_tc)
    y_tc = _peer_v_call(erow, coef, vpack, tt_gather, n_tc)
    return _resln_call(x2, y_tc, y_sc, g2, lng.reshape(1, d), lnb.reshape(1, d), seq, tm, alpha)


def kernel(x, c, w_ada, b_ada, w_in, cmp_pe, cmp_w1, cmp_w2, w_pool, pool_scale, w_lift, w_o,
           ln_g, ln_b, peer_wq, peer_keys, peer_u, peer_v):
    bsz, seq, d = x.shape
    depth = w_ada.shape[0]
    assert d == D_MODEL and seq % 512 == 0 and SLC_TOPN <= seq // SLC_LEN <= LANES // N_KV
    alpha = (2 * depth) ** 0.25

    c_pad = jnp.zeros((SUBLANES, d), F32).at[:bsz].set(c)
    mods = _ada_call(c_pad, w_ada, b_ada)[:, :bsz]
    cst = _Consts(seq)
    nchain = bsz
    bpc = bsz // nchain
    xs = [x[i * bpc:(i + 1) * bpc].reshape(bpc * seq, d) for i in range(nchain)]
    sc_tokens = SC_TOKEN_SHARE * bpc * seq // 32
    for l in range(depth):
        upack, vpack = _pack_table(peer_u[l]), _pack_table(peer_v[l])
        for i in range(nchain):
            sh1, sc1, g1, sh2, sc2, g2 = (mods[l][i * bpc:(i + 1) * bpc, j * d:(j + 1) * d].reshape(bpc, 1, d)
                                          for j in range(6))
            xi = _token_mixer_layer(xs[i], sc1, sh1, g1, w_in[l], cmp_pe[l], cmp_w1[l], cmp_w2[l], w_pool[l],
                                    pool_scale[l], w_lift[l], w_o[l], ln_g[l, 0], ln_b[l, 0], cst, bpc, seq, alpha)
            xs[i] = _peer_layer(xi, sc2, sh2, g2, peer_wq[l], peer_keys[l], peer_u[l], peer_v[l], upack, vpack,
                                ln_g[l, 1], ln_b[l, 1], seq, alpha, sc_tokens=sc_tokens)
    return jnp.concatenate(xs, axis=0).reshape(bsz, seq, d)
```

```python
import dataclasses
import functools

import jax
import jax.numpy as jnp
import numpy as np
from jax import lax
from jax.experimental import pallas as pl
from jax.experimental.pallas import tpu as pltpu
from jax.experimental.pallas import tpu_sc as plsc

F32 = jnp.float32
BF16 = jnp.bfloat16
I32 = jnp.int32
HI = lax.Precision.HIGHEST

D_MODEL = 1024
N_HEADS = 8
HEAD_DIM = 64
N_KV = 2
HPG = N_HEADS // N_KV
ROT_DIM = HEAD_DIM // 4
ROT_HALF = ROT_DIM // 2
ROPE_THETA = 500000.0
CMP_LEN = 32
CMP_STRIDE = 16
SLC_LEN = 64
SLC_TOPN = 16
WINDOW = 512
SCALE = HEAD_DIM ** -0.5
NEG = -1e30
FORCE_INIT = 1e6
FORCE_LOCAL = 2e6
POOL_GROUPS = 4
POOL_WINDOWS = (2, 4, 8, 16)
POOL_WIDTH = 512
POOL_GW = POOL_WIDTH // POOL_GROUPS
POOL_HALO = 16
ATT_WIDTH = N_HEADS * HEAD_DIM
KV_WIDTH = 3 * 2 * N_KV * HEAD_DIM
GATE_NSA = 3 * N_HEADS
MERGE_GATES = 2 * D_MODEL
PEER_HEADS = 8
N_KEYS = 128
PEER_TOPK = 16
PEER_DK = 128
PEER_SEL = PEER_HEADS * PEER_TOPK
LN_EPS = 1e-5

LANES = 128
SUBLANES = 8
VMEM_LIMIT = 56 * 1024 * 1024

NT_DIMS = (((1,), (1,)), ((), ()))


def _cparams(sem):
    return pltpu.CompilerParams(dimension_semantics=sem, vmem_limit_bytes=VMEM_LIMIT)


def _ada_kernel(c_ref, w_ref, b_ref, o_ref):
    c = c_ref[...]
    ca = c * jax.nn.sigmoid(c)
    o_ref[...] = jnp.dot(ca, w_ref[...], precision=HI, preferred_element_type=F32) + b_ref[...]


def _ada_call(c_pad, w_ada, b_ada):
    depth = w_ada.shape[0]
    nblk = w_ada.shape[2] // D_MODEL
    rows = c_pad.shape[0]
    return pl.pallas_call(
        _ada_kernel,
        grid=(depth, nblk),
        in_specs=[
            pl.BlockSpec((rows, D_MODEL), lambda l, j: (0, 0)),
            pl.BlockSpec((None, D_MODEL, D_MODEL), lambda l, j: (l, 0, j)),
            pl.BlockSpec((None, 1, D_MODEL), lambda l, j: (l, 0, j)),
        ],
        out_specs=pl.BlockSpec((None, rows, D_MODEL), lambda l, j: (l, 0, j)),
        out_shape=jax.ShapeDtypeStruct((depth, rows, nblk * D_MODEL), F32),
        compiler_params=_cparams(("arbitrary", "arbitrary")),
        name="ada_mod",
    )(c_pad, w_ada, b_ada.reshape(depth, 1, -1))


IN_COLS = ATT_WIDTH + KV_WIDTH + POOL_WIDTH + MERGE_GATES + LANES


def _rope_lanes(z, rc, rs1, rs2):
    return z * rc + pltpu.roll(z, ROT_HALF, 1) * rs1 + pltpu.roll(z, LANES - ROT_HALF, 1) * rs2


def _inproj_kernel(x_ref, sc_ref, sh_ref, w_ref, rc_ref, rs1_ref, rs2_ref,
                   q_ref, kv_ref, p_ref, mrg_ref, gn_ref):
    h = x_ref[...] * (1.0 + sc_ref[...]) + sh_ref[...]
    a = jnp.dot(h.astype(BF16), w_ref[...], preferred_element_type=F32)
    rc, rs1, rs2 = rc_ref[...], rs1_ref[...], rs2_ref[...]
    for j in range(ATT_WIDTH // LANES):
        q_ref[:, j * LANES:(j + 1) * LANES] = _rope_lanes(a[:, j * LANES:(j + 1) * LANES], rc, rs1, rs2)
    for br in range(3):
        c0 = ATT_WIDTH + br * 2 * LANES
        k = a[:, c0:c0 + LANES]
        if br > 0:
            k = _rope_lanes(k, rc, rs1, rs2)
        kv_ref[:, br * 2 * LANES:br * 2 * LANES + LANES] = k
        kv_ref[:, br * 2 * LANES + LANES:(br + 1) * 2 * LANES] = a[:, c0 + LANES:c0 + 2 * LANES]
    c1 = ATT_WIDTH + KV_WIDTH
    p_ref[...] = a[:, c1:c1 + POOL_WIDTH]
    mrg_ref[...] = a[:, c1 + POOL_WIDTH:c1 + POOL_WIDTH + MERGE_GATES]
    gn_ref[...] = a[:, c1 + POOL_WIDTH + MERGE_GATES:]


def _inproj_call(x2, sc, sh, w, rc, rs1, rs2, seq, tm):
    n = x2.shape[0]
    tpb = seq // tm
    tok = lambda i: (i, 0)
    bat = lambda i: (i // tpb, 0, 0)
    pos = lambda i: (i % tpb, 0)
    full = lambda i: (0, 0)
    return pl.pallas_call(
        _inproj_kernel,
        grid=(n // tm,),
        in_specs=[
            pl.BlockSpec((tm, D_MODEL), tok),
            pl.BlockSpec((None, 1, D_MODEL), bat),
            pl.BlockSpec((None, 1, D_MODEL), bat),
            pl.BlockSpec((D_MODEL, IN_COLS), full),
            pl.BlockSpec((tm, LANES), pos),
            pl.BlockSpec((tm, LANES), pos),
            pl.BlockSpec((tm, LANES), pos),
        ],
        out_specs=[
            pl.BlockSpec((tm, ATT_WIDTH), tok),
            pl.BlockSpec((tm, KV_WIDTH), tok),
            pl.BlockSpec((tm, POOL_WIDTH), tok),
            pl.BlockSpec((tm, MERGE_GATES), tok),
            pl.BlockSpec((tm, LANES), tok),
        ],
        out_shape=[
            jax.ShapeDtypeStruct((n, ATT_WIDTH), F32),
            jax.ShapeDtypeStruct((n, KV_WIDTH), F32),
            jax.ShapeDtypeStruct((n, POOL_WIDTH), F32),
            jax.ShapeDtypeStruct((n, MERGE_GATES), F32),
            jax.ShapeDtypeStruct((n, LANES), F32),
        ],
        compiler_params=_cparams(("arbitrary",)),
        name="in_proj",
    )(x2, sc, sh, w, rc, rs1, rs2)


def _compress_kernel(z_ref, pet_ref, peb_ref, w1t_ref, w1b_ref, w2_ref, rc_ref, rs1_ref, rs2_ref, o_ref):
    z = z_ref[...]
    rows = z.shape[0]
    top = jnp.dot(z + pet_ref[...], w1t_ref[...], precision=HI, preferred_element_type=F32)
    bot = jnp.dot(z + peb_ref[...], w1b_ref[...], precision=HI, preferred_element_type=F32)
    pre = top + pltpu.roll(bot, rows - 1, 0)
    y = jnp.dot(jax.nn.gelu(pre), w2_ref[...], precision=HI, preferred_element_type=F32)
    o_ref[...] = _rope_lanes(y, rc_ref[...], rs1_ref[...], rs2_ref[...])


def _compress_call(z, pet, peb, w1t, w1b, w2, rc, rs1, rs2):
    b, _, rows, width = z.shape
    kvsel = lambda i, j: (j, 0, 0)
    return pl.pallas_call(
        _compress_kernel,
        grid=(b, 2),
        in_specs=[
            pl.BlockSpec((None, None, rows, width), lambda i, j: (i, j, 0, 0)),
            pl.BlockSpec((None, 1, width), kvsel),
            pl.BlockSpec((None, 1, width), kvsel),
            pl.BlockSpec((None, width, LANES), kvsel),
            pl.BlockSpec((None, width, LANES), kvsel),
            pl.BlockSpec((None, LANES, LANES), kvsel),
            pl.BlockSpec((None, rows, LANES), kvsel),
            pl.BlockSpec((None, rows, LANES), kvsel),
            pl.BlockSpec((None, rows, LANES), kvsel),
        ],
        out_specs=pl.BlockSpec((None, None, rows, LANES), lambda i, j: (i, j, 0, 0)),
        out_shape=jax.ShapeDtypeStruct((b, 2, rows, LANES), F32),
        compiler_params=_cparams(("arbitrary", "arbitrary")),
        name="compress",
    )(z, pet, peb, w1t, w1b, w2, rc, rs1, rs2)


def _cmpsel_kernel(q_ref, kc_ref, vc_ref, c2s_ref, o_ref, sel_ref, *, tq):
    t0 = pl.program_id(1) * tq
    kc = kc_ref[...]
    vc = vc_ref[...]
    rows = kc.shape[0]
    trow = t0 + lax.broadcasted_iota(I32, (tq, rows), 0)
    cend = lax.broadcasted_iota(I32, (tq, rows), 1) * CMP_STRIDE + (CMP_LEN - 1)
    vis = cend <= trow
    anyv = (trow[:, :1] >= CMP_LEN - 1).astype(F32)
    imp = jnp.zeros((tq, LANES), F32)
    for g in range(N_KV):
        kg = kc[:, g * HEAD_DIM:(g + 1) * HEAD_DIM].astype(BF16)
        vg = vc[:, g * HEAD_DIM:(g + 1) * HEAD_DIM].astype(BF16)
        psum = jnp.zeros((tq, rows), F32)
        for h in range(HPG):
            hd = g * HPG + h
            qh = q_ref[:, hd * HEAD_DIM:(hd + 1) * HEAD_DIM].astype(BF16)
            s = lax.dot_general(qh, kg, NT_DIMS, preferred_element_type=F32) * SCALE
            s = jnp.where(vis, s, NEG)
            e = jnp.exp(s - jnp.max(s, axis=-1, keepdims=True))
            p = e / jnp.sum(e, axis=-1, keepdims=True) * anyv
            o_ref[:, hd * HEAD_DIM:(hd + 1) * HEAD_DIM] = jnp.dot(
                p.astype(BF16), vg, preferred_element_type=F32)
            psum = psum + p
        imp = imp + jnp.dot(psum.astype(BF16), c2s_ref[g], preferred_element_type=F32)
    lane = lax.broadcasted_iota(I32, (tq, LANES), 1)
    blk = lane & (SLC_LEN - 1)
    cur = lax.shift_right_logical(t0 + lax.broadcasted_iota(I32, (tq, LANES), 0), 6)
    score = jnp.where(blk <= cur, imp, NEG)
    score = jnp.where(blk == 0, FORCE_INIT, score)
    score = jnp.where(blk == cur, FORCE_LOCAL, score)
    sc_t = score.T
    nblk = LANES // N_KV
    jrow = lax.broadcasted_iota(I32, (nblk, tq), 0)
    sel_parts = []
    for g in range(N_KV):
        sc = sc_t[g * nblk:(g + 1) * nblk]
        cnt = jnp.zeros((nblk, tq), I32)
        for k in range(nblk):
            rk = sc[k:k + 1, :]
            ge = (rk >= sc).astype(I32)
            gt = (rk > sc).astype(I32)
            cnt = cnt + jnp.where(jrow > k, ge, gt)
        sel_parts.append((cnt < SLC_TOPN).astype(F32))
    sel_ref[...] = jnp.concatenate(sel_parts, axis=0).T


def _cmpsel_call(q, kvc, c2s, seq, tq):
    n = q.shape[0]
    b = n // seq
    nq = seq // tq
    rows = kvc.shape[2]
    tok = lambda i, j: (i * nq + j, 0)
    return pl.pallas_call(
        functools.partial(_cmpsel_kernel, tq=tq),
        grid=(b, nq),
        in_specs=[
            pl.BlockSpec((tq, ATT_WIDTH), tok),
            pl.BlockSpec((None, None, rows, LANES), lambda i, j: (i, 0, 0, 0)),
            pl.BlockSpec((None, None, rows, LANES), lambda i, j: (i, 1, 0, 0)),
            pl.BlockSpec((N_KV, rows, LANES), lambda i, j: (0, 0, 0)),
        ],
        out_specs=[pl.BlockSpec((tq, ATT_WIDTH), tok), pl.BlockSpec((tq, LANES), tok)],
        out_shape=[jax.ShapeDtypeStruct((n, ATT_WIDTH), F32), jax.ShapeDtypeStruct((n, LANES), F32)],
        compiler_params=_cparams(("arbitrary", "arbitrary")),
        name="cmp_select",
    )(q, kvc, kvc, c2s)


def _slc_kernel(q_ref, k_ref, v_ref, sel_ref, o_ref, qs_sc, m_sc, l_sc, acc_sc, *, tq, tk):
    qi = pl.program_id(1)
    kt = pl.program_id(2)
    nk = pl.num_programs(2)

    @pl.when(kt == 0)
    def _init():
        for hd in range(N_HEADS):
            g, h = divmod(hd, HPG)
            qs_sc[g, h * tq:(h + 1) * tq, :] = (q_ref[:, hd * HEAD_DIM:(hd + 1) * HEAD_DIM] * SCALE).astype(BF16)
        m_sc[...] = jnp.full(m_sc.shape, NEG, F32)
        l_sc[...] = jnp.zeros(l_sc.shape, F32)
        acc_sc[...] = jnp.zeros(acc_sc.shape, F32)

    @pl.when(kt * tk <= qi * tq + (tq - 1))
    def _step():
        t = qi * tq + lax.broadcasted_iota(I32, (tq, tk), 0)
        kp = kt * tk + lax.broadcasted_iota(I32, (tq, tk), 1)
        causal = kp <= t
        nblk = LANES // N_KV
        jb = lax.broadcasted_iota(I32, (nblk, tk), 0)
        kb = lax.shift_right_logical(kt * tk + lax.broadcasted_iota(I32, (nblk, tk), 1), 6)
        expand = (jb == kb).astype(BF16)
        for g in range(N_KV):
            selg = sel_ref[:, g * nblk:(g + 1) * nblk].astype(BF16)
            member = jnp.dot(selg, expand, preferred_element_type=F32)
            bias = jnp.where(jnp.logical_and(causal, member > 0.5), 0.0, NEG)
            bias = jnp.concatenate([bias] * HPG, axis=0)
            kg = k_ref[:, g * HEAD_DIM:(g + 1) * HEAD_DIM].astype(BF16)
            vg = v_ref[:, g * HEAD_DIM:(g + 1) * HEAD_DIM].astype(BF16)
            s = lax.dot_general(qs_sc[g], kg, NT_DIMS, preferred_element_type=F32) + bias
            chunks = [s[:, c * LANES:(c + 1) * LANES] for c in range(tk // LANES)]
            mc = chunks[0]
            for x in chunks[1:]:
                mc = jnp.maximum(mc, x)
            m_old = m_sc[g]
            m_new = jnp.maximum(m_old, jnp.max(mc, axis=-1, keepdims=True))
            alpha = jnp.exp(m_old - m_new)
            ps = [jnp.exp(x - m_new) for x in chunks]
            lsum = ps[0]
            for x in ps[1:]:
                lsum = lsum + x
            l_sc[g] = alpha * l_sc[g] + lsum
            p = jnp.concatenate(ps, axis=1).astype(BF16)
            acc_sc[g] = alpha[:, :HEAD_DIM] * acc_sc[g] + jnp.dot(p, vg, preferred_element_type=F32)
            m_sc[g] = m_new

    @pl.when(kt == nk - 1)
    def _fin():
        for hd in range(N_HEADS):
            g, h = divmod(hd, HPG)
            l = jnp.sum(l_sc[g, h * tq:(h + 1) * tq, :], axis=-1, keepdims=True)
            o_ref[:, hd * HEAD_DIM:(hd + 1) * HEAD_DIM] = acc_sc[g, h * tq:(h + 1) * tq, :] / l


def _slc_call(q, kv, sel, seq, tq, tk):
    n = q.shape[0]
    b = n // seq
    nq = seq // tq
    nk = seq // tk
    tok = lambda i, j, k: (i * nq + j, 0)

    def key_map(col):
        def f(i, j, k):
            last = (j * tq + tq - 1) // tk
            return (i * nk + jnp.minimum(k, last), col)
        return f

    return pl.pallas_call(
        functools.partial(_slc_kernel, tq=tq, tk=tk),
        grid=(b, nq, nk),
        in_specs=[
            pl.BlockSpec((tq, ATT_WIDTH), tok),
            pl.BlockSpec((tk, LANES), key_map(2)),
            pl.BlockSpec((tk, LANES), key_map(3)),
            pl.BlockSpec((tq, LANES), tok),
        ],
        out_specs=pl.BlockSpec((tq, ATT_WIDTH), tok),
        out_shape=jax.ShapeDtypeStruct((n, ATT_WIDTH), F32),
        scratch_shapes=[
            pltpu.VMEM((N_KV, HPG * tq, HEAD_DIM), BF16),
            pltpu.VMEM((N_KV, HPG * tq, LANES), F32),
            pltpu.VMEM((N_KV, HPG * tq, LANES), F32),
            pltpu.VMEM((N_KV, HPG * tq, HEAD_DIM), F32),
        ],
        compiler_params=_cparams(("arbitrary", "arbitrary", "arbitrary")),
        name="slc_attn",
    )(q, kv, kv, sel)


def _win_kernel(q_ref, *refs, tq, nkb):
    k_refs = refs[:nkb]
    v_refs = refs[nkb:2 * nkb]
    o_ref = refs[2 * nkb]
    qi = pl.program_id(1)
    t = qi * tq + lax.broadcasted_iota(I32, (tq, tq), 0)
    col = lax.broadcasted_iota(I32, (tq, tq), 1)
    biases = []
    for j in range(nkb):
        kp = (qi - (nkb - 1) + j) * tq + col
        diff = t - kp
        ok = jnp.logical_and(jnp.logical_and(diff >= 0, diff < WINDOW), kp >= 0)
        biases.append(jnp.concatenate([jnp.where(ok, 0.0, NEG)] * HPG, axis=0))
    for g in range(N_KV):
        qs = jnp.concatenate(
            [(q_ref[:, (g * HPG + h) * HEAD_DIM:(g * HPG + h + 1) * HEAD_DIM] * SCALE).astype(BF16)
             for h in range(HPG)], axis=0)
        chunks = []
        for j in range(nkb):
            kg = k_refs[j][:, g * HEAD_DIM:(g + 1) * HEAD_DIM].astype(BF16)
            s = lax.dot_general(qs, kg, NT_DIMS, preferred_element_type=F32) + biases[j]
            chunks += [s[:, c * LANES:(c + 1) * LANES] for c in range(tq // LANES)]
        mc = chunks[0]
        for x in chunks[1:]:
            mc = jnp.maximum(mc, x)
        m = jnp.broadcast_to(jnp.max(mc, axis=-1, keepdims=True), mc.shape)
        ps = [jnp.exp(x - m) for x in chunks]
        lsum = ps[0]
        for x in ps[1:]:
            lsum = lsum + x
        l = jnp.sum(lsum, axis=-1, keepdims=True)
        per = tq // LANES
        o = None
        for j in range(nkb):
            vg = v_refs[j][:, g * HEAD_DIM:(g + 1) * HEAD_DIM].astype(BF16)
            pj = jnp.concatenate(ps[j * per:(j + 1) * per], axis=1).astype(BF16)
            oj = jnp.dot(pj, vg, preferred_element_type=F32)
            o = oj if o is None else o + oj
        o = o / l
        for h in range(HPG):
            hd = g * HPG + h
            o_ref[:, hd * HEAD_DIM:(hd + 1) * HEAD_DIM] = o[h * tq:(h + 1) * tq]


def _win_call(q, kv, seq, tq):
    n = q.shape[0]
    b = n // seq
    nq = seq // tq
    nkb = WINDOW // tq + 1
    tok = lambda i, j: (i * nq + j, 0)

    def key_map(col, back):
        return lambda i, j: (i * nq + jnp.maximum(j - back, 0), col)

    k_specs = [pl.BlockSpec((tq, LANES), key_map(4, nkb - 1 - jj)) for jj in range(nkb)]
    v_specs = [pl.BlockSpec((tq, LANES), key_map(5, nkb - 1 - jj)) for jj in range(nkb)]
    return pl.pallas_call(
        functools.partial(_win_kernel, tq=tq, nkb=nkb),
        grid=(b, nq),
        in_specs=[pl.BlockSpec((tq, ATT_WIDTH), tok)] + k_specs + v_specs,
        out_specs=pl.BlockSpec((tq, ATT_WIDTH), tok),
        out_shape=jax.ShapeDtypeStruct((n, ATT_WIDTH), F32),
        compiler_params=_cparams(("arbitrary", "arbitrary")),
        name="win_attn",
    )(q, *([kv] * (2 * nkb)))


def _pool_kernel(p_ref, prev_ref, w_ref, sc_ref, o_ref, *, ts):
    i = pl.program_id(1)
    x = p_ref[...]
    prev = prev_ref[...] * (i > 0).astype(F32)
    xe = jnp.concatenate([prev, x], axis=0)
    t1 = (i * ts + 1 + lax.broadcasted_iota(I32, (ts, POOL_GW), 0)).astype(F32)
    for g, w in enumerate(POOL_WINDOWS):
        a = xe[:, g * POOL_GW:(g + 1) * POOL_GW]
        off = POOL_HALO
        span = 1
        while span < w:
            a = a[span:] + a[:-span]
            off -= span
            span *= 2
        sums = a[off:off + ts]
        cnt = jnp.minimum(t1, float(w))
        pooled = sums / cnt - x[:, g * POOL_GW:(g + 1) * POOL_GW]
        y = jnp.dot(pooled.astype(BF16), w_ref[g], preferred_element_type=F32)
        o_ref[:, g * POOL_GW:(g + 1) * POOL_GW] = y * sc_ref[:, g * POOL_GW:(g + 1) * POOL_GW]


def _pool_call(p_in, w_pool, pool_scale, seq, ts):
    n = p_in.shape[0]
    b = n // seq
    nt = seq // ts
    hpt = ts // POOL_HALO
    tok = lambda i, j: (i * nt + j, 0)
    return pl.pallas_call(
        functools.partial(_pool_kernel, ts=ts),
        grid=(b, nt),
        in_specs=[
            pl.BlockSpec((ts, POOL_WIDTH), tok),
            pl.BlockSpec((POOL_HALO, POOL_WIDTH), lambda i, j: (i * nt * hpt + jnp.maximum(j * hpt - 1, 0), 0)),
            pl.BlockSpec((POOL_GROUPS, POOL_GW, POOL_GW), lambda i, j: (0, 0, 0)),
            pl.BlockSpec((1, POOL_WIDTH), lambda i, j: (0, 0)),
        ],
        out_specs=pl.BlockSpec((ts, POOL_WIDTH), tok),
        out_shape=jax.ShapeDtypeStruct((n, POOL_WIDTH), F32),
        compiler_params=_cparams(("arbitrary", "arbitrary")),
        name="pool_mix",
    )(p_in, p_in, w_pool, pool_scale)


def _layer_norm(z, g, b):
    mu = jnp.mean(z, axis=-1, keepdims=True)
    zc = z - mu
    var = jnp.mean(zc * zc, axis=-1, keepdims=True)
    return zc * lax.rsqrt(var + LN_EPS) * g + b


def _merge_kernel(oc_ref, os_ref, ow_ref, gn_ref, op_ref, gm_ref, x_ref, g1_ref, lng_ref, lnb_ref,
                  wl_ref, wo_ref, eb_ref, o_ref, *, alpha):
    gate = jax.nn.sigmoid(gn_ref[...])
    gate_hi = gate.astype(BF16)
    gate_lo = (gate - gate_hi.astype(F32)).astype(BF16)
    branches = (oc_ref, os_ref, ow_ref)
    oatt = None
    for br in range(3):
        gx = (jnp.dot(gate_hi, eb_ref[br], preferred_element_type=F32)
              + jnp.dot(gate_lo, eb_ref[br], preferred_element_type=F32))
        term = gx * branches[br][...]
        oatt = term if oatt is None else oatt + term
    la = jnp.dot(oatt.astype(BF16), wl_ref[0], preferred_element_type=F32)
    lb = jnp.dot(op_ref[...].astype(BF16), wl_ref[1], preferred_element_type=F32)
    gm = jax.nn.sigmoid(gm_ref[...])
    merged = gm[:, :D_MODEL] * la + gm[:, D_MODEL:] * lb
    y = jnp.dot(merged.astype(BF16), wo_ref[...], preferred_element_type=F32)
    z = alpha * x_ref[...] + g1_ref[...] * y
    o_ref[...] = _layer_norm(z, lng_ref[...], lnb_ref[...])


def _merge_call(oc, osl, ow, gn, op, gm, x2, g1, lng, lnb, wl, wo, eb, seq, tm, alpha):
    n = x2.shape[0]
    tpb = seq // tm
    tok = lambda i: (i, 0)
    bat = lambda i: (i // tpb, 0, 0)
    return pl.pallas_call(
        functools.partial(_merge_kernel, alpha=alpha),
        grid=(n // tm,),
        in_specs=[
            pl.BlockSpec((tm, ATT_WIDTH), tok), pl.BlockSpec((tm, ATT_WIDTH), tok),
            pl.BlockSpec((tm, ATT_WIDTH), tok), pl.BlockSpec((tm, LANES), tok),
            pl.BlockSpec((tm, POOL_WIDTH), tok), pl.BlockSpec((tm, MERGE_GATES), tok),
            pl.BlockSpec((tm, D_MODEL), tok),
            pl.BlockSpec((None, 1, D_MODEL), bat),
            pl.BlockSpec((1, D_MODEL), lambda i: (0, 0)), pl.BlockSpec((1, D_MODEL), lambda i: (0, 0)),
            pl.BlockSpec((2, ATT_WIDTH, D_MODEL), lambda i: (0, 0, 0)),
            pl.BlockSpec((D_MODEL, D_MODEL), lambda i: (0, 0)),
            pl.BlockSpec((3, LANES, ATT_WIDTH), lambda i: (0, 0, 0)),
        ],
        out_specs=pl.BlockSpec((tm, D_MODEL), tok),
        out_shape=jax.ShapeDtypeStruct((n, D_MODEL), F32),
        compiler_params=_cparams(("arbitrary",)),
        name="merge_out",
    )(oc, osl, ow, gn, op, gm, x2, g1, lng, lnb, wl, wo, eb)


def _extract_top(cur, ids, n):
    rows = cur.shape[0]
    rio = lax.broadcasted_iota(I32, cur.shape, 0)
    vals, outs = [], []
    for _ in range(n):
        m = jnp.max(cur, axis=0, keepdims=True)
        pos = jnp.min(jnp.where(cur == m, rio, rows), axis=0, keepdims=True)
        hit = rio == pos
        vals.append(m)
        outs.append(pos if ids is None else jnp.max(jnp.where(hit, ids, -1), axis=0, keepdims=True))
        cur = jnp.where(hit, -jnp.inf, cur)
    return jnp.concatenate(vals, axis=0), jnp.concatenate(outs, axis=0)


def _route_kernel(x_ref, sc_ref, sh_ref, wq_ref, keys_ref, h_ref, hv_ref, e_ref, g_ref,
                  st_sc, ts_sc, ti_sc, eo_sc, go_sc):
    h = x_ref[...] * (1.0 + sc_ref[...]) + sh_ref[...]
    h_ref[...] = h
    for r in range(SUBLANES):
        hv_ref[:, r, :] = h[:, r * LANES:(r + 1) * LANES]
    qp = jnp.dot(h.astype(BF16), wq_ref[...], preferred_element_type=F32).astype(BF16)
    half = PEER_DK // 2
    for hp in range(2 * PEER_HEADS):
        st_sc[hp] = lax.dot_general(keys_ref[hp], qp[:, hp * half:(hp + 1) * half], NT_DIMS,
                                    preferred_element_type=F32)

    def half_body(hp, carry):
        vals, ids = _extract_top(st_sc[hp], None, PEER_TOPK)
        ts_sc[hp] = vals
        ti_sc[hp] = ids
        return carry

    lax.fori_loop(0, 2 * PEER_HEADS, half_body, 0)

    def head_body(hh, carry):
        s1, s2 = ts_sc[2 * hh], ts_sc[2 * hh + 1]
        i1, i2 = ti_sc[2 * hh], ti_sc[2 * hh + 1]
        brow = lax.broadcasted_iota(I32, (SUBLANES, s1.shape[1]), 0)
        cands = [s1[0:1, :] + s2]
        cidxs = [i1[0:1, :] * N_KEYS + i2]
        for a in range(1, SUBLANES):
            ok = brow < PEER_TOPK // (a + 1)
            cands.append(jnp.where(ok, s1[a:a + 1, :] + s2[:SUBLANES], -jnp.inf))
            cidxs.append(i1[a:a + 1, :] * N_KEYS + i2[:SUBLANES])
        cands.append(s1[SUBLANES:] + s2[0:1, :])
        cidxs.append(i1[SUBLANES:] * N_KEYS + i2[0:1, :])
        sv, ei = _extract_top(jnp.concatenate(cands, axis=0), jnp.concatenate(cidxs, axis=0), PEER_TOPK)
        ex = jnp.exp(sv - sv[0:1, :])
        go_sc[hh] = ex / jnp.sum(ex, axis=0, keepdims=True)
        eo_sc[hh] = ei.astype(F32)
        return carry

    lax.fori_loop(0, PEER_HEADS, head_body, 0)
    e_all = jnp.concatenate([eo_sc[hh] for hh in range(PEER_HEADS)], axis=0)
    g_all = jnp.concatenate([go_sc[hh] for hh in range(PEER_HEADS)], axis=0)
    e_ref[...] = e_all.T.astype(I32) * (D_MODEL // 2 // LANES)
    g_ref[...] = g_all.T


def _route_call(x2, sc, sh, wq, keys, seq, tt):
    n = x2.shape[0]
    tpb = seq // tt
    tok = lambda i: (i, 0)
    bat = lambda i: (i // tpb, 0, 0)
    nhp = 2 * PEER_HEADS
    return pl.pallas_call(
        _route_kernel,
        grid=(n // tt,),
        in_specs=[
            pl.BlockSpec((tt, D_MODEL), tok),
            pl.BlockSpec((None, 1, D_MODEL), bat),
            pl.BlockSpec((None, 1, D_MODEL), bat),
            pl.BlockSpec((D_MODEL, PEER_HEADS * PEER_DK), lambda i: (0, 0)),
            pl.BlockSpec((nhp, N_KEYS, PEER_DK // 2), lambda i: (0, 0, 0)),
        ],
        out_specs=[pl.BlockSpec((tt, D_MODEL), tok), pl.BlockSpec((tt, SUBLANES, LANES), lambda i: (i, 0, 0)),
                   pl.BlockSpec((tt, PEER_SEL), tok), pl.BlockSpec((tt, PEER_SEL), tok)],
        out_shape=[jax.ShapeDtypeStruct((n, D_MODEL), F32), jax.ShapeDtypeStruct((n, SUBLANES, LANES), F32),
                   jax.ShapeDtypeStruct((n, PEER_SEL), I32), jax.ShapeDtypeStruct((n, PEER_SEL), F32)],
        scratch_shapes=[
            pltpu.VMEM((nhp, N_KEYS, tt), F32),
            pltpu.VMEM((nhp, PEER_TOPK, tt), F32),
            pltpu.VMEM((nhp, PEER_TOPK, tt), I32),
            pltpu.VMEM((PEER_HEADS, PEER_TOPK, tt), F32),
            pltpu.VMEM((PEER_HEADS, PEER_TOPK, tt), F32),
        ],
        compiler_params=_cparams(("arbitrary",)),
        name="peer_route",
    )(x2, sc, sh, wq, keys)


HALF_ROWS = SUBLANES // 2
HI_MASK = -65536
PAIR_TILES = PEER_SEL // 2
PAIR_ROWS = PAIR_TILES * SUBLANES


def _load_two_experts(tab_ref, ra, rb):
    wa = tab_ref[pl.ds(pl.multiple_of(ra, HALF_ROWS), HALF_ROWS), :]
    wb = tab_ref[pl.ds(pl.multiple_of(rb, HALF_ROWS), HALF_ROWS), :]
    w2 = jnp.concatenate([wa, wb], axis=0)
    return lax.bitcast_convert_type(w2 << 16, F32), lax.bitcast_convert_type(w2 & HI_MASK, F32)


def _fold_pairs(vs):
    row = lax.broadcasted_iota(I32, (SUBLANES, LANES), 0)
    shift = HALF_ROWS // 2
    while len(vs) > 1:
        low = (row & shift) == 0
        vs = [jnp.where(low, a + pltpu.roll(a, SUBLANES - shift, 0), b + pltpu.roll(b, shift, 0))
              for a, b in zip(vs[0::2], vs[1::2])]
        shift //= 2
    return vs[0]


def _fold_order():
    idx = [[2 * i if r < HALF_ROWS else 2 * i + 1 for r in range(SUBLANES)] for i in range(HALF_ROWS)]
    shift = HALF_ROWS // 2
    while len(idx) > 1:
        idx = [[a[r] if (r & shift) == 0 else b[r] for r in range(SUBLANES)]
               for a, b in zip(idx[0::2], idx[1::2])]
        shift //= 2
    return idx[0]


def _peer_u_kernel(e_sm, tab_ref, hv_ref, gate_ref, coef_ref, *, tt):
    row = lax.broadcasted_iota(I32, (SUBLANES, LANES), 0)
    low = row < HALF_ROWS
    eye = (lax.broadcasted_iota(I32, (PEER_SEL, LANES), 0) ==
           lax.broadcasted_iota(I32, (PEER_SEL, LANES), 1))
    order = _fold_order()

    def finish(t, part):
        col = jnp.sum(part, axis=-1, keepdims=True)
        a_row = jnp.sum(jnp.where(eye, col, 0.0), axis=0, keepdims=True)
        coef_ref[t] = gate_ref[pl.ds(t, 1), :] * jax.nn.gelu(a_row)

    def token(t, part_prev):
        finish(jnp.maximum(t - 1, 0), part_prev)
        hv = hv_ref[t]
        hsw = pltpu.roll(hv, HALF_ROWS, 0)
        h_lo = jnp.where(low, hv, hsw)
        h_hi = jnp.where(low, hsw, hv)
        folded = []
        for j in range(PEER_SEL // SUBLANES):
            prods = []
            for i in range(HALF_ROWS):
                ka = j * SUBLANES + order.index(2 * i)
                kb = j * SUBLANES + order.index(2 * i + 1)
                lo, hi = _load_two_experts(tab_ref, e_sm[t, ka], e_sm[t, kb])
                prods.append(lo * h_lo + hi * h_hi)
            folded.append(_fold_pairs(prods))
        return jnp.concatenate(folded, axis=0)

    last = lax.fori_loop(0, tt, token, jnp.zeros((PEER_SEL, LANES), F32))
    finish(tt - 1, last)


def _peer_v_kernel(e_sm, coef_ref, tab_ref, y_ref, cv_sc, *, tt, nacc):
    row = lax.broadcasted_iota(I32, (SUBLANES, LANES), 0)
    low = row < HALF_ROWS
    rr = lax.broadcasted_iota(I32, (PAIR_ROWS, LANES), 0)
    kk = lax.broadcasted_iota(I32, (PAIR_ROWS, LANES), 1)
    onehot = (kk == 2 * (rr >> 3) + ((rr >> 2) & 1)).astype(F32)
    ones = jnp.ones((LANES, LANES), BF16)

    def expand(t, slot):
        lhs = (onehot * coef_ref[t]).astype(BF16)
        cv_sc[slot] = jnp.dot(lhs, ones, preferred_element_type=F32)

    def process(t, slot):
        acc_lo = [jnp.zeros((SUBLANES, LANES), F32) for _ in range(nacc)]
        acc_hi = [jnp.zeros((SUBLANES, LANES), F32) for _ in range(nacc)]
        for j in range(PAIR_TILES):
            grp = e_sm.at[t, pl.ds(2 * j // SUBLANES * SUBLANES, SUBLANES)]
            lo, hi = _load_two_experts(tab_ref, grp[2 * j % SUBLANES], grp[(2 * j + 1) % SUBLANES])
            cv = cv_sc[slot, j * SUBLANES:(j + 1) * SUBLANES, :]
            acc_lo[j % nacc] = acc_lo[j % nacc] + cv * lo
            acc_hi[j % nacc] = acc_hi[j % nacc] + cv * hi
        a_lo, a_hi = acc_lo[0], acc_hi[0]
        for i in range(1, nacc):
            a_lo = a_lo + acc_lo[i]
            a_hi = a_hi + acc_hi[i]
        a_lo = a_lo + pltpu.roll(a_lo, HALF_ROWS, 0)
        a_hi = a_hi + pltpu.roll(a_hi, HALF_ROWS, 0)
        y_ref[t] = jnp.where(low, a_lo, a_hi)

    expand(0, 0)

    def two_tokens(i, carry):
        t = 2 * i
        expand(t + 1, 1)
        process(t, 0)
        expand(jnp.minimum(t + 2, tt - 1), 0)
        process(t + 1, 1)
        return carry

    lax.fori_loop(0, tt // 2, two_tokens, 0)


def _resident_table_spec(tab):
    return pl.BlockSpec(tab.shape, lambda i: (0, 0), pipeline_mode=pl.Buffered(1))


def _peer_u_call(erow, tab, hv, gate, tt, n):
    return pl.pallas_call(
        functools.partial(_peer_u_kernel, tt=tt),
        grid=(n // tt,),
        in_specs=[
            pl.BlockSpec((tt, PEER_SEL), lambda i: (i, 0), memory_space=pltpu.SMEM),
            _resident_table_spec(tab),
            pl.BlockSpec((tt, SUBLANES, LANES), lambda i: (i, 0, 0)),
            pl.BlockSpec((tt, PEER_SEL), lambda i: (i, 0)),
        ],
        out_specs=pl.BlockSpec((tt, 1, PEER_SEL), lambda i: (i, 0, 0)),
        out_shape=jax.ShapeDtypeStruct((n, 1, PEER_SEL), F32),
        compiler_params=_cparams(("arbitrary",)),
        name="peer_u",
    )(erow, tab, hv, gate)


def _peer_v_call(erow, coef3, tab, tt, n):
    assert tt % 2 == 0
    return pl.pallas_call(
        functools.partial(_peer_v_kernel, tt=tt, nacc=4),
        grid=(n // tt,),
        in_specs=[
            pl.BlockSpec((tt, PEER_SEL), lambda i: (i, 0), memory_space=pltpu.SMEM),
            pl.BlockSpec((tt, 1, PEER_SEL), lambda i: (i, 0, 0)),
            _resident_table_spec(tab),
        ],
        out_specs=pl.BlockSpec((tt, SUBLANES, LANES), lambda i: (i, 0, 0)),
        out_shape=jax.ShapeDtypeStruct((n, SUBLANES, LANES), F32),
        scratch_shapes=[pltpu.VMEM((2, PAIR_ROWS, LANES), F32)],
        compiler_params=_cparams(("arbitrary",)),
        name="peer_v",
    )(erow, coef3, tab)


def _resln_kernel(x_ref, ytc_ref, ysc_ref, g_ref, lng_ref, lnb_ref, o_ref, *, alpha, tc_tiles):
    y_tc = jnp.concatenate([ytc_ref[:, r, :] for r in range(SUBLANES)], axis=1)
    y = jnp.where(pl.program_id(0) < tc_tiles, y_tc, ysc_ref[...])
    z = alpha * x_ref[...] + g_ref[...] * y
    o_ref[...] = _layer_norm(z, lng_ref[...], lnb_ref[...])


def _resln_call(x2, y_tc, y_sc, g2, lng, lnb, seq, tm, alpha):
    n = x2.shape[0]
    n_tc = y_tc.shape[0]
    assert n_tc % tm == 0 and n_tc > 0
    tc_tiles = n_tc // tm
    if y_sc is None:
        y_sc = jnp.zeros((tm, D_MODEL), F32)
    tpb = seq // tm
    tok = lambda i: (i, 0)
    return pl.pallas_call(
        functools.partial(_resln_kernel, alpha=alpha, tc_tiles=tc_tiles),
        grid=(n // tm,),
        in_specs=[
            pl.BlockSpec((tm, D_MODEL), tok),
            pl.BlockSpec((tm, SUBLANES, LANES), lambda i: (jnp.minimum(i, tc_tiles - 1), 0, 0)),
            pl.BlockSpec((tm, D_MODEL), lambda i: (jnp.maximum(i - tc_tiles, 0), 0)),
            pl.BlockSpec((None, 1, D_MODEL), lambda i: (i // tpb, 0, 0)),
            pl.BlockSpec((1, D_MODEL), lambda i: (0, 0)), pl.BlockSpec((1, D_MODEL), lambda i: (0, 0)),
        ],
        out_specs=pl.BlockSpec((tm, D_MODEL), tok),
        out_shape=jax.ShapeDtypeStruct((n, D_MODEL), F32),
        compiler_params=_cparams(("arbitrary",)),
        name="res_ln",
    )(x2, y_tc, y_sc, g2, lng, lnb)


def _rope_lane_tables(pos):
    inv = ROPE_THETA ** (-jnp.arange(0, ROT_DIM, 2, dtype=F32) / ROT_DIM)
    ang = pos.astype(F32)[:, None] * inv[None, :]
    cos, sin = jnp.cos(ang), jnp.sin(ang)
    lane = np.arange(LANES) % HEAD_DIM
    fidx = lane % ROT_HALF
    first = jnp.asarray(lane < ROT_HALF)
    second = jnp.asarray((lane >= ROT_HALF) & (lane < ROT_DIM))
    rot = jnp.asarray(lane < ROT_DIM)
    cl, sl = cos[:, fidx], sin[:, fidx]
    rc = jnp.where(rot, cl, 1.0)
    rs1 = jnp.where(second, sl, 0.0)
    rs2 = jnp.where(first, -sl, 0.0)
    return rc, rs1, rs2


def _pack_kernel(t_ref, o_ref):
    x = t_ref[...]
    half = x.shape[1] // 2
    lo = lax.bitcast_convert_type(x[:, :half].astype(BF16).astype(F32), I32)
    hi = lax.bitcast_convert_type(x[:, half:].astype(BF16).astype(F32), I32)
    w = lax.shift_right_logical(lo, jnp.full_like(lo, 16)) | hi
    rows = x.shape[0]
    per = half // LANES
    for r in range(per):
        o_ref[pl.ds(r, rows, stride=per), :] = w[:, r * LANES:(r + 1) * LANES]


def _pack_table(tab, te=512):
    e, d = tab.shape
    per = d // 2 // LANES
    return pl.pallas_call(
        _pack_kernel, grid=(e // te,),
        in_specs=[pl.BlockSpec((te, d), lambda i: (i, 0))],
        out_specs=pl.BlockSpec((te * per, LANES), lambda i: (i, 0)),
        out_shape=jax.ShapeDtypeStruct((e * per, LANES), I32),
        compiler_params=_cparams(("arbitrary",)), name="pack_table",
    )(tab)


def _cmp_to_slc_wide(rows, n_slc):
    st = np.arange(rows) * CMP_STRIDE
    js = np.arange(n_slc) * SLC_LEN
    ov = np.minimum(st[:, None] + CMP_LEN, js[None, :] + SLC_LEN) - np.maximum(st[:, None], js[None, :])
    c2s = np.maximum(ov, 0).astype(np.float32) / CMP_STRIDE
    wide = np.zeros((N_KV, rows, LANES), np.float32)
    nblk = LANES // N_KV
    for g in range(N_KV):
        wide[g, :, g * nblk:g * nblk + n_slc] = c2s
    return jnp.asarray(wide)


def _gate_expanders():
    eb = np.zeros((3, LANES, ATT_WIDTH), np.float32)
    for hd in range(N_HEADS):
        for br in range(3):
            eb[br, hd * 3 + br, hd * HEAD_DIM:(hd + 1) * HEAD_DIM] = 1.0
    return jnp.asarray(eb)


class _Consts:
    def __init__(self, seq):
        self.rows = seq // CMP_STRIDE
        self.rope = _rope_lane_tables(jnp.arange(seq))
        cpos = jnp.arange(self.rows) * CMP_STRIDE + CMP_LEN - 1
        crope = _rope_lane_tables(cpos)
        ident = (jnp.ones_like(crope[0]), jnp.zeros_like(crope[0]), jnp.zeros_like(crope[0]))
        self.crope = tuple(jnp.stack([a, b]) for a, b in zip(crope, ident))
        self.c2s = _cmp_to_slc_wide(self.rows, seq // SLC_LEN)
        self.eb = _gate_expanders()


def _token_mixer_layer(x2, sc1, sh1, g1, w_in, cmp_pe, cmp_w1, cmp_w2, w_pool, pool_scale, w_lift, w_o,
                       lng, lnb, cst, bsz, seq, alpha):
    d = D_MODEL
    rows = cst.rows
    s1 = ATT_WIDTH + KV_WIDTH
    s2 = s1 + GATE_NSA
    s3 = s2 + POOL_WIDTH
    w_gate = jnp.pad(w_in[:, s1:s2], ((0, 0), (0, LANES - GATE_NSA)))
    w_all = jnp.concatenate([w_in[:, :s1], w_in[:, s2:s3], w_in[:, s3:], w_gate], axis=1).astype(BF16)
    q, kv, p_in, g_mrg, g_nsa = _inproj_call(x2, sc1, sh1, w_all, *cst.rope, seq, 256)

    eye_g = jnp.eye(N_KV, dtype=F32)
    zc = kv[:, :2 * LANES].reshape(bsz, rows, CMP_STRIDE, 2, LANES)
    zc = jnp.transpose(zc, (0, 3, 1, 2, 4)).reshape(bsz, 2, rows, CMP_STRIDE * LANES)
    w1x = jnp.einsum('klde,gh->klgdhe', cmp_w1.reshape(2, CMP_LEN, HEAD_DIM, HEAD_DIM), eye_g)
    w1x = w1x.reshape(2, CMP_LEN * LANES, LANES)
    half = CMP_STRIDE * LANES
    pex = jnp.broadcast_to(cmp_pe[:, :, None, :], (2, CMP_LEN, N_KV, HEAD_DIM)).reshape(2, 1, CMP_LEN * LANES)
    w2x = jnp.einsum('kef,gh->kgehf', cmp_w2, eye_g).reshape(2, LANES, LANES)
    kvc = _compress_call(zc, pex[:, :, :half], pex[:, :, half:], w1x[:, :half], w1x[:, half:], w2x,
                         *cst.crope)

    o_cmp, sel = _cmpsel_call(q, kvc, cst.c2s.astype(BF16), seq, 256)
    o_slc = _slc_call(q, kv, sel, seq, 256, 512)
    o_win = _win_call(q, kv, seq, 256)
    o_pool = _pool_call(p_in, w_pool.astype(BF16), pool_scale.reshape(1, -1), seq, 512)
    return _merge_call(o_cmp, o_slc, o_win, g_nsa, o_pool, g_mrg, x2, g1,
                       lng.reshape(1, d), lnb.reshape(1, d),
                       w_lift.astype(BF16), w_o.astype(BF16), cst.eb.astype(BF16), seq, 256, alpha)


SC_CORES = 2
SC_SUBCORES = 16
SC_LANES = 16
SC_WORKERS = SC_CORES * SC_SUBCORES
SC_ROWS = 32
SC_TOKEN_SHARE = 20
SC_BLOCK = 256


def _sc_params():
    cp = pltpu.CompilerParams()
    if "needs_layout_passes" in pltpu.CompilerParams.__dataclass_fields__:
        cp = dataclasses.replace(cp, needs_layout_passes=False)
    return cp


def _sc_token_stream(tabs, erow_hbm, sides, out_hbm, out_bufs, tok0, tpw, base, idx_bufs, rows_v, sems,
                     begin_fn, chunk_fn):
    per_tab = PEER_SEL // SC_ROWS
    nch = per_tab * len(tabs)
    assert nch % 2 == 0

    def idx_copy(t, s):
        return pltpu.make_async_copy(erow_hbm.at[tok0 + base + t], idx_bufs[s], sems.at[2 + s])

    def out_copy(t, s):
        return pltpu.make_async_copy(out_bufs[s], out_hbm.at[base + t], sems.at[4 + s])

    def side_copy(k, t, s):
        hbm, bufs = sides[k]
        return pltpu.make_async_copy(hbm.at[tok0 + base + t], bufs[s], sems.at[6 + 2 * k + s])

    def row_copy(s, c):
        rows = idx_bufs[s].at[pl.ds((c % per_tab) * SC_ROWS, SC_ROWS)]
        return pltpu.make_async_copy(tabs[c // per_tab].at[rows], rows_v.at[c % 2], sems.at[c % 2])

    def to_expert_ids(s):
        shift = jnp.full((SC_LANES,), (D_MODEL // 2 // LANES).bit_length() - 1, I32)
        for j in range(PEER_SEL // SC_LANES):
            sl = pl.ds(j * SC_LANES, SC_LANES)
            idx_bufs[s][sl] = lax.shift_right_logical(idx_bufs[s][sl], shift)

    idx_copy(0, 0).start()
    for k in range(len(sides)):
        side_copy(k, 0, 0).start()
    idx_copy(0, 0).wait()
    to_expert_ids(0)
    row_copy(0, 0).start()

    @pl.loop(0, tpw // 2)
    def _(i):
        for slot in (0, 1):
            other = 1 - slot
            t = 2 * i + slot
            tn = jnp.minimum(t + 1, tpw - 1)
            idx_copy(tn, other).start()
            for k in range(len(sides)):
                side_copy(k, tn, other).start()
            for k in range(len(sides)):
                side_copy(k, t, slot).wait()

            @pl.when(i > 0)
            def _():
                out_copy(t, slot).wait()

            begin_fn(slot)
            for c in range(nch):
                if c + 1 < nch:
                    row_copy(slot, c + 1).start()
                else:
                    idx_copy(tn, other).wait()
                    to_expert_ids(other)
                    row_copy(other, 0).start()
                row_copy(slot, c).wait()
                chunk_fn(c, rows_v.at[c % 2], slot)
            out_copy(t, slot).start()

    row_copy(0, 0).wait()
    for k in range(len(sides)):
        side_copy(k, 0, 0).wait()
    out_copy(0, 0).wait()
    out_copy(0, 1).wait()


GELU_C0 = 0.7978845608028654
GELU_C1 = 0.044715


def _sc_peer_call(erow, hrows, gate, utab, vtab, tok0, m):
    d = utab.shape[1]
    assert m % (2 * SC_WORKERS) == 0 and d % SC_BLOCK == 0
    tpw = m // SC_WORKERS
    per_tab = PEER_SEL // SC_ROWS
    nvec = SC_BLOCK // SC_LANES
    mesh = plsc.VectorSubcoreMesh(core_axis_name="c", subcore_axis_name="s")
    vec = lambda nelem, dt: pltpu.VMEM((nelem,), dt)

    @functools.partial(
        pl.kernel, mesh=mesh, out_type=jax.ShapeDtypeStruct((m, d), F32),
        scratch_types=[vec(PEER_SEL, I32), vec(PEER_SEL, I32), vec(d, F32), vec(d, F32),
                       vec(PEER_SEL, F32), vec(PEER_SEL, F32), vec(d, F32), vec(d, F32),
                       pltpu.VMEM((2, SC_ROWS, d), F32), vec(PEER_SEL * SC_LANES, F32), vec(PEER_SEL, F32),
                       pltpu.SemaphoreType.DMA((10,))],
        compiler_params=_sc_params(), name="sc_peer")
    def run(utab_hbm, vtab_hbm, erow_hbm, h_hbm, gate_hbm, out_hbm,
            idx_a, idx_b, h_a, h_b, gate_a, gate_b, acc_a, acc_b, rows_v, part_v, coef_v, sems):
        wid = lax.axis_index("s") * SC_CORES + lax.axis_index("c")
        zero = jnp.zeros((SC_LANES,), F32)
        lane_iota = lax.iota(I32, SC_LANES)
        hbufs, gbufs, accs_v = (h_a, h_b), (gate_a, gate_b), (acc_a, acc_b)

        def begin(slot):
            for j in range(PEER_SEL):
                part_v[pl.ds(j * SC_LANES, SC_LANES)] = zero
            for j in range(d // SC_LANES):
                accs_v[slot][pl.ds(j * SC_LANES, SC_LANES)] = zero

        def coefficients(slot):
            for g in range(PEER_SEL // SC_LANES):
                rowbase = (g * SC_LANES + lane_iota) * SC_LANES
                a = plsc.load_gather(part_v, [rowbase])
                for l in range(1, SC_LANES):
                    a = a + plsc.load_gather(part_v, [rowbase + l])
                z = GELU_C0 * (a + GELU_C1 * a * a * a)
                th = 1.0 - 2.0 / (jnp.exp(2.0 * z) + 1.0)
                sl = pl.ds(g * SC_LANES, SC_LANES)
                coef_v[sl] = gbufs[slot][sl] * (0.5 * a * (1.0 + th))

        def chunk(c, buf, slot):
            if c < per_tab:
                @pl.loop(0, d // SC_BLOCK)
                def _(lb):
                    lane0 = pl.multiple_of(lb * SC_BLOCK, SC_BLOCK)
                    hs = [hbufs[slot][pl.ds(lane0 + q * SC_LANES, SC_LANES)] for q in range(nvec)]

                    @plsc.parallel_loop(0, SC_ROWS, unroll=2)
                    def _(r):
                        ps = [hs[q] * buf[r, pl.ds(lane0 + q * SC_LANES, SC_LANES)] for q in range(nvec)]
                        while len(ps) > 1:
                            ps = [x + y for x, y in zip(ps[0::2], ps[1::2])]
                        row = pl.multiple_of((c * SC_ROWS + r) * SC_LANES, SC_LANES)
                        plsc.addupdate(part_v.at[pl.ds(row, SC_LANES)], ps[0])

                if c == per_tab - 1:
                    coefficients(slot)
            else:
                cc = c - per_tab
                acc_v = accs_v[slot]

                @pl.loop(0, d // SC_BLOCK)
                def _(lb):
                    lane0 = pl.multiple_of(lb * SC_BLOCK, SC_BLOCK)
                    accs = tuple(acc_v[pl.ds(lane0 + q * SC_LANES, SC_LANES)] for q in range(nvec))

                    @plsc.parallel_loop(0, SC_ROWS, unroll=2, carry=accs)
                    def accs(r, acc):
                        ck = plsc.load_gather(coef_v, [jnp.full((SC_LANES,), cc * SC_ROWS, I32) + r])
                        return tuple(acc[q] + ck * buf[r, pl.ds(lane0 + q * SC_LANES, SC_LANES)]
                                     for q in range(nvec))

                    for q in range(nvec):
                        acc_v[pl.ds(lane0 + q * SC_LANES, SC_LANES)] = accs[q]

        _sc_token_stream((utab_hbm, vtab_hbm), erow_hbm, ((h_hbm, hbufs), (gate_hbm, gbufs)), out_hbm, accs_v,
                         tok0, tpw, wid * tpw, (idx_a, idx_b), rows_v, sems, begin, chunk)

    return run(utab, vtab, erow, hrows, gate)


def _peer_layer(x2, sc2, sh2, g2, peer_wq, peer_keys, utab, vtab, upack, vpack, lng, lnb, seq, alpha,
                tt_route=512, tt_gather=64, tm=512, sc_tokens=0):
    n, d = x2.shape
    keys = peer_keys.reshape(2 * PEER_HEADS, N_KEYS, PEER_DK // 2)
    h2, hv, erow, gate = _route_call(x2, sc2, sh2, peer_wq.astype(BF16), keys.astype(BF16), seq, tt_route)
    n_tc = n - sc_tokens
    y_sc = _sc_peer_call(erow, h2, gate, utab, vtab, n_tc, sc_tokens) if sc_tokens else None
    coef = _peer_u_call(erow, upack, hv, gate, tt_gather, n_tc)
    y_tc = _peer_v_call(erow, coef, vpack, tt_gather, n_tc)
    return _resln_call(x2, y_tc, y_sc, g2, lng.reshape(1, d), lnb.reshape(1, d), seq, tm, alpha)


def kernel(x, c, w_ada, b_ada, w_in, cmp_pe, cmp_w1, cmp_w2, w_pool, pool_scale, w_lift, w_o,
           ln_g, ln_b, peer_wq, peer_keys, peer_u, peer_v):
    bsz, seq, d = x.shape
    depth = w_ada.shape[0]
    assert d == D_MODEL and seq % 512 == 0 and SLC_TOPN <= seq // SLC_LEN <= LANES // N_KV
    alpha = (2 * depth) ** 0.25

    c_pad = jnp.zeros((SUBLANES, d), F32).at[:bsz].set(c)
    mods = _ada_call(c_pad, w_ada, b_ada)[:, :bsz]
    cst = _Consts(seq)
    nchain = bsz
    bpc = bsz // nchain
    xs = [x[i * bpc:(i + 1) * bpc].reshape(bpc * seq, d) for i in range(nchain)]
    sc_tokens = SC_TOKEN_SHARE * bpc * seq // 32
    for l in range(depth):
        upack, vpack = _pack_table(peer_u[l]), _pack_table(peer_v[l])
        for i in range(nchain):
            sh1, sc1, g1, sh2, sc2, g2 = (mods[l][i * bpc:(i + 1) * bpc, j * d:(j + 1) * d].reshape(bpc, 1, d)
                                          for j in range(6))
            xi = _token_mixer_layer(xs[i], sc1, sh1, g1, w_in[l], cmp_pe[l], cmp_w1[l], cmp_w2[l], w_pool[l],
                                    pool_scale[l], w_lift[l], w_o[l], ln_g[l, 0], ln_b[l, 0], cst, bpc, seq, alpha)
            xs[i] = _peer_layer(xi, sc2, sh2, g2, peer_wq[l], peer_keys[l], peer_u[l], peer_v[l], upack, vpack,
                                ln_g[l, 1], ln_b[l, 1], seq, alpha, sc_tokens=sc_tokens)
    return jnp.concatenate(xs, axis=0).reshape(bsz, seq, d)
```

```python
import dataclasses
import functools
from typing import NamedTuple

import jax
import jax.numpy as jnp
import numpy as np
from jax import lax
from jax.experimental import pallas as pl
from jax.experimental.pallas import tpu as pltpu
from jax.experimental.pallas import tpu_sc as plsc

F32 = jnp.float32
BF16 = jnp.bfloat16
I32 = jnp.int32
HI = lax.Precision.HIGHEST

D_MODEL = 1024
N_HEADS = 8
HEAD_DIM = 64
N_KV = 2
HPG = N_HEADS // N_KV
ROT_DIM = HEAD_DIM // 4
ROT_HALF = ROT_DIM // 2
ROPE_THETA = 500000.0
CMP_LEN = 32
CMP_STRIDE = 16
SLC_LEN = 64
SLC_TOPN = 16
WINDOW = 512
SCALE = HEAD_DIM ** -0.5
NEG = -1e30
FORCE_INIT = 1e6
FORCE_LOCAL = 2e6
POOL_GROUPS = 4
POOL_WINDOWS = (2, 4, 8, 16)
POOL_WIDTH = 512
POOL_GW = POOL_WIDTH // POOL_GROUPS
POOL_HALO = 16
ATT_WIDTH = N_HEADS * HEAD_DIM
KV_WIDTH = 3 * 2 * N_KV * HEAD_DIM
GATE_NSA = 3 * N_HEADS
MERGE_GATES = 2 * D_MODEL
PEER_HEADS = 8
N_KEYS = 128
PEER_TOPK = 16
PEER_DK = 128
PEER_SEL = PEER_HEADS * PEER_TOPK
LN_EPS = 1e-5

LANES = 128
SUBLANES = 8
VMEM_LIMIT = 56 * 1024 * 1024

NT_DIMS = (((1,), (1,)), ((), ()))


def _cparams(sem):
    return pltpu.CompilerParams(dimension_semantics=sem, vmem_limit_bytes=VMEM_LIMIT)


class _Tiles(NamedTuple):
    in_proj: int = 256
    cmp_select: int = 256
    slc_q: int = 512
    slc_k: int = 512
    win: int = 256
    pool: int = 512
    merge: int = 256
    route: int = 512
    gather: int = 128
    res_ln: int = 512


def _ada_kernel(c_ref, w_ref, b_ref, o_ref):
    c = c_ref[...]
    ca = c * jax.nn.sigmoid(c)
    o_ref[...] = jnp.dot(ca, w_ref[...], precision=HI, preferred_element_type=F32) + b_ref[...]


def _ada_call(c_pad, w_ada, b_ada):
    depth = w_ada.shape[0]
    nblk = w_ada.shape[2] // D_MODEL
    rows = c_pad.shape[0]
    return pl.pallas_call(
        _ada_kernel,
        grid=(depth, nblk),
        in_specs=[
            pl.BlockSpec((rows, D_MODEL), lambda l, j: (0, 0)),
            pl.BlockSpec((None, D_MODEL, D_MODEL), lambda l, j: (l, 0, j)),
            pl.BlockSpec((None, 1, D_MODEL), lambda l, j: (l, 0, j)),
        ],
        out_specs=pl.BlockSpec((None, rows, D_MODEL), lambda l, j: (l, 0, j)),
        out_shape=jax.ShapeDtypeStruct((depth, rows, nblk * D_MODEL), F32),
        compiler_params=_cparams(("arbitrary", "arbitrary")),
        name="ada_mod",
    )(c_pad, w_ada, b_ada.reshape(depth, 1, -1))


IN_COLS = ATT_WIDTH + KV_WIDTH + POOL_WIDTH + MERGE_GATES + LANES


def _rope_lanes(z, rc, rs1, rs2):
    return z * rc + pltpu.roll(z, ROT_HALF, 1) * rs1 + pltpu.roll(z, LANES - ROT_HALF, 1) * rs2


def _inproj_kernel(x_ref, sc_ref, sh_ref, w_ref, rc_ref, rs1_ref, rs2_ref,
                   q_ref, kv_ref, p_ref, mrg_ref, gn_ref):
    h = x_ref[...] * (1.0 + sc_ref[...]) + sh_ref[...]
    a = jnp.dot(h.astype(BF16), w_ref[...], preferred_element_type=F32)
    rc, rs1, rs2 = rc_ref[...], rs1_ref[...], rs2_ref[...]
    for j in range(ATT_WIDTH // LANES):
        q_ref[:, j * LANES:(j + 1) * LANES] = _rope_lanes(a[:, j * LANES:(j + 1) * LANES], rc, rs1, rs2)
    for br in range(3):
        c0 = ATT_WIDTH + br * 2 * LANES
        k = a[:, c0:c0 + LANES]
        if br > 0:
            k = _rope_lanes(k, rc, rs1, rs2)
        kv_ref[:, br * 2 * LANES:br * 2 * LANES + LANES] = k
        kv_ref[:, br * 2 * LANES + LANES:(br + 1) * 2 * LANES] = a[:, c0 + LANES:c0 + 2 * LANES]
    c1 = ATT_WIDTH + KV_WIDTH
    p_ref[...] = a[:, c1:c1 + POOL_WIDTH]
    mrg_ref[...] = a[:, c1 + POOL_WIDTH:c1 + POOL_WIDTH + MERGE_GATES]
    gn_ref[...] = a[:, c1 + POOL_WIDTH + MERGE_GATES:]


def _inproj_call(x2, sc, sh, w, rc, rs1, rs2, seq, tm):
    n = x2.shape[0]
    tpb = seq // tm
    tok = lambda i: (i, 0)
    bat = lambda i: (i // tpb, 0, 0)
    pos = lambda i: (i % tpb, 0)
    full = lambda i: (0, 0)
    return pl.pallas_call(
        _inproj_kernel,
        grid=(n // tm,),
        in_specs=[
            pl.BlockSpec((tm, D_MODEL), tok),
            pl.BlockSpec((None, 1, D_MODEL), bat),
            pl.BlockSpec((None, 1, D_MODEL), bat),
            pl.BlockSpec((D_MODEL, IN_COLS), full),
            pl.BlockSpec((tm, LANES), pos),
            pl.BlockSpec((tm, LANES), pos),
            pl.BlockSpec((tm, LANES), pos),
        ],
        out_specs=[
            pl.BlockSpec((tm, ATT_WIDTH), tok),
            pl.BlockSpec((tm, KV_WIDTH), tok),
            pl.BlockSpec((tm, POOL_WIDTH), tok),
            pl.BlockSpec((tm, MERGE_GATES), tok),
            pl.BlockSpec((tm, LANES), tok),
        ],
        out_shape=[
            jax.ShapeDtypeStruct((n, ATT_WIDTH), F32),
            jax.ShapeDtypeStruct((n, KV_WIDTH), F32),
            jax.ShapeDtypeStruct((n, POOL_WIDTH), F32),
            jax.ShapeDtypeStruct((n, MERGE_GATES), F32),
            jax.ShapeDtypeStruct((n, LANES), F32),
        ],
        compiler_params=_cparams(("arbitrary",)),
        name="in_proj",
    )(x2, sc, sh, w, rc, rs1, rs2)


def _compress_kernel(z_ref, pet_ref, peb_ref, w1t_ref, w1b_ref, w2_ref, rc_ref, rs1_ref, rs2_ref, o_ref):
    z = z_ref[...]
    rows = z.shape[0]
    top = jnp.dot(z + pet_ref[...], w1t_ref[...], precision=HI, preferred_element_type=F32)
    bot = jnp.dot(z + peb_ref[...], w1b_ref[...], precision=HI, preferred_element_type=F32)
    pre = top + pltpu.roll(bot, rows - 1, 0)
    y = jnp.dot(jax.nn.gelu(pre), w2_ref[...], precision=HI, preferred_element_type=F32)
    o_ref[...] = _rope_lanes(y, rc_ref[...], rs1_ref[...], rs2_ref[...])


def _compress_call(z, pet, peb, w1t, w1b, w2, rc, rs1, rs2):
    b, _, rows, width = z.shape
    kvsel = lambda i, j: (j, 0, 0)
    return pl.pallas_call(
        _compress_kernel,
        grid=(b, 2),
        in_specs=[
            pl.BlockSpec((None, None, rows, width), lambda i, j: (i, j, 0, 0)),
            pl.BlockSpec((None, 1, width), kvsel),
            pl.BlockSpec((None, 1, width), kvsel),
            pl.BlockSpec((None, width, LANES), kvsel),
            pl.BlockSpec((None, width, LANES), kvsel),
            pl.BlockSpec((None, LANES, LANES), kvsel),
            pl.BlockSpec((None, rows, LANES), kvsel),
            pl.BlockSpec((None, rows, LANES), kvsel),
            pl.BlockSpec((None, rows, LANES), kvsel),
        ],
        out_specs=pl.BlockSpec((None, None, rows, LANES), lambda i, j: (i, j, 0, 0)),
        out_shape=jax.ShapeDtypeStruct((b, 2, rows, LANES), F32),
        compiler_params=_cparams(("arbitrary", "arbitrary")),
        name="compress",
    )(z, pet, peb, w1t, w1b, w2, rc, rs1, rs2)


def _cmpsel_kernel(q_ref, kc_ref, vc_ref, c2s_ref, o_ref, sel_ref, *, tq):
    t0 = pl.program_id(1) * tq
    kc = kc_ref[...]
    vc = vc_ref[...]
    rows = kc.shape[0]
    trow = t0 + lax.broadcasted_iota(I32, (tq, rows), 0)
    cend = lax.broadcasted_iota(I32, (tq, rows), 1) * CMP_STRIDE + (CMP_LEN - 1)
    vis = cend <= trow
    anyv = (trow[:, :1] >= CMP_LEN - 1).astype(F32)
    imp = jnp.zeros((tq, LANES), F32)
    for g in range(N_KV):
        kg = kc[:, g * HEAD_DIM:(g + 1) * HEAD_DIM].astype(BF16)
        vg = vc[:, g * HEAD_DIM:(g + 1) * HEAD_DIM].astype(BF16)
        psum = jnp.zeros((tq, rows), F32)
        for h in range(HPG):
            hd = g * HPG + h
            qh = q_ref[:, hd * HEAD_DIM:(hd + 1) * HEAD_DIM].astype(BF16)
            s = lax.dot_general(qh, kg, NT_DIMS, preferred_element_type=F32) * SCALE
            s = jnp.where(vis, s, NEG)
            e = jnp.exp(s - jnp.max(s, axis=-1, keepdims=True))
            p = e / jnp.sum(e, axis=-1, keepdims=True) * anyv
            o_ref[:, hd * HEAD_DIM:(hd + 1) * HEAD_DIM] = jnp.dot(
                p.astype(BF16), vg, preferred_element_type=F32)
            psum = psum + p
        imp = imp + jnp.dot(psum.astype(BF16), c2s_ref[g], preferred_element_type=F32)
    lane = lax.broadcasted_iota(I32, (tq, LANES), 1)
    blk = lane & (SLC_LEN - 1)
    cur = lax.shift_right_logical(t0 + lax.broadcasted_iota(I32, (tq, LANES), 0), 6)
    score = jnp.where(blk <= cur, imp, NEG)
    score = jnp.where(blk == 0, FORCE_INIT, score)
    score = jnp.where(blk == cur, FORCE_LOCAL, score)
    sc_t = score.T
    nblk = LANES // N_KV
    jrow = lax.broadcasted_iota(I32, (nblk, tq), 0)
    sel_parts = []
    for g in range(N_KV):
        sc = sc_t[g * nblk:(g + 1) * nblk]
        cnt = jnp.zeros((nblk, tq), I32)
        for k in range(nblk):
            rk = sc[k:k + 1, :]
            ge = (rk >= sc).astype(I32)
            gt = (rk > sc).astype(I32)
            cnt = cnt + jnp.where(jrow > k, ge, gt)
        sel_parts.append((cnt < SLC_TOPN).astype(F32))
    sel_ref[...] = jnp.concatenate(sel_parts, axis=0).T


def _cmpsel_call(q, kvc, c2s, seq, tq):
    n = q.shape[0]
    b = n // seq
    nq = seq // tq
    rows = kvc.shape[2]
    tok = lambda i, j: (i * nq + j, 0)
    return pl.pallas_call(
        functools.partial(_cmpsel_kernel, tq=tq),
        grid=(b, nq),
        in_specs=[
            pl.BlockSpec((tq, ATT_WIDTH), tok),
            pl.BlockSpec((None, None, rows, LANES), lambda i, j: (i, 0, 0, 0)),
            pl.BlockSpec((None, None, rows, LANES), lambda i, j: (i, 1, 0, 0)),
            pl.BlockSpec((N_KV, rows, LANES), lambda i, j: (0, 0, 0)),
        ],
        out_specs=[pl.BlockSpec((tq, ATT_WIDTH), tok), pl.BlockSpec((tq, LANES), tok)],
        out_shape=[jax.ShapeDtypeStruct((n, ATT_WIDTH), F32), jax.ShapeDtypeStruct((n, LANES), F32)],
        compiler_params=_cparams(("arbitrary", "arbitrary")),
        name="cmp_select",
    )(q, kvc, kvc, c2s)


def _slc_kernel(q_ref, k_ref, v_ref, sel_ref, o_ref, qs_sc, m_sc, l_sc, acc_sc, *, tq, tk):
    qi = pl.program_id(1)
    kt = pl.program_id(2)
    nk = pl.num_programs(2)

    @pl.when(kt == 0)
    def _init():
        for hd in range(N_HEADS):
            g, h = divmod(hd, HPG)
            qs_sc[g, h * tq:(h + 1) * tq, :] = (q_ref[:, hd * HEAD_DIM:(hd + 1) * HEAD_DIM] * SCALE).astype(BF16)
        m_sc[...] = jnp.full(m_sc.shape, NEG, F32)
        l_sc[...] = jnp.zeros(l_sc.shape, F32)
        acc_sc[...] = jnp.zeros(acc_sc.shape, F32)

    @pl.when(kt * tk <= qi * tq + (tq - 1))
    def _step():
        t = qi * tq + lax.broadcasted_iota(I32, (tq, tk), 0)
        kp = kt * tk + lax.broadcasted_iota(I32, (tq, tk), 1)
        causal = kp <= t
        nblk = LANES // N_KV
        jb = lax.broadcasted_iota(I32, (nblk, tk), 0)
        kb = lax.shift_right_logical(kt * tk + lax.broadcasted_iota(I32, (nblk, tk), 1), 6)
        expand = (jb == kb).astype(BF16)
        for g in range(N_KV):
            selg = sel_ref[:, g * nblk:(g + 1) * nblk].astype(BF16)
            member = jnp.dot(selg, expand, preferred_element_type=F32)
            bias = jnp.where(jnp.logical_and(causal, member > 0.5), 0.0, NEG)
            bias = jnp.concatenate([bias] * HPG, axis=0)
            kg = k_ref[:, g * HEAD_DIM:(g + 1) * HEAD_DIM].astype(BF16)
            vg = v_ref[:, g * HEAD_DIM:(g + 1) * HEAD_DIM].astype(BF16)
            s = lax.dot_general(qs_sc[g], kg, NT_DIMS, preferred_element_type=F32) + bias
            chunks = [s[:, c * LANES:(c + 1) * LANES] for c in range(tk // LANES)]
            mc = chunks[0]
            for x in chunks[1:]:
                mc = jnp.maximum(mc, x)
            m_old = m_sc[g]
            m_new = jnp.maximum(m_old, jnp.max(mc, axis=-1, keepdims=True))
            alpha = jnp.exp(m_old - m_new)
            ps = [jnp.exp(x - m_new) for x in chunks]
            lsum = ps[0]
            for x in ps[1:]:
                lsum = lsum + x
            l_sc[g] = alpha * l_sc[g] + lsum
            p = jnp.concatenate(ps, axis=1).astype(BF16)
            acc_sc[g] = alpha[:, :HEAD_DIM] * acc_sc[g] + jnp.dot(p, vg, preferred_element_type=F32)
            m_sc[g] = m_new

    @pl.when(kt == nk - 1)
    def _fin():
        for hd in range(N_HEADS):
            g, h = divmod(hd, HPG)
            l = jnp.sum(l_sc[g, h * tq:(h + 1) * tq, :], axis=-1, keepdims=True)
            o_ref[:, hd * HEAD_DIM:(hd + 1) * HEAD_DIM] = acc_sc[g, h * tq:(h + 1) * tq, :] / l


def _slc_call(q, kv, sel, seq, tq, tk):
    n = q.shape[0]
    b = n // seq
    nq = seq // tq
    nk = seq // tk
    tok = lambda i, j, k: (i * nq + j, 0)

    def key_map(col):
        def f(i, j, k):
            last = (j * tq + tq - 1) // tk
            return (i * nk + jnp.minimum(k, last), col)
        return f

    return pl.pallas_call(
        functools.partial(_slc_kernel, tq=tq, tk=tk),
        grid=(b, nq, nk),
        in_specs=[
            pl.BlockSpec((tq, ATT_WIDTH), tok),
            pl.BlockSpec((tk, LANES), key_map(2)),
            pl.BlockSpec((tk, LANES), key_map(3)),
            pl.BlockSpec((tq, LANES), tok),
        ],
        out_specs=pl.BlockSpec((tq, ATT_WIDTH), tok),
        out_shape=jax.ShapeDtypeStruct((n, ATT_WIDTH), F32),
        scratch_shapes=[
            pltpu.VMEM((N_KV, HPG * tq, HEAD_DIM), BF16),
            pltpu.VMEM((N_KV, HPG * tq, LANES), F32),
            pltpu.VMEM((N_KV, HPG * tq, LANES), F32),
            pltpu.VMEM((N_KV, HPG * tq, HEAD_DIM), F32),
        ],
        compiler_params=_cparams(("arbitrary", "arbitrary", "arbitrary")),
        name="slc_attn",
    )(q, kv, kv, sel)


def _win_kernel(q_ref, *refs, tq, nkb):
    k_refs = refs[:nkb]
    v_refs = refs[nkb:2 * nkb]
    o_ref = refs[2 * nkb]
    qi = pl.program_id(1)
    t = qi * tq + lax.broadcasted_iota(I32, (tq, tq), 0)
    col = lax.broadcasted_iota(I32, (tq, tq), 1)
    biases = []
    for j in range(nkb):
        kp = (qi - (nkb - 1) + j) * tq + col
        diff = t - kp
        ok = jnp.logical_and(jnp.logical_and(diff >= 0, diff < WINDOW), kp >= 0)
        biases.append(jnp.concatenate([jnp.where(ok, 0.0, NEG)] * HPG, axis=0))
    for g in range(N_KV):
        qs = jnp.concatenate(
            [(q_ref[:, (g * HPG + h) * HEAD_DIM:(g * HPG + h + 1) * HEAD_DIM] * SCALE).astype(BF16)
             for h in range(HPG)], axis=0)
        chunks = []
        for j in range(nkb):
            kg = k_refs[j][:, g * HEAD_DIM:(g + 1) * HEAD_DIM].astype(BF16)
            s = lax.dot_general(qs, kg, NT_DIMS, preferred_element_type=F32) + biases[j]
            chunks += [s[:, c * LANES:(c + 1) * LANES] for c in range(tq // LANES)]
        mc = chunks[0]
        for x in chunks[1:]:
            mc = jnp.maximum(mc, x)
        m = jnp.broadcast_to(jnp.max(mc, axis=-1, keepdims=True), mc.shape)
        ps = [jnp.exp(x - m) for x in chunks]
        lsum = ps[0]
        for x in ps[1:]:
            lsum = lsum + x
        l = jnp.sum(lsum, axis=-1, keepdims=True)
        per = tq // LANES
        o = None
        for j in range(nkb):
            vg = v_refs[j][:, g * HEAD_DIM:(g + 1) * HEAD_DIM].astype(BF16)
            pj = jnp.concatenate(ps[j * per:(j + 1) * per], axis=1).astype(BF16)
            oj = jnp.dot(pj, vg, preferred_element_type=F32)
            o = oj if o is None else o + oj
        o = o / l
        for h in range(HPG):
            hd = g * HPG + h
            o_ref[:, hd * HEAD_DIM:(hd + 1) * HEAD_DIM] = o[h * tq:(h + 1) * tq]


def _win_call(q, kv, seq, tq):
    n = q.shape[0]
    b = n // seq
    nq = seq // tq
    nkb = WINDOW // tq + 1
    tok = lambda i, j: (i * nq + j, 0)

    def key_map(col, back):
        return lambda i, j: (i * nq + jnp.maximum(j - back, 0), col)

    k_specs = [pl.BlockSpec((tq, LANES), key_map(4, nkb - 1 - jj)) for jj in range(nkb)]
    v_specs = [pl.BlockSpec((tq, LANES), key_map(5, nkb - 1 - jj)) for jj in range(nkb)]
    return pl.pallas_call(
        functools.partial(_win_kernel, tq=tq, nkb=nkb),
        grid=(b, nq),
        in_specs=[pl.BlockSpec((tq, ATT_WIDTH), tok)] + k_specs + v_specs,
        out_specs=pl.BlockSpec((tq, ATT_WIDTH), tok),
        out_shape=jax.ShapeDtypeStruct((n, ATT_WIDTH), F32),
        compiler_params=_cparams(("arbitrary", "arbitrary")),
        name="win_attn",
    )(q, *([kv] * (2 * nkb)))


def _pool_kernel(p_ref, prev_ref, w_ref, sc_ref, o_ref, *, ts):
    i = pl.program_id(1)
    x = p_ref[...]
    prev = prev_ref[...] * (i > 0).astype(F32)
    xe = jnp.concatenate([prev, x], axis=0)
    t1 = (i * ts + 1 + lax.broadcasted_iota(I32, (ts, POOL_GW), 0)).astype(F32)
    for g, w in enumerate(POOL_WINDOWS):
        a = xe[:, g * POOL_GW:(g + 1) * POOL_GW]
        off = POOL_HALO
        span = 1
        while span < w:
            a = a[span:] + a[:-span]
            off -= span
            span *= 2
        sums = a[off:off + ts]
        cnt = jnp.minimum(t1, float(w))
        pooled = sums / cnt - x[:, g * POOL_GW:(g + 1) * POOL_GW]
        y = jnp.dot(pooled.astype(BF16), w_ref[g], preferred_element_type=F32)
        o_ref[:, g * POOL_GW:(g + 1) * POOL_GW] = y * sc_ref[:, g * POOL_GW:(g + 1) * POOL_GW]


def _pool_call(p_in, w_pool, pool_scale, seq, ts):
    n = p_in.shape[0]
    b = n // seq
    nt = seq // ts
    hpt = ts // POOL_HALO
    tok = lambda i, j: (i * nt + j, 0)
    return pl.pallas_call(
        functools.partial(_pool_kernel, ts=ts),
        grid=(b, nt),
        in_specs=[
            pl.BlockSpec((ts, POOL_WIDTH), tok),
            pl.BlockSpec((POOL_HALO, POOL_WIDTH), lambda i, j: (i * nt * hpt + jnp.maximum(j * hpt - 1, 0), 0)),
            pl.BlockSpec((POOL_GROUPS, POOL_GW, POOL_GW), lambda i, j: (0, 0, 0)),
            pl.BlockSpec((1, POOL_WIDTH), lambda i, j: (0, 0)),
        ],
        out_specs=pl.BlockSpec((ts, POOL_WIDTH), tok),
        out_shape=jax.ShapeDtypeStruct((n, POOL_WIDTH), F32),
        compiler_params=_cparams(("arbitrary", "arbitrary")),
        name="pool_mix",
    )(p_in, p_in, w_pool, pool_scale)


def _layer_norm(z, g, b):
    mu = jnp.mean(z, axis=-1, keepdims=True)
    zc = z - mu
    var = jnp.mean(zc * zc, axis=-1, keepdims=True)
    return zc * lax.rsqrt(var + LN_EPS) * g + b


def _merge_kernel(oc_ref, os_ref, ow_ref, gn_ref, op_ref, gm_ref, x_ref, g1_ref, lng_ref, lnb_ref,
                  wl_ref, wo_ref, eb_ref, o_ref, *, alpha):
    gate = jax.nn.sigmoid(gn_ref[...])
    branches = (oc_ref, os_ref, ow_ref)
    oatt = None
    for br in range(3):
        gx = jnp.dot(gate, eb_ref[br], precision=HI, preferred_element_type=F32)
        term = gx * branches[br][...]
        oatt = term if oatt is None else oatt + term
    la = jnp.dot(oatt.astype(BF16), wl_ref[0], preferred_element_type=F32)
    lb = jnp.dot(op_ref[...].astype(BF16), wl_ref[1], preferred_element_type=F32)
    gm = jax.nn.sigmoid(gm_ref[...])
    merged = gm[:, :D_MODEL] * la + gm[:, D_MODEL:] * lb
    y = jnp.dot(merged.astype(BF16), wo_ref[...], preferred_element_type=F32)
    z = alpha * x_ref[...] + g1_ref[...] * y
    o_ref[...] = _layer_norm(z, lng_ref[...], lnb_ref[...])


def _merge_call(oc, osl, ow, gn, op, gm, x2, g1, lng, lnb, wl, wo, eb, seq, tm, alpha):
    n = x2.shape[0]
    tpb = seq // tm
    tok = lambda i: (i, 0)
    bat = lambda i: (i // tpb, 0, 0)
    return pl.pallas_call(
        functools.partial(_merge_kernel, alpha=alpha),
        grid=(n // tm,),
        in_specs=[
            pl.BlockSpec((tm, ATT_WIDTH), tok), pl.BlockSpec((tm, ATT_WIDTH), tok),
            pl.BlockSpec((tm, ATT_WIDTH), tok), pl.BlockSpec((tm, LANES), tok),
            pl.BlockSpec((tm, POOL_WIDTH), tok), pl.BlockSpec((tm, MERGE_GATES), tok),
            pl.BlockSpec((tm, D_MODEL), tok),
            pl.BlockSpec((None, 1, D_MODEL), bat),
            pl.BlockSpec((1, D_MODEL), lambda i: (0, 0)), pl.BlockSpec((1, D_MODEL), lambda i: (0, 0)),
            pl.BlockSpec((2, ATT_WIDTH, D_MODEL), lambda i: (0, 0, 0)),
            pl.BlockSpec((D_MODEL, D_MODEL), lambda i: (0, 0)),
            pl.BlockSpec((3, LANES, ATT_WIDTH), lambda i: (0, 0, 0)),
        ],
        out_specs=pl.BlockSpec((tm, D_MODEL), tok),
        out_shape=jax.ShapeDtypeStruct((n, D_MODEL), F32),
        compiler_params=_cparams(("arbitrary",)),
        name="merge_out",
    )(oc, osl, ow, gn, op, gm, x2, g1, lng, lnb, wl, wo, eb)


def _extract_top(cur, ids, n):
    rows = cur.shape[0]
    rio = lax.broadcasted_iota(I32, cur.shape, 0)
    vals, outs = [], []
    for _ in range(n):
        m = jnp.max(cur, axis=0, keepdims=True)
        pos = jnp.min(jnp.where(cur == m, rio, rows), axis=0, keepdims=True)
        hit = rio == pos
        vals.append(m)
        outs.append(pos if ids is None else jnp.max(jnp.where(hit, ids, -1), axis=0, keepdims=True))
        cur = jnp.where(hit, -jnp.inf, cur)
    return jnp.concatenate(vals, axis=0), jnp.concatenate(outs, axis=0)


def _route_kernel(x_ref, sc_ref, sh_ref, wq_ref, keys_ref, h_ref, hv_ref, e_ref, g_ref,
                  st_sc, ts_sc, ti_sc, eo_sc, go_sc):
    h = x_ref[...] * (1.0 + sc_ref[...]) + sh_ref[...]
    h_ref[...] = h
    for r in range(SUBLANES):
        hv_ref[:, r, :] = h[:, r * LANES:(r + 1) * LANES]
    qp = jnp.dot(h.astype(BF16), wq_ref[...], preferred_element_type=F32).astype(BF16)
    half = PEER_DK // 2
    for hp in range(2 * PEER_HEADS):
        st_sc[hp] = lax.dot_general(keys_ref[hp], qp[:, hp * half:(hp + 1) * half], NT_DIMS,
                                    preferred_element_type=F32)

    def half_body(hp, carry):
        vals, ids = _extract_top(st_sc[hp], None, PEER_TOPK)
        ts_sc[hp] = vals
        ti_sc[hp] = ids
        return carry

    lax.fori_loop(0, 2 * PEER_HEADS, half_body, 0)

    def head_body(hh, carry):
        s1, s2 = ts_sc[2 * hh], ts_sc[2 * hh + 1]
        i1, i2 = ti_sc[2 * hh], ti_sc[2 * hh + 1]
        brow = lax.broadcasted_iota(I32, (SUBLANES, s1.shape[1]), 0)
        cands = [s1[0:1, :] + s2]
        cidxs = [i1[0:1, :] * N_KEYS + i2]
        for a in range(1, SUBLANES):
            ok = brow < PEER_TOPK // (a + 1)
            cands.append(jnp.where(ok, s1[a:a + 1, :] + s2[:SUBLANES], -jnp.inf))
            cidxs.append(i1[a:a + 1, :] * N_KEYS + i2[:SUBLANES])
        cands.append(s1[SUBLANES:] + s2[0:1, :])
        cidxs.append(i1[SUBLANES:] * N_KEYS + i2[0:1, :])
        sv, ei = _extract_top(jnp.concatenate(cands, axis=0), jnp.concatenate(cidxs, axis=0), PEER_TOPK)
        ex = jnp.exp(sv - sv[0:1, :])
        go_sc[hh] = ex / jnp.sum(ex, axis=0, keepdims=True)
        eo_sc[hh] = ei.astype(F32)
        return carry

    lax.fori_loop(0, PEER_HEADS, head_body, 0)
    e_all = jnp.concatenate([eo_sc[hh] for hh in range(PEER_HEADS)], axis=0)
    g_all = jnp.concatenate([go_sc[hh] for hh in range(PEER_HEADS)], axis=0)
    e_ref[...] = e_all.T.astype(I32) * (D_MODEL // 2 // LANES)
    g_ref[...] = g_all.T


def _route_call(x2, sc, sh, wq, keys, seq, tt):
    n = x2.shape[0]
    tpb = seq // tt
    tok = lambda i: (i, 0)
    bat = lambda i: (i // tpb, 0, 0)
    nhp = 2 * PEER_HEADS
    return pl.pallas_call(
        _route_kernel,
        grid=(n // tt,),
        in_specs=[
            pl.BlockSpec((tt, D_MODEL), tok),
            pl.BlockSpec((None, 1, D_MODEL), bat),
            pl.BlockSpec((None, 1, D_MODEL), bat),
            pl.BlockSpec((D_MODEL, PEER_HEADS * PEER_DK), lambda i: (0, 0)),
            pl.BlockSpec((nhp, N_KEYS, PEER_DK // 2), lambda i: (0, 0, 0)),
        ],
        out_specs=[pl.BlockSpec((tt, D_MODEL), tok), pl.BlockSpec((tt, SUBLANES, LANES), lambda i: (i, 0, 0)),
                   pl.BlockSpec((tt, PEER_SEL), tok), pl.BlockSpec((tt, PEER_SEL), tok)],
        out_shape=[jax.ShapeDtypeStruct((n, D_MODEL), F32), jax.ShapeDtypeStruct((n, SUBLANES, LANES), F32),
                   jax.ShapeDtypeStruct((n, PEER_SEL), I32), jax.ShapeDtypeStruct((n, PEER_SEL), F32)],
        scratch_shapes=[
            pltpu.VMEM((nhp, N_KEYS, tt), F32),
            pltpu.VMEM((nhp, PEER_TOPK, tt), F32),
            pltpu.VMEM((nhp, PEER_TOPK, tt), I32),
            pltpu.VMEM((PEER_HEADS, PEER_TOPK, tt), F32),
            pltpu.VMEM((PEER_HEADS, PEER_TOPK, tt), F32),
        ],
        compiler_params=_cparams(("arbitrary",)),
        name="peer_route",
    )(x2, sc, sh, wq, keys)


HALF_ROWS = SUBLANES // 2
HI_MASK = -65536
PAIR_TILES = PEER_SEL // 2
PAIR_ROWS = PAIR_TILES * SUBLANES


def _load_two_experts(tab_ref, ra, rb):
    wa = tab_ref[pl.ds(pl.multiple_of(ra, HALF_ROWS), HALF_ROWS), :]
    wb = tab_ref[pl.ds(pl.multiple_of(rb, HALF_ROWS), HALF_ROWS), :]
    w2 = jnp.concatenate([wa, wb], axis=0)
    return lax.bitcast_convert_type(w2 << 16, F32), lax.bitcast_convert_type(w2 & HI_MASK, F32)


def _fold_pairs(vs):
    row = lax.broadcasted_iota(I32, (SUBLANES, LANES), 0)
    shift = HALF_ROWS // 2
    while len(vs) > 1:
        low = (row & shift) == 0
        vs = [jnp.where(low, a + pltpu.roll(a, SUBLANES - shift, 0), b + pltpu.roll(b, shift, 0))
              for a, b in zip(vs[0::2], vs[1::2])]
        shift //= 2
    return vs[0]


def _fold_order():
    idx = [[2 * i if r < HALF_ROWS else 2 * i + 1 for r in range(SUBLANES)] for i in range(HALF_ROWS)]
    shift = HALF_ROWS // 2
    while len(idx) > 1:
        idx = [[a[r] if (r & shift) == 0 else b[r] for r in range(SUBLANES)]
               for a, b in zip(idx[0::2], idx[1::2])]
        shift //= 2
    return idx[0]


def _peer_u_kernel(e_sm, tab_ref, hv_ref, gate_ref, coef_ref, *, tt):
    row = lax.broadcasted_iota(I32, (SUBLANES, LANES), 0)
    low = row < HALF_ROWS
    eye = (lax.broadcasted_iota(I32, (PEER_SEL, LANES), 0) ==
           lax.broadcasted_iota(I32, (PEER_SEL, LANES), 1))
    order = _fold_order()

    def finish(t, part):
        col = jnp.sum(part, axis=-1, keepdims=True)
        a_row = jnp.sum(jnp.where(eye, col, 0.0), axis=0, keepdims=True)
        coef_ref[t] = gate_ref[pl.ds(t, 1), :] * jax.nn.gelu(a_row)

    def token(t, part_prev):
        finish(jnp.maximum(t - 1, 0), part_prev)
        hv = hv_ref[t]
        hsw = pltpu.roll(hv, HALF_ROWS, 0)
        h_lo = jnp.where(low, hv, hsw)
        h_hi = jnp.where(low, hsw, hv)
        folded = []
        for j in range(PEER_SEL // SUBLANES):
            prods = []
            for i in range(HALF_ROWS):
                ka = j * SUBLANES + order.index(2 * i)
                kb = j * SUBLANES + order.index(2 * i + 1)
                lo, hi = _load_two_experts(tab_ref, e_sm[t, ka], e_sm[t, kb])
                prods.append(lo * h_lo + hi * h_hi)
            folded.append(_fold_pairs(prods))
        return jnp.concatenate(folded, axis=0)

    last = lax.fori_loop(0, tt, token, jnp.zeros((PEER_SEL, LANES), F32))
    finish(tt - 1, last)


def _peer_v_kernel(e_sm, coef_ref, tab_ref, y_ref, cv_sc, *, tt, nacc):
    row = lax.broadcasted_iota(I32, (SUBLANES, LANES), 0)
    low = row < HALF_ROWS
    rr = lax.broadcasted_iota(I32, (PAIR_ROWS, LANES), 0)
    kk = lax.broadcasted_iota(I32, (PAIR_ROWS, LANES), 1)
    onehot = (kk == 2 * (rr >> 3) + ((rr >> 2) & 1)).astype(F32)
    ones = jnp.ones((LANES, LANES), BF16)

    def expand(t, slot):
        lhs = (onehot * coef_ref[t]).astype(BF16)
        cv_sc[slot] = jnp.dot(lhs, ones, preferred_element_type=F32)

    def process(t, slot):
        acc_lo = [jnp.zeros((SUBLANES, LANES), F32) for _ in range(nacc)]
        acc_hi = [jnp.zeros((SUBLANES, LANES), F32) for _ in range(nacc)]
        for j in range(PAIR_TILES):
            grp = e_sm.at[t, pl.ds(2 * j // SUBLANES * SUBLANES, SUBLANES)]
            lo, hi = _load_two_experts(tab_ref, grp[2 * j % SUBLANES], grp[(2 * j + 1) % SUBLANES])
            cv = cv_sc[slot, j * SUBLANES:(j + 1) * SUBLANES, :]
            acc_lo[j % nacc] = acc_lo[j % nacc] + cv * lo
            acc_hi[j % nacc] = acc_hi[j % nacc] + cv * hi
        a_lo, a_hi = acc_lo[0], acc_hi[0]
        for i in range(1, nacc):
            a_lo = a_lo + acc_lo[i]
            a_hi = a_hi + acc_hi[i]
        a_lo = a_lo + pltpu.roll(a_lo, HALF_ROWS, 0)
        a_hi = a_hi + pltpu.roll(a_hi, HALF_ROWS, 0)
        y_ref[t] = jnp.where(low, a_lo, a_hi)

    expand(0, 0)

    def two_tokens(i, carry):
        t = 2 * i
        expand(t + 1, 1)
        process(t, 0)
        expand(jnp.minimum(t + 2, tt - 1), 0)
        process(t + 1, 1)
        return carry

    lax.fori_loop(0, tt // 2, two_tokens, 0)


def _resident_table_spec(tab):
    return pl.BlockSpec(tab.shape, lambda i: (0, 0), pipeline_mode=pl.Buffered(1))


def _peer_u_call(erow, tab, hv, gate, tt, n):
    return pl.pallas_call(
        functools.partial(_peer_u_kernel, tt=tt),
        grid=(n // tt,),
        in_specs=[
            pl.BlockSpec((tt, PEER_SEL), lambda i: (i, 0), memory_space=pltpu.SMEM),
            _resident_table_spec(tab),
            pl.BlockSpec((tt, SUBLANES, LANES), lambda i: (i, 0, 0)),
            pl.BlockSpec((tt, PEER_SEL), lambda i: (i, 0)),
        ],
        out_specs=pl.BlockSpec((tt, 1, PEER_SEL), lambda i: (i, 0, 0)),
        out_shape=jax.ShapeDtypeStruct((n, 1, PEER_SEL), F32),
        compiler_params=_cparams(("arbitrary",)),
        name="peer_u",
    )(erow, tab, hv, gate)


def _peer_v_call(erow, coef3, tab, tt, n):
    assert tt % 2 == 0
    return pl.pallas_call(
        functools.partial(_peer_v_kernel, tt=tt, nacc=4),
        grid=(n // tt,),
        in_specs=[
            pl.BlockSpec((tt, PEER_SEL), lambda i: (i, 0), memory_space=pltpu.SMEM),
            pl.BlockSpec((tt, 1, PEER_SEL), lambda i: (i, 0, 0)),
            _resident_table_spec(tab),
        ],
        out_specs=pl.BlockSpec((tt, SUBLANES, LANES), lambda i: (i, 0, 0)),
        out_shape=jax.ShapeDtypeStruct((n, SUBLANES, LANES), F32),
        scratch_shapes=[pltpu.VMEM((2, PAIR_ROWS, LANES), F32)],
        compiler_params=_cparams(("arbitrary",)),
        name="peer_v",
    )(erow, coef3, tab)


def _resln_kernel(x_ref, ytc_ref, ysc_ref, g_ref, lng_ref, lnb_ref, o_ref, *, alpha, tc_tiles):
    y_tc = jnp.concatenate([ytc_ref[:, r, :] for r in range(SUBLANES)], axis=1)
    y = jnp.where(pl.program_id(0) < tc_tiles, y_tc, ysc_ref[...])
    z = alpha * x_ref[...] + g_ref[...] * y
    o_ref[...] = _layer_norm(z, lng_ref[...], lnb_ref[...])


def _resln_call(x2, y_tc, y_sc, g2, lng, lnb, seq, tm, alpha):
    n = x2.shape[0]
    n_tc = y_tc.shape[0]
    assert n_tc % tm == 0 and n_tc > 0
    tc_tiles = n_tc // tm
    if y_sc is None:
        y_sc = jnp.zeros((tm, D_MODEL), F32)
    tpb = seq // tm
    tok = lambda i: (i, 0)
    return pl.pallas_call(
        functools.partial(_resln_kernel, alpha=alpha, tc_tiles=tc_tiles),
        grid=(n // tm,),
        in_specs=[
            pl.BlockSpec((tm, D_MODEL), tok),
            pl.BlockSpec((tm, SUBLANES, LANES), lambda i: (jnp.minimum(i, tc_tiles - 1), 0, 0)),
            pl.BlockSpec((tm, D_MODEL), lambda i: (jnp.maximum(i - tc_tiles, 0), 0)),
            pl.BlockSpec((None, 1, D_MODEL), lambda i: (i // tpb, 0, 0)),
            pl.BlockSpec((1, D_MODEL), lambda i: (0, 0)), pl.BlockSpec((1, D_MODEL), lambda i: (0, 0)),
        ],
        out_specs=pl.BlockSpec((tm, D_MODEL), tok),
        out_shape=jax.ShapeDtypeStruct((n, D_MODEL), F32),
        compiler_params=_cparams(("arbitrary",)),
        name="res_ln",
    )(x2, y_tc, y_sc, g2, lng, lnb)


def _rope_lane_tables(pos):
    inv = ROPE_THETA ** (-jnp.arange(0, ROT_DIM, 2, dtype=F32) / ROT_DIM)
    ang = pos.astype(F32)[:, None] * inv[None, :]
    cos, sin = jnp.cos(ang), jnp.sin(ang)
    lane = np.arange(LANES) % HEAD_DIM
    fidx = lane % ROT_HALF
    first = jnp.asarray(lane < ROT_HALF)
    second = jnp.asarray((lane >= ROT_HALF) & (lane < ROT_DIM))
    rot = jnp.asarray(lane < ROT_DIM)
    cl, sl = cos[:, fidx], sin[:, fidx]
    rc = jnp.where(rot, cl, 1.0)
    rs1 = jnp.where(second, sl, 0.0)
    rs2 = jnp.where(first, -sl, 0.0)
    return rc, rs1, rs2


def _pack_kernel(t_ref, o_ref):
    x = t_ref[...]
    half = x.shape[1] // 2
    lo = lax.bitcast_convert_type(x[:, :half].astype(BF16).astype(F32), I32)
    hi = lax.bitcast_convert_type(x[:, half:].astype(BF16).astype(F32), I32)
    w = lax.shift_right_logical(lo, jnp.full_like(lo, 16)) | hi
    rows = x.shape[0]
    per = half // LANES
    for r in range(per):
        o_ref[pl.ds(r, rows, stride=per), :] = w[:, r * LANES:(r + 1) * LANES]


def _pack_table(tab, te=512):
    e, d = tab.shape
    per = d // 2 // LANES
    return pl.pallas_call(
        _pack_kernel, grid=(e // te,),
        in_specs=[pl.BlockSpec((te, d), lambda i: (i, 0))],
        out_specs=pl.BlockSpec((te * per, LANES), lambda i: (i, 0)),
        out_shape=jax.ShapeDtypeStruct((e * per, LANES), I32),
        compiler_params=_cparams(("arbitrary",)), name="pack_table",
    )(tab)


def _cmp_to_slc_wide(rows, n_slc):
    st = np.arange(rows) * CMP_STRIDE
    js = np.arange(n_slc) * SLC_LEN
    ov = np.minimum(st[:, None] + CMP_LEN, js[None, :] + SLC_LEN) - np.maximum(st[:, None], js[None, :])
    c2s = np.maximum(ov, 0).astype(np.float32) / CMP_STRIDE
    wide = np.zeros((N_KV, rows, LANES), np.float32)
    nblk = LANES // N_KV
    for g in range(N_KV):
        wide[g, :, g * nblk:g * nblk + n_slc] = c2s
    return jnp.asarray(wide)


def _gate_expanders():
    eb = np.zeros((3, LANES, ATT_WIDTH), np.float32)
    for hd in range(N_HEADS):
        for br in range(3):
            eb[br, hd * 3 + br, hd * HEAD_DIM:(hd + 1) * HEAD_DIM] = 1.0
    return jnp.asarray(eb)


class _Consts:
    def __init__(self, seq):
        self.rows = seq // CMP_STRIDE
        self.rope = _rope_lane_tables(jnp.arange(seq))
        cpos = jnp.arange(self.rows) * CMP_STRIDE + CMP_LEN - 1
        crope = _rope_lane_tables(cpos)
        ident = (jnp.ones_like(crope[0]), jnp.zeros_like(crope[0]), jnp.zeros_like(crope[0]))
        self.crope = tuple(jnp.stack([a, b]) for a, b in zip(crope, ident))
        self.c2s = _cmp_to_slc_wide(self.rows, seq // SLC_LEN)
        self.eb = _gate_expanders()


def _token_mixer_layer(x2, sc1, sh1, g1, w_in, cmp_pe, cmp_w1, cmp_w2, w_pool, pool_scale, w_lift, w_o,
                       lng, lnb, cst, bsz, seq, alpha, tiles=_Tiles()):
    d = D_MODEL
    rows = cst.rows
    s1 = ATT_WIDTH + KV_WIDTH
    s2 = s1 + GATE_NSA
    s3 = s2 + POOL_WIDTH
    w_gate = jnp.pad(w_in[:, s1:s2], ((0, 0), (0, LANES - GATE_NSA)))
    w_all = jnp.concatenate([w_in[:, :s1], w_in[:, s2:s3], w_in[:, s3:], w_gate], axis=1).astype(BF16)
    q, kv, p_in, g_mrg, g_nsa = _inproj_call(x2, sc1, sh1, w_all, *cst.rope, seq, tiles.in_proj)

    eye_g = jnp.eye(N_KV, dtype=F32)
    zc = kv[:, :2 * LANES].reshape(bsz, rows, CMP_STRIDE, 2, LANES)
    zc = jnp.transpose(zc, (0, 3, 1, 2, 4)).reshape(bsz, 2, rows, CMP_STRIDE * LANES)
    w1x = jnp.einsum('klde,gh->klgdhe', cmp_w1.reshape(2, CMP_LEN, HEAD_DIM, HEAD_DIM), eye_g)
    w1x = w1x.reshape(2, CMP_LEN * LANES, LANES)
    half = CMP_STRIDE * LANES
    pex = jnp.broadcast_to(cmp_pe[:, :, None, :], (2, CMP_LEN, N_KV, HEAD_DIM)).reshape(2, 1, CMP_LEN * LANES)
    w2x = jnp.einsum('kef,gh->kgehf', cmp_w2, eye_g).reshape(2, LANES, LANES)
    kvc = _compress_call(zc, pex[:, :, :half], pex[:, :, half:], w1x[:, :half], w1x[:, half:], w2x,
                         *cst.crope)

    o_cmp, sel = _cmpsel_call(q, kvc, cst.c2s.astype(BF16), seq, tiles.cmp_select)
    o_slc = _slc_call(q, kv, sel, seq, tiles.slc_q, tiles.slc_k)
    o_win = _win_call(q, kv, seq, tiles.win)
    o_pool = _pool_call(p_in, w_pool.astype(BF16), pool_scale.reshape(1, -1), seq, tiles.pool)
    return _merge_call(o_cmp, o_slc, o_win, g_nsa, o_pool, g_mrg, x2, g1,
                       lng.reshape(1, d), lnb.reshape(1, d),
                       w_lift.astype(BF16), w_o.astype(BF16), cst.eb, seq, tiles.merge, alpha)


SC_CORES = 2
SC_SUBCORES = 16
SC_LANES = 16
SC_WORKERS = SC_CORES * SC_SUBCORES
SC_ROWS = 32
SC_TOKEN_SHARE = 10
SC_BLOCK = 256


def _sc_params():
    cp = pltpu.CompilerParams()
    if "needs_layout_passes" in pltpu.CompilerParams.__dataclass_fields__:
        cp = dataclasses.replace(cp, needs_layout_passes=False)
    return cp


def _sc_token_stream(tabs, erow_hbm, sides, out_hbm, out_bufs, tok0, tpw, base, idx_bufs, rows_v, sems,
                     begin_fn, chunk_fn):
    per_tab = PEER_SEL // SC_ROWS
    nch = per_tab * len(tabs)
    assert nch % 2 == 0

    def idx_copy(t, s):
        return pltpu.make_async_copy(erow_hbm.at[tok0 + base + t], idx_bufs[s], sems.at[2 + s])

    def out_copy(t, s):
        return pltpu.make_async_copy(out_bufs[s], out_hbm.at[base + t], sems.at[4 + s])

    def side_copy(k, t, s):
        hbm, bufs = sides[k]
        return pltpu.make_async_copy(hbm.at[tok0 + base + t], bufs[s], sems.at[6 + 2 * k + s])

    def row_copy(s, c):
        rows = idx_bufs[s].at[pl.ds((c % per_tab) * SC_ROWS, SC_ROWS)]
        return pltpu.make_async_copy(tabs[c // per_tab].at[rows], rows_v.at[c % 2], sems.at[c % 2])

    def to_expert_ids(s):
        shift = jnp.full((SC_LANES,), (D_MODEL // 2 // LANES).bit_length() - 1, I32)
        for j in range(PEER_SEL // SC_LANES):
            sl = pl.ds(j * SC_LANES, SC_LANES)
            idx_bufs[s][sl] = lax.shift_right_logical(idx_bufs[s][sl], shift)

    idx_copy(0, 0).start()
    for k in range(len(sides)):
        side_copy(k, 0, 0).start()
    idx_copy(0, 0).wait()
    to_expert_ids(0)
    row_copy(0, 0).start()

    @pl.loop(0, tpw // 2)
    def _(i):
        for slot in (0, 1):
            other = 1 - slot
            t = 2 * i + slot
            tn = jnp.minimum(t + 1, tpw - 1)
            idx_copy(tn, other).start()
            for k in range(len(sides)):
                side_copy(k, tn, other).start()
            for k in range(len(sides)):
                side_copy(k, t, slot).wait()

            @pl.when(i > 0)
            def _():
                out_copy(t, slot).wait()

            begin_fn(slot)
            for c in range(nch):
                if c + 1 < nch:
                    row_copy(slot, c + 1).start()
                else:
                    idx_copy(tn, other).wait()
                    to_expert_ids(other)
                    row_copy(other, 0).start()
                row_copy(slot, c).wait()
                chunk_fn(c, rows_v.at[c % 2], slot)
            out_copy(t, slot).start()

    row_copy(0, 0).wait()
    for k in range(len(sides)):
        side_copy(k, 0, 0).wait()
    out_copy(0, 0).wait()
    out_copy(0, 1).wait()


GELU_C0 = 0.7978845608028654
GELU_C1 = 0.044715


def _sc_peer_call(erow, hrows, gate, utab, vtab, tok0, m):
    d = utab.shape[1]
    assert m % (2 * SC_WORKERS) == 0 and d % SC_BLOCK == 0
    tpw = m // SC_WORKERS
    per_tab = PEER_SEL // SC_ROWS
    nvec = SC_BLOCK // SC_LANES
    mesh = plsc.VectorSubcoreMesh(core_axis_name="c", subcore_axis_name="s")
    vec = lambda nelem, dt: pltpu.VMEM((nelem,), dt)

    @functools.partial(
        pl.kernel, mesh=mesh, out_type=jax.ShapeDtypeStruct((m, d), F32),
        scratch_types=[vec(PEER_SEL, I32), vec(PEER_SEL, I32), vec(d, F32), vec(d, F32),
                       vec(PEER_SEL, F32), vec(PEER_SEL, F32), vec(d, F32), vec(d, F32),
                       pltpu.VMEM((2, SC_ROWS, d), F32), vec(PEER_SEL * SC_LANES, F32), vec(PEER_SEL, F32),
                       pltpu.SemaphoreType.DMA((10,))],
        compiler_params=_sc_params(), name="sc_peer")
    def run(utab_hbm, vtab_hbm, erow_hbm, h_hbm, gate_hbm, out_hbm,
            idx_a, idx_b, h_a, h_b, gate_a, gate_b, acc_a, acc_b, rows_v, part_v, coef_v, sems):
        wid = lax.axis_index("s") * SC_CORES + lax.axis_index("c")
        zero = jnp.zeros((SC_LANES,), F32)
        lane_iota = lax.iota(I32, SC_LANES)
        hbufs, gbufs, accs_v = (h_a, h_b), (gate_a, gate_b), (acc_a, acc_b)

        def begin(slot):
            for j in range(PEER_SEL):
                part_v[pl.ds(j * SC_LANES, SC_LANES)] = zero
            for j in range(d // SC_LANES):
                accs_v[slot][pl.ds(j * SC_LANES, SC_LANES)] = zero

        def coefficients(slot):
            for g in range(PEER_SEL // SC_LANES):
                rowbase = (g * SC_LANES + lane_iota) * SC_LANES
                a = plsc.load_gather(part_v, [rowbase])
                for l in range(1, SC_LANES):
                    a = a + plsc.load_gather(part_v, [rowbase + l])
                z = GELU_C0 * (a + GELU_C1 * a * a * a)
                th = 1.0 - 2.0 / (jnp.exp(2.0 * z) + 1.0)
                sl = pl.ds(g * SC_LANES, SC_LANES)
                coef_v[sl] = gbufs[slot][sl] * (0.5 * a * (1.0 + th))

        def chunk(c, buf, slot):
            if c < per_tab:
                @pl.loop(0, d // SC_BLOCK)
                def _(lb):
                    lane0 = pl.multiple_of(lb * SC_BLOCK, SC_BLOCK)
                    hs = [hbufs[slot][pl.ds(lane0 + q * SC_LANES, SC_LANES)] for q in range(nvec)]

                    @plsc.parallel_loop(0, SC_ROWS, unroll=2)
                    def _(r):
                        ps = [hs[q] * buf[r, pl.ds(lane0 + q * SC_LANES, SC_LANES)] for q in range(nvec)]
                        while len(ps) > 1:
                            ps = [x + y for x, y in zip(ps[0::2], ps[1::2])]
                        row = pl.multiple_of((c * SC_ROWS + r) * SC_LANES, SC_LANES)
                        plsc.addupdate(part_v.at[pl.ds(row, SC_LANES)], ps[0])

                if c == per_tab - 1:
                    coefficients(slot)
            else:
                cc = c - per_tab
                acc_v = accs_v[slot]

                @pl.loop(0, d // SC_BLOCK)
                def _(lb):
                    lane0 = pl.multiple_of(lb * SC_BLOCK, SC_BLOCK)
                    accs = tuple(acc_v[pl.ds(lane0 + q * SC_LANES, SC_LANES)] for q in range(nvec))

                    @plsc.parallel_loop(0, SC_ROWS, unroll=2, carry=accs)
                    def accs(r, acc):
                        ck = plsc.load_gather(coef_v, [jnp.full((SC_LANES,), cc * SC_ROWS, I32) + r])
                        return tuple(acc[q] + ck * buf[r, pl.ds(lane0 + q * SC_LANES, SC_LANES)]
                                     for q in range(nvec))

                    for q in range(nvec):
                        acc_v[pl.ds(lane0 + q * SC_LANES, SC_LANES)] = accs[q]

        _sc_token_stream((utab_hbm, vtab_hbm), erow_hbm, ((h_hbm, hbufs), (gate_hbm, gbufs)), out_hbm, accs_v,
                         tok0, tpw, wid * tpw, (idx_a, idx_b), rows_v, sems, begin, chunk)

    return run(utab, vtab, erow, hrows, gate)


def _peer_layer(x2, sc2, sh2, g2, peer_wq, peer_keys, utab, vtab, upack, vpack, lng, lnb, seq, alpha,
                tiles=_Tiles(), sc_tokens=0):
    n, d = x2.shape
    keys = peer_keys.reshape(2 * PEER_HEADS, N_KEYS, PEER_DK // 2)
    h2, hv, erow, gate = _route_call(x2, sc2, sh2, peer_wq.astype(BF16), keys.astype(BF16), seq, tiles.route)
    n_tc = n - sc_tokens
    y_sc = _sc_peer_call(erow, h2, gate, utab, vtab, n_tc, sc_tokens) if sc_tokens else None
    coef = _peer_u_call(erow, upack, hv, gate, tiles.gather, n_tc)
    y_tc = _peer_v_call(erow, coef, vpack, tiles.gather, n_tc)
    return _resln_call(x2, y_tc, y_sc, g2, lng.reshape(1, d), lnb.reshape(1, d), seq, tiles.res_ln, alpha)


def kernel(x, c, w_ada, b_ada, w_in, cmp_pe, cmp_w1, cmp_w2, w_pool, pool_scale, w_lift, w_o,
           ln_g, ln_b, peer_wq, peer_keys, peer_u, peer_v):
    bsz, seq, d = x.shape
    depth = w_ada.shape[0]
    assert d == D_MODEL and seq % 512 == 0 and SLC_TOPN <= seq // SLC_LEN <= LANES // N_KV
    alpha = (2 * depth) ** 0.25

    c_pad = jnp.zeros((SUBLANES, d), F32).at[:bsz].set(c)
    mods = _ada_call(c_pad, w_ada, b_ada)[:, :bsz]
    cst = _Consts(seq)
    nchain = bsz
    bpc = bsz // nchain
    xs = [x[i * bpc:(i + 1) * bpc].reshape(bpc * seq, d) for i in range(nchain)]
    sc_tokens = SC_TOKEN_SHARE * bpc * seq // 16
    for l in range(depth):
        upack, vpack = _pack_table(peer_u[l]), _pack_table(peer_v[l])
        for i in range(nchain):
            sh1, sc1, g1, sh2, sc2, g2 = (mods[l][i * bpc:(i + 1) * bpc, j * d:(j + 1) * d].reshape(bpc, 1, d)
                                          for j in range(6))
            xi = _token_mixer_layer(xs[i], sc1, sh1, g1, w_in[l], cmp_pe[l], cmp_w1[l], cmp_w2[l], w_pool[l],
                                    pool_scale[l], w_lift[l], w_o[l], ln_g[l, 0], ln_b[l, 0], cst, bpc, seq, alpha)
            xs[i] = _peer_layer(xi, sc2, sh2, g2, peer_wq[l], peer_keys[l], peer_u[l], peer_v[l], upack, vpack,
                                ln_g[l, 1], ln_b[l, 1], seq, alpha, sc_tokens=sc_tokens)
    return jnp.concatenate(xs, axis=0).reshape(bsz, seq, d)
```
